```python
import jax, jax.numpy as jnp
from jax import lax
import numpy as np

D_MODEL = 2048
BATCH = 4
SEQ = 4096
DEPTH = 4

HEAD_DIM = 128
GROUP_HEADS = 4
GROUP_WIDTH = GROUP_HEADS * HEAD_DIM
N_GROUPS = 4
MIX_WIDTH = N_GROUPS * GROUP_WIDTH
Q_BLOCK = 128
MOBA_BLOCK = 256
MOBA_TOP_K = 3
MOBA_Q_CHUNK = 32
MLA_Q_RANK = 512
MLA_KV_RANK = 256
MLA_NOPE_DIM = 128
MLA_ROPE_DIM = 64
MLA_V_DIM = 128
MLA_QK_DIM = MLA_NOPE_DIM + MLA_ROPE_DIM
D_FF = 5632
CONV_WIDTH = 3
ROPE_THETA = 10000.0
EPS = 1e-6
FORGET_BIAS_LO = 1.0
FORGET_BIAS_HI = 4.0
IN_WIDTHS = (GROUP_WIDTH, GROUP_WIDTH, GROUP_WIDTH, GROUP_HEADS,
             GROUP_WIDTH, GROUP_WIDTH, GROUP_WIDTH,
             GROUP_WIDTH, GROUP_WIDTH, GROUP_WIDTH,
             MLA_Q_RANK, MLA_KV_RANK, MLA_ROPE_DIM)
IN_WIDTH = sum(IN_WIDTHS)

kernel_name = "hybrid_fox_stickbreak_moba_mla_convglu"


def rms_norm(x, g):
    xf = x.astype(jnp.float32)
    y = xf * lax.rsqrt(jnp.mean(xf * xf, axis=-1, keepdims=True) + EPS)
    return (y * g.astype(jnp.float32)).astype(x.dtype)


def to_heads(t, n_heads):
    b, s, _ = t.shape
    return t.reshape(b, s, n_heads, -1).transpose(0, 2, 1, 3)


def from_heads(t):
    b, h, s, e = t.shape
    return t.transpose(0, 2, 1, 3).reshape(b, s, h * e)


def rope_angles(seq, dim):
    inv_freq = ROPE_THETA ** (-jnp.arange(0, dim, 2, dtype=jnp.float32) / dim)
    ang = jnp.arange(seq, dtype=jnp.float32)[:, None] * inv_freq[None, :]
    return jnp.cos(ang), jnp.sin(ang)


def apply_rope(t, cos, sin):
    t1, t2 = jnp.split(t, 2, axis=-1)
    c, s = cos.astype(t.dtype), sin.astype(t.dtype)
    return jnp.concatenate([t1 * c - t2 * s, t1 * s + t2 * c], axis=-1)


def sweep_queries(block_fn, seq, chunk):
    out = lax.map(block_fn, jnp.arange(seq // chunk, dtype=jnp.int32) * chunk)
    n, b, h, c, e = out.shape
    return jnp.moveaxis(out, 0, 2).reshape(b, h, n * c, e)


def causal_softmax_attention(q, k, v, cum_log_forget=None):
    seq = q.shape[2]
    scale = q.shape[-1] ** -0.5
    key_pos = jnp.arange(seq)

    def block(start):
        qb = lax.dynamic_slice_in_dim(q, start, Q_BLOCK, axis=2)
        q_pos = start + jnp.arange(Q_BLOCK)
        logits = jnp.einsum("bhqd,bhkd->bhqk", qb, k).astype(jnp.float32) * scale
        if cum_log_forget is not None:
            cq = lax.dynamic_slice_in_dim(cum_log_forget, start, Q_BLOCK, axis=2)
            logits = logits + (cq[..., :, None] - cum_log_forget[..., None, :])
        logits = jnp.where(key_pos[None, :] <= q_pos[:, None], logits, -jnp.inf)
        p = jax.nn.softmax(logits, axis=-1)
        return jnp.einsum("bhqk,bhkd->bhqd", p.astype(v.dtype), v)

    return sweep_queries(block, seq, Q_BLOCK)


def stick_breaking_attention(q, k, v):
    seq = q.shape[2]
    scale = q.shape[-1] ** -0.5
    key_pos = jnp.arange(seq)

    def block(start):
        qb = lax.dynamic_slice_in_dim(q, start, Q_BLOCK, axis=2)
        q_pos = start + jnp.arange(Q_BLOCK)
        z = jnp.einsum("bhqd,bhkd->bhqk", qb, k).astype(jnp.float32) * scale
        strict = key_pos[None, :] < q_pos[:, None]
        log_beta = jax.nn.log_sigmoid(z)
        log_rest = jnp.where(strict, log_beta - z, 0.0)
        log_stick = lax.cumsum(log_rest, axis=3, reverse=True) - log_rest
        w = jnp.where(strict, jnp.exp(log_beta + log_stick), 0.0)
        return jnp.einsum("bhqk,bhkd->bhqd", w.astype(v.dtype), v)

    return sweep_queries(block, seq, Q_BLOCK)


def moba_attention(q, k, v):
    b, h, seq, d = q.shape
    scale = d ** -0.5
    n_blocks = -(-seq // MOBA_BLOCK)
    pad = n_blocks * MOBA_BLOCK - seq
    kp = jnp.pad(k, ((0, 0), (0, 0), (0, pad), (0, 0)))
    vp = jnp.pad(v, ((0, 0), (0, 0), (0, pad), (0, 0)))
    k_blocks = kp.reshape(b, h, n_blocks, MOBA_BLOCK, d)
    v_blocks = vp.reshape(b, h, n_blocks, MOBA_BLOCK, d)
    k_mean = jnp.mean(k_blocks.astype(jnp.float32), axis=3).astype(k.dtype)
    n_sel = min(MOBA_TOP_K, n_blocks)
    b_ix = jnp.arange(b)[:, None, None, None]
    h_ix = jnp.arange(h)[None, :, None, None]
    block_ids = jnp.arange(n_blocks)

    def chunk(start):
        qc = lax.dynamic_slice_in_dim(q, start, MOBA_Q_CHUNK, axis=2)
        q_pos = start + jnp.arange(MOBA_Q_CHUNK)
        own = start // MOBA_BLOCK
        gate = jnp.einsum("bhcd,bhnd->bhcn", qc, k_mean).astype(jnp.float32)
        gate = jnp.where(block_ids < own, gate, -jnp.inf)
        _, sel = lax.top_k(gate, n_sel)
        sel_valid = jnp.arange(n_sel) < own
        k_sel = k_blocks[b_ix, h_ix, sel]
        v_sel = v_blocks[b_ix, h_ix, sel]
        s_sel = jnp.einsum("bhcd,bhcnkd->bhcnk", qc, k_sel).astype(jnp.float32) * scale
        s_sel = jnp.where(sel_valid[:, None], s_sel, -jnp.inf)
        s_sel = s_sel.reshape(b, h, MOBA_Q_CHUNK, n_sel * MOBA_BLOCK)
        k_own = lax.dynamic_slice_in_dim(kp, own * MOBA_BLOCK, MOBA_BLOCK, axis=2)
        v_own = lax.dynamic_slice_in_dim(vp, own * MOBA_BLOCK, MOBA_BLOCK, axis=2)
        s_own = jnp.einsum("bhcd,bhkd->bhck", qc, k_own).astype(jnp.float32) * scale
        own_pos = own * MOBA_BLOCK + jnp.arange(MOBA_BLOCK)
        s_own = jnp.where(own_pos[None, :] <= q_pos[:, None], s_own, -jnp.inf)
        p = jax.nn.softmax(jnp.concatenate([s_sel, s_own], axis=-1), axis=-1)
        p_sel = p[..., :n_sel * MOBA_BLOCK].reshape(b, h, MOBA_Q_CHUNK, n_sel, MOBA_BLOCK)
        p_own = p[..., n_sel * MOBA_BLOCK:]
        return (jnp.einsum("bhcnk,bhcnkd->bhcd", p_sel.astype(v.dtype), v_sel)
                + jnp.einsum("bhck,bhkd->bhcd", p_own.astype(v.dtype), v_own))

    return sweep_queries(chunk, seq, MOBA_Q_CHUNK)


def mla_attention(c_q, c_kv, k_rope, cq_norm, ckv_norm, w_uq, w_ukv, q_norm, k_norm, cos, sin):
    b, s, _ = c_q.shape
    q = (rms_norm(c_q, cq_norm) @ w_uq).reshape(b, s, GROUP_HEADS, MLA_QK_DIM)
    kv = (rms_norm(c_kv, ckv_norm) @ w_ukv).reshape(b, s, GROUP_HEADS, MLA_NOPE_DIM + MLA_V_DIM)
    k_nope, v = jnp.split(kv, [MLA_NOPE_DIM], axis=-1)
    k_r = jnp.broadcast_to(k_rope[:, :, None, :], (b, s, GROUP_HEADS, MLA_ROPE_DIM))
    k = jnp.concatenate([k_nope, k_r], axis=-1)
    q = rms_norm(q, q_norm).transpose(0, 2, 1, 3)
    k = rms_norm(k, k_norm).transpose(0, 2, 1, 3)
    q = jnp.concatenate([q[..., :MLA_NOPE_DIM], apply_rope(q[..., MLA_NOPE_DIM:], cos, sin)], axis=-1)
    k = jnp.concatenate([k[..., :MLA_NOPE_DIM], apply_rope(k[..., MLA_NOPE_DIM:], cos, sin)], axis=-1)
    return causal_softmax_attention(q, k, v.transpose(0, 2, 1, 3))


def causal_depthwise_conv(u, w, bias):
    seq = u.shape[1]
    up = jnp.pad(u, ((0, 0), (CONV_WIDTH - 1, 0), (0, 0)))
    out = bias.astype(u.dtype) + up[:, 0:seq] * w[0]
    for j in range(1, CONV_WIDTH):
        out = out + up[:, j:j + seq] * w[j]
    return out


def mixer_sublayer(x, rope_c, rope_r, attn_norm, w_in, b_forget, fox_q_norm, fox_k_norm,
                   moba_q_norm, moba_k_norm, mla_cq_norm, mla_ckv_norm, w_uq, w_ukv,
                   mla_q_norm, mla_k_norm, mix_out_norm, w_out):
    b, s, _ = x.shape
    h = rms_norm(x, attn_norm)
    proj = h @ w_in
    offsets = np.cumsum(IN_WIDTHS)[:-1].tolist()
    (a_q, a_k, a_v, a_f, b_q, b_k, b_v, c_q, c_k, c_v,
     d_cq, d_ckv, d_kr) = jnp.split(proj, offsets, axis=-1)

    log_f = jax.nn.log_sigmoid((a_f + b_forget).astype(jnp.float32))
    cum_log_f = jnp.cumsum(log_f, axis=1).transpose(0, 2, 1)
    qa = rms_norm(to_heads(a_q, GROUP_HEADS), fox_q_norm)
    ka = rms_norm(to_heads(a_k, GROUP_HEADS), fox_k_norm)
    o_a = causal_softmax_attention(qa, ka, to_heads(a_v, GROUP_HEADS), cum_log_f)

    o_b = stick_breaking_attention(to_heads(b_q, GROUP_HEADS), to_heads(b_k, GROUP_HEADS),
                                   to_heads(b_v, GROUP_HEADS))

    cos_c, sin_c = rope_c
    qc = apply_rope(rms_norm(to_heads(c_q, GROUP_HEADS), moba_q_norm), cos_c, sin_c)
    kc = apply_rope(rms_norm(to_heads(c_k, GROUP_HEADS), moba_k_norm), cos_c, sin_c)
    o_c = moba_attention(qc, kc, to_heads(c_v, GROUP_HEADS))

    cos_r, sin_r = rope_r
    o_d = mla_attention(d_cq, d_ckv, d_kr, mla_cq_norm, mla_ckv_norm, w_uq, w_ukv,
                        mla_q_norm, mla_k_norm, cos_r, sin_r)

    y = jnp.stack([from_heads(o_a), from_heads(o_b), from_heads(o_c), from_heads(o_d)], axis=2)
    y = rms_norm(y, mix_out_norm.reshape(N_GROUPS, GROUP_WIDTH))
    return x + y.reshape(b, s, MIX_WIDTH) @ w_out


def ffn_sublayer(x, ffn_norm, w_up, conv_w, conv_b, w_down):
    h = rms_norm(x, ffn_norm)
    u = causal_depthwise_conv(h @ w_up, conv_w, conv_b)
    gate, val = jnp.split(u, 2, axis=-1)
    return x + (jax.nn.silu(gate) * val) @ w_down


def setup_inputs(seed: int = 0) -> dict:
    key = jax.random.key(seed)
    ks = jax.random.split(key, 21)
    f32 = jnp.float32

    def normal(k, shape, scale):
        return jax.random.normal(k, shape, f32) * scale

    def gain(k, shape):
        return 1.0 + 0.02 * jax.random.normal(k, shape, f32)

    out_scale = (2 * DEPTH) ** -0.5
    return {
        "x": normal(ks[0], (BATCH, SEQ, D_MODEL), 1.0),
        "attn_norm": gain(ks[1], (DEPTH, D_MODEL)),
        "w_in": normal(ks[2], (DEPTH, D_MODEL, IN_WIDTH), D_MODEL ** -0.5),
        "b_forget": jax.random.uniform(ks[3], (DEPTH, GROUP_HEADS), f32, FORGET_BIAS_LO, FORGET_BIAS_HI),
        "fox_q_norm": gain(ks[4], (DEPTH, HEAD_DIM)),
        "fox_k_norm": gain(ks[5], (DEPTH, HEAD_DIM)),
        "moba_q_norm": gain(ks[6], (DEPTH, HEAD_DIM)),
        "moba_k_norm": gain(ks[7], (DEPTH, HEAD_DIM)),
        "mla_cq_norm": gain(ks[8], (DEPTH, MLA_Q_RANK)),
        "mla_ckv_norm": gain(ks[9], (DEPTH, MLA_KV_RANK)),
        "w_uq": normal(ks[10], (DEPTH, MLA_Q_RANK, GROUP_HEADS * MLA_QK_DIM), MLA_Q_RANK ** -0.5),
        "w_ukv": normal(ks[11], (DEPTH, MLA_KV_RANK, GROUP_HEADS * (MLA_NOPE_DIM + MLA_V_DIM)), MLA_KV_RANK ** -0.5),
        "mla_q_norm": gain(ks[12], (DEPTH, MLA_QK_DIM)),
        "mla_k_norm": gain(ks[13], (DEPTH, MLA_QK_DIM)),
        "mix_out_norm": gain(ks[14], (DEPTH, MIX_WIDTH)),
        "w_out": normal(ks[15], (DEPTH, MIX_WIDTH, D_MODEL), MIX_WIDTH ** -0.5 * out_scale),
        "ffn_norm": gain(ks[16], (DEPTH, D_MODEL)),
        "w_up": normal(ks[17], (DEPTH, D_MODEL, 2 * D_FF), D_MODEL ** -0.5),
        "conv_w": normal(ks[18], (DEPTH, CONV_WIDTH, 2 * D_FF), CONV_WIDTH ** -0.5),
        "conv_b": normal(ks[19], (DEPTH, 2 * D_FF), 0.01),
        "w_down": normal(ks[20], (DEPTH, D_FF, D_MODEL), D_FF ** -0.5 * out_scale),
    }


def reference(x, attn_norm, w_in, b_forget, fox_q_norm, fox_k_norm, moba_q_norm, moba_k_norm,
              mla_cq_norm, mla_ckv_norm, w_uq, w_ukv, mla_q_norm, mla_k_norm, mix_out_norm,
              w_out, ffn_norm, w_up, conv_w, conv_b, w_down):
    seq = x.shape[1]
    rope_c = rope_angles(seq, HEAD_DIM)
    rope_r = rope_angles(seq, MLA_ROPE_DIM)
    for l in range(DEPTH):
        x = mixer_sublayer(x, rope_c, rope_r, attn_norm[l], w_in[l], b_forget[l],
                           fox_q_norm[l], fox_k_norm[l], moba_q_norm[l], moba_k_norm[l],
                           mla_cq_norm[l], mla_ckv_norm[l], w_uq[l], w_ukv[l],
                           mla_q_norm[l], mla_k_norm[l], mix_out_norm[l], w_out[l])
        x = ffn_sublayer(x, ffn_norm[l], w_up[l], conv_w[l], conv_b[l], w_down[l])
    return x
```

```python
import functools

import jax
import jax.numpy as jnp
from jax import lax
from jax.experimental import pallas as pl
from jax.experimental.pallas import tpu as pltpu

F32 = jnp.float32
BF16 = jnp.bfloat16

D_MODEL = 2048
HEAD_DIM = 128
GROUP_HEADS = 4
GROUP_WIDTH = GROUP_HEADS * HEAD_DIM
N_GROUPS = 4
MOBA_BLOCK = 256
MOBA_TOP_K = 3
MLA_Q_RANK = 512
MLA_KV_RANK = 256
MLA_NOPE_DIM = 128
MLA_ROPE_DIM = 64
MLA_QK_DIM = MLA_NOPE_DIM + MLA_ROPE_DIM
MLA_HEAD_PAD = 256
D_FF = 5632
CONV_WIDTH = 3
ROPE_THETA = 10000.0
EPS = 1e-6

PROJ_WIDTH = 11 * GROUP_WIDTH
T_AQ, T_AK, T_CQ, T_CK, T_DCQ, T_DKV, T_AV, T_BQ, T_BK, T_BV, T_CV = range(11)

ATT_TILE = 256
NEG = -1e30
VMEM_LIMIT = 56 * 1024 * 1024
SCALE = HEAD_DIM ** -0.5
SCALE_MLA = MLA_QK_DIM ** -0.5
HALO = 16


def _dot(a, b):
    return jnp.dot(a, b, preferred_element_type=F32)


def _dot_nt(a, b):
    return lax.dot_general(a, b, (((1,), (1,)), ((), ())), preferred_element_type=F32)


def _split3(a):
    hi = a.astype(BF16)
    r1 = a - hi.astype(F32)
    mid = r1.astype(BF16)
    lo = (r1 - mid.astype(F32)).astype(BF16)
    return hi, mid, lo


def _log_sigmoid_pair(z):
    sp = jnp.log(1.0 + jnp.exp(-jnp.abs(z)))
    return jnp.minimum(z, 0.0) - sp, -jnp.maximum(z, 0.0) - sp


def _rms(v, g, denom=None):
    if denom is None:
        ms = jnp.mean(v * v, axis=-1, keepdims=True)
    else:
        ms = jnp.sum(v * v, axis=-1, keepdims=True) * (1.0 / denom)
    return v * lax.rsqrt(ms + EPS) * g


def _inproj_kernel(x_ref, g_ref, w_ref, wf_ref, bf_ref, proj_ref, negc_ref, h_scr, carry_scr,
                   *, tiles_per_batch, tm):
    i = pl.program_id(0)
    j = pl.program_id(1)

    @pl.when(j == 0)
    def _():
        h = _rms(x_ref[...], g_ref[...]).astype(BF16)
        h_scr[...] = h
        z = _dot_nt(wf_ref[...], h) + bf_ref[:, 0:1]
        logf, _ = _log_sigmoid_pair(z)
        r = lax.broadcasted_iota(jnp.int32, (tm, tm), 0)
        c = lax.broadcasted_iota(jnp.int32, (tm, tm), 1)
        tri = jnp.where(r <= c, 1.0, 0.0).astype(BF16)
        hi, mid, lo = _split3(logf)
        cum = _dot(hi, tri) + _dot(mid, tri) + _dot(lo, tri)

        @pl.when(i % tiles_per_batch == 0)
        def _():
            carry_scr[...] = jnp.zeros_like(carry_scr)

        cum = cum + carry_scr[:, 0:1]
        carry_scr[...] = jnp.broadcast_to(cum[:, tm - 1:tm], carry_scr.shape)
        for t in range(tm // ATT_TILE):
            negc_ref[0, t] = -cum[:, t * ATT_TILE:(t + 1) * ATT_TILE]

    proj_ref[...] = _dot(h_scr[...], w_ref[...]).astype(BF16)


def _inproj(x2, g, w, wf, bf, batch, seq):
    n = x2.shape[0]
    tm = 512
    tpb = seq // tm
    kern = functools.partial(_inproj_kernel, tiles_per_batch=tpb, tm=tm)
    return pl.pallas_call(
        kern,
        grid=(n // tm, PROJ_WIDTH // GROUP_WIDTH),
        in_specs=[
            pl.BlockSpec((tm, D_MODEL), lambda i, j: (i, 0)),
            pl.BlockSpec((1, D_MODEL), lambda i, j: (0, 0)),
            pl.BlockSpec((D_MODEL, GROUP_WIDTH), lambda i, j: (0, j)),
            pl.BlockSpec((8, D_MODEL), lambda i, j: (0, 0)),
            pl.BlockSpec((8, 128), lambda i, j: (0, 0)),
        ],
        out_specs=[
            pl.BlockSpec((tm, GROUP_WIDTH), lambda i, j: (i, j)),
            pl.BlockSpec((1, tm // ATT_TILE, 8, ATT_TILE), lambda i, j: (i // tpb, i % tpb, 0, 0)),
        ],
        out_shape=[
            jax.ShapeDtypeStruct((n, PROJ_WIDTH), BF16),
            jax.ShapeDtypeStruct((batch, seq // ATT_TILE, 8, ATT_TILE), F32),
        ],
        scratch_shapes=[pltpu.VMEM((tm, D_MODEL), BF16), pltpu.VMEM((8, 128), F32)],
        compiler_params=pltpu.CompilerParams(
            dimension_semantics=("arbitrary", "arbitrary"), vmem_limit_bytes=VMEM_LIMIT),
        name="inproj",
    )(x2, g, w, wf, bf)


def _prep_kernel(a_ref, c_ref, d_ref, gaq, gak, gcq, gck, gdcq, gdckv, gdq, gdk, wuq, wukv,
                 cosc, sinc, cosd, sind,
                 qka_ref, qkc_ref, sel_ref, qd_ref, kd_ref, vd_ref, kmt_scr,
                 *, tiles_per_batch, tm):
    i = pl.program_id(0)
    t = i % tiles_per_batch
    hd = HEAD_DIM
    blocks_per_tile = tm // MOBA_BLOCK

    for h in range(GROUP_HEADS):
        q = a_ref[:, h * hd:(h + 1) * hd].astype(F32)
        qka_ref[:, h * hd:(h + 1) * hd] = (_rms(q, gaq[...]) * SCALE).astype(BF16)
        k = a_ref[:, GROUP_WIDTH + h * hd:GROUP_WIDTH + (h + 1) * hd].astype(F32)
        qka_ref[:, GROUP_WIDTH + h * hd:GROUP_WIDTH + (h + 1) * hd] = _rms(k, gak[...]).astype(BF16)

    @pl.when(i == 0)
    def _():
        kmt_scr[...] = jnp.zeros_like(kmt_scr)

    cc = cosc[...]
    sc = sinc[...]
    sub8 = lax.broadcasted_iota(jnp.int32, (8, hd), 0)

    def rope_c(v):
        return v * cc + pltpu.roll(v, hd // 2, axis=1) * sc

    for h in range(GROUP_HEADS):
        k = rope_c(_rms(c_ref[:, GROUP_WIDTH + h * hd:GROUP_WIDTH + (h + 1) * hd].astype(F32), gck[...]))
        qkc_ref[:, GROUP_WIDTH + h * hd:GROUP_WIDTH + (h + 1) * hd] = k.astype(BF16)
        for blk in range(blocks_per_tile):
            km = jnp.mean(k[blk * MOBA_BLOCK:(blk + 1) * MOBA_BLOCK], axis=0, keepdims=True)
            nblk = t * blocks_per_tile + blk
            base = pl.multiple_of(h * hd + (nblk // 8) * 8, 8)
            old = kmt_scr[pl.ds(base, 8), h * hd:(h + 1) * hd]
            kmt_scr[pl.ds(base, 8), h * hd:(h + 1) * hd] = jnp.where(sub8 == nblk % 8, km, old)

    qs = []
    for h in range(GROUP_HEADS):
        q = rope_c(_rms(c_ref[:, h * hd:(h + 1) * hd].astype(F32), gcq[...])) * SCALE
        qkc_ref[:, h * hd:(h + 1) * hd] = q.astype(BF16)
        qs.append(q)
    q_all = jnp.concatenate(qs, axis=1)
    qh, qm, _ = _split3(q_all)
    kh, km_, _ = _split3(kmt_scr[...])
    gate = _dot_nt(qh, kh) + _dot_nt(qh, km_) + _dot_nt(qm, kh)

    lane = lax.broadcasted_iota(jnp.int32, (tm, hd), 1)
    rowi = lax.broadcasted_iota(jnp.int32, (tm, hd), 0)
    own = t * blocks_per_tile + rowi // MOBA_BLOCK
    lane_f = lane.astype(F32)
    ninf = jnp.float32(-jnp.inf)
    for h in range(GROUP_HEADS):
        g = jnp.where(lane < own, gate[:, h * hd:(h + 1) * hd], ninf)
        sel = jnp.zeros((tm, hd), F32)
        for _ in range(MOBA_TOP_K):
            mx = jnp.max(g, axis=1, keepdims=True)
            cand = jnp.where(g == mx, lane_f, 1e9)
            cand = jnp.where(mx > ninf, cand, 1e9)
            idx = jnp.min(cand, axis=1, keepdims=True)
            pick = lane_f == idx
            sel = jnp.where(pick, 1.0, sel)
            g = jnp.where(pick, ninf, g)
        sel_ref[:, h * hd:(h + 1) * hd] = sel.astype(BF16)

    cd = cosd[...]
    sd = sind[...]

    def rope_d(v):
        return v * cd + pltpu.roll(v, hd // 2, axis=1) * sd

    hp = MLA_HEAD_PAD
    cq = _rms(d_ref[:, 0:MLA_Q_RANK].astype(F32), gdcq[...]).astype(BF16)
    qf = _dot(cq, wuq[...])
    for h in range(GROUP_HEADS):
        qn = _rms(qf[:, h * hp:(h + 1) * hp], gdq[...], denom=MLA_QK_DIM)
        qd_ref[:, h * hp:h * hp + hd] = (qn[:, :hd] * SCALE_MLA).astype(BF16)
        qd_ref[:, h * hp + hd:(h + 1) * hp] = (rope_d(qn[:, hd:]) * SCALE_MLA).astype(BF16)

    ckv = _rms(d_ref[:, MLA_Q_RANK:MLA_Q_RANK + MLA_KV_RANK].astype(F32), gdckv[...]).astype(BF16)
    kvf = _dot(ckv, wukv[...])
    vd_ref[...] = kvf[:, GROUP_WIDTH:].astype(BF16)
    kr = d_ref[:, MLA_Q_RANK + MLA_KV_RANK:MLA_Q_RANK + MLA_KV_RANK + hd].astype(F32)
    ss_r = jnp.sum(kr * kr, axis=-1, keepdims=True)
    gk = gdk[...]
    for h in range(GROUP_HEADS):
        kn = kvf[:, h * hd:(h + 1) * hd]
        ms = (jnp.sum(kn * kn, axis=-1, keepdims=True) + ss_r) * (1.0 / MLA_QK_DIM)
        r = lax.rsqrt(ms + EPS)
        kd_ref[:, h * hp:h * hp + hd] = (kn * r * gk[:, :hd]).astype(BF16)
        kd_ref[:, h * hp + hd:(h + 1) * hp] = rope_d(kr * r * gk[:, hd:]).astype(BF16)


def _prep(proj, gaq, gak, gcq, gck, gdcq, gdckv, gdq, gdk, wuq, wukv, cosc, sinc, cosd, sind, seq):
    n = proj.shape[0]
    tm = 512
    tpb = seq // tm
    kern = functools.partial(_prep_kernel, tiles_per_batch=tpb, tm=tm)
    row = lambda i: (i, 0)
    const = lambda i: (0, 0)
    tab = lambda i: (i % tpb, 0)
    w2 = 2 * GROUP_WIDTH
    return pl.pallas_call(
        kern,
        grid=(n // tm,),
        in_specs=[
            pl.BlockSpec((tm, w2), lambda i: (i, 0)),
            pl.BlockSpec((tm, w2), lambda i: (i, 1)),
            pl.BlockSpec((tm, w2), lambda i: (i, 2)),
            pl.BlockSpec((1, HEAD_DIM), const), pl.BlockSpec((1, HEAD_DIM), const),
            pl.BlockSpec((1, HEAD_DIM), const), pl.BlockSpec((1, HEAD_DIM), const),
            pl.BlockSpec((1, MLA_Q_RANK), const), pl.BlockSpec((1, MLA_KV_RANK), const),
            pl.BlockSpec((1, MLA_HEAD_PAD), const), pl.BlockSpec((1, MLA_HEAD_PAD), const),
            pl.BlockSpec((MLA_Q_RANK, GROUP_HEADS * MLA_HEAD_PAD), const),
            pl.BlockSpec((MLA_KV_RANK, 2 * GROUP_WIDTH), const),
            pl.BlockSpec((tm, HEAD_DIM), tab), pl.BlockSpec((tm, HEAD_DIM), tab),
            pl.BlockSpec((tm, HEAD_DIM), tab), pl.BlockSpec((tm, HEAD_DIM), tab),
        ],
        out_specs=[
            pl.BlockSpec((tm, w2), row), pl.BlockSpec((tm, w2), row),
            pl.BlockSpec((tm, GROUP_WIDTH), row),
            pl.BlockSpec((tm, w2), row), pl.BlockSpec((tm, w2), row),
            pl.BlockSpec((tm, GROUP_WIDTH), row),
        ],
        out_shape=[
            jax.ShapeDtypeStruct((n, w2), BF16), jax.ShapeDtypeStruct((n, w2), BF16),
            jax.ShapeDtypeStruct((n, GROUP_WIDTH), BF16),
            jax.ShapeDtypeStruct((n, w2), BF16), jax.ShapeDtypeStruct((n, w2), BF16),
            jax.ShapeDtypeStruct((n, GROUP_WIDTH), BF16),
        ],
        scratch_shapes=[pltpu.VMEM((GROUP_WIDTH, GROUP_WIDTH), F32)],
        compiler_params=pltpu.CompilerParams(
            dimension_semantics=("arbitrary",), vmem_limit_bytes=VMEM_LIMIT),
        name="prep",
    )(proj, proj, proj, gaq, gak, gcq, gck, gdcq, gdckv, gdq, gdk, wuq, wukv, cosc, sinc, cosd, sind)


def _softmax_step(s, v, carry):
    m, l, acc = carry
    m_new = jnp.maximum(m, jnp.max(s, axis=1, keepdims=True))
    alpha = jnp.exp(m - m_new)
    p = jnp.exp(s - m_new)
    l = alpha * l + jnp.sum(p, axis=1, keepdims=True)
    acc = alpha * acc + _dot(p.astype(BF16), v)
    return m_new, l, acc


def _softmax_init(tq, dv):
    return (jnp.full((tq, 1), NEG, F32), jnp.zeros((tq, 1), F32), jnp.zeros((tq, dv), F32))


def _causal_mask(tq):
    row = lax.broadcasted_iota(jnp.int32, (tq, tq), 0)
    col = lax.broadcasted_iota(jnp.int32, (tq, tq), 1)
    return row, col


def _softmax_attn_kernel(*refs, tq, dk, has_bias):
    if has_bias:
        q_ref, k_ref, v_ref, nc_ref, o_ref = refs
    else:
        q_ref, k_ref, v_ref, o_ref = refs
    qi = pl.program_id(1)
    row, col = _causal_mask(tq)
    causal = col <= row
    dv = HEAD_DIM
    for h in range(GROUP_HEADS):
        q = q_ref[:, h * dk:(h + 1) * dk]

        def tile(j, carry, diag, h=h, q=q):
            off = pl.multiple_of(j * tq, tq)
            k = k_ref[pl.ds(off, tq), h * dk:(h + 1) * dk]
            v = v_ref[pl.ds(off, tq), h * dv:(h + 1) * dv]
            s = _dot_nt(q, k)
            if has_bias:
                s = s + nc_ref[0, j, h:h + 1, :]
            if diag:
                s = jnp.where(causal, s, NEG)
            return _softmax_step(s, v, carry)

        carry = tile(qi, _softmax_init(tq, dv), True)
        m, l, acc = lax.fori_loop(0, qi, lambda j, c: tile(j, c, False), carry)
        o_ref[:, h * dv:(h + 1) * dv] = (acc / l).astype(BF16)


def _softmax_attn(q_arr, q_col, k_arr, k_col, v_arr, v_col, negc, batch, seq, dk):
    tq = ATT_TILE
    nq = seq // tq
    n = batch * seq
    has_bias = negc is not None
    kern = functools.partial(_softmax_attn_kernel, tq=tq, dk=dk, has_bias=has_bias)
    in_specs = [
        pl.BlockSpec((tq, GROUP_HEADS * dk), lambda b, i: (b * nq + i, q_col)),
        pl.BlockSpec((seq, GROUP_HEADS * dk), lambda b, i: (b, k_col)),
        pl.BlockSpec((seq, GROUP_WIDTH), lambda b, i: (b, v_col)),
    ]
    args = [q_arr, k_arr, v_arr]
    if has_bias:
        in_specs.append(pl.BlockSpec((1, nq, 8, tq), lambda b, i: (b, 0, 0, 0)))
        args.append(negc)
    return pl.pallas_call(
        kern,
        grid=(batch, nq),
        in_specs=in_specs,
        out_specs=pl.BlockSpec((tq, GROUP_WIDTH), lambda b, i: (b * nq + i, 0)),
        out_shape=jax.ShapeDtypeStruct((n, GROUP_WIDTH), BF16),
        compiler_params=pltpu.CompilerParams(
            dimension_semantics=("parallel", "parallel"), vmem_limit_bytes=VMEM_LIMIT),
        name="fox_attn" if has_bias else "mla_attn",
    )(*args)


def _stickbreak_kernel(q_ref, k_ref, v_ref, o_ref, *, tq):
    qi = pl.program_id(1)
    row, col = _causal_mask(tq)
    strict = col < row
    later = jnp.where(row > col, 1.0, 0.0).astype(BF16)
    hd = HEAD_DIM
    for h in range(GROUP_HEADS):
        q = (q_ref[:, h * hd:(h + 1) * hd].astype(F32) * SCALE).astype(BF16)

        def tile(j, carry, diag, h=h, q=q):
            rest, acc = carry
            off = pl.multiple_of(j * tq, tq)
            k = k_ref[pl.ds(off, tq), h * hd:(h + 1) * hd]
            v = v_ref[pl.ds(off, tq), h * hd:(h + 1) * hd]
            z = _dot_nt(q, k)
            log_beta, log_rest = _log_sigmoid_pair(z)
            if diag:
                log_rest = jnp.where(strict, log_rest, 0.0)
            excl = _dot(log_rest.astype(BF16), later)
            w = jnp.exp(log_beta + excl + rest)
            if diag:
                w = jnp.where(strict, w, 0.0)
            acc = acc + _dot(w.astype(BF16), v)
            rest = rest + excl[:, 0:1] + log_rest[:, 0:1]
            return rest, acc

        carry = tile(qi, (jnp.zeros((tq, 1), F32), jnp.zeros((tq, hd), F32)), True)
        _, acc = lax.fori_loop(0, qi, lambda it, c: tile(qi - 1 - it, c, False), carry)
        o_ref[:, h * hd:(h + 1) * hd] = acc.astype(BF16)


def _stickbreak_attn(proj, batch, seq):
    tq = ATT_TILE
    nq = seq // tq
    n = batch * seq
    kern = functools.partial(_stickbreak_kernel, tq=tq)
    return pl.pallas_call(
        kern,
        grid=(batch, nq),
        in_specs=[
            pl.BlockSpec((tq, GROUP_WIDTH), lambda b, i: (b * nq + i, T_BQ)),
            pl.BlockSpec((seq, GROUP_WIDTH), lambda b, i: (b, T_BK)),
            pl.BlockSpec((seq, GROUP_WIDTH), lambda b, i: (b, T_BV)),
        ],
        out_specs=pl.BlockSpec((tq, GROUP_WIDTH), lambda b, i: (b * nq + i, 0)),
        out_shape=jax.ShapeDtypeStruct((n, GROUP_WIDTH), BF16),
        compiler_params=pltpu.CompilerParams(
            dimension_semantics=("parallel", "parallel"), vmem_limit_bytes=VMEM_LIMIT),
        name="stickbreak_attn",
    )(proj, proj, proj)


def _moba_kernel(q_ref, k_ref, v_ref, sel_ref, o_ref, *, tq):
    qi = pl.program_id(1)
    row, col = _causal_mask(tq)
    causal = col <= row
    hd = HEAD_DIM
    lane = lax.broadcasted_iota(jnp.int32, (tq, hd), 1)
    for h in range(GROUP_HEADS):
        q = q_ref[:, h * hd:(h + 1) * hd]
        sel = sel_ref[:, h * hd:(h + 1) * hd].astype(F32)

        def tile(j, carry, diag, h=h, q=q, sel=sel):
            off = pl.multiple_of(j * tq, tq)
            k = k_ref[pl.ds(off, tq), h * hd:(h + 1) * hd]
            v = v_ref[pl.ds(off, tq), h * hd:(h + 1) * hd]
            s = _dot_nt(q, k)
            if diag:
                s = jnp.where(causal, s, NEG)
            else:
                picked = jnp.max(jnp.where(lane == j, sel, 0.0), axis=1, keepdims=True)
                s = jnp.where(picked > 0.5, s, NEG)
            return _softmax_step(s, v, carry)

        carry = tile(qi, _softmax_init(tq, hd), True)
        m, l, acc = lax.fori_loop(0, qi, lambda j, c: tile(j, c, False), carry)
        o_ref[:, h * hd:(h + 1) * hd] = (acc / l).astype(BF16)


def _moba_attn(qkc, proj, sel, batch, seq):
    tq = MOBA_BLOCK
    nq = seq // tq
    n = batch * seq
    kern = functools.partial(_moba_kernel, tq=tq)
    return pl.pallas_call(
        kern,
        grid=(batch, nq),
        in_specs=[
            pl.BlockSpec((tq, GROUP_WIDTH), lambda b, i: (b * nq + i, 0)),
            pl.BlockSpec((seq, GROUP_WIDTH), lambda b, i: (b, 1)),
            pl.BlockSpec((seq, GROUP_WIDTH), lambda b, i: (b, T_CV)),
            pl.BlockSpec((tq, GROUP_WIDTH), lambda b, i: (b * nq + i, 0)),
        ],
        out_specs=pl.BlockSpec((tq, GROUP_WIDTH), lambda b, i: (b * nq + i, 0)),
        out_shape=jax.ShapeDtypeStruct((n, GROUP_WIDTH), BF16),
        compiler_params=pltpu.CompilerParams(
            dimension_semantics=("parallel", "parallel"), vmem_limit_bytes=VMEM_LIMIT),
        name="moba_attn",
    )(qkc, qkc, proj, sel)


def _outproj_kernel(oa_ref, ob_ref, oc_ref, od_ref, g_ref, w_ref, x_ref, out_ref):
    acc = x_ref[...]
    for gi, o_ref in enumerate((oa_ref, ob_ref, oc_ref, od_ref)):
        y = _rms(o_ref[...].astype(F32), g_ref[gi:gi + 1, :]).astype(BF16)
        acc = acc + _dot(y, w_ref[gi * GROUP_WIDTH:(gi + 1) * GROUP_WIDTH, :])
    out_ref[...] = acc


def _outproj(oa, ob, oc, od, g, w, x2):
    n = x2.shape[0]
    tm = 512
    row = lambda i: (i, 0)
    const = lambda i: (0, 0)
    return pl.pallas_call(
        _outproj_kernel,
        grid=(n // tm,),
        in_specs=[pl.BlockSpec((tm, GROUP_WIDTH), row)] * 4 + [
            pl.BlockSpec((N_GROUPS, GROUP_WIDTH), const),
            pl.BlockSpec((N_GROUPS * GROUP_WIDTH, D_MODEL), const),
            pl.BlockSpec((tm, D_MODEL), row),
        ],
        out_specs=pl.BlockSpec((tm, D_MODEL), row),
        out_shape=jax.ShapeDtypeStruct((n, D_MODEL), F32),
        compiler_params=pltpu.CompilerParams(
            dimension_semantics=("parallel",), vmem_limit_bytes=VMEM_LIMIT),
        name="outproj",
    )(oa, ob, oc, od, g, w, x2)


def _ffn_up_kernel(x_ref, xp_ref, g_ref, wg_ref, wv_ref, cwg_ref, cwv_ref, cbg_ref, cbv_ref,
                   out_ref, h_scr, *, tiles_per_batch, tm):
    i = pl.program_id(0)
    j = pl.program_id(1)

    @pl.when(j == 0)
    def _():
        h_scr[HALO:, :] = _rms(x_ref[...], g_ref[...]).astype(BF16)
        keep = jnp.where(i % tiles_per_batch == 0, 0.0, 1.0)
        h_scr[0:HALO, :] = (_rms(xp_ref[...], g_ref[...]) * keep).astype(BF16)

    hs = h_scr[...]

    def conv(w_ref, cw_ref, cb_ref):
        u = _dot(hs, w_ref[...])
        y = cb_ref[...] + cw_ref[2:3, :] * u[HALO:]
        for back in range(1, CONV_WIDTH):
            y = y + cw_ref[CONV_WIDTH - 1 - back:CONV_WIDTH - back, :] * pltpu.roll(u, back, axis=0)[HALO:]
        return y

    gate = conv(wg_ref, cwg_ref, cbg_ref)
    val = conv(wv_ref, cwv_ref, cbv_ref)
    out_ref[...] = (gate / (1.0 + jnp.exp(-gate)) * val).astype(BF16)


def _ffn_up(x2, g, w_up, conv_w, conv_b, seq):
    n = x2.shape[0]
    tm = 512
    tn = 512
    nj = D_FF // tn
    tpb = seq // tm
    kern = functools.partial(_ffn_up_kernel, tiles_per_batch=tpb, tm=tm)
    halo_blocks = tm // HALO
    return pl.pallas_call(
        kern,
        grid=(n // tm, nj),
        in_specs=[
            pl.BlockSpec((tm, D_MODEL), lambda i, j: (i, 0)),
            pl.BlockSpec((HALO, D_MODEL), lambda i, j: (jnp.maximum(i * halo_blocks - 1, 0), 0)),
            pl.BlockSpec((1, D_MODEL), lambda i, j: (0, 0)),
            pl.BlockSpec((D_MODEL, tn), lambda i, j: (0, j)),
            pl.BlockSpec((D_MODEL, tn), lambda i, j: (0, j + nj)),
            pl.BlockSpec((CONV_WIDTH, tn), lambda i, j: (0, j)),
            pl.BlockSpec((CONV_WIDTH, tn), lambda i, j: (0, j + nj)),
            pl.BlockSpec((1, tn), lambda i, j: (0, j)),
            pl.BlockSpec((1, tn), lambda i, j: (0, j + nj)),
        ],
        out_specs=pl.BlockSpec((tm, tn), lambda i, j: (i, j)),
        out_shape=jax.ShapeDtypeStruct((n, D_FF), BF16),
        scratch_shapes=[pltpu.VMEM((HALO + tm, D_MODEL), BF16)],
        compiler_params=pltpu.CompilerParams(
            dimension_semantics=("parallel", "arbitrary"), vmem_limit_bytes=VMEM_LIMIT),
        name="ffn_up",
    )(x2, x2, g, w_up, w_up, conv_w, conv_w, conv_b, conv_b)


def _ffn_down_kernel(a_ref, w_ref, x_ref, out_ref):
    out_ref[...] = x_ref[...] + _dot(a_ref[...], w_ref[...])


def _ffn_down(act, w_down, x2):
    n = x2.shape[0]
    tm = 1024
    tn = 512
    return pl.pallas_call(
        _ffn_down_kernel,
        grid=(n // tm, D_MODEL // tn),
        in_specs=[
            pl.BlockSpec((tm, D_FF), lambda i, j: (i, 0)),
            pl.BlockSpec((D_FF, tn), lambda i, j: (0, j)),
            pl.BlockSpec((tm, tn), lambda i, j: (i, j)),
        ],
        out_specs=pl.BlockSpec((tm, tn), lambda i, j: (i, j)),
        out_shape=jax.ShapeDtypeStruct((n, D_MODEL), F32),
        compiler_params=pltpu.CompilerParams(
            dimension_semantics=("parallel", "parallel"), vmem_limit_bytes=VMEM_LIMIT),
        name="ffn_down",
    )(act, w_down, x2)


def _rope_tables(seq):
    def angles(dim):
        inv_freq = ROPE_THETA ** (-jnp.arange(0, dim, 2, dtype=F32) / dim)
        ang = jnp.arange(seq, dtype=F32)[:, None] * inv_freq[None, :]
        return jnp.cos(ang), jnp.sin(ang)

    c, s = angles(HEAD_DIM)
    cosc = jnp.concatenate([c, c], axis=1)
    sinc = jnp.concatenate([-s, s], axis=1)
    c, s = angles(MLA_ROPE_DIM)
    z = jnp.zeros_like(c)
    cosd = jnp.concatenate([c, z, c, z], axis=1)
    sind = jnp.concatenate([-s, z, s, z], axis=1)
    return cosc, sinc, cosd, sind


def _spread_rope(t):
    half = MLA_ROPE_DIM // 2
    z = jnp.zeros(t.shape[:-1] + (half,), t.dtype)
    return jnp.concatenate([t[..., :half], z, t[..., half:], z], axis=-1)


def _layout_w_in(w):
    gw = GROUP_WIDTH
    o = 0
    pieces = {}
    for name, width in (("aq", gw), ("ak", gw), ("av", gw), ("af", GROUP_HEADS), ("bq", gw), ("bk", gw),
                        ("bv", gw), ("cq", gw), ("ck", gw), ("cv", gw), ("dcq", MLA_Q_RANK),
                        ("dckv", MLA_KV_RANK), ("dkr", MLA_ROPE_DIM)):
        pieces[name] = w[:, o:o + width]
        o += width
    pad = jnp.zeros((w.shape[0], gw - MLA_KV_RANK - HEAD_DIM), w.dtype)
    tiles = [pieces["aq"], pieces["ak"], pieces["cq"], pieces["ck"], pieces["dcq"],
             pieces["dckv"], _spread_rope(pieces["dkr"]), pad,
             pieces["av"], pieces["bq"], pieces["bk"], pieces["bv"], pieces["cv"]]
    w_main = jnp.concatenate(tiles, axis=1).astype(BF16)
    wf = jnp.zeros((8, w.shape[0]), F32).at[:GROUP_HEADS].set(pieces["af"].T).astype(BF16)
    return w_main, wf


def _layout_w_uq(w):
    w = w.reshape(MLA_Q_RANK, GROUP_HEADS, MLA_QK_DIM)
    w = jnp.concatenate([w[..., :MLA_NOPE_DIM], _spread_rope(w[..., MLA_NOPE_DIM:])], axis=-1)
    return w.reshape(MLA_Q_RANK, GROUP_HEADS * MLA_HEAD_PAD).astype(BF16)


def _layout_w_ukv(w):
    w = w.reshape(MLA_KV_RANK, GROUP_HEADS, 2, HEAD_DIM)
    return jnp.swapaxes(w, 1, 2).reshape(MLA_KV_RANK, 2 * GROUP_WIDTH).astype(BF16)


def _layout_qk_gain(g):
    return jnp.concatenate([g[:MLA_NOPE_DIM], _spread_rope(g[MLA_NOPE_DIM:])])[None, :]


def kernel(x, attn_norm, w_in, b_forget, fox_q_norm, fox_k_norm, moba_q_norm, moba_k_norm, mla_cq_norm, mla_ckv_norm, w_uq, w_ukv, mla_q_norm, mla_k_norm, mix_out_norm, w_out, ffn_norm, w_up, conv_w, conv_b, w_down):
    batch, seq, d_model = x.shape
    depth = w_in.shape[0]
    assert d_model == D_MODEL and seq % 512 == 0
    n = batch * seq
    x2 = x.reshape(n, d_model)
    cosc, sinc, cosd, sind = _rope_tables(seq)
    row = lambda v: v[None, :]

    for l in range(depth):
        w_main, wf = _layout_w_in(w_in[l])
        bf = jnp.zeros((8, 128), F32).at[:GROUP_HEADS, :].set(b_forget[l][:, None])
        proj, negc = _inproj(x2, row(attn_norm[l]), w_main, wf, bf, batch, seq)

        qka, qkc, sel, qd, kd, vd = _prep(
            proj, row(fox_q_norm[l]), row(fox_k_norm[l]), row(moba_q_norm[l]), row(moba_k_norm[l]),
            row(mla_cq_norm[l]), row(mla_ckv_norm[l]),
            _layout_qk_gain(mla_q_norm[l]), _layout_qk_gain(mla_k_norm[l]),
            _layout_w_uq(w_uq[l]), _layout_w_ukv(w_ukv[l]), cosc, sinc, cosd, sind, seq)

        o_a = _softmax_attn(qka, 0, qka, 1, proj, T_AV, negc, batch, seq, HEAD_DIM)
        o_b = _stickbreak_attn(proj, batch, seq)
        o_c = _moba_attn(qkc, proj, sel, batch, seq)
        o_d = _softmax_attn(qd, 0, kd, 0, vd, 0, None, batch, seq, MLA_HEAD_PAD)

        x2 = _outproj(o_a, o_b, o_c, o_d, mix_out_norm[l].reshape(N_GROUPS, GROUP_WIDTH),
                      w_out[l].astype(BF16), x2)

        act = _ffn_up(x2, row(ffn_norm[l]), w_up[l].astype(BF16), conv_w[l], row(conv_b[l]), seq)
        x2 = _ffn_down(act, w_down[l].astype(BF16), x2)

    return x2.reshape(batch, seq, d_model)
```

```python
import functools
import math

import jax
import jax.numpy as jnp
from jax import lax
from jax.experimental import pallas as pl
from jax.experimental.pallas import tpu as pltpu

F32 = jnp.float32
BF16 = jnp.bfloat16

D_MODEL = 2048
HEAD_DIM = 128
GROUP_HEADS = 4
GROUP_WIDTH = GROUP_HEADS * HEAD_DIM
N_GROUPS = 4
MOBA_BLOCK = 256
MOBA_TOP_K = 3
MOBA_MAX_BLOCKS = 16
MLA_Q_RANK = 512
MLA_KV_RANK = 256
MLA_NOPE_DIM = 128
MLA_ROPE_DIM = 64
MLA_QK_DIM = MLA_NOPE_DIM + MLA_ROPE_DIM
WIDE_HEAD = 256
D_FF = 5632
CONV_WIDTH = 3
ROPE_THETA = 10000.0
EPS = 1e-6

T_AQ, T_AK, T_CQ, T_CK, T_DCQ, T_DKV, T_BQ, T_BK = range(8)
PROJ_TILES = 8
V_GROUPS = 3

TILE = 512
NEG = -1e30
VMEM_LIMIT = 56 * 1024 * 1024
LOG2E = math.log2(math.e)
SCALE = HEAD_DIM ** -0.5
SCALE_MLA = MLA_QK_DIM ** -0.5
HALO = 16


def _dot(a, b):
    return jnp.dot(a, b, preferred_element_type=F32)


def _dot_nt(a, b):
    return lax.dot_general(a, b, (((1,), (1,)), ((), ())), preferred_element_type=F32)


def _split3(a):
    hi = a.astype(BF16)
    r1 = a - hi.astype(F32)
    mid = r1.astype(BF16)
    lo = (r1 - mid.astype(F32)).astype(BF16)
    return hi, mid, lo


def _log_sigmoid_pair(z):
    sp = jnp.log(1.0 + jnp.exp(-jnp.abs(z)))
    return jnp.minimum(z, 0.0) - sp, -jnp.maximum(z, 0.0) - sp


def _rms(v, g, denom=None):
    if denom is None:
        ms = jnp.mean(v * v, axis=-1, keepdims=True)
    else:
        ms = jnp.sum(v * v, axis=-1, keepdims=True) * (1.0 / denom)
    return v * lax.rsqrt(ms + EPS) * g


def _inproj_kernel(x_ref, g_ref, w_ref, wvt_ref, wf_ref, bf_ref, proj_ref, vt_ref, negc_ref,
                   h_scr, carry_scr, *, tiles_per_batch, tm):
    i = pl.program_id(0)
    j = pl.program_id(1)

    @pl.when(j == 0)
    def _():
        h = _rms(x_ref[...], g_ref[...]).astype(BF16)
        h_scr[...] = h
        z = _dot_nt(wf_ref[...], h) + bf_ref[:, 0:1]
        logf, _ = _log_sigmoid_pair(z)
        r = lax.broadcasted_iota(jnp.int32, (tm, tm), 0)
        c = lax.broadcasted_iota(jnp.int32, (tm, tm), 1)
        tri = jnp.where(r <= c, 1.0, 0.0).astype(BF16)
        hi, mid, lo = _split3(logf)
        cum = _dot(hi, tri) + _dot(mid, tri) + _dot(lo, tri)

        @pl.when(i % tiles_per_batch == 0)
        def _():
            carry_scr[...] = jnp.zeros_like(carry_scr)

        cum = cum + carry_scr[:, 0:1]
        carry_scr[...] = jnp.broadcast_to(cum[:, tm - 1:tm], carry_scr.shape)
        negc_ref[...] = cum * (-LOG2E)

    @pl.when(j < PROJ_TILES)
    def _():
        proj_ref[...] = _dot(h_scr[...], w_ref[...]).astype(BF16)

    @pl.when(j >= PROJ_TILES)
    def _():
        vt_ref[...] = _dot_nt(wvt_ref[...], h_scr[...]).astype(BF16)


def _inproj(x2, g, w, wvt, wf, bf, batch, seq):
    n = x2.shape[0]
    tm = TILE
    tpb = seq // tm
    kern = functools.partial(_inproj_kernel, tiles_per_batch=tpb, tm=tm)
    last = PROJ_TILES - 1
    return pl.pallas_call(
        kern,
        grid=(n // tm, PROJ_TILES + V_GROUPS),
        in_specs=[
            pl.BlockSpec((tm, D_MODEL), lambda i, j: (i, 0)),
            pl.BlockSpec((1, D_MODEL), lambda i, j: (0, 0)),
            pl.BlockSpec((D_MODEL, GROUP_WIDTH), lambda i, j: (0, jnp.minimum(j, last))),
            pl.BlockSpec((None, GROUP_WIDTH, D_MODEL), lambda i, j: (jnp.maximum(j - PROJ_TILES, 0), 0, 0)),
            pl.BlockSpec((8, D_MODEL), lambda i, j: (0, 0)),
            pl.BlockSpec((8, 128), lambda i, j: (0, 0)),
        ],
        out_specs=[
            pl.BlockSpec((tm, GROUP_WIDTH), lambda i, j: (i, jnp.minimum(j, last))),
            pl.BlockSpec((None, None, None, GROUP_WIDTH, tm),
                         lambda i, j: (jnp.maximum(j - PROJ_TILES, 0), i // tpb, i % tpb, 0, 0)),
            pl.BlockSpec((None, 8, tm), lambda i, j: (i // tpb, 0, i % tpb)),
        ],
        out_shape=[
            jax.ShapeDtypeStruct((n, PROJ_TILES * GROUP_WIDTH), BF16),
            jax.ShapeDtypeStruct((V_GROUPS, batch, tpb, GROUP_WIDTH, tm), BF16),
            jax.ShapeDtypeStruct((batch, 8, seq), F32),
        ],
        scratch_shapes=[pltpu.VMEM((tm, D_MODEL), BF16), pltpu.VMEM((8, 128), F32)],
        compiler_params=pltpu.CompilerParams(
            dimension_semantics=("arbitrary", "arbitrary"), vmem_limit_bytes=VMEM_LIMIT),
        name="inproj",
    )(x2, g, w, wvt, wf, bf)


def _prep_kernel(a_ref, c_ref, d_ref, nc_ref, gaq, gak, gcq, gck, gdcq, gdckv, gdq, gdk, wuq, wuk, wuvt,
                 cosc, sinc, cosd, sind,
                 qa_ref, ka_ref, qkc_ref, sel_ref, qd_ref, kd_ref, vtd_ref, kmt_scr,
                 *, tiles_per_batch, tm):
    i = pl.program_id(0)
    t = i % tiles_per_batch
    hd = HEAD_DIM
    wh = WIDE_HEAD
    blocks_per_tile = tm // MOBA_BLOCK
    lane = lax.broadcasted_iota(jnp.int32, (tm, hd), 1)

    ones_aug = jnp.where(lane < 3, 1.0, 0.0).astype(BF16)
    nc = nc_ref[...]
    for h in range(GROUP_HEADS):
        q = a_ref[:, h * hd:(h + 1) * hd].astype(F32)
        qa_ref[:, h * wh:h * wh + hd] = (_rms(q, gaq[...]) * (SCALE * LOG2E)).astype(BF16)
        qa_ref[:, h * wh + hd:(h + 1) * wh] = ones_aug
        k = a_ref[:, GROUP_WIDTH + h * hd:GROUP_WIDTH + (h + 1) * hd].astype(F32)
        ka_ref[:, h * wh:h * wh + hd] = _rms(k, gak[...]).astype(BF16)
        hi, mid, lo = _split3(nc[:, h:h + 1])
        aug = jnp.where(lane == 0, hi.astype(F32),
                        jnp.where(lane == 1, mid.astype(F32), jnp.where(lane == 2, lo.astype(F32), 0.0)))
        ka_ref[:, h * wh + hd:(h + 1) * wh] = aug.astype(BF16)

    @pl.when(i == 0)
    def _():
        kmt_scr[...] = jnp.zeros_like(kmt_scr)

    cc = cosc[...]
    sc = sinc[...]
    sub8 = lax.broadcasted_iota(jnp.int32, (8, hd), 0)
    nb = MOBA_MAX_BLOCKS

    def rope_c(v):
        return v * cc + pltpu.roll(v, hd // 2, axis=1) * sc

    for h in range(GROUP_HEADS):
        k = rope_c(_rms(c_ref[:, GROUP_WIDTH + h * hd:GROUP_WIDTH + (h + 1) * hd].astype(F32), gck[...]))
        qkc_ref[:, GROUP_WIDTH + h * hd:GROUP_WIDTH + (h + 1) * hd] = k.astype(BF16)
        for blk in range(blocks_per_tile):
            km = jnp.mean(k[blk * MOBA_BLOCK:(blk + 1) * MOBA_BLOCK], axis=0, keepdims=True)
            nblk = t * blocks_per_tile + blk
            base = pl.multiple_of(h * nb + (nblk // 8) * 8, 8)
            old = kmt_scr[pl.ds(base, 8), h * hd:(h + 1) * hd]
            kmt_scr[pl.ds(base, 8), h * hd:(h + 1) * hd] = jnp.where(sub8 == nblk % 8, km, old)

    qs = []
    for h in range(GROUP_HEADS):
        q = rope_c(_rms(c_ref[:, h * hd:(h + 1) * hd].astype(F32), gcq[...])) * (SCALE * LOG2E)
        qkc_ref[:, h * hd:(h + 1) * hd] = q.astype(BF16)
        qs.append(q)
    q_all = jnp.concatenate(qs, axis=1)
    qh, qm, _ = _split3(q_all)
    kh, km_, _ = _split3(kmt_scr[...])
    gate_t = _dot_nt(kh, qh) + _dot_nt(km_, qh) + _dot_nt(kh, qm)

    blk_row = lax.broadcasted_iota(jnp.int32, (nb, tm), 0)
    q_col = lax.broadcasted_iota(jnp.int32, (nb, tm), 1)
    own = t * blocks_per_tile + q_col // MOBA_BLOCK
    blk_f = blk_row.astype(F32)
    ninf = jnp.float32(-jnp.inf)
    for h in range(GROUP_HEADS):
        g = jnp.where(blk_row < own, gate_t[h * nb:(h + 1) * nb, :], ninf)
        sel = jnp.zeros((nb, tm), F32)
        for _ in range(MOBA_TOP_K):
            mx = jnp.max(g, axis=0, keepdims=True)
            cand = jnp.where(g == mx, blk_f, 1e9)
            cand = jnp.where(mx > ninf, cand, 1e9)
            idx = jnp.min(cand, axis=0, keepdims=True)
            pick = blk_f == idx
            sel = jnp.where(pick, 1.0, sel)
            g = jnp.where(pick, ninf, g)
        sel_ref[h * nb:(h + 1) * nb, :] = sel

    cd = cosd[...]
    sd = sind[...]

    def rope_d(v):
        return v * cd + pltpu.roll(v, hd // 2, axis=1) * sd

    cq = _rms(d_ref[:, 0:MLA_Q_RANK].astype(F32), gdcq[...]).astype(BF16)
    qf = _dot(cq, wuq[...])
    for h in range(GROUP_HEADS):
        qn = _rms(qf[:, h * wh:(h + 1) * wh], gdq[...], denom=MLA_QK_DIM)
        qd_ref[:, h * wh:h * wh + hd] = (qn[:, :hd] * (SCALE_MLA * LOG2E)).astype(BF16)
        qd_ref[:, h * wh + hd:(h + 1) * wh] = (rope_d(qn[:, hd:]) * (SCALE_MLA * LOG2E)).astype(BF16)

    ckv = _rms(d_ref[:, MLA_Q_RANK:MLA_Q_RANK + MLA_KV_RANK].astype(F32), gdckv[...]).astype(BF16)
    knf = _dot(ckv, wuk[...])
    vtd_ref[...] = _dot_nt(wuvt[...], ckv).astype(BF16)
    kr = d_ref[:, MLA_Q_RANK + MLA_KV_RANK:MLA_Q_RANK + MLA_KV_RANK + hd].astype(F32)
    ss_r = jnp.sum(kr * kr, axis=-1, keepdims=True)
    gk = gdk[...]
    for h in range(GROUP_HEADS):
        kn = knf[:, h * hd:(h + 1) * hd]
        ms = (jnp.sum(kn * kn, axis=-1, keepdims=True) + ss_r) * (1.0 / MLA_QK_DIM)
        r = lax.rsqrt(ms + EPS)
        kd_ref[:, h * wh:h * wh + hd] = (kn * r * gk[:, :hd]).astype(BF16)
        kd_ref[:, h * wh + hd:(h + 1) * wh] = rope_d(kr * r * gk[:, hd:]).astype(BF16)


def _prep(proj, negc_col, gaq, gak, gcq, gck, gdcq, gdckv, gdq, gdk, wuq, wuk, wuvt,
          cosc, sinc, cosd, sind, batch, seq):
    n = proj.shape[0]
    tm = TILE
    tpb = seq // tm
    assert seq // MOBA_BLOCK <= MOBA_MAX_BLOCKS
    kern = functools.partial(_prep_kernel, tiles_per_batch=tpb, tm=tm)
    row = lambda i: (i, 0)
    const = lambda i: (0, 0)
    tab = lambda i: (i % tpb, 0)
    w2 = 2 * GROUP_WIDTH
    nsel = GROUP_HEADS * MOBA_MAX_BLOCKS
    return pl.pallas_call(
        kern,
        grid=(n // tm,),
        in_specs=[
            pl.BlockSpec((tm, w2), lambda i: (i, 0)),
            pl.BlockSpec((tm, w2), lambda i: (i, 1)),
            pl.BlockSpec((tm, w2), lambda i: (i, 2)),
            pl.BlockSpec((tm, 8), row),
            pl.BlockSpec((1, HEAD_DIM), const), pl.BlockSpec((1, HEAD_DIM), const),
            pl.BlockSpec((1, HEAD_DIM), const), pl.BlockSpec((1, HEAD_DIM), const),
            pl.BlockSpec((1, MLA_Q_RANK), const), pl.BlockSpec((1, MLA_KV_RANK), const),
            pl.BlockSpec((1, WIDE_HEAD), const), pl.BlockSpec((1, WIDE_HEAD), const),
            pl.BlockSpec((MLA_Q_RANK, GROUP_HEADS * WIDE_HEAD), const),
            pl.BlockSpec((MLA_KV_RANK, GROUP_WIDTH), const),
            pl.BlockSpec((GROUP_WIDTH, MLA_KV_RANK), const),
            pl.BlockSpec((tm, HEAD_DIM), tab), pl.BlockSpec((tm, HEAD_DIM), tab),
            pl.BlockSpec((tm, HEAD_DIM), tab), pl.BlockSpec((tm, HEAD_DIM), tab),
        ],
        out_specs=[
            pl.BlockSpec((tm, w2), row), pl.BlockSpec((tm, w2), row), pl.BlockSpec((tm, w2), row),
            pl.BlockSpec((None, nsel, tm), lambda i: (i, 0, 0)),
            pl.BlockSpec((tm, w2), row), pl.BlockSpec((tm, w2), row),
            pl.BlockSpec((None, None, GROUP_WIDTH, tm), lambda i: (i // tpb, i % tpb, 0, 0)),
        ],
        out_shape=[
            jax.ShapeDtypeStruct((n, w2), BF16), jax.ShapeDtypeStruct((n, w2), BF16),
            jax.ShapeDtypeStruct((n, w2), BF16),
            jax.ShapeDtypeStruct((n // tm, nsel, tm), F32),
            jax.ShapeDtypeStruct((n, w2), BF16), jax.ShapeDtypeStruct((n, w2), BF16),
            jax.ShapeDtypeStruct((batch, tpb, GROUP_WIDTH, tm), BF16),
        ],
        scratch_shapes=[pltpu.VMEM((nsel, GROUP_WIDTH), F32)],
        compiler_params=pltpu.CompilerParams(
            dimension_semantics=("arbitrary",), vmem_limit_bytes=VMEM_LIMIT),
        name="prep",
    )(proj, proj, proj, negc_col, gaq, gak, gcq, gck, gdcq, gdckv, gdq, gdk, wuq, wuk, wuvt,
      cosc, sinc, cosd, sind)


def _key_query_iota(t):
    key = lax.broadcasted_iota(jnp.int32, (t, t), 0)
    qry = lax.broadcasted_iota(jnp.int32, (t, t), 1)
    return key, qry


def _softmax_step(s, vt, carry):
    m, l, acc = carry
    m_new = jnp.maximum(m, jnp.max(s, axis=0, keepdims=True))
    alpha = jnp.exp2(m - m_new)
    p = jnp.exp2(s - m_new)
    l = alpha * l + jnp.sum(p, axis=0, keepdims=True)
    acc = alpha * acc + _dot(vt, p.astype(BF16))
    return m_new, l, acc


def _softmax_init(t):
    return (jnp.full((1, t), NEG, F32), jnp.zeros((1, t), F32), jnp.zeros((HEAD_DIM, t), F32))


def _softmax_attn_kernel(*refs, t, dk, routed):
    if routed:
        q_ref, k_ref, vt_ref, sel_ref, o_ref = refs
    else:
        q_ref, k_ref, vt_ref, o_ref = refs
    qi = pl.program_id(1)
    key, qry = _key_query_iota(t)
    causal = key <= qry
    dv = HEAD_DIM
    half = MOBA_BLOCK
    nb = MOBA_MAX_BLOCKS
    if routed:
        blk_row = lax.broadcasted_iota(jnp.int32, (nb, t), 0)
        second_half = lax.broadcasted_iota(jnp.int32, (1, t), 1) >= half

        def routed_to(h, n):
            return jnp.max(jnp.where(blk_row == n, sel_ref[h * nb:(h + 1) * nb, :], 0.0), axis=0, keepdims=True)

    def step(j, carry, diag):
        off = pl.multiple_of(j * t, t)
        out = []
        for h in range(GROUP_HEADS):
            k = k_ref[pl.ds(off, t), h * dk:(h + 1) * dk]
            s = _dot_nt(k, q_ref[:, h * dk:(h + 1) * dk])
            if diag:
                s = jnp.where(causal, s, NEG)
                if routed:
                    keep = jnp.where(second_half, routed_to(h, 2 * j), 1.0) > 0.5
                    s = jnp.concatenate([jnp.where(keep, s[:half], NEG), s[half:]], axis=0)
            elif routed:
                s = jnp.concatenate([jnp.where(routed_to(h, 2 * j) > 0.5, s[:half], NEG),
                                     jnp.where(routed_to(h, 2 * j + 1) > 0.5, s[half:], NEG)], axis=0)
            out.append(_softmax_step(s, vt_ref[j, h * dv:(h + 1) * dv, :], carry[h]))
        return tuple(out)

    carry = step(qi, tuple(_softmax_init(t) for _ in range(GROUP_HEADS)), True)
    carry = lax.fori_loop(0, qi, lambda j, c: step(j, c, False), carry)
    for h in range(GROUP_HEADS):
        m, l, acc = carry[h]
        o_ref[:, h * dv:(h + 1) * dv] = (acc / l).T.astype(BF16)


def _softmax_attn(name, q_arr, q_col, k_arr, k_col, vt_arr, vt_group, sel, batch, seq, dk):
    t = TILE
    nq = seq // t
    n = batch * seq
    routed = sel is not None
    kern = functools.partial(_softmax_attn_kernel, t=t, dk=dk, routed=routed)
    if vt_group is None:
        vt_spec = pl.BlockSpec((None, nq, GROUP_WIDTH, t), lambda b, i: (b, 0, 0, 0))
    else:
        vt_spec = pl.BlockSpec((None, None, nq, GROUP_WIDTH, t), lambda b, i: (vt_group, b, 0, 0, 0))
    in_specs = [
        pl.BlockSpec((t, GROUP_HEADS * dk), lambda b, i: (b * nq + i, q_col)),
        pl.BlockSpec((seq, GROUP_HEADS * dk), lambda b, i: (b, k_col)),
        vt_spec,
    ]
    args = [q_arr, k_arr, vt_arr]
    if routed:
        in_specs.append(pl.BlockSpec((None, GROUP_HEADS * MOBA_MAX_BLOCKS, t), lambda b, i: (b * nq + i, 0, 0)))
        args.append(sel)
    return pl.pallas_call(
        kern,
        grid=(batch, nq),
        in_specs=in_specs,
        out_specs=pl.BlockSpec((t, GROUP_WIDTH), lambda b, i: (b * nq + i, 0)),
        out_shape=jax.ShapeDtypeStruct((n, GROUP_WIDTH), BF16),
        compiler_params=pltpu.CompilerParams(
            dimension_semantics=("parallel", "parallel"), vmem_limit_bytes=VMEM_LIMIT),
        name=name,
    )(*args)


def _stickbreak_kernel(q_ref, k_ref, vt_ref, o_ref, *, t):
    qi = pl.program_id(1)
    key, qry = _key_query_iota(t)
    strict = key < qry
    half = t // 2
    kk, kj = _key_query_iota(half)
    later = jnp.where(kj > kk, 1.0, 0.0).astype(BF16)
    hd = HEAD_DIM

    def step(j, carry, diag):
        off = pl.multiple_of(j * t, t)
        out = []
        for h in range(GROUP_HEADS):
            rest, acc = carry[h]
            q = (q_ref[:, h * hd:(h + 1) * hd].astype(F32) * SCALE).astype(BF16)
            k = k_ref[pl.ds(off, t), h * hd:(h + 1) * hd]
            z = _dot_nt(k, q)
            log_beta, log_rest = _log_sigmoid_pair(z)
            if diag:
                log_rest = jnp.where(strict, log_rest, 0.0)
            lr_top, lr_bot = log_rest[:half], log_rest[half:]
            ex_bot = _dot(later, lr_bot.astype(BF16))
            tot_bot = ex_bot[0:1] + lr_bot[0:1]
            ex_top = _dot(later, lr_top.astype(BF16))
            tot_top = ex_top[0:1] + lr_top[0:1]
            excl = jnp.concatenate([ex_top + tot_bot, ex_bot], axis=0)
            w = jnp.exp(log_beta + excl + rest)
            if diag:
                w = jnp.where(strict, w, 0.0)
            acc = acc + _dot(vt_ref[j, h * hd:(h + 1) * hd, :], w.astype(BF16))
            out.append((rest + tot_top + tot_bot, acc))
        return tuple(out)

    init = tuple((jnp.zeros((1, t), F32), jnp.zeros((hd, t), F32)) for _ in range(GROUP_HEADS))
    carry = step(qi, init, True)
    carry = lax.fori_loop(0, qi, lambda it, c: step(qi - 1 - it, c, False), carry)
    for h in range(GROUP_HEADS):
        o_ref[:, h * hd:(h + 1) * hd] = carry[h][1].T.astype(BF16)


def _stickbreak_attn(proj, vt3, batch, seq):
    t = TILE
    nq = seq // t
    n = batch * seq
    kern = functools.partial(_stickbreak_kernel, t=t)
    return pl.pallas_call(
        kern,
        grid=(batch, nq),
        in_specs=[
            pl.BlockSpec((t, GROUP_WIDTH), lambda b, i: (b * nq + i, T_BQ)),
            pl.BlockSpec((seq, GROUP_WIDTH), lambda b, i: (b, T_BK)),
            pl.BlockSpec((None, None, nq, GROUP_WIDTH, t), lambda b, i: (1, b, 0, 0, 0)),
        ],
        out_specs=pl.BlockSpec((t, GROUP_WIDTH), lambda b, i: (b * nq + i, 0)),
        out_shape=jax.ShapeDtypeStruct((n, GROUP_WIDTH), BF16),
        compiler_params=pltpu.CompilerParams(
            dimension_semantics=("parallel", "parallel"), vmem_limit_bytes=VMEM_LIMIT),
        name="stickbreak_attn",
    )(proj, proj, vt3)


def _outproj_kernel(oa_ref, ob_ref, oc_ref, od_ref, g_ref, w_ref, x_ref, out_ref):
    acc = x_ref[...]
    for gi, o_ref in enumerate((oa_ref, ob_ref, oc_ref, od_ref)):
        y = _rms(o_ref[...].astype(F32), g_ref[gi:gi + 1, :]).astype(BF16)
        acc = acc + _dot(y, w_ref[gi * GROUP_WIDTH:(gi + 1) * GROUP_WIDTH, :])
    out_ref[...] = acc


def _outproj(oa, ob, oc, od, g, w, x2):
    n = x2.shape[0]
    tm = 512
    row = lambda i: (i, 0)
    const = lambda i: (0, 0)
    return pl.pallas_call(
        _outproj_kernel,
        grid=(n // tm,),
        in_specs=[pl.BlockSpec((tm, GROUP_WIDTH), row)] * 4 + [
            pl.BlockSpec((N_GROUPS, GROUP_WIDTH), const),
            pl.BlockSpec((N_GROUPS * GROUP_WIDTH, D_MODEL), const),
            pl.BlockSpec((tm, D_MODEL), row),
        ],
        out_specs=pl.BlockSpec((tm, D_MODEL), row),
        out_shape=jax.ShapeDtypeStruct((n, D_MODEL), F32),
        compiler_params=pltpu.CompilerParams(
            dimension_semantics=("parallel",), vmem_limit_bytes=VMEM_LIMIT),
        name="outproj",
    )(oa, ob, oc, od, g, w, x2)


def _ffn_up_kernel(x_ref, xp_ref, g_ref, wg_ref, wv_ref, cwg_ref, cwv_ref, cbg_ref, cbv_ref,
                   out_ref, h_scr, *, tiles_per_batch, tm):
    i = pl.program_id(0)
    j = pl.program_id(1)

    @pl.when(j == 0)
    def _():
        h_scr[HALO:, :] = _rms(x_ref[...], g_ref[...]).astype(BF16)
        keep = jnp.where(i % tiles_per_batch == 0, 0.0, 1.0)
        h_scr[0:HALO, :] = (_rms(xp_ref[...], g_ref[...]) * keep).astype(BF16)

    hs = h_scr[...]

    def conv(w_ref, cw_ref, cb_ref):
        u = _dot(hs, w_ref[...])
        y = cb_ref[...] + cw_ref[2:3, :] * u[HALO:]
        for back in range(1, CONV_WIDTH):
            y = y + cw_ref[CONV_WIDTH - 1 - back:CONV_WIDTH - back, :] * pltpu.roll(u, back, axis=0)[HALO:]
        return y

    gate = conv(wg_ref, cwg_ref, cbg_ref)
    val = conv(wv_ref, cwv_ref, cbv_ref)
    out_ref[...] = (gate / (1.0 + jnp.exp(-gate)) * val).astype(BF16)


def _ffn_up(x2, g, w_up, conv_w, conv_b, seq):
    n = x2.shape[0]
    tm = 512
    tn = 512
    nj = D_FF // tn
    tpb = seq // tm
    kern = functools.partial(_ffn_up_kernel, tiles_per_batch=tpb, tm=tm)
    halo_blocks = tm // HALO
    return pl.pallas_call(
        kern,
        grid=(n // tm, nj),
        in_specs=[
            pl.BlockSpec((tm, D_MODEL), lambda i, j: (i, 0)),
            pl.BlockSpec((HALO, D_MODEL), lambda i, j: (jnp.maximum(i * halo_blocks - 1, 0), 0)),
            pl.BlockSpec((1, D_MODEL), lambda i, j: (0, 0)),
            pl.BlockSpec((D_MODEL, tn), lambda i, j: (0, j)),
            pl.BlockSpec((D_MODEL, tn), lambda i, j: (0, j + nj)),
            pl.BlockSpec((CONV_WIDTH, tn), lambda i, j: (0, j)),
            pl.BlockSpec((CONV_WIDTH, tn), lambda i, j: (0, j + nj)),
            pl.BlockSpec((1, tn), lambda i, j: (0, j)),
            pl.BlockSpec((1, tn), lambda i, j: (0, j + nj)),
        ],
        out_specs=pl.BlockSpec((tm, tn), lambda i, j: (i, j)),
        out_shape=jax.ShapeDtypeStruct((n, D_FF), BF16),
        scratch_shapes=[pltpu.VMEM((HALO + tm, D_MODEL), BF16)],
        compiler_params=pltpu.CompilerParams(
            dimension_semantics=("parallel", "arbitrary"), vmem_limit_bytes=VMEM_LIMIT),
        name="ffn_up",
    )(x2, x2, g, w_up, w_up, conv_w, conv_w, conv_b, conv_b)


def _ffn_down_kernel(a_ref, w_ref, x_ref, out_ref):
    out_ref[...] = x_ref[...] + _dot(a_ref[...], w_ref[...])


def _ffn_down(act, w_down, x2):
    n = x2.shape[0]
    tm = 1024
    tn = 512
    return pl.pallas_call(
        _ffn_down_kernel,
        grid=(n // tm, D_MODEL // tn),
        in_specs=[
            pl.BlockSpec((tm, D_FF), lambda i, j: (i, 0)),
            pl.BlockSpec((D_FF, tn), lambda i, j: (0, j)),
            pl.BlockSpec((tm, tn), lambda i, j: (i, j)),
        ],
        out_specs=pl.BlockSpec((tm, tn), lambda i, j: (i, j)),
        out_shape=jax.ShapeDtypeStruct((n, D_MODEL), F32),
        compiler_params=pltpu.CompilerParams(
            dimension_semantics=("parallel", "parallel"), vmem_limit_bytes=VMEM_LIMIT),
        name="ffn_down",
    )(act, w_down, x2)


def _rope_tables(seq):
    def angles(dim):
        inv_freq = ROPE_THETA ** (-jnp.arange(0, dim, 2, dtype=F32) / dim)
        ang = jnp.arange(seq, dtype=F32)[:, None] * inv_freq[None, :]
        return jnp.cos(ang), jnp.sin(ang)

    c, s = angles(HEAD_DIM)
    cosc = jnp.concatenate([c, c], axis=1)
    sinc = jnp.concatenate([-s, s], axis=1)
    c, s = angles(MLA_ROPE_DIM)
    z = jnp.zeros_like(c)
    cosd = jnp.concatenate([c, z, c, z], axis=1)
    sind = jnp.concatenate([-s, z, s, z], axis=1)
    return cosc, sinc, cosd, sind


def _spread_rope(t):
    half = MLA_ROPE_DIM // 2
    z = jnp.zeros(t.shape[:-1] + (half,), t.dtype)
    return jnp.concatenate([t[..., :half], z, t[..., half:], z], axis=-1)


def _layout_w_in(w):
    gw = GROUP_WIDTH
    o = 0
    pieces = {}
    for name, width in (("aq", gw), ("ak", gw), ("av", gw), ("af", GROUP_HEADS), ("bq", gw), ("bk", gw),
                        ("bv", gw), ("cq", gw), ("ck", gw), ("cv", gw), ("dcq", MLA_Q_RANK),
                        ("dckv", MLA_KV_RANK), ("dkr", MLA_ROPE_DIM)):
        pieces[name] = w[:, o:o + width]
        o += width
    pad = jnp.zeros((w.shape[0], gw - MLA_KV_RANK - HEAD_DIM), w.dtype)
    tiles = [pieces["aq"], pieces["ak"], pieces["cq"], pieces["ck"], pieces["dcq"],
             pieces["dckv"], _spread_rope(pieces["dkr"]), pad, pieces["bq"], pieces["bk"]]
    w_main = jnp.concatenate(tiles, axis=1).astype(BF16)
    wvt = jnp.stack([pieces["av"].T, pieces["bv"].T, pieces["cv"].T]).astype(BF16)
    wf = jnp.zeros((8, w.shape[0]), F32).at[:GROUP_HEADS].set(pieces["af"].T).astype(BF16)
    return w_main, wvt, wf


def _layout_w_uq(w):
    w = w.reshape(MLA_Q_RANK, GROUP_HEADS, MLA_QK_DIM)
    w = jnp.concatenate([w[..., :MLA_NOPE_DIM], _spread_rope(w[..., MLA_NOPE_DIM:])], axis=-1)
    return w.reshape(MLA_Q_RANK, GROUP_HEADS * WIDE_HEAD).astype(BF16)


def _layout_w_ukv(w):
    w = w.reshape(MLA_KV_RANK, GROUP_HEADS, 2, HEAD_DIM)
    wk = w[:, :, 0, :].reshape(MLA_KV_RANK, GROUP_WIDTH).astype(BF16)
    wvt = w[:, :, 1, :].reshape(MLA_KV_RANK, GROUP_WIDTH).T.astype(BF16)
    return wk, wvt


def _layout_qk_gain(g):
    return jnp.concatenate([g[:MLA_NOPE_DIM], _spread_rope(g[MLA_NOPE_DIM:])])[None, :]


def kernel(x, attn_norm, w_in, b_forget, fox_q_norm, fox_k_norm, moba_q_norm, moba_k_norm, mla_cq_norm, mla_ckv_norm, w_uq, w_ukv, mla_q_norm, mla_k_norm, mix_out_norm, w_out, ffn_norm, w_up, conv_w, conv_b, w_down):
    batch, seq, d_model = x.shape
    depth = w_in.shape[0]
    assert d_model == D_MODEL and seq % TILE == 0
    n = batch * seq
    x2 = x.reshape(n, d_model)
    cosc, sinc, cosd, sind = _rope_tables(seq)
    row = lambda v: v[None, :]

    for l in range(depth):
        w_main, wvt, wf = _layout_w_in(w_in[l])
        bf = jnp.zeros((8, 128), F32).at[:GROUP_HEADS, :].set(b_forget[l][:, None])
        proj, vt3, negc = _inproj(x2, row(attn_norm[l]), w_main, wvt, wf, bf, batch, seq)
        negc_col = negc.transpose(0, 2, 1).reshape(n, 8)

        wuk, wuvt = _layout_w_ukv(w_ukv[l])
        qa, ka, qkc, sel, qd, kd, vtd = _prep(
            proj, negc_col, row(fox_q_norm[l]), row(fox_k_norm[l]), row(moba_q_norm[l]), row(moba_k_norm[l]),
            row(mla_cq_norm[l]), row(mla_ckv_norm[l]),
            _layout_qk_gain(mla_q_norm[l]), _layout_qk_gain(mla_k_norm[l]),
            _layout_w_uq(w_uq[l]), wuk, wuvt, cosc, sinc, cosd, sind, batch, seq)

        o_a = _softmax_attn("fox_attn", qa, 0, ka, 0, vt3, 0, None, batch, seq, WIDE_HEAD)
        o_b = _stickbreak_attn(proj, vt3, batch, seq)
        o_c = _softmax_attn("moba_attn", qkc, 0, qkc, 1, vt3, 2, sel, batch, seq, HEAD_DIM)
        o_d = _softmax_attn("mla_attn", qd, 0, kd, 0, vtd, None, None, batch, seq, WIDE_HEAD)

        x2 = _outproj(o_a, o_b, o_c, o_d, mix_out_norm[l].reshape(N_GROUPS, GROUP_WIDTH),
                      w_out[l].astype(BF16), x2)

        act = _ffn_up(x2, row(ffn_norm[l]), w_up[l].astype(BF16), conv_w[l], row(conv_b[l]), seq)
        x2 = _ffn_down(act, w_down[l].astype(BF16), x2)

    return x2.reshape(batch, seq, d_model)
```

```python
import functools
import math

import jax
import jax.numpy as jnp
from jax import lax
from jax.experimental import pallas as pl
from jax.experimental.pallas import tpu as pltpu

F32 = jnp.float32
BF16 = jnp.bfloat16

D_MODEL = 2048
HEAD_DIM = 128
GROUP_HEADS = 4
GROUP_WIDTH = GROUP_HEADS * HEAD_DIM
N_GROUPS = 4
MOBA_BLOCK = 256
MOBA_TOP_K = 3
MOBA_MAX_BLOCKS = 16
MLA_Q_RANK = 512
MLA_KV_RANK = 256
MLA_NOPE_DIM = 128
MLA_ROPE_DIM = 64
MLA_QK_DIM = MLA_NOPE_DIM + MLA_ROPE_DIM
WIDE_HEAD = 256
D_FF = 5632
CONV_WIDTH = 3
ROPE_THETA = 10000.0
EPS = 1e-6

T_AQ, T_AK, T_CQ, T_CK, T_DCQ, T_DKV, T_BQ, T_BK = range(8)
PROJ_TILES = 8
V_GROUPS = 3

TILE = 512
NEG = -1e30
VMEM_LIMIT = 56 * 1024 * 1024
LOG2E = math.log2(math.e)
SCALE = HEAD_DIM ** -0.5
SCALE_MLA = MLA_QK_DIM ** -0.5
FFN_TOKENS = 1024


def _dot(a, b):
    return jnp.dot(a, b, preferred_element_type=F32)


def _dot_nt(a, b):
    return lax.dot_general(a, b, (((1,), (1,)), ((), ())), preferred_element_type=F32)


def _split3(a):
    hi = a.astype(BF16)
    r1 = a - hi.astype(F32)
    mid = r1.astype(BF16)
    lo = (r1 - mid.astype(F32)).astype(BF16)
    return hi, mid, lo


def _log_sigmoid_pair(z):
    sp = jnp.log(1.0 + jnp.exp(-jnp.abs(z)))
    return jnp.minimum(z, 0.0) - sp, -jnp.maximum(z, 0.0) - sp


def _rms(v, g, denom=None):
    if denom is None:
        ms = jnp.mean(v * v, axis=-1, keepdims=True)
    else:
        ms = jnp.sum(v * v, axis=-1, keepdims=True) * (1.0 / denom)
    return v * lax.rsqrt(ms + EPS) * g


def _inproj_kernel(x_ref, g_ref, w_ref, wvt_ref, wf_ref, bf_ref, proj_ref, vt_ref, negc_ref,
                   carry_scr, *, tiles_per_batch, tm):
    i = pl.program_id(0)
    h = _rms(x_ref[...], g_ref[...]).astype(BF16)
    z = _dot_nt(wf_ref[...], h) + bf_ref[:, 0:1]
    logf, _ = _log_sigmoid_pair(z)
    r = lax.broadcasted_iota(jnp.int32, (tm, tm), 0)
    c = lax.broadcasted_iota(jnp.int32, (tm, tm), 1)
    tri = jnp.where(r <= c, 1.0, 0.0).astype(BF16)
    hi, mid, lo = _split3(logf)
    cum = _dot(hi, tri) + _dot(mid, tri) + _dot(lo, tri)

    @pl.when(i % tiles_per_batch == 0)
    def _():
        carry_scr[...] = jnp.zeros_like(carry_scr)

    cum = cum + carry_scr[:, 0:1]
    carry_scr[...] = jnp.broadcast_to(cum[:, tm - 1:tm], carry_scr.shape)
    negc_ref[...] = cum * (-LOG2E)

    gw = GROUP_WIDTH
    for jt in range(PROJ_TILES):
        proj_ref[:, jt * gw:(jt + 1) * gw] = _dot(h, w_ref[:, jt * gw:(jt + 1) * gw]).astype(BF16)
    for gi in range(V_GROUPS):
        vt_ref[gi] = _dot_nt(wvt_ref[gi], h).astype(BF16)


def _inproj(x2, g, w, wvt, wf, bf, batch, seq):
    n = x2.shape[0]
    tm = TILE
    tpb = seq // tm
    kern = functools.partial(_inproj_kernel, tiles_per_batch=tpb, tm=tm)
    const2 = lambda i: (0, 0)
    resident = pl.Buffered(1)
    return pl.pallas_call(
        kern,
        grid=(n // tm,),
        in_specs=[
            pl.BlockSpec((tm, D_MODEL), lambda i: (i, 0)),
            pl.BlockSpec((1, D_MODEL), const2),
            pl.BlockSpec((D_MODEL, PROJ_TILES * GROUP_WIDTH), const2, pipeline_mode=resident),
            pl.BlockSpec((V_GROUPS, GROUP_WIDTH, D_MODEL), lambda i: (0, 0, 0), pipeline_mode=resident),
            pl.BlockSpec((8, D_MODEL), const2),
            pl.BlockSpec((8, 128), const2),
        ],
        out_specs=[
            pl.BlockSpec((tm, PROJ_TILES * GROUP_WIDTH), lambda i: (i, 0)),
            pl.BlockSpec((V_GROUPS, None, None, GROUP_WIDTH, tm), lambda i: (0, i // tpb, i % tpb, 0, 0)),
            pl.BlockSpec((None, 8, tm), lambda i: (i // tpb, 0, i % tpb)),
        ],
        out_shape=[
            jax.ShapeDtypeStruct((n, PROJ_TILES * GROUP_WIDTH), BF16),
            jax.ShapeDtypeStruct((V_GROUPS, batch, tpb, GROUP_WIDTH, tm), BF16),
            jax.ShapeDtypeStruct((batch, 8, seq), F32),
        ],
        scratch_shapes=[pltpu.VMEM((8, 128), F32)],
        compiler_params=pltpu.CompilerParams(
            dimension_semantics=("arbitrary",), vmem_limit_bytes=VMEM_LIMIT),
        name="inproj",
    )(x2, g, w, wvt, wf, bf)


def _prep_kernel(a_ref, c_ref, d_ref, nc_ref, gaq, gak, gcq, gck, gdcq, gdckv, gdq, gdk, wuq, wuk, wuvt,
                 cosc, sinc, cosd, sind,
                 qa_ref, ka_ref, qkc_ref, sel_ref, qd_ref, kd_ref, vtd_ref, kmt_scr,
                 *, tiles_per_batch, tm):
    i = pl.program_id(0)
    t = i % tiles_per_batch
    hd = HEAD_DIM
    wh = WIDE_HEAD
    blocks_per_tile = tm // MOBA_BLOCK
    lane = lax.broadcasted_iota(jnp.int32, (tm, hd), 1)

    ones_aug = jnp.where(lane < 3, 1.0, 0.0).astype(BF16)
    nc = nc_ref[...]
    for h in range(GROUP_HEADS):
        q = a_ref[:, h * hd:(h + 1) * hd].astype(F32)
        qa_ref[:, h * wh:h * wh + hd] = (_rms(q, gaq[...]) * (SCALE * LOG2E)).astype(BF16)
        qa_ref[:, h * wh + hd:(h + 1) * wh] = ones_aug
        k = a_ref[:, GROUP_WIDTH + h * hd:GROUP_WIDTH + (h + 1) * hd].astype(F32)
        ka_ref[:, h * wh:h * wh + hd] = _rms(k, gak[...]).astype(BF16)
        hi, mid, lo = _split3(nc[:, h:h + 1])
        aug = jnp.where(lane == 0, hi.astype(F32),
                        jnp.where(lane == 1, mid.astype(F32), jnp.where(lane == 2, lo.astype(F32), 0.0)))
        ka_ref[:, h * wh + hd:(h + 1) * wh] = aug.astype(BF16)

    @pl.when(i == 0)
    def _():
        kmt_scr[...] = jnp.zeros_like(kmt_scr)

    cc = cosc[...]
    sc = sinc[...]
    sub8 = lax.broadcasted_iota(jnp.int32, (8, hd), 0)
    nb = MOBA_MAX_BLOCKS

    def rope_c(v):
        return v * cc + pltpu.roll(v, hd // 2, axis=1) * sc

    for h in range(GROUP_HEADS):
        k = rope_c(_rms(c_ref[:, GROUP_WIDTH + h * hd:GROUP_WIDTH + (h + 1) * hd].astype(F32), gck[...]))
        qkc_ref[:, GROUP_WIDTH + h * hd:GROUP_WIDTH + (h + 1) * hd] = k.astype(BF16)
        for blk in range(blocks_per_tile):
            km = jnp.mean(k[blk * MOBA_BLOCK:(blk + 1) * MOBA_BLOCK], axis=0, keepdims=True)
            nblk = t * blocks_per_tile + blk
            base = pl.multiple_of(h * nb + (nblk // 8) * 8, 8)
            old = kmt_scr[pl.ds(base, 8), h * hd:(h + 1) * hd]
            kmt_scr[pl.ds(base, 8), h * hd:(h + 1) * hd] = jnp.where(sub8 == nblk % 8, km, old)

    qs = []
    for h in range(GROUP_HEADS):
        q = rope_c(_rms(c_ref[:, h * hd:(h + 1) * hd].astype(F32), gcq[...])) * (SCALE * LOG2E)
        qkc_ref[:, h * hd:(h + 1) * hd] = q.astype(BF16)
        qs.append(q)
    q_all = jnp.concatenate(qs, axis=1)
    qh, qm, _ = _split3(q_all)
    kh, km_, _ = _split3(kmt_scr[...])
    gate_t = _dot_nt(kh, qh) + _dot_nt(km_, qh) + _dot_nt(kh, qm)

    blk_row = lax.broadcasted_iota(jnp.int32, (nb, tm), 0)
    q_col = lax.broadcasted_iota(jnp.int32, (nb, tm), 1)
    own = t * blocks_per_tile + q_col // MOBA_BLOCK
    blk_f = blk_row.astype(F32)
    ninf = jnp.float32(-jnp.inf)
    for h in range(GROUP_HEADS):
        g = jnp.where(blk_row < own, gate_t[h * nb:(h + 1) * nb, :], ninf)
        sel = jnp.zeros((nb, tm), F32)
        for _ in range(MOBA_TOP_K):
            mx = jnp.max(g, axis=0, keepdims=True)
            cand = jnp.where(g == mx, blk_f, 1e9)
            cand = jnp.where(mx > ninf, cand, 1e9)
            idx = jnp.min(cand, axis=0, keepdims=True)
            pick = blk_f == idx
            sel = jnp.where(pick, 1.0, sel)
            g = jnp.where(pick, ninf, g)
        sel_ref[h * nb:(h + 1) * nb, :] = sel

    cd = cosd[...]
    sd = sind[...]

    def rope_d(v):
        return v * cd + pltpu.roll(v, hd // 2, axis=1) * sd

    cq = _rms(d_ref[:, 0:MLA_Q_RANK].astype(F32), gdcq[...]).astype(BF16)
    qf = _dot(cq, wuq[...])
    for h in range(GROUP_HEADS):
        qn = _rms(qf[:, h * wh:(h + 1) * wh], gdq[...], denom=MLA_QK_DIM)
        qd_ref[:, h * wh:h * wh + hd] = (qn[:, :hd] * (SCALE_MLA * LOG2E)).astype(BF16)
        qd_ref[:, h * wh + hd:(h + 1) * wh] = (rope_d(qn[:, hd:]) * (SCALE_MLA * LOG2E)).astype(BF16)

    ckv = _rms(d_ref[:, MLA_Q_RANK:MLA_Q_RANK + MLA_KV_RANK].astype(F32), gdckv[...]).astype(BF16)
    knf = _dot(ckv, wuk[...])
    vtd_ref[...] = _dot_nt(wuvt[...], ckv).astype(BF16)
    kr = d_ref[:, MLA_Q_RANK + MLA_KV_RANK:MLA_Q_RANK + MLA_KV_RANK + hd].astype(F32)
    ss_r = jnp.sum(kr * kr, axis=-1, keepdims=True)
    gk = gdk[...]
    for h in range(GROUP_HEADS):
        kn = knf[:, h * hd:(h + 1) * hd]
        ms = (jnp.sum(kn * kn, axis=-1, keepdims=True) + ss_r) * (1.0 / MLA_QK_DIM)
        r = lax.rsqrt(ms + EPS)
        kd_ref[:, h * wh:h * wh + hd] = (kn * r * gk[:, :hd]).astype(BF16)
        kd_ref[:, h * wh + hd:(h + 1) * wh] = rope_d(kr * r * gk[:, hd:]).astype(BF16)


def _prep(proj, negc_col, gaq, gak, gcq, gck, gdcq, gdckv, gdq, gdk, wuq, wuk, wuvt,
          cosc, sinc, cosd, sind, batch, seq):
    n = proj.shape[0]
    tm = TILE
    tpb = seq // tm
    assert seq // MOBA_BLOCK <= MOBA_MAX_BLOCKS
    kern = functools.partial(_prep_kernel, tiles_per_batch=tpb, tm=tm)
    row = lambda i: (i, 0)
    const = lambda i: (0, 0)
    tab = lambda i: (i % tpb, 0)
    w2 = 2 * GROUP_WIDTH
    nsel = GROUP_HEADS * MOBA_MAX_BLOCKS
    return pl.pallas_call(
        kern,
        grid=(n // tm,),
        in_specs=[
            pl.BlockSpec((tm, w2), lambda i: (i, 0)),
            pl.BlockSpec((tm, w2), lambda i: (i, 1)),
            pl.BlockSpec((tm, w2), lambda i: (i, 2)),
            pl.BlockSpec((tm, 8), row),
            pl.BlockSpec((1, HEAD_DIM), const), pl.BlockSpec((1, HEAD_DIM), const),
            pl.BlockSpec((1, HEAD_DIM), const), pl.BlockSpec((1, HEAD_DIM), const),
            pl.BlockSpec((1, MLA_Q_RANK), const), pl.BlockSpec((1, MLA_KV_RANK), const),
            pl.BlockSpec((1, WIDE_HEAD), const), pl.BlockSpec((1, WIDE_HEAD), const),
            pl.BlockSpec((MLA_Q_RANK, GROUP_HEADS * WIDE_HEAD), const),
            pl.BlockSpec((MLA_KV_RANK, GROUP_WIDTH), const),
            pl.BlockSpec((GROUP_WIDTH, MLA_KV_RANK), const),
            pl.BlockSpec((tm, HEAD_DIM), tab), pl.BlockSpec((tm, HEAD_DIM), tab),
            pl.BlockSpec((tm, HEAD_DIM), tab), pl.BlockSpec((tm, HEAD_DIM), tab),
        ],
        out_specs=[
            pl.BlockSpec((tm, w2), row), pl.BlockSpec((tm, w2), row), pl.BlockSpec((tm, w2), row),
            pl.BlockSpec((None, nsel, tm), lambda i: (i, 0, 0)),
            pl.BlockSpec((tm, w2), row), pl.BlockSpec((tm, w2), row),
            pl.BlockSpec((None, None, GROUP_WIDTH, tm), lambda i: (i // tpb, i % tpb, 0, 0)),
        ],
        out_shape=[
            jax.ShapeDtypeStruct((n, w2), BF16), jax.ShapeDtypeStruct((n, w2), BF16),
            jax.ShapeDtypeStruct((n, w2), BF16),
            jax.ShapeDtypeStruct((n // tm, nsel, tm), F32),
            jax.ShapeDtypeStruct((n, w2), BF16), jax.ShapeDtypeStruct((n, w2), BF16),
            jax.ShapeDtypeStruct((batch, tpb, GROUP_WIDTH, tm), BF16),
        ],
        scratch_shapes=[pltpu.VMEM((nsel, GROUP_WIDTH), F32)],
        compiler_params=pltpu.CompilerParams(
            dimension_semantics=("arbitrary",), vmem_limit_bytes=VMEM_LIMIT),
        name="prep",
    )(proj, proj, proj, negc_col, gaq, gak, gcq, gck, gdcq, gdckv, gdq, gdk, wuq, wuk, wuvt,
      cosc, sinc, cosd, sind)


def _key_query_iota(t):
    key = lax.broadcasted_iota(jnp.int32, (t, t), 0)
    qry = lax.broadcasted_iota(jnp.int32, (t, t), 1)
    return key, qry


def _softmax_step(s, vt, carry):
    m, l, acc = carry
    m_new = jnp.maximum(m, jnp.max(s, axis=0, keepdims=True))
    alpha = jnp.exp2(m - m_new)
    p = jnp.exp2(s - m_new)
    l = alpha * l + jnp.sum(p, axis=0, keepdims=True)
    acc = alpha * acc + _dot(vt, p.astype(BF16))
    return m_new, l, acc


def _softmax_init(t):
    return (jnp.full((1, t), NEG, F32), jnp.zeros((1, t), F32), jnp.zeros((HEAD_DIM, t), F32))


def _softmax_attn_kernel(*refs, t, dk, routed):
    if routed:
        q_ref, k_ref, vt_ref, sel_ref, o_ref = refs
    else:
        q_ref, k_ref, vt_ref, o_ref = refs
    qi = pl.program_id(1)
    key, qry = _key_query_iota(t)
    causal = key <= qry
    dv = HEAD_DIM
    half = MOBA_BLOCK
    nb = MOBA_MAX_BLOCKS
    if routed:
        blk_row = lax.broadcasted_iota(jnp.int32, (nb, t), 0)
        second_half = lax.broadcasted_iota(jnp.int32, (1, t), 1) >= half

        def routed_to(h, n):
            return jnp.max(jnp.where(blk_row == n, sel_ref[h * nb:(h + 1) * nb, :], 0.0), axis=0, keepdims=True)

    def step(j, carry, diag):
        off = pl.multiple_of(j * t, t)
        out = []
        for h in range(GROUP_HEADS):
            k = k_ref[pl.ds(off, t), h * dk:(h + 1) * dk]
            s = _dot_nt(k, q_ref[:, h * dk:(h + 1) * dk])
            if diag:
                s = jnp.where(causal, s, NEG)
                if routed:
                    keep = jnp.where(second_half, routed_to(h, 2 * j), 1.0) > 0.5
                    s = jnp.concatenate([jnp.where(keep, s[:half], NEG), s[half:]], axis=0)
            elif routed:
                s = jnp.concatenate([jnp.where(routed_to(h, 2 * j) > 0.5, s[:half], NEG),
                                     jnp.where(routed_to(h, 2 * j + 1) > 0.5, s[half:], NEG)], axis=0)
            out.append(_softmax_step(s, vt_ref[j, h * dv:(h + 1) * dv, :], carry[h]))
        return tuple(out)

    carry = step(qi, tuple(_softmax_init(t) for _ in range(GROUP_HEADS)), True)
    carry = lax.fori_loop(0, qi, lambda j, c: step(j, c, False), carry)
    for h in range(GROUP_HEADS):
        m, l, acc = carry[h]
        o_ref[:, h * dv:(h + 1) * dv] = (acc / l).T.astype(BF16)


def _softmax_attn(name, q_arr, q_col, k_arr, k_col, vt_arr, vt_group, sel, batch, seq, dk):
    t = TILE
    nq = seq // t
    n = batch * seq
    routed = sel is not None
    kern = functools.partial(_softmax_attn_kernel, t=t, dk=dk, routed=routed)
    if vt_group is None:
        vt_spec = pl.BlockSpec((None, nq, GROUP_WIDTH, t), lambda b, i: (b, 0, 0, 0))
    else:
        vt_spec = pl.BlockSpec((None, None, nq, GROUP_WIDTH, t), lambda b, i: (vt_group, b, 0, 0, 0))
    in_specs = [
        pl.BlockSpec((t, GROUP_HEADS * dk), lambda b, i: (b * nq + i, q_col)),
        pl.BlockSpec((seq, GROUP_HEADS * dk), lambda b, i: (b, k_col)),
        vt_spec,
    ]
    args = [q_arr, k_arr, vt_arr]
    if routed:
        in_specs.append(pl.BlockSpec((None, GROUP_HEADS * MOBA_MAX_BLOCKS, t), lambda b, i: (b * nq + i, 0, 0)))
        args.append(sel)
    return pl.pallas_call(
        kern,
        grid=(batch, nq),
        in_specs=in_specs,
        out_specs=pl.BlockSpec((t, GROUP_WIDTH), lambda b, i: (b * nq + i, 0)),
        out_shape=jax.ShapeDtypeStruct((n, GROUP_WIDTH), BF16),
        compiler_params=pltpu.CompilerParams(
            dimension_semantics=("parallel", "parallel"), vmem_limit_bytes=VMEM_LIMIT),
        name=name,
    )(*args)


def _stickbreak_kernel(q_ref, k_ref, vt_ref, o_ref, *, t):
    qi = pl.program_id(1)
    key, qry = _key_query_iota(t)
    strict = key < qry
    half = t // 2
    kk, kj = _key_query_iota(half)
    later = jnp.where(kj > kk, 1.0, 0.0).astype(BF16)
    hd = HEAD_DIM

    def step(j, carry, diag):
        off = pl.multiple_of(j * t, t)
        out = []
        for h in range(GROUP_HEADS):
            rest, acc = carry[h]
            q = (q_ref[:, h * hd:(h + 1) * hd].astype(F32) * SCALE).astype(BF16)
            k = k_ref[pl.ds(off, t), h * hd:(h + 1) * hd]
            z = _dot_nt(k, q)
            log_beta, log_rest = _log_sigmoid_pair(z)
            if diag:
                log_rest = jnp.where(strict, log_rest, 0.0)
            lr_top, lr_bot = log_rest[:half], log_rest[half:]
            ex_bot = _dot(later, lr_bot.astype(BF16))
            tot_bot = ex_bot[0:1] + lr_bot[0:1]
            ex_top = _dot(later, lr_top.astype(BF16))
            tot_top = ex_top[0:1] + lr_top[0:1]
            excl = jnp.concatenate([ex_top + tot_bot, ex_bot], axis=0)
            w = jnp.exp(log_beta + excl + rest)
            if diag:
                w = jnp.where(strict, w, 0.0)
            acc = acc + _dot(vt_ref[j, h * hd:(h + 1) * hd, :], w.astype(BF16))
            out.append((rest + tot_top + tot_bot, acc))
        return tuple(out)

    init = tuple((jnp.zeros((1, t), F32), jnp.zeros((hd, t), F32)) for _ in range(GROUP_HEADS))
    carry = step(qi, init, True)
    carry = lax.fori_loop(0, qi, lambda it, c: step(qi - 1 - it, c, False), carry)
    for h in range(GROUP_HEADS):
        o_ref[:, h * hd:(h + 1) * hd] = carry[h][1].T.astype(BF16)


def _stickbreak_attn(proj, vt3, batch, seq):
    t = TILE
    nq = seq // t
    n = batch * seq
    kern = functools.partial(_stickbreak_kernel, t=t)
    return pl.pallas_call(
        kern,
        grid=(batch, nq),
        in_specs=[
            pl.BlockSpec((t, GROUP_WIDTH), lambda b, i: (b * nq + i, T_BQ)),
            pl.BlockSpec((seq, GROUP_WIDTH), lambda b, i: (b, T_BK)),
            pl.BlockSpec((None, None, nq, GROUP_WIDTH, t), lambda b, i: (1, b, 0, 0, 0)),
        ],
        out_specs=pl.BlockSpec((t, GROUP_WIDTH), lambda b, i: (b * nq + i, 0)),
        out_shape=jax.ShapeDtypeStruct((n, GROUP_WIDTH), BF16),
        compiler_params=pltpu.CompilerParams(
            dimension_semantics=("parallel", "parallel"), vmem_limit_bytes=VMEM_LIMIT),
        name="stickbreak_attn",
    )(proj, proj, vt3)


def _outproj_kernel(oa_ref, ob_ref, oc_ref, od_ref, g_ref, w_ref, x_ref, gf_ref, out_ref, hf_ref):
    acc = x_ref[...]
    for gi, o_ref in enumerate((oa_ref, ob_ref, oc_ref, od_ref)):
        y = _rms(o_ref[...].astype(F32), g_ref[gi:gi + 1, :]).astype(BF16)
        acc = acc + _dot(y, w_ref[gi * GROUP_WIDTH:(gi + 1) * GROUP_WIDTH, :])
    out_ref[...] = acc
    hf_ref[...] = _rms(acc, gf_ref[...]).astype(BF16)


def _outproj(oa, ob, oc, od, g, w, x2, g_ffn):
    n = x2.shape[0]
    tm = 512
    row = lambda i: (i, 0)
    const = lambda i: (0, 0)
    return pl.pallas_call(
        _outproj_kernel,
        grid=(n // tm,),
        in_specs=[pl.BlockSpec((tm, GROUP_WIDTH), row)] * 4 + [
            pl.BlockSpec((N_GROUPS, GROUP_WIDTH), const),
            pl.BlockSpec((N_GROUPS * GROUP_WIDTH, D_MODEL), const, pipeline_mode=pl.Buffered(1)),
            pl.BlockSpec((tm, D_MODEL), row),
            pl.BlockSpec((1, D_MODEL), const),
        ],
        out_specs=[pl.BlockSpec((tm, D_MODEL), row), pl.BlockSpec((tm, D_MODEL), row)],
        out_shape=[jax.ShapeDtypeStruct((n, D_MODEL), F32), jax.ShapeDtypeStruct((n, D_MODEL), BF16)],
        compiler_params=pltpu.CompilerParams(
            dimension_semantics=("parallel",), vmem_limit_bytes=VMEM_LIMIT),
        name="outproj",
    )(oa, ob, oc, od, g, w, x2, g_ffn)


def _ffn_up_kernel(h_ref, wg_ref, wv_ref, cwg_ref, cwv_ref, cbg_ref, cbv_ref,
                   out_ref, wg_scr, wv_scr, tail_g, tail_v, *, tiles_per_batch, tm):
    i = pl.program_id(1)

    @pl.when(i == 0)
    def _():
        wg_scr[...] = wg_ref[...].astype(BF16)
        wv_scr[...] = wv_ref[...].astype(BF16)

    @pl.when(i % tiles_per_batch == 0)
    def _():
        tail_g[...] = jnp.zeros_like(tail_g)
        tail_v[...] = jnp.zeros_like(tail_v)

    h = h_ref[...]

    def conv(w_scr, cw_ref, cb_ref, tail):
        u = _dot(h, w_scr[...])
        ue = jnp.concatenate([tail[...], u], axis=0)
        tail[...] = u[tm - 8:]
        y = cb_ref[...] + cw_ref[2:3, :] * u
        for back in range(1, CONV_WIDTH):
            y = y + cw_ref[CONV_WIDTH - 1 - back:CONV_WIDTH - back, :] * pltpu.roll(ue, back, axis=0)[8:]
        return y

    gate = conv(wg_scr, cwg_ref, cbg_ref, tail_g)
    val = conv(wv_scr, cwv_ref, cbv_ref, tail_v)
    out_ref[...] = (gate / (1.0 + jnp.exp(-gate)) * val).astype(BF16)


def _ffn_up(h, w_up, conv_w, conv_b, layer, seq):
    n = h.shape[0]
    tm = FFN_TOKENS
    tn = 512
    nj = D_FF // tn
    tpb = seq // tm
    kern = functools.partial(_ffn_up_kernel, tiles_per_batch=tpb, tm=tm)
    return pl.pallas_call(
        kern,
        grid=(nj, n // tm),
        in_specs=[
            pl.BlockSpec((tm, D_MODEL), lambda j, i: (i, 0)),
            pl.BlockSpec((None, D_MODEL, tn), lambda j, i: (layer, 0, j)),
            pl.BlockSpec((None, D_MODEL, tn), lambda j, i: (layer, 0, j + nj)),
            pl.BlockSpec((None, CONV_WIDTH, tn), lambda j, i: (layer, 0, j)),
            pl.BlockSpec((None, CONV_WIDTH, tn), lambda j, i: (layer, 0, j + nj)),
            pl.BlockSpec((None, 1, tn), lambda j, i: (layer, 0, j)),
            pl.BlockSpec((None, 1, tn), lambda j, i: (layer, 0, j + nj)),
        ],
        out_specs=pl.BlockSpec((tm, tn), lambda j, i: (i, j)),
        out_shape=jax.ShapeDtypeStruct((n, D_FF), BF16),
        scratch_shapes=[pltpu.VMEM((D_MODEL, tn), BF16), pltpu.VMEM((D_MODEL, tn), BF16),
                        pltpu.VMEM((8, tn), F32), pltpu.VMEM((8, tn), F32)],
        compiler_params=pltpu.CompilerParams(
            dimension_semantics=("arbitrary", "arbitrary"), vmem_limit_bytes=VMEM_LIMIT),
        name="ffn_up",
    )(h, w_up, w_up, conv_w, conv_w, conv_b, conv_b)


def _ffn_down_kernel(a_ref, w_ref, x_ref, out_ref):
    out_ref[...] = x_ref[...] + _dot(a_ref[...], w_ref[...])


def _ffn_down(act, w_down, x2):
    n = x2.shape[0]
    tm = FFN_TOKENS
    tn = 512
    return pl.pallas_call(
        _ffn_down_kernel,
        grid=(n // tm, D_MODEL // tn),
        in_specs=[
            pl.BlockSpec((tm, D_FF), lambda i, j: (i, 0)),
            pl.BlockSpec((D_FF, tn), lambda i, j: (0, j)),
            pl.BlockSpec((tm, tn), lambda i, j: (i, j)),
        ],
        out_specs=pl.BlockSpec((tm, tn), lambda i, j: (i, j)),
        out_shape=jax.ShapeDtypeStruct((n, D_MODEL), F32),
        compiler_params=pltpu.CompilerParams(
            dimension_semantics=("parallel", "parallel"), vmem_limit_bytes=VMEM_LIMIT),
        name="ffn_down",
    )(act, w_down, x2)


def _rope_tables(seq):
    def angles(dim):
        inv_freq = ROPE_THETA ** (-jnp.arange(0, dim, 2, dtype=F32) / dim)
        ang = jnp.arange(seq, dtype=F32)[:, None] * inv_freq[None, :]
        return jnp.cos(ang), jnp.sin(ang)

    c, s = angles(HEAD_DIM)
    cosc = jnp.concatenate([c, c], axis=1)
    sinc = jnp.concatenate([-s, s], axis=1)
    c, s = angles(MLA_ROPE_DIM)
    z = jnp.zeros_like(c)
    cosd = jnp.concatenate([c, z, c, z], axis=1)
    sind = jnp.concatenate([-s, z, s, z], axis=1)
    return cosc, sinc, cosd, sind


def _spread_rope(t):
    half = MLA_ROPE_DIM // 2
    z = jnp.zeros(t.shape[:-1] + (half,), t.dtype)
    return jnp.concatenate([t[..., :half], z, t[..., half:], z], axis=-1)


def _layout_w_in(w):
    gw = GROUP_WIDTH
    o = 0
    pieces = {}
    for name, width in (("aq", gw), ("ak", gw), ("av", gw), ("af", GROUP_HEADS), ("bq", gw), ("bk", gw),
                        ("bv", gw), ("cq", gw), ("ck", gw), ("cv", gw), ("dcq", MLA_Q_RANK),
                        ("dckv", MLA_KV_RANK), ("dkr", MLA_ROPE_DIM)):
        pieces[name] = w[:, o:o + width]
        o += width
    pad = jnp.zeros((w.shape[0], gw - MLA_KV_RANK - HEAD_DIM), w.dtype)
    tiles = [pieces["aq"], pieces["ak"], pieces["cq"], pieces["ck"], pieces["dcq"],
             pieces["dckv"], _spread_rope(pieces["dkr"]), pad, pieces["bq"], pieces["bk"]]
    w_main = jnp.concatenate(tiles, axis=1).astype(BF16)
    wvt = jnp.stack([pieces["av"].T, pieces["bv"].T, pieces["cv"].T]).astype(BF16)
    wf = jnp.zeros((8, w.shape[0]), F32).at[:GROUP_HEADS].set(pieces["af"].T).astype(BF16)
    return w_main, wvt, wf


def _layout_w_uq(w):
    w = w.reshape(MLA_Q_RANK, GROUP_HEADS, MLA_QK_DIM)
    w = jnp.concatenate([w[..., :MLA_NOPE_DIM], _spread_rope(w[..., MLA_NOPE_DIM:])], axis=-1)
    return w.reshape(MLA_Q_RANK, GROUP_HEADS * WIDE_HEAD).astype(BF16)


def _layout_w_ukv(w):
    w = w.reshape(MLA_KV_RANK, GROUP_HEADS, 2, HEAD_DIM)
    wk = w[:, :, 0, :].reshape(MLA_KV_RANK, GROUP_WIDTH).astype(BF16)
    wvt = w[:, :, 1, :].reshape(MLA_KV_RANK, GROUP_WIDTH).T.astype(BF16)
    return wk, wvt


def _layout_qk_gain(g):
    return jnp.concatenate([g[:MLA_NOPE_DIM], _spread_rope(g[MLA_NOPE_DIM:])])[None, :]


def kernel(x, attn_norm, w_in, b_forget, fox_q_norm, fox_k_norm, moba_q_norm, moba_k_norm, mla_cq_norm, mla_ckv_norm, w_uq, w_ukv, mla_q_norm, mla_k_norm, mix_out_norm, w_out, ffn_norm, w_up, conv_w, conv_b, w_down):
    batch, seq, d_model = x.shape
    depth = w_in.shape[0]
    assert d_model == D_MODEL and seq % TILE == 0 and seq % FFN_TOKENS == 0
    n = batch * seq
    x2 = x.reshape(n, d_model)
    cosc, sinc, cosd, sind = _rope_tables(seq)
    row = lambda v: v[None, :]

    for l in range(depth):
        w_main, wvt, wf = _layout_w_in(w_in[l])
        bf = jnp.zeros((8, 128), F32).at[:GROUP_HEADS, :].set(b_forget[l][:, None])
        proj, vt3, negc = _inproj(x2, row(attn_norm[l]), w_main, wvt, wf, bf, batch, seq)
        negc_col = negc.transpose(0, 2, 1).reshape(n, 8)

        wuk, wuvt = _layout_w_ukv(w_ukv[l])
        qa, ka, qkc, sel, qd, kd, vtd = _prep(
            proj, negc_col, row(fox_q_norm[l]), row(fox_k_norm[l]), row(moba_q_norm[l]), row(moba_k_norm[l]),
            row(mla_cq_norm[l]), row(mla_ckv_norm[l]),
            _layout_qk_gain(mla_q_norm[l]), _layout_qk_gain(mla_k_norm[l]),
            _layout_w_uq(w_uq[l]), wuk, wuvt, cosc, sinc, cosd, sind, batch, seq)

        o_a = _softmax_attn("fox_attn", qa, 0, ka, 0, vt3, 0, None, batch, seq, WIDE_HEAD)
        o_b = _stickbreak_attn(proj, vt3, batch, seq)
        o_c = _softmax_attn("moba_attn", qkc, 0, qkc, 1, vt3, 2, sel, batch, seq, HEAD_DIM)
        o_d = _softmax_attn("mla_attn", qd, 0, kd, 0, vtd, None, None, batch, seq, WIDE_HEAD)

        x2, h_ffn = _outproj(o_a, o_b, o_c, o_d, mix_out_norm[l].reshape(N_GROUPS, GROUP_WIDTH),
                             w_out[l].astype(BF16), x2, row(ffn_norm[l]))

        act = _ffn_up(h_ffn, w_up, conv_w, conv_b[:, None, :], l, seq)
        x2 = _ffn_down(act, w_down[l].astype(BF16), x2)

    return x2.reshape(batch, seq, d_model)
```

```python
import functools
import math

import jax
import jax.numpy as jnp
from jax import lax
from jax.experimental import pallas as pl
from jax.experimental.pallas import tpu as pltpu

F32 = jnp.float32
BF16 = jnp.bfloat16

D_MODEL = 2048
HEAD_DIM = 128
GROUP_HEADS = 4
GROUP_WIDTH = GROUP_HEADS * HEAD_DIM
N_GROUPS = 4
MOBA_BLOCK = 256
MOBA_TOP_K = 3
MOBA_MAX_BLOCKS = 16
MLA_Q_RANK = 512
MLA_KV_RANK = 256
MLA_NOPE_DIM = 128
MLA_ROPE_DIM = 64
MLA_QK_DIM = MLA_NOPE_DIM + MLA_ROPE_DIM
WIDE_HEAD = 256
D_FF = 5632
CONV_WIDTH = 3
ROPE_THETA = 10000.0
EPS = 1e-6

T_AQ, T_AK, T_CQ, T_CK, T_DCQ, T_DKV, T_BQ, T_BK = range(8)
PROJ_TILES = 8
V_GROUPS = 3

TILE = 512
NEG = -1e30
VMEM_LIMIT = 56 * 1024 * 1024
LOG2E = math.log2(math.e)
SCALE = HEAD_DIM ** -0.5
SCALE_MLA = MLA_QK_DIM ** -0.5
FFN_TOKENS = 1024
SUM_ROWS = 16
MAX_SAFE_BOUND = 56.0
MOBA_MIN_MASK = 256.0
STICK_DONE = 160.0


def _dot(a, b):
    return jnp.dot(a, b, preferred_element_type=F32)


def _dot_nt(a, b):
    return lax.dot_general(a, b, (((1,), (1,)), ((), ())), preferred_element_type=F32)


def _split3(a):
    hi = a.astype(BF16)
    r1 = a - hi.astype(F32)
    mid = r1.astype(BF16)
    lo = (r1 - mid.astype(F32)).astype(BF16)
    return hi, mid, lo


def _log_sigmoid_pair(z):
    sp = jnp.log(1.0 + jnp.exp(-jnp.abs(z)))
    return jnp.minimum(z, 0.0) - sp, -jnp.maximum(z, 0.0) - sp


def _rms(v, g, denom=None):
    if denom is None:
        ms = jnp.mean(v * v, axis=-1, keepdims=True)
    else:
        ms = jnp.sum(v * v, axis=-1, keepdims=True) * (1.0 / denom)
    return v * lax.rsqrt(ms + EPS) * g


def _inproj_kernel(x_ref, g_ref, w_ref, wvt_ref, wf_ref, bf_ref, proj_ref, vt_ref, negc_ref,
                   carry_scr, *, tiles_per_batch, tm):
    i = pl.program_id(0)
    h = _rms(x_ref[...], g_ref[...]).astype(BF16)
    z = _dot_nt(wf_ref[...], h) + bf_ref[:, 0:1]
    logf, _ = _log_sigmoid_pair(z)
    r = lax.broadcasted_iota(jnp.int32, (tm, tm), 0)
    c = lax.broadcasted_iota(jnp.int32, (tm, tm), 1)
    tri = jnp.where(r <= c, 1.0, 0.0).astype(BF16)
    hi, mid, lo = _split3(logf)
    cum = _dot(hi, tri) + _dot(mid, tri) + _dot(lo, tri)

    @pl.when(i % tiles_per_batch == 0)
    def _():
        carry_scr[...] = jnp.zeros_like(carry_scr)

    cum = cum + carry_scr[:, 0:1]
    carry_scr[...] = jnp.broadcast_to(cum[:, tm - 1:tm], carry_scr.shape)
    negc_ref[...] = cum * (-LOG2E)

    gw = GROUP_WIDTH
    for jt in range(PROJ_TILES):
        proj_ref[:, jt * gw:(jt + 1) * gw] = _dot(h, w_ref[:, jt * gw:(jt + 1) * gw]).astype(BF16)
    for gi in range(V_GROUPS):
        vt_ref[gi] = _dot_nt(wvt_ref[gi], h).astype(BF16)


def _inproj(x2, g, w, wvt, wf, bf, batch, seq):
    n = x2.shape[0]
    tm = TILE
    tpb = seq // tm
    kern = functools.partial(_inproj_kernel, tiles_per_batch=tpb, tm=tm)
    const2 = lambda i: (0, 0)
    resident = pl.Buffered(1)
    return pl.pallas_call(
        kern,
        grid=(n // tm,),
        in_specs=[
            pl.BlockSpec((tm, D_MODEL), lambda i: (i, 0)),
            pl.BlockSpec((1, D_MODEL), const2),
            pl.BlockSpec((D_MODEL, PROJ_TILES * GROUP_WIDTH), const2, pipeline_mode=resident),
            pl.BlockSpec((V_GROUPS, GROUP_WIDTH, D_MODEL), lambda i: (0, 0, 0), pipeline_mode=resident),
            pl.BlockSpec((8, D_MODEL), const2),
            pl.BlockSpec((8, 128), const2),
        ],
        out_specs=[
            pl.BlockSpec((tm, PROJ_TILES * GROUP_WIDTH), lambda i: (i, 0)),
            pl.BlockSpec((V_GROUPS, None, None, GROUP_WIDTH, tm), lambda i: (0, i // tpb, i % tpb, 0, 0)),
            pl.BlockSpec((None, 8, tm), lambda i: (i // tpb, 0, i % tpb)),
        ],
        out_shape=[
            jax.ShapeDtypeStruct((n, PROJ_TILES * GROUP_WIDTH), BF16),
            jax.ShapeDtypeStruct((V_GROUPS, batch, tpb, GROUP_WIDTH, tm), BF16),
            jax.ShapeDtypeStruct((batch, 8, seq), F32),
        ],
        scratch_shapes=[pltpu.VMEM((8, 128), F32)],
        compiler_params=pltpu.CompilerParams(
            dimension_semantics=("arbitrary",), vmem_limit_bytes=VMEM_LIMIT),
        name="inproj",
    )(x2, g, w, wvt, wf, bf)


def _prep_kernel(a_ref, c_ref, d_ref, nc_ref, bnd_ref, gaq, gak, gcq, gck, gdcq, gdckv, gdq, gdk, wuq, wuk, wuvt,
                 cosc, sinc, cosd, sind,
                 qa_ref, ka_ref, qc_ref, kc_ref, qd_ref, kd_ref, vtd_ref, kmt_scr,
                 *, tiles_per_batch, tm):
    i = pl.program_id(0)
    t = i % tiles_per_batch
    hd = HEAD_DIM
    wh = WIDE_HEAD
    blocks_per_tile = tm // MOBA_BLOCK
    lane = lax.broadcasted_iota(jnp.int32, (tm, hd), 1)
    bound_a = bnd_ref[0:1, :]
    bound_c = bnd_ref[1:2, :]
    big_c = bnd_ref[2:3, :]
    bound_d = bnd_ref[3:4, :]

    nc = nc_ref[...]
    for h in range(GROUP_HEADS):
        q = a_ref[:, h * hd:(h + 1) * hd].astype(F32)
        qa_ref[:, h * wh:h * wh + hd] = (_rms(q, gaq[...]) * (SCALE * LOG2E)).astype(BF16)
        k = a_ref[:, GROUP_WIDTH + h * hd:GROUP_WIDTH + (h + 1) * hd].astype(F32)
        ka_ref[:, h * wh:h * wh + hd] = _rms(k, gak[...]).astype(BF16)
        hi, mid, lo = (p.astype(F32) for p in _split3(nc[:, h:h + 1]))
        k_aug = jnp.where(lane == 0, hi, jnp.where(lane == 1, mid, jnp.where(lane == 2, lo, jnp.where(
            lane < 6, 1.0, jnp.where(lane == 6, -bound_a, 0.0)))))
        q_aug = jnp.where(lane < 3, 1.0, jnp.where(lane == 3, -hi, jnp.where(lane == 4, -mid, jnp.where(
            lane == 5, -lo, jnp.where(lane == 6, 1.0, 0.0)))))
        ka_ref[:, h * wh + hd:(h + 1) * wh] = k_aug.astype(BF16)
        qa_ref[:, h * wh + hd:(h + 1) * wh] = q_aug.astype(BF16)

    @pl.when(i == 0)
    def _():
        kmt_scr[...] = jnp.zeros_like(kmt_scr)

    cc = cosc[...]
    sc = sinc[...]
    sub8 = lax.broadcasted_iota(jnp.int32, (8, hd), 0)
    key_blk = t * blocks_per_tile + lax.broadcasted_iota(jnp.int32, (tm, hd), 0) // MOBA_BLOCK

    def rope_c(v):
        return v * cc + pltpu.roll(v, hd // 2, axis=1) * sc

    kc_aug = jnp.where(lane == key_blk, big_c, jnp.where(lane == MOBA_MAX_BLOCKS, -(bound_c + big_c), 0.0))
    kc_aug = kc_aug.astype(BF16)
    for h in range(GROUP_HEADS):
        k = rope_c(_rms(c_ref[:, GROUP_WIDTH + h * hd:GROUP_WIDTH + (h + 1) * hd].astype(F32), gck[...]))
        kc_ref[:, h * wh:h * wh + hd] = k.astype(BF16)
        kc_ref[:, h * wh + hd:(h + 1) * wh] = kc_aug
        for blk in range(blocks_per_tile):
            km = jnp.mean(k[blk * MOBA_BLOCK:(blk + 1) * MOBA_BLOCK], axis=0, keepdims=True)
            nblk = t * blocks_per_tile + blk
            base = pl.multiple_of(h * hd + (nblk // 8) * 8, 8)
            old = kmt_scr[pl.ds(base, 8), h * hd:(h + 1) * hd]
            kmt_scr[pl.ds(base, 8), h * hd:(h + 1) * hd] = jnp.where(sub8 == nblk % 8, km, old)

    qs = []
    for h in range(GROUP_HEADS):
        q = rope_c(_rms(c_ref[:, h * hd:(h + 1) * hd].astype(F32), gcq[...])) * (SCALE * LOG2E)
        qc_ref[:, h * wh:h * wh + hd] = q.astype(BF16)
        qs.append(q)
    q_all = jnp.concatenate(qs, axis=1)
    qh, qm, _ = _split3(q_all)
    kh, km_, _ = _split3(kmt_scr[...])
    gate_t = _dot_nt(kh, qh) + _dot_nt(km_, qh) + _dot_nt(kh, qm)

    blk_row = lax.broadcasted_iota(jnp.int32, (hd, tm), 0)
    own = t * blocks_per_tile + lax.broadcasted_iota(jnp.int32, (hd, tm), 1) // MOBA_BLOCK
    blk_f = blk_row.astype(F32)
    ninf = jnp.float32(-jnp.inf)
    for h in range(GROUP_HEADS):
        g = jnp.where(blk_row < own, gate_t[h * hd:(h + 1) * hd, :], ninf)
        sel = jnp.where(blk_row == own, 1.0, 0.0)
        for _ in range(MOBA_TOP_K):
            mx = jnp.max(g, axis=0, keepdims=True)
            cand = jnp.where(g == mx, blk_f, 1e9)
            cand = jnp.where(mx > ninf, cand, 1e9)
            idx = jnp.min(cand, axis=0, keepdims=True)
            pick = blk_f == idx
            sel = jnp.where(pick, 1.0, sel)
            g = jnp.where(pick, ninf, g)
        q_aug = jnp.where(lane == MOBA_MAX_BLOCKS, 1.0, sel.T)
        qc_ref[:, h * wh + hd:(h + 1) * wh] = q_aug.astype(BF16)

    cd = cosd[...]
    sd = sind[...]
    free_lane = lane == MLA_ROPE_DIM // 2

    def rope_d(v):
        return v * cd + pltpu.roll(v, hd // 2, axis=1) * sd

    cq = _rms(d_ref[:, 0:MLA_Q_RANK].astype(F32), gdcq[...]).astype(BF16)
    qf = _dot(cq, wuq[...])
    for h in range(GROUP_HEADS):
        qn = _rms(qf[:, h * wh:(h + 1) * wh], gdq[...], denom=MLA_QK_DIM)
        qd_ref[:, h * wh:h * wh + hd] = (qn[:, :hd] * (SCALE_MLA * LOG2E)).astype(BF16)
        q_rot = rope_d(qn[:, hd:]) * (SCALE_MLA * LOG2E)
        qd_ref[:, h * wh + hd:(h + 1) * wh] = jnp.where(free_lane, 1.0, q_rot).astype(BF16)

    ckv = _rms(d_ref[:, MLA_Q_RANK:MLA_Q_RANK + MLA_KV_RANK].astype(F32), gdckv[...]).astype(BF16)
    knf = _dot(ckv, wuk[...])
    vtd_ref[...] = _dot_nt(wuvt[...], ckv).astype(BF16)
    kr = d_ref[:, MLA_Q_RANK + MLA_KV_RANK:MLA_Q_RANK + MLA_KV_RANK + hd].astype(F32)
    ss_r = jnp.sum(kr * kr, axis=-1, keepdims=True)
    gk = gdk[...]
    for h in range(GROUP_HEADS):
        kn = knf[:, h * hd:(h + 1) * hd]
        ms = (jnp.sum(kn * kn, axis=-1, keepdims=True) + ss_r) * (1.0 / MLA_QK_DIM)
        r = lax.rsqrt(ms + EPS)
        kd_ref[:, h * wh:h * wh + hd] = (kn * r * gk[:, :hd]).astype(BF16)
        k_rot = rope_d(kr * r * gk[:, hd:])
        kd_ref[:, h * wh + hd:(h + 1) * wh] = jnp.where(free_lane, -bound_d, k_rot).astype(BF16)


def _prep(proj, negc_col, bounds, gaq, gak, gcq, gck, gdcq, gdckv, gdq, gdk, wuq, wuk, wuvt,
          cosc, sinc, cosd, sind, batch, seq):
    n = proj.shape[0]
    tm = TILE
    tpb = seq // tm
    assert seq // MOBA_BLOCK <= MOBA_MAX_BLOCKS
    kern = functools.partial(_prep_kernel, tiles_per_batch=tpb, tm=tm)
    row = lambda i: (i, 0)
    const = lambda i: (0, 0)
    tab = lambda i: (i % tpb, 0)
    w2 = 2 * GROUP_WIDTH
    return pl.pallas_call(
        kern,
        grid=(n // tm,),
        in_specs=[
            pl.BlockSpec((tm, w2), lambda i: (i, 0)),
            pl.BlockSpec((tm, w2), lambda i: (i, 1)),
            pl.BlockSpec((tm, w2), lambda i: (i, 2)),
            pl.BlockSpec((tm, 8), row),
            pl.BlockSpec((8, HEAD_DIM), const),
            pl.BlockSpec((1, HEAD_DIM), const), pl.BlockSpec((1, HEAD_DIM), const),
            pl.BlockSpec((1, HEAD_DIM), const), pl.BlockSpec((1, HEAD_DIM), const),
            pl.BlockSpec((1, MLA_Q_RANK), const), pl.BlockSpec((1, MLA_KV_RANK), const),
            pl.BlockSpec((1, WIDE_HEAD), const), pl.BlockSpec((1, WIDE_HEAD), const),
            pl.BlockSpec((MLA_Q_RANK, GROUP_HEADS * WIDE_HEAD), const),
            pl.BlockSpec((MLA_KV_RANK, GROUP_WIDTH), const),
            pl.BlockSpec((GROUP_WIDTH, MLA_KV_RANK), const),
            pl.BlockSpec((tm, HEAD_DIM), tab), pl.BlockSpec((tm, HEAD_DIM), tab),
            pl.BlockSpec((tm, HEAD_DIM), tab), pl.BlockSpec((tm, HEAD_DIM), tab),
        ],
        out_specs=[pl.BlockSpec((tm, w2), row)] * 6 + [
            pl.BlockSpec((None, None, GROUP_WIDTH, tm), lambda i: (i // tpb, i % tpb, 0, 0)),
        ],
        out_shape=[jax.ShapeDtypeStruct((n, w2), BF16)] * 6 + [
            jax.ShapeDtypeStruct((batch, tpb, GROUP_WIDTH, tm), BF16),
        ],
        scratch_shapes=[pltpu.VMEM((GROUP_WIDTH, GROUP_WIDTH), F32)],
        compiler_params=pltpu.CompilerParams(
            dimension_semantics=("arbitrary",), vmem_limit_bytes=VMEM_LIMIT),
        name="prep",
    )(proj, proj, proj, negc_col, bounds, gaq, gak, gcq, gck, gdcq, gdckv, gdq, gdk, wuq, wuk, wuvt,
      cosc, sinc, cosd, sind)


def _key_query_iota(t):
    key = lax.broadcasted_iota(jnp.int32, (t, t), 0)
    qry = lax.broadcasted_iota(jnp.int32, (t, t), 1)
    return key, qry


def _softmax_step(s, vt, carry):
    m, acc = carry
    m_new = jnp.maximum(m, jnp.max(s, axis=0, keepdims=True))
    alpha = jnp.exp2(m - m_new)
    p = jnp.exp2(s - m_new)
    return m_new, alpha * acc + _dot(vt, p.astype(BF16))


def _softmax_init(t):
    return (jnp.full((1, t), NEG, F32), jnp.zeros((HEAD_DIM + SUM_ROWS, t), F32))


def _softmax_attn_kernel(q_ref, k_ref, vt_ref, o_ref, acc_scr, *, t, bounded):
    qi = pl.program_id(1)
    key, qry = _key_query_iota(t)
    causal = key <= qry
    dk = WIDE_HEAD
    dv = HEAD_DIM
    ones_rows = jnp.ones((SUM_ROWS, t), BF16)

    def logits(j, h, diag):
        off = pl.multiple_of(j * t, t)
        s = _dot_nt(k_ref[pl.ds(off, t), h * dk:(h + 1) * dk], q_ref[:, h * dk:(h + 1) * dk])
        return jnp.where(causal, s, NEG) if diag else s

    def values(j, h):
        return jnp.concatenate([vt_ref[j, h * dv:(h + 1) * dv, :], ones_rows], axis=0)

    if bounded:
        acc_scr[...] = jnp.zeros_like(acc_scr)

        def step(j, diag):
            for h in range(GROUP_HEADS):
                p = jnp.exp2(logits(j, h, diag)).astype(BF16)
                acc_scr[h] += _dot(values(j, h), p)

        def body(j, c):
            step(j, False)
            return c

        lax.fori_loop(0, qi, body, 0)
        step(qi, True)
        accs = [acc_scr[h] for h in range(GROUP_HEADS)]
    else:
        def step(j, carry, diag):
            out = []
            for h in range(GROUP_HEADS):
                out.append(_softmax_step(logits(j, h, diag), values(j, h), carry[h]))
            return tuple(out)

        carry = step(qi, tuple(_softmax_init(t) for _ in range(GROUP_HEADS)), True)
        carry = lax.fori_loop(0, qi, lambda j, c: step(j, c, False), carry)
        accs = [c[1] for c in carry]

    for h in range(GROUP_HEADS):
        a = accs[h]
        o_ref[:, h * dv:(h + 1) * dv] = (a[:dv] / a[dv:dv + 1]).T.astype(BF16)


def _softmax_attn(name, q_arr, k_arr, vt_arr, vt_group, batch, seq, bounded):
    t = TILE
    nq = seq // t
    n = batch * seq
    kern = functools.partial(_softmax_attn_kernel, t=t, bounded=bounded)
    if vt_group is None:
        vt_spec = pl.BlockSpec((None, nq, GROUP_WIDTH, t), lambda b, i: (b, 0, 0, 0))
    else:
        vt_spec = pl.BlockSpec((None, None, nq, GROUP_WIDTH, t), lambda b, i: (vt_group, b, 0, 0, 0))
    return pl.pallas_call(
        kern,
        grid=(batch, nq),
        in_specs=[
            pl.BlockSpec((t, GROUP_HEADS * WIDE_HEAD), lambda b, i: (b * nq + i, 0)),
            pl.BlockSpec((seq, GROUP_HEADS * WIDE_HEAD), lambda b, i: (b, 0)),
            vt_spec,
        ],
        out_specs=pl.BlockSpec((t, GROUP_WIDTH), lambda b, i: (b * nq + i, 0)),
        out_shape=jax.ShapeDtypeStruct((n, GROUP_WIDTH), BF16),
        scratch_shapes=[pltpu.VMEM((GROUP_HEADS, HEAD_DIM + SUM_ROWS, t), F32)],
        compiler_params=pltpu.CompilerParams(
            dimension_semantics=("parallel", "parallel"), vmem_limit_bytes=VMEM_LIMIT),
        name=name + ("" if bounded else "_online"),
    )(q_arr, k_arr, vt_arr)


def _bounded_or_online(bound, name, *args):
    return lax.cond(bound < MAX_SAFE_BOUND,
                    lambda: _softmax_attn(name, *args, bounded=True),
                    lambda: _softmax_attn(name, *args, bounded=False))


def _stickbreak_kernel(q_ref, k_ref, vt_ref, o_ref, acc_scr, rest_scr, *, t):
    qi = pl.program_id(1)
    key, qry = _key_query_iota(t)
    strict = key < qry
    half = t // 2
    kk, kj = _key_query_iota(half)
    later = jnp.where(kj > kk, 1.0, 0.0).astype(BF16)
    hd = HEAD_DIM
    acc_scr[...] = jnp.zeros_like(acc_scr)
    rest_scr[...] = jnp.zeros_like(rest_scr)

    def step(j, diag):
        off = pl.multiple_of(j * t, t)
        low = None
        for h in range(GROUP_HEADS):
            used = rest_scr[h]
            q = (q_ref[:, h * hd:(h + 1) * hd].astype(F32) * (SCALE * LOG2E)).astype(BF16)
            z = _dot_nt(k_ref[pl.ds(off, t), h * hd:(h + 1) * hd], q)
            drop = jnp.maximum(z, 0.0) + jnp.log(1.0 + jnp.exp2(-jnp.abs(z))) * LOG2E
            if diag:
                drop = jnp.where(strict, drop, 0.0)
            d_top, d_bot = drop[:half], drop[half:]
            ex_bot = _dot(later, d_bot.astype(BF16))
            tot_bot = ex_bot[0:1] + d_bot[0:1]
            ex_top = _dot(later, d_top.astype(BF16))
            tot_top = ex_top[0:1] + d_top[0:1]
            between = jnp.concatenate([ex_top + (tot_bot + used), ex_bot + used], axis=0)
            w = jnp.exp2((z - drop) - between)
            if diag:
                w = jnp.where(strict, w, 0.0)
            acc_scr[h] += _dot(vt_ref[j, h * hd:(h + 1) * hd, :], w.astype(BF16))
            used = used + tot_top + tot_bot
            rest_scr[h] = used
            low = used if low is None else jnp.minimum(low, used)
        return jnp.min(low)

    def cond(c):
        j, low = c
        return jnp.logical_and(j >= 0, low < STICK_DONE)

    def body(c):
        j, _ = c
        return j - 1, step(j, False)

    lax.while_loop(cond, body, (qi - 1, step(qi, True)))
    for h in range(GROUP_HEADS):
        o_ref[:, h * hd:(h + 1) * hd] = acc_scr[h].T.astype(BF16)


def _stickbreak_attn(proj, vt3, batch, seq):
    t = TILE
    nq = seq // t
    n = batch * seq
    kern = functools.partial(_stickbreak_kernel, t=t)
    return pl.pallas_call(
        kern,
        grid=(batch, nq),
        in_specs=[
            pl.BlockSpec((t, GROUP_WIDTH), lambda b, i: (b * nq + i, T_BQ)),
            pl.BlockSpec((seq, GROUP_WIDTH), lambda b, i: (b, T_BK)),
            pl.BlockSpec((None, None, nq, GROUP_WIDTH, t), lambda b, i: (1, b, 0, 0, 0)),
        ],
        out_specs=pl.BlockSpec((t, GROUP_WIDTH), lambda b, i: (b * nq + i, 0)),
        out_shape=jax.ShapeDtypeStruct((n, GROUP_WIDTH), BF16),
        scratch_shapes=[pltpu.VMEM((GROUP_HEADS, HEAD_DIM, t), F32), pltpu.VMEM((GROUP_HEADS, 1, t), F32)],
        compiler_params=pltpu.CompilerParams(
            dimension_semantics=("parallel", "parallel"), vmem_limit_bytes=VMEM_LIMIT),
        name="stickbreak_attn",
    )(proj, proj, vt3)


def _outproj_kernel(oa_ref, ob_ref, oc_ref, od_ref, g_ref, w_ref, x_ref, gf_ref, out_ref, hf_ref):
    acc = x_ref[...]
    for gi, o_ref in enumerate((oa_ref, ob_ref, oc_ref, od_ref)):
        y = _rms(o_ref[...].astype(F32), g_ref[gi:gi + 1, :]).astype(BF16)
        acc = acc + _dot(y, w_ref[gi * GROUP_WIDTH:(gi + 1) * GROUP_WIDTH, :])
    out_ref[...] = acc
    hf_ref[...] = _rms(acc, gf_ref[...]).astype(BF16)


def _outproj(oa, ob, oc, od, g, w, x2, g_ffn):
    n = x2.shape[0]
    tm = 512
    row = lambda i: (i, 0)
    const = lambda i: (0, 0)
    return pl.pallas_call(
        _outproj_kernel,
        grid=(n // tm,),
        in_specs=[pl.BlockSpec((tm, GROUP_WIDTH), row)] * 4 + [
            pl.BlockSpec((N_GROUPS, GROUP_WIDTH), const),
            pl.BlockSpec((N_GROUPS * GROUP_WIDTH, D_MODEL), const, pipeline_mode=pl.Buffered(1)),
            pl.BlockSpec((tm, D_MODEL), row),
            pl.BlockSpec((1, D_MODEL), const),
        ],
        out_specs=[pl.BlockSpec((tm, D_MODEL), row), pl.BlockSpec((tm, D_MODEL), row)],
        out_shape=[jax.ShapeDtypeStruct((n, D_MODEL), F32), jax.ShapeDtypeStruct((n, D_MODEL), BF16)],
        compiler_params=pltpu.CompilerParams(
            dimension_semantics=("parallel",), vmem_limit_bytes=VMEM_LIMIT),
        name="outproj",
    )(oa, ob, oc, od, g, w, x2, g_ffn)


def _ffn_up_kernel(h_ref, wg_ref, wv_ref, cwg_ref, cwv_ref, cbg_ref, cbv_ref,
                   out_ref, wg_scr, wv_scr, tail_g, tail_v, *, tiles_per_batch, tm):
    i = pl.program_id(1)

    @pl.when(i == 0)
    def _():
        wg_scr[...] = wg_ref[...].astype(BF16)
        wv_scr[...] = wv_ref[...].astype(BF16)

    @pl.when(i % tiles_per_batch == 0)
    def _():
        tail_g[...] = jnp.zeros_like(tail_g)
        tail_v[...] = jnp.zeros_like(tail_v)

    h = h_ref[...]

    def conv(w_scr, cw_ref, cb_ref, tail):
        u = _dot(h, w_scr[...])
        ue = jnp.concatenate([tail[...], u], axis=0)
        tail[...] = u[tm - 8:]
        y = cb_ref[...] + cw_ref[2:3, :] * u
        for back in range(1, CONV_WIDTH):
            y = y + cw_ref[CONV_WIDTH - 1 - back:CONV_WIDTH - back, :] * pltpu.roll(ue, back, axis=0)[8:]
        return y

    gate = conv(wg_scr, cwg_ref, cbg_ref, tail_g)
    val = conv(wv_scr, cwv_ref, cbv_ref, tail_v)
    out_ref[...] = (gate / (1.0 + jnp.exp(-gate)) * val).astype(BF16)


def _ffn_up(h, w_up, conv_w, conv_b, layer, seq):
    n = h.shape[0]
    tm = FFN_TOKENS
    tn = 512
    nj = D_FF // tn
    tpb = seq // tm
    kern = functools.partial(_ffn_up_kernel, tiles_per_batch=tpb, tm=tm)
    return pl.pallas_call(
        kern,
        grid=(nj, n // tm),
        in_specs=[
            pl.BlockSpec((tm, D_MODEL), lambda j, i: (i, 0)),
            pl.BlockSpec((None, D_MODEL, tn), lambda j, i: (layer, 0, j)),
            pl.BlockSpec((None, D_MODEL, tn), lambda j, i: (layer, 0, j + nj)),
            pl.BlockSpec((None, CONV_WIDTH, tn), lambda j, i: (layer, 0, j)),
            pl.BlockSpec((None, CONV_WIDTH, tn), lambda j, i: (layer, 0, j + nj)),
            pl.BlockSpec((None, 1, tn), lambda j, i: (layer, 0, j)),
            pl.BlockSpec((None, 1, tn), lambda j, i: (layer, 0, j + nj)),
        ],
        out_specs=pl.BlockSpec((tm, tn), lambda j, i: (i, j)),
        out_shape=jax.ShapeDtypeStruct((n, D_FF), BF16),
        scratch_shapes=[pltpu.VMEM((D_MODEL, tn), BF16), pltpu.VMEM((D_MODEL, tn), BF16),
                        pltpu.VMEM((8, tn), F32), pltpu.VMEM((8, tn), F32)],
        compiler_params=pltpu.CompilerParams(
            dimension_semantics=("arbitrary", "arbitrary"), vmem_limit_bytes=VMEM_LIMIT),
        name="ffn_up",
    )(h, w_up, w_up, conv_w, conv_w, conv_b, conv_b)


def _ffn_down_kernel(a_ref, w_ref, x_ref, out_ref):
    out_ref[...] = x_ref[...] + _dot(a_ref[...], w_ref[...])


def _ffn_down(act, w_down, x2):
    n = x2.shape[0]
    tm = FFN_TOKENS
    tn = 512
    return pl.pallas_call(
        _ffn_down_kernel,
        grid=(n // tm, D_MODEL // tn),
        in_specs=[
            pl.BlockSpec((tm, D_FF), lambda i, j: (i, 0)),
            pl.BlockSpec((D_FF, tn), lambda i, j: (0, j)),
            pl.BlockSpec((tm, tn), lambda i, j: (i, j)),
        ],
        out_specs=pl.BlockSpec((tm, tn), lambda i, j: (i, j)),
        out_shape=jax.ShapeDtypeStruct((n, D_MODEL), F32),
        compiler_params=pltpu.CompilerParams(
            dimension_semantics=("parallel", "parallel"), vmem_limit_bytes=VMEM_LIMIT),
        name="ffn_down",
    )(act, w_down, x2)


def _rope_tables(seq):
    def angles(dim):
        inv_freq = ROPE_THETA ** (-jnp.arange(0, dim, 2, dtype=F32) / dim)
        ang = jnp.arange(seq, dtype=F32)[:, None] * inv_freq[None, :]
        return jnp.cos(ang), jnp.sin(ang)

    c, s = angles(HEAD_DIM)
    cosc = jnp.concatenate([c, c], axis=1)
    sinc = jnp.concatenate([-s, s], axis=1)
    c, s = angles(MLA_ROPE_DIM)
    z = jnp.zeros_like(c)
    cosd = jnp.concatenate([c, z, c, z], axis=1)
    sind = jnp.concatenate([-s, z, s, z], axis=1)
    return cosc, sinc, cosd, sind


def _spread_rope(t):
    half = MLA_ROPE_DIM // 2
    z = jnp.zeros(t.shape[:-1] + (half,), t.dtype)
    return jnp.concatenate([t[..., :half], z, t[..., half:], z], axis=-1)


def _layout_w_in(w):
    gw = GROUP_WIDTH
    o = 0
    pieces = {}
    for name, width in (("aq", gw), ("ak", gw), ("av", gw), ("af", GROUP_HEADS), ("bq", gw), ("bk", gw),
                        ("bv", gw), ("cq", gw), ("ck", gw), ("cv", gw), ("dcq", MLA_Q_RANK),
                        ("dckv", MLA_KV_RANK), ("dkr", MLA_ROPE_DIM)):
        pieces[name] = w[:, o:o + width]
        o += width
    pad = jnp.zeros((w.shape[0], gw - MLA_KV_RANK - HEAD_DIM), w.dtype)
    tiles = [pieces["aq"], pieces["ak"], pieces["cq"], pieces["ck"], pieces["dcq"],
             pieces["dckv"], _spread_rope(pieces["dkr"]), pad, pieces["bq"], pieces["bk"]]
    w_main = jnp.concatenate(tiles, axis=1).astype(BF16)
    wvt = jnp.stack([pieces["av"].T, pieces["bv"].T, pieces["cv"].T]).astype(BF16)
    wf = jnp.zeros((8, w.shape[0]), F32).at[:GROUP_HEADS].set(pieces["af"].T).astype(BF16)
    return w_main, wvt, wf


def _layout_w_uq(w):
    w = w.reshape(MLA_Q_RANK, GROUP_HEADS, MLA_QK_DIM)
    w = jnp.concatenate([w[..., :MLA_NOPE_DIM], _spread_rope(w[..., MLA_NOPE_DIM:])], axis=-1)
    return w.reshape(MLA_Q_RANK, GROUP_HEADS * WIDE_HEAD).astype(BF16)


def _layout_w_ukv(w):
    w = w.reshape(MLA_KV_RANK, GROUP_HEADS, 2, HEAD_DIM)
    wk = w[:, :, 0, :].reshape(MLA_KV_RANK, GROUP_WIDTH).astype(BF16)
    wvt = w[:, :, 1, :].reshape(MLA_KV_RANK, GROUP_WIDTH).T.astype(BF16)
    return wk, wvt


def _logit_bound(gq, gk, dim, scale):
    return dim * scale * LOG2E * jnp.max(jnp.abs(gq)) * jnp.max(jnp.abs(gk)) * 1.01 + 0.5


def _layout_qk_gain(g):
    return jnp.concatenate([g[:MLA_NOPE_DIM], _spread_rope(g[MLA_NOPE_DIM:])])[None, :]


def kernel(x, attn_norm, w_in, b_forget, fox_q_norm, fox_k_norm, moba_q_norm, moba_k_norm, mla_cq_norm, mla_ckv_norm, w_uq, w_ukv, mla_q_norm, mla_k_norm, mix_out_norm, w_out, ffn_norm, w_up, conv_w, conv_b, w_down):
    batch, seq, d_model = x.shape
    depth = w_in.shape[0]
    assert d_model == D_MODEL and seq % TILE == 0 and seq % FFN_TOKENS == 0
    n = batch * seq
    x2 = x.reshape(n, d_model)
    cosc, sinc, cosd, sind = _rope_tables(seq)
    row = lambda v: v[None, :]

    for l in range(depth):
        w_main, wvt, wf = _layout_w_in(w_in[l])
        bf = jnp.zeros((8, 128), F32).at[:GROUP_HEADS, :].set(b_forget[l][:, None])
        proj, vt3, negc = _inproj(x2, row(attn_norm[l]), w_main, wvt, wf, bf, batch, seq)
        negc_col = negc.transpose(0, 2, 1).reshape(n, 8)

        bound_a = _logit_bound(fox_q_norm[l], fox_k_norm[l], HEAD_DIM, SCALE)
        bound_c = _logit_bound(moba_q_norm[l], moba_k_norm[l], HEAD_DIM, SCALE)
        bound_d = _logit_bound(mla_q_norm[l], mla_k_norm[l], MLA_QK_DIM, SCALE_MLA)
        big_c = jnp.maximum(MOBA_MIN_MASK, jnp.exp2(jnp.ceil(jnp.log2(4.0 * bound_c + 64.0))))
        bounds = jnp.zeros((8, HEAD_DIM), F32).at[:4].set(
            jnp.stack([bound_a, bound_c, big_c, bound_d])[:, None])

        wuk, wuvt = _layout_w_ukv(w_ukv[l])
        qa, ka, qc, kc, qd, kd, vtd = _prep(
            proj, negc_col, bounds, row(fox_q_norm[l]), row(fox_k_norm[l]), row(moba_q_norm[l]),
            row(moba_k_norm[l]), row(mla_cq_norm[l]), row(mla_ckv_norm[l]),
            _layout_qk_gain(mla_q_norm[l]), _layout_qk_gain(mla_k_norm[l]),
            _layout_w_uq(w_uq[l]), wuk, wuvt, cosc, sinc, cosd, sind, batch, seq)

        o_a = _bounded_or_online(bound_a, "fox_attn", qa, ka, vt3, 0, batch, seq)
        o_b = _stickbreak_attn(proj, vt3, batch, seq)
        o_c = _bounded_or_online(bound_c, "moba_attn", qc, kc, vt3, 2, batch, seq)
        o_d = _bounded_or_online(bound_d, "mla_attn", qd, kd, vtd, None, batch, seq)

        x2, h_ffn = _outproj(o_a, o_b, o_c, o_d, mix_out_norm[l].reshape(N_GROUPS, GROUP_WIDTH),
                             w_out[l].astype(BF16), x2, row(ffn_norm[l]))

        act = _ffn_up(h_ffn, w_up, conv_w, conv_b[:, None, :], l, seq)
        x2 = _ffn_down(act, w_down[l].astype(BF16), x2)

    return x2.reshape(batch, seq, d_model)
```

```python
import functools
import math

import jax
import jax.numpy as jnp
from jax import lax
from jax.experimental import pallas as pl
from jax.experimental.pallas import tpu as pltpu

F32 = jnp.float32
BF16 = jnp.bfloat16

D_MODEL = 2048
HEAD_DIM = 128
GROUP_HEADS = 4
GROUP_WIDTH = GROUP_HEADS * HEAD_DIM
N_GROUPS = 4
MOBA_BLOCK = 256
MOBA_TOP_K = 3
MOBA_MAX_BLOCKS = 16
MLA_Q_RANK = 512
MLA_KV_RANK = 256
MLA_NOPE_DIM = 128
MLA_ROPE_DIM = 64
MLA_QK_DIM = MLA_NOPE_DIM + MLA_ROPE_DIM
WIDE_HEAD = 256
D_FF = 5632
CONV_WIDTH = 3
ROPE_THETA = 10000.0
EPS = 1e-6

T_AQ, T_AK, T_CQ, T_CK, T_DCQ, T_DKV, T_BQ, T_BK = range(8)
PROJ_TILES = 8
V_GROUPS = 3

TILE = 512
NEG = -1e30
VMEM_LIMIT = 56 * 1024 * 1024
LOG2E = math.log2(math.e)
SCALE = HEAD_DIM ** -0.5
SCALE_MLA = MLA_QK_DIM ** -0.5
FFN_TOKENS = 1024
MXU_WIDTH = 256
SUM_ROWS = 16
MAX_SAFE_BOUND = 56.0
MOBA_MIN_MASK = 256.0
STICK_DONE = 160.0


def _dot(a, b):
    return jnp.dot(a, b, preferred_element_type=F32)


def _dot_nt(a, b):
    return lax.dot_general(a, b, (((1,), (1,)), ((), ())), preferred_element_type=F32)


def _split3(a):
    hi = a.astype(BF16)
    r1 = a - hi.astype(F32)
    mid = r1.astype(BF16)
    lo = (r1 - mid.astype(F32)).astype(BF16)
    return hi, mid, lo


def _log_sigmoid_pair(z):
    sp = jnp.log(1.0 + jnp.exp(-jnp.abs(z)))
    return jnp.minimum(z, 0.0) - sp, -jnp.maximum(z, 0.0) - sp


def _rms(v, g, denom=None):
    if denom is None:
        ms = jnp.mean(v * v, axis=-1, keepdims=True)
    else:
        ms = jnp.sum(v * v, axis=-1, keepdims=True) * (1.0 / denom)
    return v * lax.rsqrt(ms + EPS) * g


def _inproj_kernel(x_ref, g_ref, w_ref, wvt_ref, wf_ref, bf_ref, proj_ref, vt_ref, negc_ref,
                   carry_scr, *, tiles_per_batch, tm):
    i = pl.program_id(0)
    h = _rms(x_ref[...], g_ref[...]).astype(BF16)
    z = _dot_nt(wf_ref[...], h) + bf_ref[:, 0:1]
    logf, _ = _log_sigmoid_pair(z)
    r = lax.broadcasted_iota(jnp.int32, (tm, tm), 0)
    c = lax.broadcasted_iota(jnp.int32, (tm, tm), 1)
    tri = jnp.where(r <= c, 1.0, 0.0).astype(BF16)
    hi, mid, lo = _split3(logf)
    cum = _dot(hi, tri) + _dot(mid, tri) + _dot(lo, tri)

    @pl.when(i % tiles_per_batch == 0)
    def _():
        carry_scr[...] = jnp.zeros_like(carry_scr)

    cum = cum + carry_scr[:, 0:1]
    carry_scr[...] = jnp.broadcast_to(cum[:, tm - 1:tm], carry_scr.shape)
    negc_ref[...] = cum * (-LOG2E)

    gw = GROUP_WIDTH
    for jt in range(PROJ_TILES):
        proj_ref[:, jt * gw:(jt + 1) * gw] = _dot(h, w_ref[:, jt * gw:(jt + 1) * gw]).astype(BF16)
    for gi in range(V_GROUPS):
        vt_ref[gi] = _dot_nt(wvt_ref[gi], h).astype(BF16)


def _inproj(x2, g, w, wvt, wf, bf, batch, seq):
    n = x2.shape[0]
    tm = TILE
    tpb = seq // tm
    kern = functools.partial(_inproj_kernel, tiles_per_batch=tpb, tm=tm)
    const2 = lambda i: (0, 0)
    resident = pl.Buffered(1)
    return pl.pallas_call(
        kern,
        grid=(n // tm,),
        in_specs=[
            pl.BlockSpec((tm, D_MODEL), lambda i: (i, 0)),
            pl.BlockSpec((1, D_MODEL), const2),
            pl.BlockSpec((D_MODEL, PROJ_TILES * GROUP_WIDTH), const2, pipeline_mode=resident),
            pl.BlockSpec((V_GROUPS, GROUP_WIDTH, D_MODEL), lambda i: (0, 0, 0), pipeline_mode=resident),
            pl.BlockSpec((8, D_MODEL), const2),
            pl.BlockSpec((8, 128), const2),
        ],
        out_specs=[
            pl.BlockSpec((tm, PROJ_TILES * GROUP_WIDTH), lambda i: (i, 0)),
            pl.BlockSpec((V_GROUPS, None, None, GROUP_WIDTH, tm), lambda i: (0, i // tpb, i % tpb, 0, 0)),
            pl.BlockSpec((None, 8, tm), lambda i: (i // tpb, 0, i % tpb)),
        ],
        out_shape=[
            jax.ShapeDtypeStruct((n, PROJ_TILES * GROUP_WIDTH), BF16),
            jax.ShapeDtypeStruct((V_GROUPS, batch, tpb, GROUP_WIDTH, tm), BF16),
            jax.ShapeDtypeStruct((batch, 8, seq), F32),
        ],
        scratch_shapes=[pltpu.VMEM((8, 128), F32)],
        compiler_params=pltpu.CompilerParams(
            dimension_semantics=("arbitrary",), vmem_limit_bytes=VMEM_LIMIT),
        name="inproj",
    )(x2, g, w, wvt, wf, bf)


def _prep_kernel(a_ref, c_ref, d_ref, nc_ref, bnd_ref, gaq, gak, gcq, gck, gdcq, gdckv, gdq, gdk, wuq, wuk, wuvt,
                 cosc, sinc, cosd, sind,
                 qa_ref, ka_ref, qc_ref, kc_ref, qd_ref, kd_ref, vtd_ref, kmt_scr,
                 *, tiles_per_batch, tm):
    i = pl.program_id(0)
    t = i % tiles_per_batch
    hd = HEAD_DIM
    wh = WIDE_HEAD
    blocks_per_tile = tm // MOBA_BLOCK
    lane = lax.broadcasted_iota(jnp.int32, (tm, hd), 1)
    bound_a = bnd_ref[0:1, :]
    bound_c = bnd_ref[1:2, :]
    big_c = bnd_ref[2:3, :]
    bound_d = bnd_ref[3:4, :]

    nc = nc_ref[...]
    for h in range(GROUP_HEADS):
        q = a_ref[:, h * hd:(h + 1) * hd].astype(F32)
        qa_ref[:, h * wh:h * wh + hd] = (_rms(q, gaq[...]) * (SCALE * LOG2E)).astype(BF16)
        k = a_ref[:, GROUP_WIDTH + h * hd:GROUP_WIDTH + (h + 1) * hd].astype(F32)
        ka_ref[:, h * wh:h * wh + hd] = _rms(k, gak[...]).astype(BF16)
        hi, mid, lo = (p.astype(F32) for p in _split3(nc[:, h:h + 1]))
        k_aug = jnp.where(lane == 0, hi, jnp.where(lane == 1, mid, jnp.where(lane == 2, lo, jnp.where(
            lane < 6, 1.0, jnp.where(lane == 6, -bound_a, 0.0)))))
        q_aug = jnp.where(lane < 3, 1.0, jnp.where(lane == 3, -hi, jnp.where(lane == 4, -mid, jnp.where(
            lane == 5, -lo, jnp.where(lane == 6, 1.0, 0.0)))))
        ka_ref[:, h * wh + hd:(h + 1) * wh] = k_aug.astype(BF16)
        qa_ref[:, h * wh + hd:(h + 1) * wh] = q_aug.astype(BF16)

    @pl.when(i == 0)
    def _():
        kmt_scr[...] = jnp.zeros_like(kmt_scr)

    cc = cosc[...]
    sc = sinc[...]
    sub8 = lax.broadcasted_iota(jnp.int32, (8, hd), 0)
    key_blk = t * blocks_per_tile + lax.broadcasted_iota(jnp.int32, (tm, hd), 0) // MOBA_BLOCK

    def rope_c(v):
        return v * cc + pltpu.roll(v, hd // 2, axis=1) * sc

    kc_aug = jnp.where(lane == key_blk, big_c, jnp.where(lane == MOBA_MAX_BLOCKS, -(bound_c + big_c), 0.0))
    kc_aug = kc_aug.astype(BF16)
    for h in range(GROUP_HEADS):
        k = rope_c(_rms(c_ref[:, GROUP_WIDTH + h * hd:GROUP_WIDTH + (h + 1) * hd].astype(F32), gck[...]))
        kc_ref[:, h * wh:h * wh + hd] = k.astype(BF16)
        kc_ref[:, h * wh + hd:(h + 1) * wh] = kc_aug
        for blk in range(blocks_per_tile):
            km = jnp.mean(k[blk * MOBA_BLOCK:(blk + 1) * MOBA_BLOCK], axis=0, keepdims=True)
            nblk = t * blocks_per_tile + blk
            base = pl.multiple_of(h * hd + (nblk // 8) * 8, 8)
            old = kmt_scr[pl.ds(base, 8), h * hd:(h + 1) * hd]
            kmt_scr[pl.ds(base, 8), h * hd:(h + 1) * hd] = jnp.where(sub8 == nblk % 8, km, old)

    qs = []
    for h in range(GROUP_HEADS):
        q = rope_c(_rms(c_ref[:, h * hd:(h + 1) * hd].astype(F32), gcq[...])) * (SCALE * LOG2E)
        qc_ref[:, h * wh:h * wh + hd] = q.astype(BF16)
        qs.append(q)
    q_all = jnp.concatenate(qs, axis=1)
    qh, qm, _ = _split3(q_all)
    kh, km_, _ = _split3(kmt_scr[...])
    gate_t = _dot_nt(kh, qh) + _dot_nt(km_, qh) + _dot_nt(kh, qm)

    blk_row = lax.broadcasted_iota(jnp.int32, (hd, tm), 0)
    own = t * blocks_per_tile + lax.broadcasted_iota(jnp.int32, (hd, tm), 1) // MOBA_BLOCK
    blk_f = blk_row.astype(F32)
    ninf = jnp.float32(-jnp.inf)
    for h in range(GROUP_HEADS):
        g = jnp.where(blk_row < own, gate_t[h * hd:(h + 1) * hd, :], ninf)
        sel = jnp.where(blk_row == own, 1.0, 0.0)
        for _ in range(MOBA_TOP_K):
            mx = jnp.max(g, axis=0, keepdims=True)
            cand = jnp.where(g == mx, blk_f, 1e9)
            cand = jnp.where(mx > ninf, cand, 1e9)
            idx = jnp.min(cand, axis=0, keepdims=True)
            pick = blk_f == idx
            sel = jnp.where(pick, 1.0, sel)
            g = jnp.where(pick, ninf, g)
        q_aug = jnp.where(lane == MOBA_MAX_BLOCKS, 1.0, sel.T)
        qc_ref[:, h * wh + hd:(h + 1) * wh] = q_aug.astype(BF16)

    cd = cosd[...]
    sd = sind[...]
    free_lane = lane == MLA_ROPE_DIM // 2

    def rope_d(v):
        return v * cd + pltpu.roll(v, hd // 2, axis=1) * sd

    cq = _rms(d_ref[:, 0:MLA_Q_RANK].astype(F32), gdcq[...]).astype(BF16)
    qf = _dot(cq, wuq[...])
    for h in range(GROUP_HEADS):
        qn = _rms(qf[:, h * wh:(h + 1) * wh], gdq[...], denom=MLA_QK_DIM)
        qd_ref[:, h * wh:h * wh + hd] = (qn[:, :hd] * (SCALE_MLA * LOG2E)).astype(BF16)
        q_rot = rope_d(qn[:, hd:]) * (SCALE_MLA * LOG2E)
        qd_ref[:, h * wh + hd:(h + 1) * wh] = jnp.where(free_lane, 1.0, q_rot).astype(BF16)

    ckv = _rms(d_ref[:, MLA_Q_RANK:MLA_Q_RANK + MLA_KV_RANK].astype(F32), gdckv[...]).astype(BF16)
    knf = _dot(ckv, wuk[...])
    vtd_ref[...] = _dot_nt(wuvt[...], ckv).astype(BF16)
    kr = d_ref[:, MLA_Q_RANK + MLA_KV_RANK:MLA_Q_RANK + MLA_KV_RANK + hd].astype(F32)
    ss_r = jnp.sum(kr * kr, axis=-1, keepdims=True)
    gk = gdk[...]
    for h in range(GROUP_HEADS):
        kn = knf[:, h * hd:(h + 1) * hd]
        ms = (jnp.sum(kn * kn, axis=-1, keepdims=True) + ss_r) * (1.0 / MLA_QK_DIM)
        r = lax.rsqrt(ms + EPS)
        kd_ref[:, h * wh:h * wh + hd] = (kn * r * gk[:, :hd]).astype(BF16)
        k_rot = rope_d(kr * r * gk[:, hd:])
        kd_ref[:, h * wh + hd:(h + 1) * wh] = jnp.where(free_lane, -bound_d, k_rot).astype(BF16)


def _prep(proj, negc_col, bounds, gaq, gak, gcq, gck, gdcq, gdckv, gdq, gdk, wuq, wuk, wuvt,
          cosc, sinc, cosd, sind, batch, seq):
    n = proj.shape[0]
    tm = TILE
    tpb = seq // tm
    assert seq // MOBA_BLOCK <= MOBA_MAX_BLOCKS
    kern = functools.partial(_prep_kernel, tiles_per_batch=tpb, tm=tm)
    row = lambda i: (i, 0)
    const = lambda i: (0, 0)
    tab = lambda i: (i % tpb, 0)
    w2 = 2 * GROUP_WIDTH
    return pl.pallas_call(
        kern,
        grid=(n // tm,),
        in_specs=[
            pl.BlockSpec((tm, w2), lambda i: (i, 0)),
            pl.BlockSpec((tm, w2), lambda i: (i, 1)),
            pl.BlockSpec((tm, w2), lambda i: (i, 2)),
            pl.BlockSpec((tm, 8), row),
            pl.BlockSpec((8, HEAD_DIM), const),
            pl.BlockSpec((1, HEAD_DIM), const), pl.BlockSpec((1, HEAD_DIM), const),
            pl.BlockSpec((1, HEAD_DIM), const), pl.BlockSpec((1, HEAD_DIM), const),
            pl.BlockSpec((1, MLA_Q_RANK), const), pl.BlockSpec((1, MLA_KV_RANK), const),
            pl.BlockSpec((1, WIDE_HEAD), const), pl.BlockSpec((1, WIDE_HEAD), const),
            pl.BlockSpec((MLA_Q_RANK, GROUP_HEADS * WIDE_HEAD), const),
            pl.BlockSpec((MLA_KV_RANK, GROUP_WIDTH), const),
            pl.BlockSpec((GROUP_WIDTH, MLA_KV_RANK), const),
            pl.BlockSpec((tm, HEAD_DIM), tab), pl.BlockSpec((tm, HEAD_DIM), tab),
            pl.BlockSpec((tm, HEAD_DIM), tab), pl.BlockSpec((tm, HEAD_DIM), tab),
        ],
        out_specs=[pl.BlockSpec((tm, w2), row)] * 6 + [
            pl.BlockSpec((None, None, GROUP_WIDTH, tm), lambda i: (i // tpb, i % tpb, 0, 0)),
        ],
        out_shape=[jax.ShapeDtypeStruct((n, w2), BF16)] * 6 + [
            jax.ShapeDtypeStruct((batch, tpb, GROUP_WIDTH, tm), BF16),
        ],
        scratch_shapes=[pltpu.VMEM((GROUP_WIDTH, GROUP_WIDTH), F32)],
        compiler_params=pltpu.CompilerParams(
            dimension_semantics=("arbitrary",), vmem_limit_bytes=VMEM_LIMIT),
        name="prep",
    )(proj, proj, proj, negc_col, bounds, gaq, gak, gcq, gck, gdcq, gdckv, gdq, gdk, wuq, wuk, wuvt,
      cosc, sinc, cosd, sind)


def _key_query_iota(t):
    key = lax.broadcasted_iota(jnp.int32, (t, t), 0)
    qry = lax.broadcasted_iota(jnp.int32, (t, t), 1)
    return key, qry


def _softmax_step(s, vt, carry):
    m, acc = carry
    m_new = jnp.maximum(m, jnp.max(s, axis=0, keepdims=True))
    alpha = jnp.exp2(m - m_new)
    p = jnp.exp2(s - m_new)
    return m_new, alpha * acc + _dot(vt, p.astype(BF16))


def _softmax_init(t):
    return (jnp.full((1, t), NEG, F32), jnp.zeros((HEAD_DIM + SUM_ROWS, t), F32))


def _softmax_attn_kernel(q_ref, k_ref, vt_ref, o_ref, acc_scr, *, t, bounded):
    qi = pl.program_id(1)
    key, qry = _key_query_iota(t)
    causal = key <= qry
    dk = WIDE_HEAD
    dv = HEAD_DIM
    ones_rows = jnp.ones((SUM_ROWS, t), BF16)

    def logits(j, h, diag):
        off = pl.multiple_of(j * t, t)
        s = _dot_nt(k_ref[pl.ds(off, t), h * dk:(h + 1) * dk], q_ref[:, h * dk:(h + 1) * dk])
        return jnp.where(causal, s, NEG) if diag else s

    def values(j, h):
        return jnp.concatenate([vt_ref[j, h * dv:(h + 1) * dv, :], ones_rows], axis=0)

    if bounded:
        acc_scr[...] = jnp.zeros_like(acc_scr)

        def step(j, diag):
            for h in range(GROUP_HEADS):
                p = jnp.exp2(logits(j, h, diag)).astype(BF16)
                acc_scr[h] += _dot(values(j, h), p)

        def body(j, c):
            step(j, False)
            return c

        lax.fori_loop(0, qi, body, 0)
        step(qi, True)
        accs = [acc_scr[h] for h in range(GROUP_HEADS)]
    else:
        def step(j, carry, diag):
            out = []
            for h in range(GROUP_HEADS):
                out.append(_softmax_step(logits(j, h, diag), values(j, h), carry[h]))
            return tuple(out)

        carry = step(qi, tuple(_softmax_init(t) for _ in range(GROUP_HEADS)), True)
        carry = lax.fori_loop(0, qi, lambda j, c: step(j, c, False), carry)
        accs = [c[1] for c in carry]

    for h in range(GROUP_HEADS):
        a = accs[h]
        o_ref[:, h * dv:(h + 1) * dv] = (a[:dv] / a[dv:dv + 1]).T.astype(BF16)


def _softmax_attn(name, q_arr, k_arr, vt_arr, vt_group, batch, seq, bounded):
    t = TILE
    nq = seq // t
    n = batch * seq
    kern = functools.partial(_softmax_attn_kernel, t=t, bounded=bounded)
    if vt_group is None:
        vt_spec = pl.BlockSpec((None, nq, GROUP_WIDTH, t), lambda b, i: (b, 0, 0, 0))
    else:
        vt_spec = pl.BlockSpec((None, None, nq, GROUP_WIDTH, t), lambda b, i: (vt_group, b, 0, 0, 0))
    return pl.pallas_call(
        kern,
        grid=(batch, nq),
        in_specs=[
            pl.BlockSpec((t, GROUP_HEADS * WIDE_HEAD), lambda b, i: (b * nq + i, 0)),
            pl.BlockSpec((seq, GROUP_HEADS * WIDE_HEAD), lambda b, i: (b, 0)),
            vt_spec,
        ],
        out_specs=pl.BlockSpec((t, GROUP_WIDTH), lambda b, i: (b * nq + i, 0)),
        out_shape=jax.ShapeDtypeStruct((n, GROUP_WIDTH), BF16),
        scratch_shapes=[pltpu.VMEM((GROUP_HEADS, HEAD_DIM + SUM_ROWS, t), F32)],
        compiler_params=pltpu.CompilerParams(
            dimension_semantics=("parallel", "parallel"), vmem_limit_bytes=VMEM_LIMIT),
        name=name + ("" if bounded else "_online"),
    )(q_arr, k_arr, vt_arr)


def _bounded_or_online(bound, name, *args):
    return lax.cond(bound < MAX_SAFE_BOUND,
                    lambda: _softmax_attn(name, *args, bounded=True),
                    lambda: _softmax_attn(name, *args, bounded=False))


def _stickbreak_kernel(q_ref, k_ref, vt_ref, o_ref, acc_scr, rest_scr, *, t):
    qi = pl.program_id(1)
    key, qry = _key_query_iota(t)
    strict = key < qry
    half = t // 2
    kk, kj = _key_query_iota(half)
    later = jnp.where(kj > kk, 1.0, 0.0).astype(BF16)
    hd = HEAD_DIM
    acc_scr[...] = jnp.zeros_like(acc_scr)
    rest_scr[...] = jnp.zeros_like(rest_scr)

    def step(j, diag):
        off = pl.multiple_of(j * t, t)
        low = None
        for h in range(GROUP_HEADS):
            used = rest_scr[h]
            q = (q_ref[:, h * hd:(h + 1) * hd].astype(F32) * (SCALE * LOG2E)).astype(BF16)
            z = _dot_nt(k_ref[pl.ds(off, t), h * hd:(h + 1) * hd], q)
            drop = jnp.maximum(z, 0.0) + jnp.log(1.0 + jnp.exp2(-jnp.abs(z))) * LOG2E
            if diag:
                drop = jnp.where(strict, drop, 0.0)
            d_top, d_bot = drop[:half], drop[half:]
            ex_bot = _dot(later, d_bot.astype(BF16))
            tot_bot = ex_bot[0:1] + d_bot[0:1]
            ex_top = _dot(later, d_top.astype(BF16))
            tot_top = ex_top[0:1] + d_top[0:1]
            between = jnp.concatenate([ex_top + (tot_bot + used), ex_bot + used], axis=0)
            w = jnp.exp2((z - drop) - between)
            if diag:
                w = jnp.where(strict, w, 0.0)
            acc_scr[h] += _dot(vt_ref[j, h * hd:(h + 1) * hd, :], w.astype(BF16))
            used = used + tot_top + tot_bot
            rest_scr[h] = used
            low = used if low is None else jnp.minimum(low, used)
        return jnp.min(low)

    def cond(c):
        j, low = c
        return jnp.logical_and(j >= 0, low < STICK_DONE)

    def body(c):
        j, _ = c
        return j - 1, step(j, False)

    lax.while_loop(cond, body, (qi - 1, step(qi, True)))
    for h in range(GROUP_HEADS):
        o_ref[:, h * hd:(h + 1) * hd] = acc_scr[h].T.astype(BF16)


def _stickbreak_attn(proj, vt3, batch, seq):
    t = TILE
    nq = seq // t
    n = batch * seq
    kern = functools.partial(_stickbreak_kernel, t=t)
    return pl.pallas_call(
        kern,
        grid=(batch, nq),
        in_specs=[
            pl.BlockSpec((t, GROUP_WIDTH), lambda b, i: (b * nq + i, T_BQ)),
            pl.BlockSpec((seq, GROUP_WIDTH), lambda b, i: (b, T_BK)),
            pl.BlockSpec((None, None, nq, GROUP_WIDTH, t), lambda b, i: (1, b, 0, 0, 0)),
        ],
        out_specs=pl.BlockSpec((t, GROUP_WIDTH), lambda b, i: (b * nq + i, 0)),
        out_shape=jax.ShapeDtypeStruct((n, GROUP_WIDTH), BF16),
        scratch_shapes=[pltpu.VMEM((GROUP_HEADS, HEAD_DIM, t), F32), pltpu.VMEM((GROUP_HEADS, 1, t), F32)],
        compiler_params=pltpu.CompilerParams(
            dimension_semantics=("parallel", "parallel"), vmem_limit_bytes=VMEM_LIMIT),
        name="stickbreak_attn",
    )(proj, proj, vt3)


def _outproj_kernel(oa_ref, ob_ref, oc_ref, od_ref, g_ref, w_ref, x_ref, gf_ref, out_ref, hf_ref):
    acc = x_ref[...]
    for gi, o_ref in enumerate((oa_ref, ob_ref, oc_ref, od_ref)):
        y = _rms(o_ref[...].astype(F32), g_ref[gi:gi + 1, :]).astype(BF16)
        acc = acc + _dot(y, w_ref[gi * GROUP_WIDTH:(gi + 1) * GROUP_WIDTH, :])
    out_ref[...] = acc
    hf_ref[...] = _rms(acc, gf_ref[...]).astype(BF16)


def _outproj(oa, ob, oc, od, g, w, x2, g_ffn):
    n = x2.shape[0]
    tm = 512
    row = lambda i: (i, 0)
    const = lambda i: (0, 0)
    return pl.pallas_call(
        _outproj_kernel,
        grid=(n // tm,),
        in_specs=[pl.BlockSpec((tm, GROUP_WIDTH), row)] * 4 + [
            pl.BlockSpec((N_GROUPS, GROUP_WIDTH), const),
            pl.BlockSpec((N_GROUPS * GROUP_WIDTH, D_MODEL), const, pipeline_mode=pl.Buffered(1)),
            pl.BlockSpec((tm, D_MODEL), row),
            pl.BlockSpec((1, D_MODEL), const),
        ],
        out_specs=[pl.BlockSpec((tm, D_MODEL), row), pl.BlockSpec((tm, D_MODEL), row)],
        out_shape=[jax.ShapeDtypeStruct((n, D_MODEL), F32), jax.ShapeDtypeStruct((n, D_MODEL), BF16)],
        compiler_params=pltpu.CompilerParams(
            dimension_semantics=("parallel",), vmem_limit_bytes=VMEM_LIMIT),
        name="outproj",
    )(oa, ob, oc, od, g, w, x2, g_ffn)


def _ffn_up_kernel(h_ref, wg_ref, wv_ref, cwg_ref, cwv_ref, cbg_ref, cbv_ref,
                   out_ref, wg_scr, wv_scr, tail_g, tail_v, *, tiles_per_batch, tm):
    i = pl.program_id(1)

    @pl.when(i == 0)
    def _():
        wg_scr[...] = wg_ref[...].astype(BF16)
        wv_scr[...] = wv_ref[...].astype(BF16)

    @pl.when(i % tiles_per_batch == 0)
    def _():
        tail_g[...] = jnp.zeros_like(tail_g)
        tail_v[...] = jnp.zeros_like(tail_v)

    def conv(h, w_scr, cw_ref, cb_ref, prev):
        u = _dot(h, w_scr[...])
        ue = jnp.concatenate([prev, u], axis=0)
        y = cb_ref[...] + cw_ref[2:3, :] * u
        for back in range(1, CONV_WIDTH):
            y = y + cw_ref[CONV_WIDTH - 1 - back:CONV_WIDTH - back, :] * pltpu.roll(ue, back, axis=0)[8:]
        return y, u[u.shape[0] - 8:]

    h = h_ref[...]
    gate, tail_g[...] = conv(h, wg_scr, cwg_ref, cbg_ref, tail_g[...])
    val, tail_v[...] = conv(h, wv_scr, cwv_ref, cbv_ref, tail_v[...])
    out_ref[...] = (gate / (1.0 + jnp.exp(-gate)) * val).astype(BF16)


def _ffn_up(h, w_up, conv_w, conv_b, layer, seq):
    n = h.shape[0]
    tm = FFN_TOKENS
    tn = 512
    nj = D_FF // tn
    tpb = seq // tm
    kern = functools.partial(_ffn_up_kernel, tiles_per_batch=tpb, tm=tm)
    return pl.pallas_call(
        kern,
        grid=(nj, n // tm),
        in_specs=[
            pl.BlockSpec((tm, D_MODEL), lambda j, i: (i, 0)),
            pl.BlockSpec((None, D_MODEL, tn), lambda j, i: (layer, 0, j)),
            pl.BlockSpec((None, D_MODEL, tn), lambda j, i: (layer, 0, j + nj)),
            pl.BlockSpec((None, CONV_WIDTH, tn), lambda j, i: (layer, 0, j)),
            pl.BlockSpec((None, CONV_WIDTH, tn), lambda j, i: (layer, 0, j + nj)),
            pl.BlockSpec((None, 1, tn), lambda j, i: (layer, 0, j)),
            pl.BlockSpec((None, 1, tn), lambda j, i: (layer, 0, j + nj)),
        ],
        out_specs=pl.BlockSpec((tm, tn), lambda j, i: (i, j)),
        out_shape=jax.ShapeDtypeStruct((n, D_FF), BF16),
        scratch_shapes=[pltpu.VMEM((D_MODEL, tn), BF16), pltpu.VMEM((D_MODEL, tn), BF16),
                        pltpu.VMEM((8, tn), F32), pltpu.VMEM((8, tn), F32)],
        compiler_params=pltpu.CompilerParams(
            dimension_semantics=("arbitrary", "arbitrary"), vmem_limit_bytes=VMEM_LIMIT),
        name="ffn_up",
    )(h, w_up, w_up, conv_w, conv_w, conv_b, conv_b)


def _ffn_down_kernel(a_ref, w_ref, x_ref, out_ref):
    a = a_ref[...]
    tn = 2 * MXU_WIDTH
    for c in range(D_MODEL // tn):
        cols = slice(c * tn, (c + 1) * tn)
        out_ref[:, cols] = x_ref[:, cols] + _dot(a, w_ref[:, cols])


def _ffn_down(act, w_down, x2):
    n = x2.shape[0]
    tm = 512
    row = lambda i: (i, 0)
    return pl.pallas_call(
        _ffn_down_kernel,
        grid=(n // tm,),
        in_specs=[
            pl.BlockSpec((tm, D_FF), row),
            pl.BlockSpec((D_FF, D_MODEL), lambda i: (0, 0), pipeline_mode=pl.Buffered(1)),
            pl.BlockSpec((tm, D_MODEL), row),
        ],
        out_specs=pl.BlockSpec((tm, D_MODEL), row),
        out_shape=jax.ShapeDtypeStruct((n, D_MODEL), F32),
        compiler_params=pltpu.CompilerParams(
            dimension_semantics=("parallel",), vmem_limit_bytes=VMEM_LIMIT),
        name="ffn_down",
    )(act, w_down, x2)


def _rope_tables(seq):
    def angles(dim):
        inv_freq = ROPE_THETA ** (-jnp.arange(0, dim, 2, dtype=F32) / dim)
        ang = jnp.arange(seq, dtype=F32)[:, None] * inv_freq[None, :]
        return jnp.cos(ang), jnp.sin(ang)

    c, s = angles(HEAD_DIM)
    cosc = jnp.concatenate([c, c], axis=1)
    sinc = jnp.concatenate([-s, s], axis=1)
    c, s = angles(MLA_ROPE_DIM)
    z = jnp.zeros_like(c)
    cosd = jnp.concatenate([c, z, c, z], axis=1)
    sind = jnp.concatenate([-s, z, s, z], axis=1)
    return cosc, sinc, cosd, sind


def _spread_rope(t):
    half = MLA_ROPE_DIM // 2
    z = jnp.zeros(t.shape[:-1] + (half,), t.dtype)
    return jnp.concatenate([t[..., :half], z, t[..., half:], z], axis=-1)


def _layout_w_in(w):
    gw = GROUP_WIDTH
    o = 0
    pieces = {}
    for name, width in (("aq", gw), ("ak", gw), ("av", gw), ("af", GROUP_HEADS), ("bq", gw), ("bk", gw),
                        ("bv", gw), ("cq", gw), ("ck", gw), ("cv", gw), ("dcq", MLA_Q_RANK),
                        ("dckv", MLA_KV_RANK), ("dkr", MLA_ROPE_DIM)):
        pieces[name] = w[:, o:o + width]
        o += width
    pad = jnp.zeros((w.shape[0], gw - MLA_KV_RANK - HEAD_DIM), w.dtype)
    tiles = [pieces["aq"], pieces["ak"], pieces["cq"], pieces["ck"], pieces["dcq"],
             pieces["dckv"], _spread_rope(pieces["dkr"]), pad, pieces["bq"], pieces["bk"]]
    w_main = jnp.concatenate(tiles, axis=1).astype(BF16)
    wvt = jnp.stack([pieces["av"].T, pieces["bv"].T, pieces["cv"].T]).astype(BF16)
    wf = jnp.zeros((8, w.shape[0]), F32).at[:GROUP_HEADS].set(pieces["af"].T).astype(BF16)
    return w_main, wvt, wf


def _layout_w_uq(w):
    w = w.reshape(MLA_Q_RANK, GROUP_HEADS, MLA_QK_DIM)
    w = jnp.concatenate([w[..., :MLA_NOPE_DIM], _spread_rope(w[..., MLA_NOPE_DIM:])], axis=-1)
    return w.reshape(MLA_Q_RANK, GROUP_HEADS * WIDE_HEAD).astype(BF16)


def _layout_w_ukv(w):
    w = w.reshape(MLA_KV_RANK, GROUP_HEADS, 2, HEAD_DIM)
    wk = w[:, :, 0, :].reshape(MLA_KV_RANK, GROUP_WIDTH).astype(BF16)
    wvt = w[:, :, 1, :].reshape(MLA_KV_RANK, GROUP_WIDTH).T.astype(BF16)
    return wk, wvt


def _logit_bound(gq, gk, dim, scale):
    return dim * scale * LOG2E * jnp.max(jnp.abs(gq)) * jnp.max(jnp.abs(gk)) * 1.01 + 0.5


def _layout_qk_gain(g):
    return jnp.concatenate([g[:MLA_NOPE_DIM], _spread_rope(g[MLA_NOPE_DIM:])])[None, :]


def kernel(x, attn_norm, w_in, b_forget, fox_q_norm, fox_k_norm, moba_q_norm, moba_k_norm, mla_cq_norm, mla_ckv_norm, w_uq, w_ukv, mla_q_norm, mla_k_norm, mix_out_norm, w_out, ffn_norm, w_up, conv_w, conv_b, w_down):
    batch, seq, d_model = x.shape
    depth = w_in.shape[0]
    assert d_model == D_MODEL and seq % TILE == 0 and seq % FFN_TOKENS == 0
    n = batch * seq
    x2 = x.reshape(n, d_model)
    cosc, sinc, cosd, sind = _rope_tables(seq)
    row = lambda v: v[None, :]

    for l in range(depth):
        w_main, wvt, wf = _layout_w_in(w_in[l])
        bf = jnp.zeros((8, 128), F32).at[:GROUP_HEADS, :].set(b_forget[l][:, None])
        proj, vt3, negc = _inproj(x2, row(attn_norm[l]), w_main, wvt, wf, bf, batch, seq)
        negc_col = negc.transpose(0, 2, 1).reshape(n, 8)

        bound_a = _logit_bound(fox_q_norm[l], fox_k_norm[l], HEAD_DIM, SCALE)
        bound_c = _logit_bound(moba_q_norm[l], moba_k_norm[l], HEAD_DIM, SCALE)
        bound_d = _logit_bound(mla_q_norm[l], mla_k_norm[l], MLA_QK_DIM, SCALE_MLA)
        big_c = jnp.maximum(MOBA_MIN_MASK, jnp.exp2(jnp.ceil(jnp.log2(4.0 * bound_c + 64.0))))
        bounds = jnp.zeros((8, HEAD_DIM), F32).at[:4].set(
            jnp.stack([bound_a, bound_c, big_c, bound_d])[:, None])

        wuk, wuvt = _layout_w_ukv(w_ukv[l])
        qa, ka, qc, kc, qd, kd, vtd = _prep(
            proj, negc_col, bounds, row(fox_q_norm[l]), row(fox_k_norm[l]), row(moba_q_norm[l]),
            row(moba_k_norm[l]), row(mla_cq_norm[l]), row(mla_ckv_norm[l]),
            _layout_qk_gain(mla_q_norm[l]), _layout_qk_gain(mla_k_norm[l]),
            _layout_w_uq(w_uq[l]), wuk, wuvt, cosc, sinc, cosd, sind, batch, seq)

        o_a = _bounded_or_online(bound_a, "fox_attn", qa, ka, vt3, 0, batch, seq)
        o_b = _stickbreak_attn(proj, vt3, batch, seq)
        o_c = _bounded_or_online(bound_c, "moba_attn", qc, kc, vt3, 2, batch, seq)
        o_d = _bounded_or_online(bound_d, "mla_attn", qd, kd, vtd, None, batch, seq)

        x2, h_ffn = _outproj(o_a, o_b, o_c, o_d, mix_out_norm[l].reshape(N_GROUPS, GROUP_WIDTH),
                             w_out[l].astype(BF16), x2, row(ffn_norm[l]))

        act = _ffn_up(h_ffn, w_up, conv_w, conv_b[:, None, :], l, seq)
        x2 = _ffn_down(act, w_down[l].astype(BF16), x2)

    return x2.reshape(batch, seq, d_model)
```

```python
import functools
import math

import jax
import jax.numpy as jnp
from jax import lax
from jax.experimental import pallas as pl
from jax.experimental.pallas import tpu as pltpu

F32 = jnp.float32
BF16 = jnp.bfloat16

D_MODEL = 2048
HEAD_DIM = 128
GROUP_HEADS = 4
GROUP_WIDTH = GROUP_HEADS * HEAD_DIM
N_GROUPS = 4
MOBA_BLOCK = 256
MOBA_TOP_K = 3
MOBA_MAX_BLOCKS = 16
MLA_Q_RANK = 512
MLA_KV_RANK = 256
MLA_NOPE_DIM = 128
MLA_ROPE_DIM = 64
MLA_QK_DIM = MLA_NOPE_DIM + MLA_ROPE_DIM
WIDE_HEAD = 256
D_FF = 5632
CONV_WIDTH = 3
ROPE_THETA = 10000.0
EPS = 1e-6

T_AQ, T_AK, T_CQ, T_CK, T_DCQ, T_DKV, T_BQ, T_BK = range(8)
PROJ_TILES = 8
V_GROUPS = 3

TILE = 512
NEG = -1e30
VMEM_LIMIT = 56 * 1024 * 1024
LOG2E = math.log2(math.e)
SCALE = HEAD_DIM ** -0.5
SCALE_MLA = MLA_QK_DIM ** -0.5
FFN_TOKENS = 1024
MXU_WIDTH = 256
SUM_ROWS = 16
MAX_SAFE_BOUND = 56.0
MOBA_MIN_MASK = 256.0
STICK_DONE = 160.0


def _dot(a, b):
    return jnp.dot(a, b, preferred_element_type=F32)


def _dot_nt(a, b):
    return lax.dot_general(a, b, (((1,), (1,)), ((), ())), preferred_element_type=F32)


def _split3(a):
    hi = a.astype(BF16)
    r1 = a - hi.astype(F32)
    mid = r1.astype(BF16)
    lo = (r1 - mid.astype(F32)).astype(BF16)
    return hi, mid, lo


def _log_sigmoid_pair(z):
    sp = jnp.log(1.0 + jnp.exp(-jnp.abs(z)))
    return jnp.minimum(z, 0.0) - sp, -jnp.maximum(z, 0.0) - sp


def _rms(v, g, denom=None):
    if denom is None:
        ms = jnp.mean(v * v, axis=-1, keepdims=True)
    else:
        ms = jnp.sum(v * v, axis=-1, keepdims=True) * (1.0 / denom)
    return v * lax.rsqrt(ms + EPS) * g


def _inproj_kernel(x_ref, g_ref, w_ref, wvt_ref, wf_ref, bf_ref, proj_ref, vt_ref, negc_ref,
                   carry_scr, *, tiles_per_batch, tm):
    i = pl.program_id(0)
    h = _rms(x_ref[...], g_ref[...]).astype(BF16)
    z = _dot_nt(wf_ref[...], h) + bf_ref[:, 0:1]
    logf, _ = _log_sigmoid_pair(z)
    r = lax.broadcasted_iota(jnp.int32, (tm, tm), 0)
    c = lax.broadcasted_iota(jnp.int32, (tm, tm), 1)
    tri = jnp.where(r <= c, 1.0, 0.0).astype(BF16)
    hi, mid, lo = _split3(logf)
    cum = _dot(hi, tri) + _dot(mid, tri) + _dot(lo, tri)

    @pl.when(i % tiles_per_batch == 0)
    def _():
        carry_scr[...] = jnp.zeros_like(carry_scr)

    cum = cum + carry_scr[:, 0:1]
    carry_scr[...] = jnp.broadcast_to(cum[:, tm - 1:tm], carry_scr.shape)
    negc_ref[...] = cum * (-LOG2E)

    gw = GROUP_WIDTH
    for jt in range(PROJ_TILES):
        proj_ref[:, jt * gw:(jt + 1) * gw] = _dot(h, w_ref[:, jt * gw:(jt + 1) * gw]).astype(BF16)
    for gi in range(V_GROUPS):
        vt_ref[gi] = _dot_nt(wvt_ref[gi], h).astype(BF16)


def _inproj(x2, g, w, wvt, wf, bf, batch, seq):
    n = x2.shape[0]
    tm = TILE
    tpb = seq // tm
    kern = functools.partial(_inproj_kernel, tiles_per_batch=tpb, tm=tm)
    const2 = lambda i: (0, 0)
    resident = pl.Buffered(1)
    return pl.pallas_call(
        kern,
        grid=(n // tm,),
        in_specs=[
            pl.BlockSpec((tm, D_MODEL), lambda i: (i, 0)),
            pl.BlockSpec((1, D_MODEL), const2),
            pl.BlockSpec((D_MODEL, PROJ_TILES * GROUP_WIDTH), const2, pipeline_mode=resident),
            pl.BlockSpec((V_GROUPS, GROUP_WIDTH, D_MODEL), lambda i: (0, 0, 0), pipeline_mode=resident),
            pl.BlockSpec((8, D_MODEL), const2),
            pl.BlockSpec((8, 128), const2),
        ],
        out_specs=[
            pl.BlockSpec((tm, PROJ_TILES * GROUP_WIDTH), lambda i: (i, 0)),
            pl.BlockSpec((V_GROUPS, None, None, GROUP_WIDTH, tm), lambda i: (0, i // tpb, i % tpb, 0, 0)),
            pl.BlockSpec((None, 8, tm), lambda i: (i // tpb, 0, i % tpb)),
        ],
        out_shape=[
            jax.ShapeDtypeStruct((n, PROJ_TILES * GROUP_WIDTH), BF16),
            jax.ShapeDtypeStruct((V_GROUPS, batch, tpb, GROUP_WIDTH, tm), BF16),
            jax.ShapeDtypeStruct((batch, 8, seq), F32),
        ],
        scratch_shapes=[pltpu.VMEM((8, 128), F32)],
        compiler_params=pltpu.CompilerParams(
            dimension_semantics=("arbitrary",), vmem_limit_bytes=VMEM_LIMIT),
        name="inproj",
    )(x2, g, w, wvt, wf, bf)


def _prep_kernel(a_ref, c_ref, d_ref, nc_ref, bnd_ref, gaq, gak, gcq, gck, gdcq, gdckv, gdq, gdk, wuq, wuk, wuvt,
                 cosc, sinc, cosd, sind,
                 qa_ref, ka_ref, qc_ref, kc_ref, qd_ref, kd_ref, vtd_ref, kmt_scr,
                 *, tiles_per_batch, tm):
    i = pl.program_id(0)
    t = i % tiles_per_batch
    hd = HEAD_DIM
    wh = WIDE_HEAD
    blocks_per_tile = tm // MOBA_BLOCK
    lane = lax.broadcasted_iota(jnp.int32, (tm, hd), 1)
    bound_a = bnd_ref[0:1, :]
    bound_c = bnd_ref[1:2, :]
    big_c = bnd_ref[2:3, :]
    bound_d = bnd_ref[3:4, :]

    nc = nc_ref[...]
    for h in range(GROUP_HEADS):
        q = a_ref[:, h * hd:(h + 1) * hd].astype(F32)
        qa_ref[:, h * wh:h * wh + hd] = (_rms(q, gaq[...]) * (SCALE * LOG2E)).astype(BF16)
        k = a_ref[:, GROUP_WIDTH + h * hd:GROUP_WIDTH + (h + 1) * hd].astype(F32)
        ka_ref[:, h * wh:h * wh + hd] = _rms(k, gak[...]).astype(BF16)
        hi, mid, lo = (p.astype(F32) for p in _split3(nc[:, h:h + 1]))
        k_aug = jnp.where(lane == 0, hi, jnp.where(lane == 1, mid, jnp.where(lane == 2, lo, jnp.where(
            lane < 6, 1.0, jnp.where(lane == 6, -bound_a, 0.0)))))
        q_aug = jnp.where(lane < 3, 1.0, jnp.where(lane == 3, -hi, jnp.where(lane == 4, -mid, jnp.where(
            lane == 5, -lo, jnp.where(lane == 6, 1.0, 0.0)))))
        ka_ref[:, h * wh + hd:(h + 1) * wh] = k_aug.astype(BF16)
        qa_ref[:, h * wh + hd:(h + 1) * wh] = q_aug.astype(BF16)

    @pl.when(i == 0)
    def _():
        kmt_scr[...] = jnp.zeros_like(kmt_scr)

    cc = cosc[...]
    sc = sinc[...]
    sub8 = lax.broadcasted_iota(jnp.int32, (8, hd), 0)
    key_blk = t * blocks_per_tile + lax.broadcasted_iota(jnp.int32, (tm, hd), 0) // MOBA_BLOCK

    def rope_c(v):
        return v * cc + pltpu.roll(v, hd // 2, axis=1) * sc

    kc_aug = jnp.where(lane == key_blk, big_c, jnp.where(lane == MOBA_MAX_BLOCKS, -(bound_c + big_c), 0.0))
    kc_aug = kc_aug.astype(BF16)
    for h in range(GROUP_HEADS):
        k = rope_c(_rms(c_ref[:, GROUP_WIDTH + h * hd:GROUP_WIDTH + (h + 1) * hd].astype(F32), gck[...]))
        kc_ref[:, h * wh:h * wh + hd] = k.astype(BF16)
        kc_ref[:, h * wh + hd:(h + 1) * wh] = kc_aug
        for blk in range(blocks_per_tile):
            km = jnp.mean(k[blk * MOBA_BLOCK:(blk + 1) * MOBA_BLOCK], axis=0, keepdims=True)
            nblk = t * blocks_per_tile + blk
            base = pl.multiple_of(h * hd + (nblk // 8) * 8, 8)
            old = kmt_scr[pl.ds(base, 8), h * hd:(h + 1) * hd]
            kmt_scr[pl.ds(base, 8), h * hd:(h + 1) * hd] = jnp.where(sub8 == nblk % 8, km, old)

    qs = []
    for h in range(GROUP_HEADS):
        q = rope_c(_rms(c_ref[:, h * hd:(h + 1) * hd].astype(F32), gcq[...])) * (SCALE * LOG2E)
        qc_ref[:, h * wh:h * wh + hd] = q.astype(BF16)
        qs.append(q)
    q_all = jnp.concatenate(qs, axis=1)
    qh, qm, _ = _split3(q_all)
    kh, km_, _ = _split3(kmt_scr[...])
    gate_t = _dot_nt(kh, qh) + _dot_nt(km_, qh) + _dot_nt(kh, qm)

    blk_row = lax.broadcasted_iota(jnp.int32, (hd, tm), 0)
    own = t * blocks_per_tile + lax.broadcasted_iota(jnp.int32, (hd, tm), 1) // MOBA_BLOCK
    blk_f = blk_row.astype(F32)
    ninf = jnp.float32(-jnp.inf)
    for h in range(GROUP_HEADS):
        g = jnp.where(blk_row < own, gate_t[h * hd:(h + 1) * hd, :], ninf)
        sel = jnp.where(blk_row == own, 1.0, 0.0)
        for _ in range(MOBA_TOP_K):
            mx = jnp.max(g, axis=0, keepdims=True)
            cand = jnp.where(g == mx, blk_f, 1e9)
            cand = jnp.where(mx > ninf, cand, 1e9)
            idx = jnp.min(cand, axis=0, keepdims=True)
            pick = blk_f == idx
            sel = jnp.where(pick, 1.0, sel)
            g = jnp.where(pick, ninf, g)
        q_aug = jnp.where(lane == MOBA_MAX_BLOCKS, 1.0, sel.T)
        qc_ref[:, h * wh + hd:(h + 1) * wh] = q_aug.astype(BF16)

    cd = cosd[...]
    sd = sind[...]
    free_lane = lane == MLA_ROPE_DIM // 2

    def rope_d(v):
        return v * cd + pltpu.roll(v, hd // 2, axis=1) * sd

    cq = _rms(d_ref[:, 0:MLA_Q_RANK].astype(F32), gdcq[...]).astype(BF16)
    qf = _dot(cq, wuq[...])
    for h in range(GROUP_HEADS):
        qn = _rms(qf[:, h * wh:(h + 1) * wh], gdq[...], denom=MLA_QK_DIM)
        qd_ref[:, h * wh:h * wh + hd] = (qn[:, :hd] * (SCALE_MLA * LOG2E)).astype(BF16)
        q_rot = rope_d(qn[:, hd:]) * (SCALE_MLA * LOG2E)
        qd_ref[:, h * wh + hd:(h + 1) * wh] = jnp.where(free_lane, 1.0, q_rot).astype(BF16)

    ckv = _rms(d_ref[:, MLA_Q_RANK:MLA_Q_RANK + MLA_KV_RANK].astype(F32), gdckv[...]).astype(BF16)
    knf = _dot(ckv, wuk[...])
    vtd_ref[...] = _dot_nt(wuvt[...], ckv).astype(BF16)
    kr = d_ref[:, MLA_Q_RANK + MLA_KV_RANK:MLA_Q_RANK + MLA_KV_RANK + hd].astype(F32)
    ss_r = jnp.sum(kr * kr, axis=-1, keepdims=True)
    gk = gdk[...]
    for h in range(GROUP_HEADS):
        kn = knf[:, h * hd:(h + 1) * hd]
        ms = (jnp.sum(kn * kn, axis=-1, keepdims=True) + ss_r) * (1.0 / MLA_QK_DIM)
        r = lax.rsqrt(ms + EPS)
        kd_ref[:, h * wh:h * wh + hd] = (kn * r * gk[:, :hd]).astype(BF16)
        k_rot = rope_d(kr * r * gk[:, hd:])
        kd_ref[:, h * wh + hd:(h + 1) * wh] = jnp.where(free_lane, -bound_d, k_rot).astype(BF16)


def _prep(proj, negc_col, bounds, gaq, gak, gcq, gck, gdcq, gdckv, gdq, gdk, wuq, wuk, wuvt,
          cosc, sinc, cosd, sind, batch, seq):
    n = proj.shape[0]
    tm = TILE
    tpb = seq // tm
    assert seq // MOBA_BLOCK <= MOBA_MAX_BLOCKS
    kern = functools.partial(_prep_kernel, tiles_per_batch=tpb, tm=tm)
    row = lambda i: (i, 0)
    const = lambda i: (0, 0)
    tab = lambda i: (i % tpb, 0)
    w2 = 2 * GROUP_WIDTH
    return pl.pallas_call(
        kern,
        grid=(n // tm,),
        in_specs=[
            pl.BlockSpec((tm, w2), lambda i: (i, 0)),
            pl.BlockSpec((tm, w2), lambda i: (i, 1)),
            pl.BlockSpec((tm, w2), lambda i: (i, 2)),
            pl.BlockSpec((tm, 8), row),
            pl.BlockSpec((8, HEAD_DIM), const),
            pl.BlockSpec((1, HEAD_DIM), const), pl.BlockSpec((1, HEAD_DIM), const),
            pl.BlockSpec((1, HEAD_DIM), const), pl.BlockSpec((1, HEAD_DIM), const),
            pl.BlockSpec((1, MLA_Q_RANK), const), pl.BlockSpec((1, MLA_KV_RANK), const),
            pl.BlockSpec((1, WIDE_HEAD), const), pl.BlockSpec((1, WIDE_HEAD), const),
            pl.BlockSpec((MLA_Q_RANK, GROUP_HEADS * WIDE_HEAD), const),
            pl.BlockSpec((MLA_KV_RANK, GROUP_WIDTH), const),
            pl.BlockSpec((GROUP_WIDTH, MLA_KV_RANK), const),
            pl.BlockSpec((tm, HEAD_DIM), tab), pl.BlockSpec((tm, HEAD_DIM), tab),
            pl.BlockSpec((tm, HEAD_DIM), tab), pl.BlockSpec((tm, HEAD_DIM), tab),
        ],
        out_specs=[pl.BlockSpec((tm, w2), row)] * 6 + [
            pl.BlockSpec((None, None, GROUP_WIDTH, tm), lambda i: (i // tpb, i % tpb, 0, 0)),
        ],
        out_shape=[jax.ShapeDtypeStruct((n, w2), BF16)] * 6 + [
            jax.ShapeDtypeStruct((batch, tpb, GROUP_WIDTH, tm), BF16),
        ],
        scratch_shapes=[pltpu.VMEM((GROUP_WIDTH, GROUP_WIDTH), F32)],
        compiler_params=pltpu.CompilerParams(
            dimension_semantics=("arbitrary",), vmem_limit_bytes=VMEM_LIMIT),
        name="prep",
    )(proj, proj, proj, negc_col, bounds, gaq, gak, gcq, gck, gdcq, gdckv, gdq, gdk, wuq, wuk, wuvt,
      cosc, sinc, cosd, sind)


def _key_query_iota(t):
    key = lax.broadcasted_iota(jnp.int32, (t, t), 0)
    qry = lax.broadcasted_iota(jnp.int32, (t, t), 1)
    return key, qry


def _softmax_step(s, vt, carry):
    m, acc = carry
    m_new = jnp.maximum(m, jnp.max(s, axis=0, keepdims=True))
    alpha = jnp.exp2(m - m_new)
    p = jnp.exp2(s - m_new)
    return m_new, alpha * acc + _dot(vt, p.astype(BF16))


def _softmax_init(t):
    return (jnp.full((1, t), NEG, F32), jnp.zeros((HEAD_DIM + SUM_ROWS, t), F32))


def _softmax_attn_kernel(q_ref, k_ref, vt_ref, o_ref, acc_scr, *, t, bounded):
    qi = pl.program_id(1)
    key, qry = _key_query_iota(t)
    causal = key <= qry
    dk = WIDE_HEAD
    dv = HEAD_DIM
    ones_rows = jnp.ones((SUM_ROWS, t), BF16)

    def logits(j, h, diag):
        off = pl.multiple_of(j * t, t)
        s = _dot_nt(k_ref[pl.ds(off, t), h * dk:(h + 1) * dk], q_ref[:, h * dk:(h + 1) * dk])
        return jnp.where(causal, s, NEG) if diag else s

    def values(j, h):
        return jnp.concatenate([vt_ref[j, h * dv:(h + 1) * dv, :], ones_rows], axis=0)

    if bounded:
        acc_scr[...] = jnp.zeros_like(acc_scr)

        def step(j, diag):
            s = [logits(j, h, diag) for h in range(GROUP_HEADS)]
            p = [jnp.exp2(s[h]).astype(BF16) for h in range(GROUP_HEADS)]
            for h in range(GROUP_HEADS):
                acc_scr[h] += _dot(values(j, h), p[h])

        def body(j, c):
            step(j, False)
            return c

        lax.fori_loop(0, qi, body, 0)
        step(qi, True)
        accs = [acc_scr[h] for h in range(GROUP_HEADS)]
    else:
        def step(j, carry, diag):
            out = []
            for h in range(GROUP_HEADS):
                out.append(_softmax_step(logits(j, h, diag), values(j, h), carry[h]))
            return tuple(out)

        carry = step(qi, tuple(_softmax_init(t) for _ in range(GROUP_HEADS)), True)
        carry = lax.fori_loop(0, qi, lambda j, c: step(j, c, False), carry)
        accs = [c[1] for c in carry]

    for h in range(GROUP_HEADS):
        a = accs[h]
        o_ref[:, h * dv:(h + 1) * dv] = (a[:dv] / a[dv:dv + 1]).T.astype(BF16)


def _softmax_attn(name, q_arr, k_arr, vt_arr, vt_group, batch, seq, bounded):
    t = TILE
    nq = seq // t
    n = batch * seq
    kern = functools.partial(_softmax_attn_kernel, t=t, bounded=bounded)
    if vt_group is None:
        vt_spec = pl.BlockSpec((None, nq, GROUP_WIDTH, t), lambda b, i: (b, 0, 0, 0))
    else:
        vt_spec = pl.BlockSpec((None, None, nq, GROUP_WIDTH, t), lambda b, i: (vt_group, b, 0, 0, 0))
    return pl.pallas_call(
        kern,
        grid=(batch, nq),
        in_specs=[
            pl.BlockSpec((t, GROUP_HEADS * WIDE_HEAD), lambda b, i: (b * nq + i, 0)),
            pl.BlockSpec((seq, GROUP_HEADS * WIDE_HEAD), lambda b, i: (b, 0)),
            vt_spec,
        ],
        out_specs=pl.BlockSpec((t, GROUP_WIDTH), lambda b, i: (b * nq + i, 0)),
        out_shape=jax.ShapeDtypeStruct((n, GROUP_WIDTH), BF16),
        scratch_shapes=[pltpu.VMEM((GROUP_HEADS, HEAD_DIM + SUM_ROWS, t), F32)],
        compiler_params=pltpu.CompilerParams(
            dimension_semantics=("parallel", "parallel"), vmem_limit_bytes=VMEM_LIMIT),
        name=name + ("" if bounded else "_online"),
    )(q_arr, k_arr, vt_arr)


def _bounded_or_online(bound, name, *args):
    return lax.cond(bound < MAX_SAFE_BOUND,
                    lambda: _softmax_attn(name, *args, bounded=True),
                    lambda: _softmax_attn(name, *args, bounded=False))


def _stickbreak_kernel(q_ref, k_ref, vt_ref, o_ref, acc_scr, rest_scr, *, t):
    qi = pl.program_id(1)
    key, qry = _key_query_iota(t)
    strict = key < qry
    half = t // 2
    kk, kj = _key_query_iota(half)
    later = jnp.where(kj > kk, 1.0, 0.0).astype(BF16)
    hd = HEAD_DIM
    acc_scr[...] = jnp.zeros_like(acc_scr)
    rest_scr[...] = jnp.zeros_like(rest_scr)

    def step(j, diag):
        off = pl.multiple_of(j * t, t)
        low = None
        zs = []
        for h in range(GROUP_HEADS):
            q = (q_ref[:, h * hd:(h + 1) * hd].astype(F32) * (SCALE * LOG2E)).astype(BF16)
            zs.append(_dot_nt(k_ref[pl.ds(off, t), h * hd:(h + 1) * hd], q))
        for h in range(GROUP_HEADS):
            used = rest_scr[h]
            z = zs[h]
            drop = jnp.maximum(z, 0.0) + jnp.log(1.0 + jnp.exp2(-jnp.abs(z))) * LOG2E
            if diag:
                drop = jnp.where(strict, drop, 0.0)
            d_top, d_bot = drop[:half], drop[half:]
            ex_bot = _dot(later, d_bot.astype(BF16))
            tot_bot = ex_bot[0:1] + d_bot[0:1]
            ex_top = _dot(later, d_top.astype(BF16))
            tot_top = ex_top[0:1] + d_top[0:1]
            between = jnp.concatenate([ex_top + (tot_bot + used), ex_bot + used], axis=0)
            w = jnp.exp2((z - drop) - between)
            if diag:
                w = jnp.where(strict, w, 0.0)
            acc_scr[h] += _dot(vt_ref[j, h * hd:(h + 1) * hd, :], w.astype(BF16))
            used = used + tot_top + tot_bot
            rest_scr[h] = used
            low = used if low is None else jnp.minimum(low, used)
        return jnp.min(low)

    def cond(c):
        j, low = c
        return jnp.logical_and(j >= 0, low < STICK_DONE)

    def body(c):
        j, _ = c
        return j - 1, step(j, False)

    lax.while_loop(cond, body, (qi - 1, step(qi, True)))
    for h in range(GROUP_HEADS):
        o_ref[:, h * hd:(h + 1) * hd] = acc_scr[h].T.astype(BF16)


def _stickbreak_attn(proj, vt3, batch, seq):
    t = TILE
    nq = seq // t
    n = batch * seq
    kern = functools.partial(_stickbreak_kernel, t=t)
    return pl.pallas_call(
        kern,
        grid=(batch, nq),
        in_specs=[
            pl.BlockSpec((t, GROUP_WIDTH), lambda b, i: (b * nq + i, T_BQ)),
            pl.BlockSpec((seq, GROUP_WIDTH), lambda b, i: (b, T_BK)),
            pl.BlockSpec((None, None, nq, GROUP_WIDTH, t), lambda b, i: (1, b, 0, 0, 0)),
        ],
        out_specs=pl.BlockSpec((t, GROUP_WIDTH), lambda b, i: (b * nq + i, 0)),
        out_shape=jax.ShapeDtypeStruct((n, GROUP_WIDTH), BF16),
        scratch_shapes=[pltpu.VMEM((GROUP_HEADS, HEAD_DIM, t), F32), pltpu.VMEM((GROUP_HEADS, 1, t), F32)],
        compiler_params=pltpu.CompilerParams(
            dimension_semantics=("parallel", "parallel"), vmem_limit_bytes=VMEM_LIMIT),
        name="stickbreak_attn",
    )(proj, proj, vt3)


def _outproj_kernel(oa_ref, ob_ref, oc_ref, od_ref, g_ref, w_ref, x_ref, gf_ref, out_ref, hf_ref):
    acc = x_ref[...]
    for gi, o_ref in enumerate((oa_ref, ob_ref, oc_ref, od_ref)):
        y = _rms(o_ref[...].astype(F32), g_ref[gi:gi + 1, :]).astype(BF16)
        acc = acc + _dot(y, w_ref[gi * GROUP_WIDTH:(gi + 1) * GROUP_WIDTH, :])
    out_ref[...] = acc
    hf_ref[...] = _rms(acc, gf_ref[...]).astype(BF16)


def _outproj(oa, ob, oc, od, g, w, x2, g_ffn):
    n = x2.shape[0]
    tm = 512
    row = lambda i: (i, 0)
    const = lambda i: (0, 0)
    return pl.pallas_call(
        _outproj_kernel,
        grid=(n // tm,),
        in_specs=[pl.BlockSpec((tm, GROUP_WIDTH), row)] * 4 + [
            pl.BlockSpec((N_GROUPS, GROUP_WIDTH), const),
            pl.BlockSpec((N_GROUPS * GROUP_WIDTH, D_MODEL), const, pipeline_mode=pl.Buffered(1)),
            pl.BlockSpec((tm, D_MODEL), row),
            pl.BlockSpec((1, D_MODEL), const),
        ],
        out_specs=[pl.BlockSpec((tm, D_MODEL), row), pl.BlockSpec((tm, D_MODEL), row)],
        out_shape=[jax.ShapeDtypeStruct((n, D_MODEL), F32), jax.ShapeDtypeStruct((n, D_MODEL), BF16)],
        compiler_params=pltpu.CompilerParams(
            dimension_semantics=("parallel",), vmem_limit_bytes=VMEM_LIMIT),
        name="outproj",
    )(oa, ob, oc, od, g, w, x2, g_ffn)


def _ffn_up_kernel(h_ref, wg_ref, wv_ref, cwg_ref, cwv_ref, cbg_ref, cbv_ref,
                   out_ref, wg_scr, wv_scr, tail_g, tail_v, *, tiles_per_batch, tm):
    i = pl.program_id(1)

    @pl.when(i == 0)
    def _():
        wg_scr[...] = wg_ref[...].astype(BF16)
        wv_scr[...] = wv_ref[...].astype(BF16)

    @pl.when(i % tiles_per_batch == 0)
    def _():
        tail_g[...] = jnp.zeros_like(tail_g)
        tail_v[...] = jnp.zeros_like(tail_v)

    def conv(h, w_scr, cw_ref, cb_ref, prev):
        u = _dot(h, w_scr[...])
        ue = jnp.concatenate([prev, u], axis=0)
        y = cb_ref[...] + cw_ref[2:3, :] * u
        for back in range(1, CONV_WIDTH):
            y = y + cw_ref[CONV_WIDTH - 1 - back:CONV_WIDTH - back, :] * pltpu.roll(ue, back, axis=0)[8:]
        return y, u[u.shape[0] - 8:]

    h = h_ref[...]
    gate, tail_g[...] = conv(h, wg_scr, cwg_ref, cbg_ref, tail_g[...])
    val, tail_v[...] = conv(h, wv_scr, cwv_ref, cbv_ref, tail_v[...])
    out_ref[...] = (gate / (1.0 + jnp.exp(-gate)) * val).astype(BF16)


def _ffn_up(h, w_up, conv_w, conv_b, layer, seq):
    n = h.shape[0]
    tm = FFN_TOKENS
    tn = 512
    nj = D_FF // tn
    tpb = seq // tm
    kern = functools.partial(_ffn_up_kernel, tiles_per_batch=tpb, tm=tm)
    return pl.pallas_call(
        kern,
        grid=(nj, n // tm),
        in_specs=[
            pl.BlockSpec((tm, D_MODEL), lambda j, i: (i, 0)),
            pl.BlockSpec((None, D_MODEL, tn), lambda j, i: (layer, 0, j)),
            pl.BlockSpec((None, D_MODEL, tn), lambda j, i: (layer, 0, j + nj)),
            pl.BlockSpec((None, CONV_WIDTH, tn), lambda j, i: (layer, 0, j)),
            pl.BlockSpec((None, CONV_WIDTH, tn), lambda j, i: (layer, 0, j + nj)),
            pl.BlockSpec((None, 1, tn), lambda j, i: (layer, 0, j)),
            pl.BlockSpec((None, 1, tn), lambda j, i: (layer, 0, j + nj)),
        ],
        out_specs=pl.BlockSpec((tm, tn), lambda j, i: (i, j)),
        out_shape=jax.ShapeDtypeStruct((n, D_FF), BF16),
        scratch_shapes=[pltpu.VMEM((D_MODEL, tn), BF16), pltpu.VMEM((D_MODEL, tn), BF16),
                        pltpu.VMEM((8, tn), F32), pltpu.VMEM((8, tn), F32)],
        compiler_params=pltpu.CompilerParams(
            dimension_semantics=("arbitrary", "arbitrary"), vmem_limit_bytes=VMEM_LIMIT),
        name="ffn_up",
    )(h, w_up, w_up, conv_w, conv_w, conv_b, conv_b)


def _ffn_down_kernel(a_ref, w_ref, x_ref, out_ref):
    a = a_ref[...]
    tn = 2 * MXU_WIDTH
    for c in range(D_MODEL // tn):
        cols = slice(c * tn, (c + 1) * tn)
        out_ref[:, cols] = x_ref[:, cols] + _dot(a, w_ref[:, cols])


def _ffn_down(act, w_down, x2):
    n = x2.shape[0]
    tm = 512
    row = lambda i: (i, 0)
    return pl.pallas_call(
        _ffn_down_kernel,
        grid=(n // tm,),
        in_specs=[
            pl.BlockSpec((tm, D_FF), row),
            pl.BlockSpec((D_FF, D_MODEL), lambda i: (0, 0), pipeline_mode=pl.Buffered(1)),
            pl.BlockSpec((tm, D_MODEL), row),
        ],
        out_specs=pl.BlockSpec((tm, D_MODEL), row),
        out_shape=jax.ShapeDtypeStruct((n, D_MODEL), F32),
        compiler_params=pltpu.CompilerParams(
            dimension_semantics=("parallel",), vmem_limit_bytes=VMEM_LIMIT),
        name="ffn_down",
    )(act, w_down, x2)


def _rope_tables(seq):
    def angles(dim):
        inv_freq = ROPE_THETA ** (-jnp.arange(0, dim, 2, dtype=F32) / dim)
        ang = jnp.arange(seq, dtype=F32)[:, None] * inv_freq[None, :]
        return jnp.cos(ang), jnp.sin(ang)

    c, s = angles(HEAD_DIM)
    cosc = jnp.concatenate([c, c], axis=1)
    sinc = jnp.concatenate([-s, s], axis=1)
    c, s = angles(MLA_ROPE_DIM)
    z = jnp.zeros_like(c)
    cosd = jnp.concatenate([c, z, c, z], axis=1)
    sind = jnp.concatenate([-s, z, s, z], axis=1)
    return cosc, sinc, cosd, sind


def _spread_rope(t):
    half = MLA_ROPE_DIM // 2
    z = jnp.zeros(t.shape[:-1] + (half,), t.dtype)
    return jnp.concatenate([t[..., :half], z, t[..., half:], z], axis=-1)


def _layout_w_in(w):
    gw = GROUP_WIDTH
    o = 0
    pieces = {}
    for name, width in (("aq", gw), ("ak", gw), ("av", gw), ("af", GROUP_HEADS), ("bq", gw), ("bk", gw),
                        ("bv", gw), ("cq", gw), ("ck", gw), ("cv", gw), ("dcq", MLA_Q_RANK),
                        ("dckv", MLA_KV_RANK), ("dkr", MLA_ROPE_DIM)):
        pieces[name] = w[:, o:o + width]
        o += width
    pad = jnp.zeros((w.shape[0], gw - MLA_KV_RANK - HEAD_DIM), w.dtype)
    tiles = [pieces["aq"], pieces["ak"], pieces["cq"], pieces["ck"], pieces["dcq"],
             pieces["dckv"], _spread_rope(pieces["dkr"]), pad, pieces["bq"], pieces["bk"]]
    w_main = jnp.concatenate(tiles, axis=1).astype(BF16)
    wvt = jnp.stack([pieces["av"].T, pieces["bv"].T, pieces["cv"].T]).astype(BF16)
    wf = jnp.zeros((8, w.shape[0]), F32).at[:GROUP_HEADS].set(pieces["af"].T).astype(BF16)
    return w_main, wvt, wf


def _layout_w_uq(w):
    w = w.reshape(MLA_Q_RANK, GROUP_HEADS, MLA_QK_DIM)
    w = jnp.concatenate([w[..., :MLA_NOPE_DIM], _spread_rope(w[..., MLA_NOPE_DIM:])], axis=-1)
    return w.reshape(MLA_Q_RANK, GROUP_HEADS * WIDE_HEAD).astype(BF16)


def _layout_w_ukv(w):
    w = w.reshape(MLA_KV_RANK, GROUP_HEADS, 2, HEAD_DIM)
    wk = w[:, :, 0, :].reshape(MLA_KV_RANK, GROUP_WIDTH).astype(BF16)
    wvt = w[:, :, 1, :].reshape(MLA_KV_RANK, GROUP_WIDTH).T.astype(BF16)
    return wk, wvt


def _logit_bound(gq, gk, dim, scale):
    return dim * scale * LOG2E * jnp.max(jnp.abs(gq)) * jnp.max(jnp.abs(gk)) * 1.01 + 0.5


def _layout_qk_gain(g):
    return jnp.concatenate([g[:MLA_NOPE_DIM], _spread_rope(g[MLA_NOPE_DIM:])])[None, :]


def kernel(x, attn_norm, w_in, b_forget, fox_q_norm, fox_k_norm, moba_q_norm, moba_k_norm, mla_cq_norm, mla_ckv_norm, w_uq, w_ukv, mla_q_norm, mla_k_norm, mix_out_norm, w_out, ffn_norm, w_up, conv_w, conv_b, w_down):
    batch, seq, d_model = x.shape
    depth = w_in.shape[0]
    assert d_model == D_MODEL and seq % TILE == 0 and seq % FFN_TOKENS == 0
    n = batch * seq
    x2 = x.reshape(n, d_model)
    cosc, sinc, cosd, sind = _rope_tables(seq)
    row = lambda v: v[None, :]

    for l in range(depth):
        w_main, wvt, wf = _layout_w_in(w_in[l])
        bf = jnp.zeros((8, 128), F32).at[:GROUP_HEADS, :].set(b_forget[l][:, None])
        proj, vt3, negc = _inproj(x2, row(attn_norm[l]), w_main, wvt, wf, bf, batch, seq)
        negc_col = negc.transpose(0, 2, 1).reshape(n, 8)

        bound_a = _logit_bound(fox_q_norm[l], fox_k_norm[l], HEAD_DIM, SCALE)
        bound_c = _logit_bound(moba_q_norm[l], moba_k_norm[l], HEAD_DIM, SCALE)
        bound_d = _logit_bound(mla_q_norm[l], mla_k_norm[l], MLA_QK_DIM, SCALE_MLA)
        big_c = jnp.maximum(MOBA_MIN_MASK, jnp.exp2(jnp.ceil(jnp.log2(4.0 * bound_c + 64.0))))
        bounds = jnp.zeros((8, HEAD_DIM), F32).at[:4].set(
            jnp.stack([bound_a, bound_c, big_c, bound_d])[:, None])

        wuk, wuvt = _layout_w_ukv(w_ukv[l])
        qa, ka, qc, kc, qd, kd, vtd = _prep(
            proj, negc_col, bounds, row(fox_q_norm[l]), row(fox_k_norm[l]), row(moba_q_norm[l]),
            row(moba_k_norm[l]), row(mla_cq_norm[l]), row(mla_ckv_norm[l]),
            _layout_qk_gain(mla_q_norm[l]), _layout_qk_gain(mla_k_norm[l]),
            _layout_w_uq(w_uq[l]), wuk, wuvt, cosc, sinc, cosd, sind, batch, seq)

        o_a = _bounded_or_online(bound_a, "fox_attn", qa, ka, vt3, 0, batch, seq)
        o_b = _stickbreak_attn(proj, vt3, batch, seq)
        o_c = _bounded_or_online(bound_c, "moba_attn", qc, kc, vt3, 2, batch, seq)
        o_d = _bounded_or_online(bound_d, "mla_attn", qd, kd, vtd, None, batch, seq)

        x2, h_ffn = _outproj(o_a, o_b, o_c, o_d, mix_out_norm[l].reshape(N_GROUPS, GROUP_WIDTH),
                             w_out[l].astype(BF16), x2, row(ffn_norm[l]))

        act = _ffn_up(h_ffn, w_up, conv_w, conv_b[:, None, :], l, seq)
        x2 = _ffn_down(act, w_down[l].astype(BF16), x2)

    return x2.reshape(batch, seq, d_model)
```

```python
import functools
import math

import numpy as np
import jax
import jax.numpy as jnp
from jax import lax
from jax.experimental import pallas as pl
from jax.experimental.pallas import tpu as pltpu

F32 = jnp.float32
BF16 = jnp.bfloat16

D_MODEL = 2048
HEAD_DIM = 128
GROUP_HEADS = 4
GROUP_WIDTH = GROUP_HEADS * HEAD_DIM
N_GROUPS = 4
MOBA_BLOCK = 256
MOBA_TOP_K = 3
MOBA_MAX_BLOCKS = 16
MLA_Q_RANK = 512
MLA_KV_RANK = 256
MLA_NOPE_DIM = 128
MLA_ROPE_DIM = 64
MLA_QK_DIM = MLA_NOPE_DIM + MLA_ROPE_DIM
WIDE_HEAD = 256
D_FF = 5632
CONV_WIDTH = 3
ROPE_THETA = 10000.0
EPS = 1e-6

T_AQ, T_AK, T_CQ, T_CK, T_DCQ, T_DKV, T_BQ, T_BK = range(8)
PROJ_TILES = 8
V_GROUPS = 3

TILE = 512
NEG = -1e30
VMEM_LIMIT = 56 * 1024 * 1024
LOG2E = math.log2(math.e)
SCALE = HEAD_DIM ** -0.5
SCALE_MLA = MLA_QK_DIM ** -0.5
FFN_TOKENS = 1024
MXU_WIDTH = 256
SUM_ROWS = 16
MAX_SAFE_BOUND = 56.0
MOBA_MIN_MASK = 256.0
STICK_DONE = 160.0


def _dot(a, b):
    return jnp.dot(a, b, preferred_element_type=F32)


def _dot_nt(a, b):
    return lax.dot_general(a, b, (((1,), (1,)), ((), ())), preferred_element_type=F32)


def _split3(a):
    hi = a.astype(BF16)
    r1 = a - hi.astype(F32)
    mid = r1.astype(BF16)
    lo = (r1 - mid.astype(F32)).astype(BF16)
    return hi, mid, lo


def _log_sigmoid_pair(z):
    sp = jnp.log(1.0 + jnp.exp(-jnp.abs(z)))
    return jnp.minimum(z, 0.0) - sp, -jnp.maximum(z, 0.0) - sp


def _rms(v, g, denom=None):
    if denom is None:
        ms = jnp.mean(v * v, axis=-1, keepdims=True)
    else:
        ms = jnp.sum(v * v, axis=-1, keepdims=True) * (1.0 / denom)
    return v * lax.rsqrt(ms + EPS) * g


def _sumsq_lanes(v):
    return _dot((v * v).astype(BF16), jnp.ones((v.shape[-1], HEAD_DIM), BF16))


def _rms_lanes(v, g, denom=None):
    width = v.shape[-1]
    r = lax.rsqrt(_sumsq_lanes(v) * (1.0 / (denom or width)) + EPS)
    if width > HEAD_DIM:
        r = jnp.concatenate([r] * (width // HEAD_DIM), axis=1)
    return v * r * g


def _inproj_kernel(x_ref, g_ref, w_ref, wvt_ref, wf_ref, bf_ref, proj_ref, vt_ref, negc_ref,
                   carry_scr, *, tiles_per_batch, tm):
    i = pl.program_id(0)
    h = _rms(x_ref[...], g_ref[...]).astype(BF16)
    z = _dot_nt(wf_ref[...], h) + bf_ref[:, 0:1]
    logf, _ = _log_sigmoid_pair(z)
    r = lax.broadcasted_iota(jnp.int32, (tm, tm), 0)
    c = lax.broadcasted_iota(jnp.int32, (tm, tm), 1)
    tri = jnp.where(r <= c, 1.0, 0.0).astype(BF16)
    hi, mid, lo = _split3(logf)
    cum = _dot(hi, tri) + _dot(mid, tri) + _dot(lo, tri)

    @pl.when(i % tiles_per_batch == 0)
    def _():
        carry_scr[...] = jnp.zeros_like(carry_scr)

    cum = cum + carry_scr[:, 0:1]
    carry_scr[...] = jnp.broadcast_to(cum[:, tm - 1:tm], carry_scr.shape)
    negc_ref[...] = cum * (-LOG2E)

    gw = GROUP_WIDTH
    for jt in range(PROJ_TILES):
        proj_ref[:, jt * gw:(jt + 1) * gw] = _dot(h, w_ref[:, jt * gw:(jt + 1) * gw]).astype(BF16)
    for gi in range(V_GROUPS):
        vt_ref[gi] = _dot_nt(wvt_ref[gi], h).astype(BF16)


def _inproj(x2, g, w, wvt, wf, bf, batch, seq):
    n = x2.shape[0]
    tm = TILE
    tpb = seq // tm
    kern = functools.partial(_inproj_kernel, tiles_per_batch=tpb, tm=tm)
    const2 = lambda i: (0, 0)
    resident = pl.Buffered(1)
    return pl.pallas_call(
        kern,
        grid=(n // tm,),
        in_specs=[
            pl.BlockSpec((tm, D_MODEL), lambda i: (i, 0)),
            pl.BlockSpec((1, D_MODEL), const2),
            pl.BlockSpec((D_MODEL, PROJ_TILES * GROUP_WIDTH), const2, pipeline_mode=resident),
            pl.BlockSpec((V_GROUPS, GROUP_WIDTH, D_MODEL), lambda i: (0, 0, 0), pipeline_mode=resident),
            pl.BlockSpec((8, D_MODEL), const2),
            pl.BlockSpec((8, 128), const2),
        ],
        out_specs=[
            pl.BlockSpec((tm, PROJ_TILES * GROUP_WIDTH), lambda i: (i, 0)),
            pl.BlockSpec((V_GROUPS, None, None, GROUP_WIDTH, tm), lambda i: (0, i // tpb, i % tpb, 0, 0)),
            pl.BlockSpec((None, 8, tm), lambda i: (i // tpb, 0, i % tpb)),
        ],
        out_shape=[
            jax.ShapeDtypeStruct((n, PROJ_TILES * GROUP_WIDTH), BF16),
            jax.ShapeDtypeStruct((V_GROUPS, batch, tpb, GROUP_WIDTH, tm), BF16),
            jax.ShapeDtypeStruct((batch, 8, seq), F32),
        ],
        scratch_shapes=[pltpu.VMEM((8, 128), F32)],
        compiler_params=pltpu.CompilerParams(
            dimension_semantics=("arbitrary",), vmem_limit_bytes=VMEM_LIMIT),
        name="inproj",
    )(x2, g, w, wvt, wf, bf)


def _prep_kernel(a_ref, c_ref, d_ref, nc_ref, bnd_ref, gaq, gak, gcq, gck, gdcq, gdckv, gdq, gdk, wuq, wuk, wuvt,
                 cosc, sinc, cosd, sind,
                 qa_ref, ka_ref, qc_ref, kc_ref, qd_ref, kd_ref, vtd_ref, kmt_scr,
                 *, tiles_per_batch, tm):
    i = pl.program_id(0)
    t = i % tiles_per_batch
    hd = HEAD_DIM
    wh = WIDE_HEAD
    blocks_per_tile = tm // MOBA_BLOCK
    lane = lax.broadcasted_iota(jnp.int32, (tm, hd), 1)
    bound_a = bnd_ref[0:1, :]
    bound_c = bnd_ref[1:2, :]
    big_c = bnd_ref[2:3, :]
    bound_d = bnd_ref[3:4, :]

    nc = nc_ref[...]
    for h in range(GROUP_HEADS):
        q = a_ref[:, h * hd:(h + 1) * hd].astype(F32)
        qa_ref[:, h * wh:h * wh + hd] = (_rms_lanes(q, gaq[...]) * (SCALE * LOG2E)).astype(BF16)
        k = a_ref[:, GROUP_WIDTH + h * hd:GROUP_WIDTH + (h + 1) * hd].astype(F32)
        ka_ref[:, h * wh:h * wh + hd] = _rms_lanes(k, gak[...]).astype(BF16)
        hi, mid, lo = (p.astype(F32) for p in _split3(nc[:, h:h + 1]))
        k_aug = jnp.where(lane == 0, hi, jnp.where(lane == 1, mid, jnp.where(lane == 2, lo, jnp.where(
            lane < 6, 1.0, jnp.where(lane == 6, -bound_a, 0.0)))))
        q_aug = jnp.where(lane < 3, 1.0, jnp.where(lane == 3, -hi, jnp.where(lane == 4, -mid, jnp.where(
            lane == 5, -lo, jnp.where(lane == 6, 1.0, 0.0)))))
        ka_ref[:, h * wh + hd:(h + 1) * wh] = k_aug.astype(BF16)
        qa_ref[:, h * wh + hd:(h + 1) * wh] = q_aug.astype(BF16)

    @pl.when(i == 0)
    def _():
        kmt_scr[...] = jnp.zeros_like(kmt_scr)

    cc = cosc[...]
    sc = sinc[...]
    sub8 = lax.broadcasted_iota(jnp.int32, (8, hd), 0)
    key_blk = t * blocks_per_tile + lax.broadcasted_iota(jnp.int32, (tm, hd), 0) // MOBA_BLOCK

    def rope_c(v):
        return v * cc + pltpu.roll(v, hd // 2, axis=1) * sc

    kc_aug = jnp.where(lane == key_blk, big_c, jnp.where(lane == MOBA_MAX_BLOCKS, -(bound_c + big_c), 0.0))
    kc_aug = kc_aug.astype(BF16)
    for h in range(GROUP_HEADS):
        k = rope_c(_rms_lanes(c_ref[:, GROUP_WIDTH + h * hd:GROUP_WIDTH + (h + 1) * hd].astype(F32), gck[...]))
        kc_ref[:, h * wh:h * wh + hd] = k.astype(BF16)
        kc_ref[:, h * wh + hd:(h + 1) * wh] = kc_aug
        for blk in range(blocks_per_tile):
            km = jnp.mean(k[blk * MOBA_BLOCK:(blk + 1) * MOBA_BLOCK], axis=0, keepdims=True)
            nblk = t * blocks_per_tile + blk
            base = pl.multiple_of(h * hd + (nblk // 8) * 8, 8)
            old = kmt_scr[pl.ds(base, 8), h * hd:(h + 1) * hd]
            kmt_scr[pl.ds(base, 8), h * hd:(h + 1) * hd] = jnp.where(sub8 == nblk % 8, km, old)

    qs = []
    for h in range(GROUP_HEADS):
        q = rope_c(_rms_lanes(c_ref[:, h * hd:(h + 1) * hd].astype(F32), gcq[...])) * (SCALE * LOG2E)
        qc_ref[:, h * wh:h * wh + hd] = q.astype(BF16)
        qs.append(q)
    q_all = jnp.concatenate(qs, axis=1)
    qh, qm, _ = _split3(q_all)
    kh, km_, _ = _split3(kmt_scr[...])
    gate_t = _dot_nt(kh, qh) + _dot_nt(km_, qh) + _dot_nt(kh, qm)

    nb = MOBA_MAX_BLOCKS
    blk_row = lax.broadcasted_iota(jnp.int32, (nb, tm), 0)
    own = t * blocks_per_tile + lax.broadcasted_iota(jnp.int32, (nb, tm), 1) // MOBA_BLOCK
    blk_f = blk_row.astype(F32)
    ninf = jnp.float32(-jnp.inf)
    no_rows = jnp.zeros((hd - nb, tm), F32)
    for h in range(GROUP_HEADS):
        g = jnp.where(blk_row < own, gate_t[h * hd:h * hd + nb, :], ninf)
        sel = jnp.where(blk_row == own, 1.0, 0.0)
        for _ in range(MOBA_TOP_K):
            mx = jnp.max(g, axis=0, keepdims=True)
            cand = jnp.where(g == mx, blk_f, 1e9)
            cand = jnp.where(mx > ninf, cand, 1e9)
            idx = jnp.min(cand, axis=0, keepdims=True)
            pick = blk_f == idx
            sel = jnp.where(pick, 1.0, sel)
            g = jnp.where(pick, ninf, g)
        sel_rows = jnp.concatenate([sel, no_rows], axis=0).T
        q_aug = jnp.where(lane == nb, 1.0, sel_rows)
        qc_ref[:, h * wh + hd:(h + 1) * wh] = q_aug.astype(BF16)

    cd = cosd[...]
    sd = sind[...]
    free_lane = lane == MLA_ROPE_DIM // 2

    def rope_d(v):
        return v * cd + pltpu.roll(v, hd // 2, axis=1) * sd

    cq = _rms_lanes(d_ref[:, 0:MLA_Q_RANK].astype(F32), gdcq[...]).astype(BF16)
    qf = _dot(cq, wuq[...])
    for h in range(GROUP_HEADS):
        qn = _rms_lanes(qf[:, h * wh:(h + 1) * wh], gdq[...], denom=MLA_QK_DIM)
        qd_ref[:, h * wh:h * wh + hd] = (qn[:, :hd] * (SCALE_MLA * LOG2E)).astype(BF16)
        q_rot = rope_d(qn[:, hd:]) * (SCALE_MLA * LOG2E)
        qd_ref[:, h * wh + hd:(h + 1) * wh] = jnp.where(free_lane, 1.0, q_rot).astype(BF16)

    ckv = _rms_lanes(d_ref[:, MLA_Q_RANK:MLA_Q_RANK + MLA_KV_RANK].astype(F32), gdckv[...]).astype(BF16)
    knf = _dot(ckv, wuk[...])
    vtd_ref[...] = _dot_nt(wuvt[...], ckv).astype(BF16)
    kr = d_ref[:, MLA_Q_RANK + MLA_KV_RANK:MLA_Q_RANK + MLA_KV_RANK + hd].astype(F32)
    ss_r = _sumsq_lanes(kr)
    gk = gdk[...]
    for h in range(GROUP_HEADS):
        kn = knf[:, h * hd:(h + 1) * hd]
        r = lax.rsqrt((_sumsq_lanes(kn) + ss_r) * (1.0 / MLA_QK_DIM) + EPS)
        kd_ref[:, h * wh:h * wh + hd] = (kn * r * gk[:, :hd]).astype(BF16)
        k_rot = rope_d(kr * r * gk[:, hd:])
        kd_ref[:, h * wh + hd:(h + 1) * wh] = jnp.where(free_lane, -bound_d, k_rot).astype(BF16)


def _prep(proj, negc_col, bounds, gaq, gak, gcq, gck, gdcq, gdckv, gdq, gdk, wuq, wuk, wuvt,
          cosc, sinc, cosd, sind, batch, seq):
    n = proj.shape[0]
    tm = TILE
    tpb = seq // tm
    assert seq // MOBA_BLOCK <= MOBA_MAX_BLOCKS
    kern = functools.partial(_prep_kernel, tiles_per_batch=tpb, tm=tm)
    row = lambda i: (i, 0)
    const = lambda i: (0, 0)
    tab = lambda i: (i % tpb, 0)
    w2 = 2 * GROUP_WIDTH
    return pl.pallas_call(
        kern,
        grid=(n // tm,),
        in_specs=[
            pl.BlockSpec((tm, w2), lambda i: (i, 0)),
            pl.BlockSpec((tm, w2), lambda i: (i, 1)),
            pl.BlockSpec((tm, w2), lambda i: (i, 2)),
            pl.BlockSpec((tm, 8), row),
            pl.BlockSpec((8, HEAD_DIM), const),
            pl.BlockSpec((1, HEAD_DIM), const), pl.BlockSpec((1, HEAD_DIM), const),
            pl.BlockSpec((1, HEAD_DIM), const), pl.BlockSpec((1, HEAD_DIM), const),
            pl.BlockSpec((1, MLA_Q_RANK), const), pl.BlockSpec((1, MLA_KV_RANK), const),
            pl.BlockSpec((1, WIDE_HEAD), const), pl.BlockSpec((1, WIDE_HEAD), const),
            pl.BlockSpec((MLA_Q_RANK, GROUP_HEADS * WIDE_HEAD), const),
            pl.BlockSpec((MLA_KV_RANK, GROUP_WIDTH), const),
            pl.BlockSpec((GROUP_WIDTH, MLA_KV_RANK), const),
            pl.BlockSpec((tm, HEAD_DIM), tab), pl.BlockSpec((tm, HEAD_DIM), tab),
            pl.BlockSpec((tm, HEAD_DIM), tab), pl.BlockSpec((tm, HEAD_DIM), tab),
        ],
        out_specs=[pl.BlockSpec((tm, w2), row)] * 6 + [
            pl.BlockSpec((None, None, GROUP_WIDTH, tm), lambda i: (i // tpb, i % tpb, 0, 0)),
        ],
        out_shape=[jax.ShapeDtypeStruct((n, w2), BF16)] * 6 + [
            jax.ShapeDtypeStruct((batch, tpb, GROUP_WIDTH, tm), BF16),
        ],
        scratch_shapes=[pltpu.VMEM((GROUP_WIDTH, GROUP_WIDTH), F32)],
        compiler_params=pltpu.CompilerParams(
            dimension_semantics=("arbitrary",), vmem_limit_bytes=VMEM_LIMIT),
        name="prep",
    )(proj, proj, proj, negc_col, bounds, gaq, gak, gcq, gck, gdcq, gdckv, gdq, gdk, wuq, wuk, wuvt,
      cosc, sinc, cosd, sind)


def _key_query_iota(t):
    key = lax.broadcasted_iota(jnp.int32, (t, t), 0)
    qry = lax.broadcasted_iota(jnp.int32, (t, t), 1)
    return key, qry


def _softmax_step(s, vt, carry):
    m, acc = carry
    m_new = jnp.maximum(m, jnp.max(s, axis=0, keepdims=True))
    alpha = jnp.exp2(m - m_new)
    p = jnp.exp2(s - m_new)
    return m_new, alpha * acc + _dot(vt, p.astype(BF16))


def _softmax_init(t):
    return (jnp.full((1, t), NEG, F32), jnp.zeros((HEAD_DIM + SUM_ROWS, t), F32))


def _softmax_attn_kernel(q_ref, k_ref, vt_ref, o_ref, acc_scr, *, t, bounded):
    qi = pl.program_id(1)
    key, qry = _key_query_iota(t)
    causal = key <= qry
    dk = WIDE_HEAD
    dv = HEAD_DIM
    ones_rows = jnp.ones((SUM_ROWS, t), BF16)

    def logits(j, h, diag):
        off = pl.multiple_of(j * t, t)
        s = _dot_nt(k_ref[pl.ds(off, t), h * dk:(h + 1) * dk], q_ref[:, h * dk:(h + 1) * dk])
        return jnp.where(causal, s, NEG) if diag else s

    def values(j, h):
        return jnp.concatenate([vt_ref[j, h * dv:(h + 1) * dv, :], ones_rows], axis=0)

    if bounded:
        acc_scr[...] = jnp.zeros_like(acc_scr)

        def step(tiles, diag):
            s = [[logits(j, h, diag) for h in range(GROUP_HEADS)] for j in tiles]
            for n, j in enumerate(tiles):
                for h in range(GROUP_HEADS):
                    acc_scr[h] += _dot(values(j, h), jnp.exp2(s[n][h]).astype(BF16))

        def pair(jj, c):
            step((2 * jj, 2 * jj + 1), False)
            return c

        lax.fori_loop(0, qi // 2, pair, 0)

        @pl.when(qi % 2 == 1)
        def _():
            step((qi - 1,), False)

        step((qi,), True)
        accs = [acc_scr[h] for h in range(GROUP_HEADS)]
    else:
        def step(j, carry, diag):
            out = []
            for h in range(GROUP_HEADS):
                out.append(_softmax_step(logits(j, h, diag), values(j, h), carry[h]))
            return tuple(out)

        carry = step(qi, tuple(_softmax_init(t) for _ in range(GROUP_HEADS)), True)
        carry = lax.fori_loop(0, qi, lambda j, c: step(j, c, False), carry)
        accs = [c[1] for c in carry]

    for h in range(GROUP_HEADS):
        a = accs[h]
        o_ref[:, h * dv:(h + 1) * dv] = (a[:dv] / a[dv:dv + 1]).T.astype(BF16)


def _softmax_attn(name, q_arr, k_arr, vt_arr, vt_group, batch, seq, bounded):
    t = TILE
    nq = seq // t
    n = batch * seq
    kern = functools.partial(_softmax_attn_kernel, t=t, bounded=bounded)
    if vt_group is None:
        vt_spec = pl.BlockSpec((None, nq, GROUP_WIDTH, t), lambda b, i: (b, 0, 0, 0))
    else:
        vt_spec = pl.BlockSpec((None, None, nq, GROUP_WIDTH, t), lambda b, i: (vt_group, b, 0, 0, 0))
    return pl.pallas_call(
        kern,
        grid=(batch, nq),
        in_specs=[
            pl.BlockSpec((t, GROUP_HEADS * WIDE_HEAD), lambda b, i: (b * nq + i, 0)),
            pl.BlockSpec((seq, GROUP_HEADS * WIDE_HEAD), lambda b, i: (b, 0)),
            vt_spec,
        ],
        out_specs=pl.BlockSpec((t, GROUP_WIDTH), lambda b, i: (b * nq + i, 0)),
        out_shape=jax.ShapeDtypeStruct((n, GROUP_WIDTH), BF16),
        scratch_shapes=[pltpu.VMEM((GROUP_HEADS, HEAD_DIM + SUM_ROWS, t), F32)],
        compiler_params=pltpu.CompilerParams(
            dimension_semantics=("parallel", "parallel"), vmem_limit_bytes=VMEM_LIMIT),
        name=name + ("" if bounded else "_online"),
    )(q_arr, k_arr, vt_arr)


def _bounded_or_online(bound, name, *args):
    return lax.cond(bound < MAX_SAFE_BOUND,
                    lambda: _softmax_attn(name, *args, bounded=True),
                    lambda: _softmax_attn(name, *args, bounded=False))


def _stickbreak_kernel(q_ref, k_ref, vt_ref, o_ref, acc_scr, rest_scr, *, t):
    qi = pl.program_id(1)
    key, qry = _key_query_iota(t)
    strict = key < qry
    half = t // 2
    kk, kj = _key_query_iota(half)
    later = jnp.where(kj > kk, 1.0, 0.0).astype(BF16)
    hd = HEAD_DIM
    acc_scr[...] = jnp.zeros_like(acc_scr)
    rest_scr[...] = jnp.zeros_like(rest_scr)

    def step(j, diag):
        off = pl.multiple_of(j * t, t)
        low = None
        zs = []
        for h in range(GROUP_HEADS):
            q = (q_ref[:, h * hd:(h + 1) * hd].astype(F32) * (SCALE * LOG2E)).astype(BF16)
            zs.append(_dot_nt(k_ref[pl.ds(off, t), h * hd:(h + 1) * hd], q))
        for h in range(GROUP_HEADS):
            used = rest_scr[h]
            z = zs[h]
            drop = jnp.maximum(z, 0.0) + jnp.log(1.0 + jnp.exp2(-jnp.abs(z))) * LOG2E
            if diag:
                drop = jnp.where(strict, drop, 0.0)
            d_top, d_bot = drop[:half], drop[half:]
            ex_bot = _dot(later, d_bot.astype(BF16))
            tot_bot = ex_bot[0:1] + d_bot[0:1]
            ex_top = _dot(later, d_top.astype(BF16))
            tot_top = ex_top[0:1] + d_top[0:1]
            between = jnp.concatenate([ex_top + (tot_bot + used), ex_bot + used], axis=0)
            w = jnp.exp2((z - drop) - between)
            if diag:
                w = jnp.where(strict, w, 0.0)
            acc_scr[h] += _dot(vt_ref[j, h * hd:(h + 1) * hd, :], w.astype(BF16))
            used = used + tot_top + tot_bot
            rest_scr[h] = used
            low = used if low is None else jnp.minimum(low, used)
        return jnp.min(low)

    def cond(c):
        j, low = c
        return jnp.logical_and(j >= 0, low < STICK_DONE)

    def body(c):
        j, _ = c
        return j - 1, step(j, False)

    lax.while_loop(cond, body, (qi - 1, step(qi, True)))
    for h in range(GROUP_HEADS):
        o_ref[:, h * hd:(h + 1) * hd] = acc_scr[h].T.astype(BF16)


def _stickbreak_attn(proj, vt3, batch, seq):
    t = TILE
    nq = seq // t
    n = batch * seq
    kern = functools.partial(_stickbreak_kernel, t=t)
    return pl.pallas_call(
        kern,
        grid=(batch, nq),
        in_specs=[
            pl.BlockSpec((t, GROUP_WIDTH), lambda b, i: (b * nq + i, T_BQ)),
            pl.BlockSpec((seq, GROUP_WIDTH), lambda b, i: (b, T_BK)),
            pl.BlockSpec((None, None, nq, GROUP_WIDTH, t), lambda b, i: (1, b, 0, 0, 0)),
        ],
        out_specs=pl.BlockSpec((t, GROUP_WIDTH), lambda b, i: (b * nq + i, 0)),
        out_shape=jax.ShapeDtypeStruct((n, GROUP_WIDTH), BF16),
        scratch_shapes=[pltpu.VMEM((GROUP_HEADS, HEAD_DIM, t), F32), pltpu.VMEM((GROUP_HEADS, 1, t), F32)],
        compiler_params=pltpu.CompilerParams(
            dimension_semantics=("parallel", "parallel"), vmem_limit_bytes=VMEM_LIMIT),
        name="stickbreak_attn",
    )(proj, proj, vt3)


def _outproj_kernel(oa_ref, ob_ref, oc_ref, od_ref, g_ref, w_ref, x_ref, gf_ref, out_ref, hf_ref):
    acc = x_ref[...]
    for gi, o_ref in enumerate((oa_ref, ob_ref, oc_ref, od_ref)):
        y = _rms(o_ref[...].astype(F32), g_ref[gi:gi + 1, :]).astype(BF16)
        acc = acc + _dot(y, w_ref[gi * GROUP_WIDTH:(gi + 1) * GROUP_WIDTH, :])
    out_ref[...] = acc
    hf_ref[...] = _rms(acc, gf_ref[...]).astype(BF16)


def _outproj(oa, ob, oc, od, g, w, x2, g_ffn):
    n = x2.shape[0]
    tm = 512
    row = lambda i: (i, 0)
    const = lambda i: (0, 0)
    return pl.pallas_call(
        _outproj_kernel,
        grid=(n // tm,),
        in_specs=[pl.BlockSpec((tm, GROUP_WIDTH), row)] * 4 + [
            pl.BlockSpec((N_GROUPS, GROUP_WIDTH), const),
            pl.BlockSpec((N_GROUPS * GROUP_WIDTH, D_MODEL), const, pipeline_mode=pl.Buffered(1)),
            pl.BlockSpec((tm, D_MODEL), row),
            pl.BlockSpec((1, D_MODEL), const),
        ],
        out_specs=[pl.BlockSpec((tm, D_MODEL), row), pl.BlockSpec((tm, D_MODEL), row)],
        out_shape=[jax.ShapeDtypeStruct((n, D_MODEL), F32), jax.ShapeDtypeStruct((n, D_MODEL), BF16)],
        compiler_params=pltpu.CompilerParams(
            dimension_semantics=("parallel",), vmem_limit_bytes=VMEM_LIMIT),
        name="outproj",
    )(oa, ob, oc, od, g, w, x2, g_ffn)


def _ffn_up_kernel(h_ref, wg_ref, wv_ref, cwg_ref, cwv_ref, cbg_ref, cbv_ref,
                   out_ref, wg_scr, wv_scr, tail_g, tail_v, *, tiles_per_batch, tm):
    i = pl.program_id(1)

    @pl.when(i == 0)
    def _():
        wg_scr[...] = wg_ref[...].astype(BF16)
        wv_scr[...] = wv_ref[...].astype(BF16)

    @pl.when(i % tiles_per_batch == 0)
    def _():
        tail_g[...] = jnp.zeros_like(tail_g)
        tail_v[...] = jnp.zeros_like(tail_v)

    def conv(h, w_scr, cw_ref, cb_ref, prev):
        u = _dot(h, w_scr[...])
        ue = jnp.concatenate([prev, u], axis=0)
        y = cb_ref[...] + cw_ref[2:3, :] * u
        for back in range(1, CONV_WIDTH):
            y = y + cw_ref[CONV_WIDTH - 1 - back:CONV_WIDTH - back, :] * pltpu.roll(ue, back, axis=0)[8:]
        return y, u[u.shape[0] - 8:]

    h = h_ref[...]
    gate, tail_g[...] = conv(h, wg_scr, cwg_ref, cbg_ref, tail_g[...])
    val, tail_v[...] = conv(h, wv_scr, cwv_ref, cbv_ref, tail_v[...])
    out_ref[...] = (gate / (1.0 + jnp.exp(-gate)) * val).astype(BF16)


def _ffn_up(h, w_up, conv_w, conv_b, layer, seq):
    n = h.shape[0]
    tm = FFN_TOKENS
    tn = 512
    nj = D_FF // tn
    tpb = seq // tm
    kern = functools.partial(_ffn_up_kernel, tiles_per_batch=tpb, tm=tm)
    return pl.pallas_call(
        kern,
        grid=(nj, n // tm),
        in_specs=[
            pl.BlockSpec((tm, D_MODEL), lambda j, i: (i, 0)),
            pl.BlockSpec((None, D_MODEL, tn), lambda j, i: (layer, 0, j)),
            pl.BlockSpec((None, D_MODEL, tn), lambda j, i: (layer, 0, j + nj)),
            pl.BlockSpec((None, CONV_WIDTH, tn), lambda j, i: (layer, 0, j)),
            pl.BlockSpec((None, CONV_WIDTH, tn), lambda j, i: (layer, 0, j + nj)),
            pl.BlockSpec((None, 1, tn), lambda j, i: (layer, 0, j)),
            pl.BlockSpec((None, 1, tn), lambda j, i: (layer, 0, j + nj)),
        ],
        out_specs=pl.BlockSpec((tm, tn), lambda j, i: (i, j)),
        out_shape=jax.ShapeDtypeStruct((n, D_FF), BF16),
        scratch_shapes=[pltpu.VMEM((D_MODEL, tn), BF16), pltpu.VMEM((D_MODEL, tn), BF16),
                        pltpu.VMEM((8, tn), F32), pltpu.VMEM((8, tn), F32)],
        compiler_params=pltpu.CompilerParams(
            dimension_semantics=("arbitrary", "arbitrary"), vmem_limit_bytes=VMEM_LIMIT),
        name="ffn_up",
    )(h, w_up, w_up, conv_w, conv_w, conv_b, conv_b)


def _ffn_down_kernel(a_ref, w_ref, x_ref, out_ref):
    a = a_ref[...]
    tn = 2 * MXU_WIDTH
    for c in range(D_MODEL // tn):
        cols = slice(c * tn, (c + 1) * tn)
        out_ref[:, cols] = x_ref[:, cols] + _dot(a, w_ref[:, cols])


def _ffn_down(act, w_down, x2):
    n = x2.shape[0]
    tm = 512
    row = lambda i: (i, 0)
    return pl.pallas_call(
        _ffn_down_kernel,
        grid=(n // tm,),
        in_specs=[
            pl.BlockSpec((tm, D_FF), row),
            pl.BlockSpec((D_FF, D_MODEL), lambda i: (0, 0), pipeline_mode=pl.Buffered(1)),
            pl.BlockSpec((tm, D_MODEL), row),
        ],
        out_specs=pl.BlockSpec((tm, D_MODEL), row),
        out_shape=jax.ShapeDtypeStruct((n, D_MODEL), F32),
        compiler_params=pltpu.CompilerParams(
            dimension_semantics=("parallel",), vmem_limit_bytes=VMEM_LIMIT),
        name="ffn_down",
    )(act, w_down, x2)


def _rope_tables(seq):
    def angles(dim):
        inv_freq = ROPE_THETA ** (-np.arange(0, dim, 2, dtype=np.float64) / dim)
        ang = np.arange(seq, dtype=np.float64)[:, None] * inv_freq[None, :]
        return np.cos(ang), np.sin(ang)

    c, s = angles(HEAD_DIM)
    cosc = np.concatenate([c, c], axis=1)
    sinc = np.concatenate([-s, s], axis=1)
    c, s = angles(MLA_ROPE_DIM)
    z = np.zeros_like(c)
    cosd = np.concatenate([c, z, c, z], axis=1)
    sind = np.concatenate([-s, z, s, z], axis=1)
    return tuple(jnp.asarray(t, F32) for t in (cosc, sinc, cosd, sind))


def _spread_rope(t):
    half = MLA_ROPE_DIM // 2
    z = jnp.zeros(t.shape[:-1] + (half,), t.dtype)
    return jnp.concatenate([t[..., :half], z, t[..., half:], z], axis=-1)


def _layout_w_in(w):
    gw = GROUP_WIDTH
    o = 0
    pieces = {}
    for name, width in (("aq", gw), ("ak", gw), ("av", gw), ("af", GROUP_HEADS), ("bq", gw), ("bk", gw),
                        ("bv", gw), ("cq", gw), ("ck", gw), ("cv", gw), ("dcq", MLA_Q_RANK),
                        ("dckv", MLA_KV_RANK), ("dkr", MLA_ROPE_DIM)):
        pieces[name] = w[:, o:o + width]
        o += width
    pad = jnp.zeros((w.shape[0], gw - MLA_KV_RANK - HEAD_DIM), w.dtype)
    tiles = [pieces["aq"], pieces["ak"], pieces["cq"], pieces["ck"], pieces["dcq"],
             pieces["dckv"], _spread_rope(pieces["dkr"]), pad, pieces["bq"], pieces["bk"]]
    w_main = jnp.concatenate(tiles, axis=1).astype(BF16)
    wvt = jnp.stack([pieces["av"].T, pieces["bv"].T, pieces["cv"].T]).astype(BF16)
    wf = jnp.zeros((8, w.shape[0]), F32).at[:GROUP_HEADS].set(pieces["af"].T).astype(BF16)
    return w_main, wvt, wf


def _layout_w_uq(w):
    w = w.reshape(MLA_Q_RANK, GROUP_HEADS, MLA_QK_DIM)
    w = jnp.concatenate([w[..., :MLA_NOPE_DIM], _spread_rope(w[..., MLA_NOPE_DIM:])], axis=-1)
    return w.reshape(MLA_Q_RANK, GROUP_HEADS * WIDE_HEAD).astype(BF16)


def _layout_w_ukv(w):
    w = w.reshape(MLA_KV_RANK, GROUP_HEADS, 2, HEAD_DIM)
    wk = w[:, :, 0, :].reshape(MLA_KV_RANK, GROUP_WIDTH).astype(BF16)
    wvt = w[:, :, 1, :].reshape(MLA_KV_RANK, GROUP_WIDTH).T.astype(BF16)
    return wk, wvt


def _logit_bound(gq, gk, dim, scale):
    return dim * scale * LOG2E * jnp.max(jnp.abs(gq)) * jnp.max(jnp.abs(gk)) * 1.01 + 0.5


def _layout_qk_gain(g):
    return jnp.concatenate([g[:MLA_NOPE_DIM], _spread_rope(g[MLA_NOPE_DIM:])])[None, :]


def kernel(x, attn_norm, w_in, b_forget, fox_q_norm, fox_k_norm, moba_q_norm, moba_k_norm, mla_cq_norm, mla_ckv_norm, w_uq, w_ukv, mla_q_norm, mla_k_norm, mix_out_norm, w_out, ffn_norm, w_up, conv_w, conv_b, w_down):
    batch, seq, d_model = x.shape
    depth = w_in.shape[0]
    assert d_model == D_MODEL and seq % TILE == 0 and seq % FFN_TOKENS == 0
    n = batch * seq
    x2 = x.reshape(n, d_model)
    cosc, sinc, cosd, sind = _rope_tables(seq)
    row = lambda v: v[None, :]

    for l in range(depth):
        w_main, wvt, wf = _layout_w_in(w_in[l])
        bf = jnp.zeros((8, 128), F32).at[:GROUP_HEADS, :].set(b_forget[l][:, None])
        proj, vt3, negc = _inproj(x2, row(attn_norm[l]), w_main, wvt, wf, bf, batch, seq)
        negc_col = negc.transpose(0, 2, 1).reshape(n, 8)

        bound_a = _logit_bound(fox_q_norm[l], fox_k_norm[l], HEAD_DIM, SCALE)
        bound_c = _logit_bound(moba_q_norm[l], moba_k_norm[l], HEAD_DIM, SCALE)
        bound_d = _logit_bound(mla_q_norm[l], mla_k_norm[l], MLA_QK_DIM, SCALE_MLA)
        big_c = jnp.maximum(MOBA_MIN_MASK, jnp.exp2(jnp.ceil(jnp.log2(4.0 * bound_c + 64.0))))
        bounds = jnp.zeros((8, HEAD_DIM), F32).at[:4].set(
            jnp.stack([bound_a, bound_c, big_c, bound_d])[:, None])

        wuk, wuvt = _layout_w_ukv(w_ukv[l])
        qa, ka, qc, kc, qd, kd, vtd = _prep(
            proj, negc_col, bounds, row(fox_q_norm[l]), row(fox_k_norm[l]), row(moba_q_norm[l]),
            row(moba_k_norm[l]), row(mla_cq_norm[l]), row(mla_ckv_norm[l]),
            _layout_qk_gain(mla_q_norm[l]), _layout_qk_gain(mla_k_norm[l]),
            _layout_w_uq(w_uq[l]), wuk, wuvt, cosc, sinc, cosd, sind, batch, seq)

        o_a = _bounded_or_online(bound_a, "fox_attn", qa, ka, vt3, 0, batch, seq)
        o_b = _stickbreak_attn(proj, vt3, batch, seq)
        o_c = _bounded_or_online(bound_c, "moba_attn", qc, kc, vt3, 2, batch, seq)
        o_d = _bounded_or_online(bound_d, "mla_attn", qd, kd, vtd, None, batch, seq)

        x2, h_ffn = _outproj(o_a, o_b, o_c, o_d, mix_out_norm[l].reshape(N_GROUPS, GROUP_WIDTH),
                             w_out[l].astype(BF16), x2, row(ffn_norm[l]))

        act = _ffn_up(h_ffn, w_up, conv_w, conv_b[:, None, :], l, seq)
        x2 = _ffn_down(act, w_down[l].astype(BF16), x2)

    return x2.reshape(batch, seq, d_model)
```

```python
import functools
import math

import numpy as np
import jax
import jax.numpy as jnp
from jax import lax
from jax.experimental import pallas as pl
from jax.experimental.pallas import tpu as pltpu

F32 = jnp.float32
BF16 = jnp.bfloat16

D_MODEL = 2048
HEAD_DIM = 128
GROUP_HEADS = 4
GROUP_WIDTH = GROUP_HEADS * HEAD_DIM
N_GROUPS = 4
MOBA_BLOCK = 256
MOBA_TOP_K = 3
MOBA_MAX_BLOCKS = 16
MLA_Q_RANK = 512
MLA_KV_RANK = 256
MLA_NOPE_DIM = 128
MLA_ROPE_DIM = 64
MLA_QK_DIM = MLA_NOPE_DIM + MLA_ROPE_DIM
WIDE_HEAD = 256
D_FF = 5632
CONV_WIDTH = 3
ROPE_THETA = 10000.0
EPS = 1e-6

T_AQ, T_AK, T_CQ, T_CK, T_DCQ, T_DKV, T_BQ, T_BK = range(8)
PROJ_TILES = 8
V_GROUPS = 3

TILE = 512
NEG = -1e30
VMEM_LIMIT = 56 * 1024 * 1024
LOG2E = math.log2(math.e)
SCALE = HEAD_DIM ** -0.5
SCALE_MLA = MLA_QK_DIM ** -0.5
FFN_TOKENS = 1024
MXU_WIDTH = 256
SUM_ROWS = 16
MAX_SAFE_BOUND = 56.0
MOBA_MIN_MASK = 256.0
STICK_DONE = 160.0


def _dot(a, b):
    return jnp.dot(a, b, preferred_element_type=F32)


def _dot_nt(a, b):
    return lax.dot_general(a, b, (((1,), (1,)), ((), ())), preferred_element_type=F32)


def _split3(a):
    hi = a.astype(BF16)
    r1 = a - hi.astype(F32)
    mid = r1.astype(BF16)
    lo = (r1 - mid.astype(F32)).astype(BF16)
    return hi, mid, lo


def _log_sigmoid_pair(z):
    sp = jnp.log(1.0 + jnp.exp(-jnp.abs(z)))
    return jnp.minimum(z, 0.0) - sp, -jnp.maximum(z, 0.0) - sp


def _rms(v, g, denom=None):
    if denom is None:
        ms = jnp.mean(v * v, axis=-1, keepdims=True)
    else:
        ms = jnp.sum(v * v, axis=-1, keepdims=True) * (1.0 / denom)
    return v * lax.rsqrt(ms + EPS) * g


def _sumsq_lanes(v):
    return _dot((v * v).astype(BF16), jnp.ones((v.shape[-1], HEAD_DIM), BF16))


def _rms_lanes(v, g, denom=None):
    width = v.shape[-1]
    r = lax.rsqrt(_sumsq_lanes(v) * (1.0 / (denom or width)) + EPS)
    if width > HEAD_DIM:
        r = jnp.concatenate([r] * (width // HEAD_DIM), axis=1)
    return v * r * g


def _inproj_kernel(x_ref, g_ref, w_ref, wvt_ref, wf_ref, bf_ref, proj_ref, vt_ref, negc_ref,
                   carry_scr, *, tiles_per_batch, tm):
    i = pl.program_id(0)
    h = _rms(x_ref[...], g_ref[...]).astype(BF16)
    z = _dot_nt(wf_ref[...], h) + bf_ref[:, 0:1]
    logf, _ = _log_sigmoid_pair(z)
    r = lax.broadcasted_iota(jnp.int32, (tm, tm), 0)
    c = lax.broadcasted_iota(jnp.int32, (tm, tm), 1)
    tri = jnp.where(r <= c, 1.0, 0.0).astype(BF16)
    hi, mid, lo = _split3(logf)
    cum = _dot(hi, tri) + _dot(mid, tri) + _dot(lo, tri)

    @pl.when(i % tiles_per_batch == 0)
    def _():
        carry_scr[...] = jnp.zeros_like(carry_scr)

    cum = cum + carry_scr[:, 0:1]
    carry_scr[...] = jnp.broadcast_to(cum[:, tm - 1:tm], carry_scr.shape)
    negc_ref[...] = cum * (-LOG2E)

    gw = GROUP_WIDTH
    for jt in range(PROJ_TILES):
        proj_ref[:, jt * gw:(jt + 1) * gw] = _dot(h, w_ref[:, jt * gw:(jt + 1) * gw]).astype(BF16)
    for gi in range(V_GROUPS):
        vt_ref[gi] = _dot_nt(wvt_ref[gi], h).astype(BF16)


def _layer_spec(shape, layer, **kw):
    zeros = (0,) * len(shape)
    return pl.BlockSpec((None,) + tuple(shape), lambda *_: (layer,) + zeros, **kw)


def _inproj(x2, g, w, wvt, wf, bf, layer, batch, seq):
    n = x2.shape[0]
    tm = TILE
    tpb = seq // tm
    kern = functools.partial(_inproj_kernel, tiles_per_batch=tpb, tm=tm)
    resident = pl.Buffered(1)
    return pl.pallas_call(
        kern,
        grid=(n // tm,),
        in_specs=[
            pl.BlockSpec((tm, D_MODEL), lambda i: (i, 0)),
            _layer_spec((1, D_MODEL), layer),
            _layer_spec((D_MODEL, PROJ_TILES * GROUP_WIDTH), layer, pipeline_mode=resident),
            _layer_spec((V_GROUPS, GROUP_WIDTH, D_MODEL), layer, pipeline_mode=resident),
            _layer_spec((8, D_MODEL), layer),
            _layer_spec((8, 128), layer),
        ],
        out_specs=[
            pl.BlockSpec((tm, PROJ_TILES * GROUP_WIDTH), lambda i: (i, 0)),
            pl.BlockSpec((V_GROUPS, None, None, GROUP_WIDTH, tm), lambda i: (0, i // tpb, i % tpb, 0, 0)),
            pl.BlockSpec((None, 8, tm), lambda i: (i // tpb, 0, i % tpb)),
        ],
        out_shape=[
            jax.ShapeDtypeStruct((n, PROJ_TILES * GROUP_WIDTH), BF16),
            jax.ShapeDtypeStruct((V_GROUPS, batch, tpb, GROUP_WIDTH, tm), BF16),
            jax.ShapeDtypeStruct((batch, 8, seq), F32),
        ],
        scratch_shapes=[pltpu.VMEM((8, 128), F32)],
        compiler_params=pltpu.CompilerParams(
            dimension_semantics=("arbitrary",), vmem_limit_bytes=VMEM_LIMIT),
        name="inproj",
    )(x2, g, w, wvt, wf, bf)


def _prep_kernel(a_ref, c_ref, d_ref, nc_ref, bnd_ref, gaq, gak, gcq, gck, gdcq, gdckv, gdq, gdk, wuq, wuk, wuvt,
                 cosc, sinc, cosd, sind,
                 qa_ref, ka_ref, qc_ref, kc_ref, qd_ref, kd_ref, vtd_ref, kmt_scr,
                 *, tiles_per_batch, tm):
    i = pl.program_id(0)
    t = i % tiles_per_batch
    hd = HEAD_DIM
    wh = WIDE_HEAD
    blocks_per_tile = tm // MOBA_BLOCK
    lane = lax.broadcasted_iota(jnp.int32, (tm, hd), 1)
    bound_a = bnd_ref[0:1, :]
    bound_c = bnd_ref[1:2, :]
    big_c = bnd_ref[2:3, :]
    bound_d = bnd_ref[3:4, :]

    nc = nc_ref[...]
    for h in range(GROUP_HEADS):
        q = a_ref[:, h * hd:(h + 1) * hd].astype(F32)
        qa_ref[:, h * wh:h * wh + hd] = (_rms_lanes(q, gaq[...]) * (SCALE * LOG2E)).astype(BF16)
        k = a_ref[:, GROUP_WIDTH + h * hd:GROUP_WIDTH + (h + 1) * hd].astype(F32)
        ka_ref[:, h * wh:h * wh + hd] = _rms_lanes(k, gak[...]).astype(BF16)
        hi, mid, lo = (p.astype(F32) for p in _split3(nc[:, h:h + 1]))
        k_aug = jnp.where(lane == 0, hi, jnp.where(lane == 1, mid, jnp.where(lane == 2, lo, jnp.where(
            lane < 6, 1.0, jnp.where(lane == 6, -bound_a, 0.0)))))
        q_aug = jnp.where(lane < 3, 1.0, jnp.where(lane == 3, -hi, jnp.where(lane == 4, -mid, jnp.where(
            lane == 5, -lo, jnp.where(lane == 6, 1.0, 0.0)))))
        ka_ref[:, h * wh + hd:(h + 1) * wh] = k_aug.astype(BF16)
        qa_ref[:, h * wh + hd:(h + 1) * wh] = q_aug.astype(BF16)

    @pl.when(i == 0)
    def _():
        kmt_scr[...] = jnp.zeros_like(kmt_scr)

    cc = cosc[...]
    sc = sinc[...]
    sub8 = lax.broadcasted_iota(jnp.int32, (8, hd), 0)
    key_blk = t * blocks_per_tile + lax.broadcasted_iota(jnp.int32, (tm, hd), 0) // MOBA_BLOCK

    def rope_c(v):
        return v * cc + pltpu.roll(v, hd // 2, axis=1) * sc

    kc_aug = jnp.where(lane == key_blk, big_c, jnp.where(lane == MOBA_MAX_BLOCKS, -(bound_c + big_c), 0.0))
    kc_aug = kc_aug.astype(BF16)
    for h in range(GROUP_HEADS):
        k = rope_c(_rms_lanes(c_ref[:, GROUP_WIDTH + h * hd:GROUP_WIDTH + (h + 1) * hd].astype(F32), gck[...]))
        kc_ref[:, h * wh:h * wh + hd] = k.astype(BF16)
        kc_ref[:, h * wh + hd:(h + 1) * wh] = kc_aug
        for blk in range(blocks_per_tile):
            km = jnp.mean(k[blk * MOBA_BLOCK:(blk + 1) * MOBA_BLOCK], axis=0, keepdims=True)
            nblk = t * blocks_per_tile + blk
            base = pl.multiple_of(h * hd + (nblk // 8) * 8, 8)
            old = kmt_scr[pl.ds(base, 8), h * hd:(h + 1) * hd]
            kmt_scr[pl.ds(base, 8), h * hd:(h + 1) * hd] = jnp.where(sub8 == nblk % 8, km, old)

    qs = []
    for h in range(GROUP_HEADS):
        q = rope_c(_rms_lanes(c_ref[:, h * hd:(h + 1) * hd].astype(F32), gcq[...])) * (SCALE * LOG2E)
        qc_ref[:, h * wh:h * wh + hd] = q.astype(BF16)
        qs.append(q)
    q_all = jnp.concatenate(qs, axis=1)
    qh, qm, _ = _split3(q_all)
    kh, km_, _ = _split3(kmt_scr[...])
    gate_t = _dot_nt(kh, qh) + _dot_nt(km_, qh) + _dot_nt(kh, qm)

    nb = MOBA_MAX_BLOCKS
    blk_row = lax.broadcasted_iota(jnp.int32, (nb, tm), 0)
    own = t * blocks_per_tile + lax.broadcasted_iota(jnp.int32, (nb, tm), 1) // MOBA_BLOCK
    blk_f = blk_row.astype(F32)
    ninf = jnp.float32(-jnp.inf)
    no_rows = jnp.zeros((hd - nb, tm), F32)
    for h in range(GROUP_HEADS):
        g = jnp.where(blk_row < own, gate_t[h * hd:h * hd + nb, :], ninf)
        sel = jnp.where(blk_row == own, 1.0, 0.0)
        for _ in range(MOBA_TOP_K):
            mx = jnp.max(g, axis=0, keepdims=True)
            cand = jnp.where(g == mx, blk_f, 1e9)
            cand = jnp.where(mx > ninf, cand, 1e9)
            idx = jnp.min(cand, axis=0, keepdims=True)
            pick = blk_f == idx
            sel = jnp.where(pick, 1.0, sel)
            g = jnp.where(pick, ninf, g)
        sel_rows = jnp.concatenate([sel, no_rows], axis=0).T
        q_aug = jnp.where(lane == nb, 1.0, sel_rows)
        qc_ref[:, h * wh + hd:(h + 1) * wh] = q_aug.astype(BF16)

    cd = cosd[...]
    sd = sind[...]
    free_lane = lane == MLA_ROPE_DIM // 2

    def rope_d(v):
        return v * cd + pltpu.roll(v, hd // 2, axis=1) * sd

    cq = _rms_lanes(d_ref[:, 0:MLA_Q_RANK].astype(F32), gdcq[...]).astype(BF16)
    qf = _dot(cq, wuq[...])
    for h in range(GROUP_HEADS):
        qn = _rms_lanes(qf[:, h * wh:(h + 1) * wh], gdq[...], denom=MLA_QK_DIM)
        qd_ref[:, h * wh:h * wh + hd] = (qn[:, :hd] * (SCALE_MLA * LOG2E)).astype(BF16)
        q_rot = rope_d(qn[:, hd:]) * (SCALE_MLA * LOG2E)
        qd_ref[:, h * wh + hd:(h + 1) * wh] = jnp.where(free_lane, 1.0, q_rot).astype(BF16)

    ckv = _rms_lanes(d_ref[:, MLA_Q_RANK:MLA_Q_RANK + MLA_KV_RANK].astype(F32), gdckv[...]).astype(BF16)
    knf = _dot(ckv, wuk[...])
    vtd_ref[...] = _dot_nt(wuvt[...], ckv).astype(BF16)
    kr = d_ref[:, MLA_Q_RANK + MLA_KV_RANK:MLA_Q_RANK + MLA_KV_RANK + hd].astype(F32)
    ss_r = _sumsq_lanes(kr)
    gk = gdk[...]
    for h in range(GROUP_HEADS):
        kn = knf[:, h * hd:(h + 1) * hd]
        r = lax.rsqrt((_sumsq_lanes(kn) + ss_r) * (1.0 / MLA_QK_DIM) + EPS)
        kd_ref[:, h * wh:h * wh + hd] = (kn * r * gk[:, :hd]).astype(BF16)
        k_rot = rope_d(kr * r * gk[:, hd:])
        kd_ref[:, h * wh + hd:(h + 1) * wh] = jnp.where(free_lane, -bound_d, k_rot).astype(BF16)


def _prep(proj, negc_col, bounds, gaq, gak, gcq, gck, gdcq, gdckv, gdq, gdk, wuq, wuk, wuvt,
          cosc, sinc, cosd, sind, layer, batch, seq):
    n = proj.shape[0]
    tm = TILE
    tpb = seq // tm
    assert seq // MOBA_BLOCK <= MOBA_MAX_BLOCKS
    kern = functools.partial(_prep_kernel, tiles_per_batch=tpb, tm=tm)
    row = lambda i: (i, 0)
    tab = lambda i: (i % tpb, 0)
    w2 = 2 * GROUP_WIDTH
    return pl.pallas_call(
        kern,
        grid=(n // tm,),
        in_specs=[
            pl.BlockSpec((tm, w2), lambda i: (i, 0)),
            pl.BlockSpec((tm, w2), lambda i: (i, 1)),
            pl.BlockSpec((tm, w2), lambda i: (i, 2)),
            pl.BlockSpec((tm, 8), row),
            _layer_spec((8, HEAD_DIM), layer),
            _layer_spec((1, HEAD_DIM), layer), _layer_spec((1, HEAD_DIM), layer),
            _layer_spec((1, HEAD_DIM), layer), _layer_spec((1, HEAD_DIM), layer),
            _layer_spec((1, MLA_Q_RANK), layer), _layer_spec((1, MLA_KV_RANK), layer),
            _layer_spec((1, WIDE_HEAD), layer), _layer_spec((1, WIDE_HEAD), layer),
            _layer_spec((MLA_Q_RANK, GROUP_HEADS * WIDE_HEAD), layer),
            _layer_spec((MLA_KV_RANK, GROUP_WIDTH), layer),
            _layer_spec((GROUP_WIDTH, MLA_KV_RANK), layer),
            pl.BlockSpec((tm, HEAD_DIM), tab), pl.BlockSpec((tm, HEAD_DIM), tab),
            pl.BlockSpec((tm, HEAD_DIM), tab), pl.BlockSpec((tm, HEAD_DIM), tab),
        ],
        out_specs=[pl.BlockSpec((tm, w2), row)] * 6 + [
            pl.BlockSpec((None, None, GROUP_WIDTH, tm), lambda i: (i // tpb, i % tpb, 0, 0)),
        ],
        out_shape=[jax.ShapeDtypeStruct((n, w2), BF16)] * 6 + [
            jax.ShapeDtypeStruct((batch, tpb, GROUP_WIDTH, tm), BF16),
        ],
        scratch_shapes=[pltpu.VMEM((GROUP_WIDTH, GROUP_WIDTH), F32)],
        compiler_params=pltpu.CompilerParams(
            dimension_semantics=("arbitrary",), vmem_limit_bytes=VMEM_LIMIT),
        name="prep",
    )(proj, proj, proj, negc_col, bounds, gaq, gak, gcq, gck, gdcq, gdckv, gdq, gdk, wuq, wuk, wuvt,
      cosc, sinc, cosd, sind)


def _key_query_iota(t):
    key = lax.broadcasted_iota(jnp.int32, (t, t), 0)
    qry = lax.broadcasted_iota(jnp.int32, (t, t), 1)
    return key, qry


def _softmax_step(s, vt, carry):
    m, acc = carry
    m_new = jnp.maximum(m, jnp.max(s, axis=0, keepdims=True))
    alpha = jnp.exp2(m - m_new)
    p = jnp.exp2(s - m_new)
    return m_new, alpha * acc + _dot(vt, p.astype(BF16))


def _softmax_init(t):
    return (jnp.full((1, t), NEG, F32), jnp.zeros((HEAD_DIM + SUM_ROWS, t), F32))


def _softmax_attn_kernel(q_ref, k_ref, vt_ref, o_ref, acc_scr, *, t, bounded):
    qi = pl.program_id(1)
    key, qry = _key_query_iota(t)
    causal = key <= qry
    dk = WIDE_HEAD
    dv = HEAD_DIM
    ones_rows = jnp.ones((SUM_ROWS, t), BF16)

    def logits(j, h, diag):
        off = pl.multiple_of(j * t, t)
        s = _dot_nt(k_ref[pl.ds(off, t), h * dk:(h + 1) * dk], q_ref[:, h * dk:(h + 1) * dk])
        return jnp.where(causal, s, NEG) if diag else s

    def values(j, h):
        return jnp.concatenate([vt_ref[j, h * dv:(h + 1) * dv, :], ones_rows], axis=0)

    if bounded:
        acc_scr[...] = jnp.zeros_like(acc_scr)

        def step(tiles, diag):
            s = [[logits(j, h, diag) for h in range(GROUP_HEADS)] for j in tiles]
            for n, j in enumerate(tiles):
                for h in range(GROUP_HEADS):
                    acc_scr[h] += _dot(values(j, h), jnp.exp2(s[n][h]).astype(BF16))

        def pair(jj, c):
            step((2 * jj, 2 * jj + 1), False)
            return c

        lax.fori_loop(0, qi // 2, pair, 0)

        @pl.when(qi % 2 == 1)
        def _():
            step((qi - 1,), False)

        step((qi,), True)
        accs = [acc_scr[h] for h in range(GROUP_HEADS)]
    else:
        def step(j, carry, diag):
            out = []
            for h in range(GROUP_HEADS):
                out.append(_softmax_step(logits(j, h, diag), values(j, h), carry[h]))
            return tuple(out)

        carry = step(qi, tuple(_softmax_init(t) for _ in range(GROUP_HEADS)), True)
        carry = lax.fori_loop(0, qi, lambda j, c: step(j, c, False), carry)
        accs = [c[1] for c in carry]

    for h in range(GROUP_HEADS):
        a = accs[h]
        o_ref[:, h * dv:(h + 1) * dv] = (a[:dv] / a[dv:dv + 1]).T.astype(BF16)


def _softmax_attn(name, q_arr, k_arr, vt_arr, vt_group, batch, seq, bounded):
    t = TILE
    nq = seq // t
    n = batch * seq
    kern = functools.partial(_softmax_attn_kernel, t=t, bounded=bounded)
    if vt_group is None:
        vt_spec = pl.BlockSpec((None, nq, GROUP_WIDTH, t), lambda b, i: (b, 0, 0, 0))
    else:
        vt_spec = pl.BlockSpec((None, None, nq, GROUP_WIDTH, t), lambda b, i: (vt_group, b, 0, 0, 0))
    return pl.pallas_call(
        kern,
        grid=(batch, nq),
        in_specs=[
            pl.BlockSpec((t, GROUP_HEADS * WIDE_HEAD), lambda b, i: (b * nq + i, 0)),
            pl.BlockSpec((seq, GROUP_HEADS * WIDE_HEAD), lambda b, i: (b, 0)),
            vt_spec,
        ],
        out_specs=pl.BlockSpec((t, GROUP_WIDTH), lambda b, i: (b * nq + i, 0)),
        out_shape=jax.ShapeDtypeStruct((n, GROUP_WIDTH), BF16),
        scratch_shapes=[pltpu.VMEM((GROUP_HEADS, HEAD_DIM + SUM_ROWS, t), F32)],
        compiler_params=pltpu.CompilerParams(
            dimension_semantics=("parallel", "parallel"), vmem_limit_bytes=VMEM_LIMIT),
        name=name + ("" if bounded else "_online"),
    )(q_arr, k_arr, vt_arr)


def _bounded_or_online(bound, name, *args):
    return lax.cond(bound < MAX_SAFE_BOUND,
                    lambda: _softmax_attn(name, *args, bounded=True),
                    lambda: _softmax_attn(name, *args, bounded=False))


def _stickbreak_kernel(q_ref, k_ref, vt_ref, o_ref, acc_scr, rest_scr, *, t):
    qi = pl.program_id(1)
    key, qry = _key_query_iota(t)
    strict = key < qry
    half = t // 2
    kk, kj = _key_query_iota(half)
    later = jnp.where(kj > kk, 1.0, 0.0).astype(BF16)
    hd = HEAD_DIM
    acc_scr[...] = jnp.zeros_like(acc_scr)
    rest_scr[...] = jnp.zeros_like(rest_scr)

    def step(j, diag):
        off = pl.multiple_of(j * t, t)
        low = None
        zs = []
        for h in range(GROUP_HEADS):
            q = (q_ref[:, h * hd:(h + 1) * hd].astype(F32) * (SCALE * LOG2E)).astype(BF16)
            zs.append(_dot_nt(k_ref[pl.ds(off, t), h * hd:(h + 1) * hd], q))
        for h in range(GROUP_HEADS):
            used = rest_scr[h]
            z = zs[h]
            drop = jnp.maximum(z, 0.0) + jnp.log(1.0 + jnp.exp2(-jnp.abs(z))) * LOG2E
            if diag:
                drop = jnp.where(strict, drop, 0.0)
            d_top, d_bot = drop[:half], drop[half:]
            ex_bot = _dot(later, d_bot.astype(BF16))
            tot_bot = ex_bot[0:1] + d_bot[0:1]
            ex_top = _dot(later, d_top.astype(BF16))
            tot_top = ex_top[0:1] + d_top[0:1]
            between = jnp.concatenate([ex_top + (tot_bot + used), ex_bot + used], axis=0)
            w = jnp.exp2((z - drop) - between)
            if diag:
                w = jnp.where(strict, w, 0.0)
            acc_scr[h] += _dot(vt_ref[j, h * hd:(h + 1) * hd, :], w.astype(BF16))
            used = used + tot_top + tot_bot
            rest_scr[h] = used
            low = used if low is None else jnp.minimum(low, used)
        return jnp.min(low)

    def cond(c):
        j, low = c
        return jnp.logical_and(j >= 0, low < STICK_DONE)

    def body(c):
        j, _ = c
        return j - 1, step(j, False)

    lax.while_loop(cond, body, (qi - 1, step(qi, True)))
    for h in range(GROUP_HEADS):
        o_ref[:, h * hd:(h + 1) * hd] = acc_scr[h].T.astype(BF16)


def _stickbreak_attn(proj, vt3, batch, seq):
    t = TILE
    nq = seq // t
    n = batch * seq
    kern = functools.partial(_stickbreak_kernel, t=t)
    return pl.pallas_call(
        kern,
        grid=(batch, nq),
        in_specs=[
            pl.BlockSpec((t, GROUP_WIDTH), lambda b, i: (b * nq + i, T_BQ)),
            pl.BlockSpec((seq, GROUP_WIDTH), lambda b, i: (b, T_BK)),
            pl.BlockSpec((None, None, nq, GROUP_WIDTH, t), lambda b, i: (1, b, 0, 0, 0)),
        ],
        out_specs=pl.BlockSpec((t, GROUP_WIDTH), lambda b, i: (b * nq + i, 0)),
        out_shape=jax.ShapeDtypeStruct((n, GROUP_WIDTH), BF16),
        scratch_shapes=[pltpu.VMEM((GROUP_HEADS, HEAD_DIM, t), F32), pltpu.VMEM((GROUP_HEADS, 1, t), F32)],
        compiler_params=pltpu.CompilerParams(
            dimension_semantics=("parallel", "parallel"), vmem_limit_bytes=VMEM_LIMIT),
        name="stickbreak_attn",
    )(proj, proj, vt3)


def _outproj_kernel(oa_ref, ob_ref, oc_ref, od_ref, g_ref, w_ref, x_ref, gf_ref, out_ref, hf_ref):
    acc = x_ref[...]
    for gi, o_ref in enumerate((oa_ref, ob_ref, oc_ref, od_ref)):
        y = _rms(o_ref[...].astype(F32), g_ref[gi:gi + 1, :]).astype(BF16)
        acc = acc + _dot(y, w_ref[gi * GROUP_WIDTH:(gi + 1) * GROUP_WIDTH, :])
    out_ref[...] = acc
    hf_ref[...] = _rms(acc, gf_ref[...]).astype(BF16)


def _outproj(oa, ob, oc, od, g, w, x2, g_ffn, layer):
    n = x2.shape[0]
    tm = 512
    row = lambda i: (i, 0)
    return pl.pallas_call(
        _outproj_kernel,
        grid=(n // tm,),
        in_specs=[pl.BlockSpec((tm, GROUP_WIDTH), row)] * 4 + [
            _layer_spec((N_GROUPS, GROUP_WIDTH), layer),
            _layer_spec((N_GROUPS * GROUP_WIDTH, D_MODEL), layer, pipeline_mode=pl.Buffered(1)),
            pl.BlockSpec((tm, D_MODEL), row),
            _layer_spec((1, D_MODEL), layer),
        ],
        out_specs=[pl.BlockSpec((tm, D_MODEL), row), pl.BlockSpec((tm, D_MODEL), row)],
        out_shape=[jax.ShapeDtypeStruct((n, D_MODEL), F32), jax.ShapeDtypeStruct((n, D_MODEL), BF16)],
        compiler_params=pltpu.CompilerParams(
            dimension_semantics=("parallel",), vmem_limit_bytes=VMEM_LIMIT),
        name="outproj",
    )(oa, ob, oc, od, g, w, x2, g_ffn)


def _ffn_up_kernel(h_ref, wg_ref, wv_ref, cwg_ref, cwv_ref, cbg_ref, cbv_ref,
                   out_ref, wg_scr, wv_scr, tail_g, tail_v, *, tiles_per_batch, tm):
    i = pl.program_id(1)

    @pl.when(i == 0)
    def _():
        wg_scr[...] = wg_ref[...].astype(BF16)
        wv_scr[...] = wv_ref[...].astype(BF16)

    @pl.when(i % tiles_per_batch == 0)
    def _():
        tail_g[...] = jnp.zeros_like(tail_g)
        tail_v[...] = jnp.zeros_like(tail_v)

    def conv(h, w_scr, cw_ref, cb_ref, prev):
        u = _dot(h, w_scr[...])
        ue = jnp.concatenate([prev, u], axis=0)
        y = cb_ref[...] + cw_ref[2:3, :] * u
        for back in range(1, CONV_WIDTH):
            y = y + cw_ref[CONV_WIDTH - 1 - back:CONV_WIDTH - back, :] * pltpu.roll(ue, back, axis=0)[8:]
        return y, u[u.shape[0] - 8:]

    h = h_ref[...]
    gate, tail_g[...] = conv(h, wg_scr, cwg_ref, cbg_ref, tail_g[...])
    val, tail_v[...] = conv(h, wv_scr, cwv_ref, cbv_ref, tail_v[...])
    out_ref[...] = (gate / (1.0 + jnp.exp(-gate)) * val).astype(BF16)


def _ffn_up(h, w_up, conv_w, conv_b, layer, seq):
    n = h.shape[0]
    tm = FFN_TOKENS
    tn = 512
    nj = D_FF // tn
    tpb = seq // tm
    kern = functools.partial(_ffn_up_kernel, tiles_per_batch=tpb, tm=tm)
    return pl.pallas_call(
        kern,
        grid=(nj, n // tm),
        in_specs=[
            pl.BlockSpec((tm, D_MODEL), lambda j, i: (i, 0)),
            pl.BlockSpec((None, D_MODEL, tn), lambda j, i: (layer, 0, j)),
            pl.BlockSpec((None, D_MODEL, tn), lambda j, i: (layer, 0, j + nj)),
            pl.BlockSpec((None, CONV_WIDTH, tn), lambda j, i: (layer, 0, j)),
            pl.BlockSpec((None, CONV_WIDTH, tn), lambda j, i: (layer, 0, j + nj)),
            pl.BlockSpec((None, 1, tn), lambda j, i: (layer, 0, j)),
            pl.BlockSpec((None, 1, tn), lambda j, i: (layer, 0, j + nj)),
        ],
        out_specs=pl.BlockSpec((tm, tn), lambda j, i: (i, j)),
        out_shape=jax.ShapeDtypeStruct((n, D_FF), BF16),
        scratch_shapes=[pltpu.VMEM((D_MODEL, tn), BF16), pltpu.VMEM((D_MODEL, tn), BF16),
                        pltpu.VMEM((8, tn), F32), pltpu.VMEM((8, tn), F32)],
        compiler_params=pltpu.CompilerParams(
            dimension_semantics=("arbitrary", "arbitrary"), vmem_limit_bytes=VMEM_LIMIT),
        name="ffn_up",
    )(h, w_up, w_up, conv_w, conv_w, conv_b, conv_b)


def _ffn_down_kernel(a_ref, w_ref, x_ref, out_ref):
    a = a_ref[...]
    tn = 2 * MXU_WIDTH
    for c in range(D_MODEL // tn):
        cols = slice(c * tn, (c + 1) * tn)
        out_ref[:, cols] = x_ref[:, cols] + _dot(a, w_ref[:, cols])


def _ffn_down(act, w_down, x2, layer):
    n = x2.shape[0]
    tm = 512
    row = lambda i: (i, 0)
    return pl.pallas_call(
        _ffn_down_kernel,
        grid=(n // tm,),
        in_specs=[
            pl.BlockSpec((tm, D_FF), row),
            _layer_spec((D_FF, D_MODEL), layer, pipeline_mode=pl.Buffered(1)),
            pl.BlockSpec((tm, D_MODEL), row),
        ],
        out_specs=pl.BlockSpec((tm, D_MODEL), row),
        out_shape=jax.ShapeDtypeStruct((n, D_MODEL), F32),
        compiler_params=pltpu.CompilerParams(
            dimension_semantics=("parallel",), vmem_limit_bytes=VMEM_LIMIT),
        name="ffn_down",
    )(act, w_down, x2)


def _rope_tables(seq):
    def angles(dim):
        inv_freq = ROPE_THETA ** (-np.arange(0, dim, 2, dtype=np.float64) / dim)
        ang = np.arange(seq, dtype=np.float64)[:, None] * inv_freq[None, :]
        return np.cos(ang), np.sin(ang)

    c, s = angles(HEAD_DIM)
    cosc = np.concatenate([c, c], axis=1)
    sinc = np.concatenate([-s, s], axis=1)
    c, s = angles(MLA_ROPE_DIM)
    z = np.zeros_like(c)
    cosd = np.concatenate([c, z, c, z], axis=1)
    sind = np.concatenate([-s, z, s, z], axis=1)
    return tuple(jnp.asarray(t, F32) for t in (cosc, sinc, cosd, sind))


def _spread_rope(t):
    half = MLA_ROPE_DIM // 2
    z = jnp.zeros(t.shape[:-1] + (half,), t.dtype)
    return jnp.concatenate([t[..., :half], z, t[..., half:], z], axis=-1)


def _layout_w_in(w):
    gw = GROUP_WIDTH
    o = 0
    pieces = {}
    for name, width in (("aq", gw), ("ak", gw), ("av", gw), ("af", GROUP_HEADS), ("bq", gw), ("bk", gw),
                        ("bv", gw), ("cq", gw), ("ck", gw), ("cv", gw), ("dcq", MLA_Q_RANK),
                        ("dckv", MLA_KV_RANK), ("dkr", MLA_ROPE_DIM)):
        pieces[name] = w[..., o:o + width]
        o += width
    pad = jnp.zeros(w.shape[:-1] + (gw - MLA_KV_RANK - HEAD_DIM,), w.dtype)
    tiles = [pieces["aq"], pieces["ak"], pieces["cq"], pieces["ck"], pieces["dcq"],
             pieces["dckv"], _spread_rope(pieces["dkr"]), pad, pieces["bq"], pieces["bk"]]
    w_main = jnp.concatenate(tiles, axis=-1).astype(BF16)
    wvt = jnp.stack([jnp.swapaxes(pieces[v], -1, -2) for v in ("av", "bv", "cv")], axis=1).astype(BF16)
    wf = jnp.pad(jnp.swapaxes(pieces["af"], -1, -2), ((0, 0), (0, 8 - GROUP_HEADS), (0, 0))).astype(BF16)
    return w_main, wvt, wf


def _layout_w_uq(w):
    w = w.reshape(-1, MLA_Q_RANK, GROUP_HEADS, MLA_QK_DIM)
    w = jnp.concatenate([w[..., :MLA_NOPE_DIM], _spread_rope(w[..., MLA_NOPE_DIM:])], axis=-1)
    return w.reshape(-1, MLA_Q_RANK, GROUP_HEADS * WIDE_HEAD).astype(BF16)


def _layout_w_ukv(w):
    w = w.reshape(-1, MLA_KV_RANK, GROUP_HEADS, 2, HEAD_DIM)
    wk = w[:, :, :, 0, :].reshape(-1, MLA_KV_RANK, GROUP_WIDTH).astype(BF16)
    wvt = jnp.swapaxes(w[:, :, :, 1, :].reshape(-1, MLA_KV_RANK, GROUP_WIDTH), -1, -2).astype(BF16)
    return wk, wvt


def _logit_bound(gq, gk, dim, scale):
    return dim * scale * LOG2E * jnp.max(jnp.abs(gq), axis=-1) * jnp.max(jnp.abs(gk), axis=-1) * 1.01 + 0.5


def _layout_qk_gain(g):
    return jnp.concatenate([g[:, :MLA_NOPE_DIM], _spread_rope(g[:, MLA_NOPE_DIM:])], axis=-1)[:, None, :]


def kernel(x, attn_norm, w_in, b_forget, fox_q_norm, fox_k_norm, moba_q_norm, moba_k_norm, mla_cq_norm, mla_ckv_norm, w_uq, w_ukv, mla_q_norm, mla_k_norm, mix_out_norm, w_out, ffn_norm, w_up, conv_w, conv_b, w_down):
    batch, seq, d_model = x.shape
    depth = w_in.shape[0]
    assert d_model == D_MODEL and seq % TILE == 0 and seq % FFN_TOKENS == 0
    n = batch * seq
    x2 = x.reshape(n, d_model)
    cosc, sinc, cosd, sind = _rope_tables(seq)

    rows = lambda v: v[:, None, :]
    w_main, wvt, wf = _layout_w_in(w_in)
    wuq = _layout_w_uq(w_uq)
    wuk, wuvt = _layout_w_ukv(w_ukv)
    w_out_bf = w_out.astype(BF16)
    w_down_bf = w_down.astype(BF16)
    bf = jnp.pad(jnp.broadcast_to(b_forget[:, :, None], (depth, GROUP_HEADS, 128)),
                 ((0, 0), (0, 8 - GROUP_HEADS), (0, 0)))
    bound_a = _logit_bound(fox_q_norm, fox_k_norm, HEAD_DIM, SCALE)
    bound_c = _logit_bound(moba_q_norm, moba_k_norm, HEAD_DIM, SCALE)
    bound_d = _logit_bound(mla_q_norm, mla_k_norm, MLA_QK_DIM, SCALE_MLA)
    big_c = jnp.maximum(MOBA_MIN_MASK, jnp.exp2(jnp.ceil(jnp.log2(4.0 * bound_c + 64.0))))
    bounds = jnp.stack([bound_a, bound_c, big_c, bound_d] + [jnp.zeros_like(bound_a)] * 4, axis=1)
    bounds = jnp.broadcast_to(bounds[:, :, None], (depth, 8, HEAD_DIM))
    gdq, gdk = _layout_qk_gain(mla_q_norm), _layout_qk_gain(mla_k_norm)
    g_mix = mix_out_norm.reshape(depth, N_GROUPS, GROUP_WIDTH)
    conv_b3 = conv_b[:, None, :]

    for l in range(depth):
        proj, vt3, negc = _inproj(x2, rows(attn_norm), w_main, wvt, wf, bf, l, batch, seq)
        negc_col = negc.transpose(0, 2, 1).reshape(n, 8)

        qa, ka, qc, kc, qd, kd, vtd = _prep(
            proj, negc_col, bounds, rows(fox_q_norm), rows(fox_k_norm), rows(moba_q_norm), rows(moba_k_norm),
            rows(mla_cq_norm), rows(mla_ckv_norm), gdq, gdk, wuq, wuk, wuvt,
            cosc, sinc, cosd, sind, l, batch, seq)

        o_a = _bounded_or_online(bound_a[l], "fox_attn", qa, ka, vt3, 0, batch, seq)
        o_b = _stickbreak_attn(proj, vt3, batch, seq)
        o_c = _bounded_or_online(bound_c[l], "moba_attn", qc, kc, vt3, 2, batch, seq)
        o_d = _bounded_or_online(bound_d[l], "mla_attn", qd, kd, vtd, None, batch, seq)

        x2, h_ffn = _outproj(o_a, o_b, o_c, o_d, g_mix, w_out_bf, x2, rows(ffn_norm), l)

        act = _ffn_up(h_ffn, w_up, conv_w, conv_b3, l, seq)
        x2 = _ffn_down(act, w_down_bf, x2, l)

    return x2.reshape(batch, seq, d_model)
```

```python
import functools
import math

import numpy as np
import jax
import jax.numpy as jnp
from jax import lax
from jax.experimental import pallas as pl
from jax.experimental.pallas import tpu as pltpu

F32 = jnp.float32
BF16 = jnp.bfloat16

D_MODEL = 2048
HEAD_DIM = 128
GROUP_HEADS = 4
GROUP_WIDTH = GROUP_HEADS * HEAD_DIM
N_GROUPS = 4
MOBA_BLOCK = 256
MOBA_TOP_K = 3
MOBA_MAX_BLOCKS = 16
MLA_Q_RANK = 512
MLA_KV_RANK = 256
MLA_NOPE_DIM = 128
MLA_ROPE_DIM = 64
MLA_QK_DIM = MLA_NOPE_DIM + MLA_ROPE_DIM
WIDE_HEAD = 256
D_FF = 5632
CONV_WIDTH = 3
ROPE_THETA = 10000.0
EPS = 1e-6

T_AQ, T_AK, T_CQ, T_CK, T_DCQ, T_DKV, T_BQ, T_BK = range(8)
PROJ_TILES = 8
V_GROUPS = 3

TILE = 512
STICK_TILE = 256
NEG = -1e30
VMEM_LIMIT = 56 * 1024 * 1024
LOG2E = math.log2(math.e)
SCALE = HEAD_DIM ** -0.5
SCALE_MLA = MLA_QK_DIM ** -0.5
FFN_TOKENS = 1024
MXU_WIDTH = 256
SUM_ROWS = 16
MAX_SAFE_BOUND = 56.0
MOBA_MIN_MASK = 256.0
STICK_DONE = 160.0


def _dot(a, b):
    return jnp.dot(a, b, preferred_element_type=F32)


def _dot_nt(a, b):
    return lax.dot_general(a, b, (((1,), (1,)), ((), ())), preferred_element_type=F32)


def _split3(a):
    hi = a.astype(BF16)
    r1 = a - hi.astype(F32)
    mid = r1.astype(BF16)
    lo = (r1 - mid.astype(F32)).astype(BF16)
    return hi, mid, lo


def _log_sigmoid_pair(z):
    sp = jnp.log(1.0 + jnp.exp(-jnp.abs(z)))
    return jnp.minimum(z, 0.0) - sp, -jnp.maximum(z, 0.0) - sp


def _rms(v, g, denom=None):
    if denom is None:
        ms = jnp.mean(v * v, axis=-1, keepdims=True)
    else:
        ms = jnp.sum(v * v, axis=-1, keepdims=True) * (1.0 / denom)
    return v * lax.rsqrt(ms + EPS) * g


def _sumsq_lanes(v):
    return _dot((v * v).astype(BF16), jnp.ones((v.shape[-1], HEAD_DIM), BF16))


def _rms_lanes(v, g, denom=None):
    width = v.shape[-1]
    r = lax.rsqrt(_sumsq_lanes(v) * (1.0 / (denom or width)) + EPS)
    if width > HEAD_DIM:
        r = jnp.concatenate([r] * (width // HEAD_DIM), axis=1)
    return v * r * g


def _inproj_kernel(x_ref, g_ref, w_ref, wvt_ref, wf_ref, bf_ref, proj_ref, vt_ref, vtb_ref, negc_ref,
                   carry_scr, *, tiles_per_batch, tm):
    i = pl.program_id(0)
    h = _rms(x_ref[...], g_ref[...]).astype(BF16)
    z = _dot_nt(wf_ref[...], h) + bf_ref[:, 0:1]
    logf, _ = _log_sigmoid_pair(z)
    r = lax.broadcasted_iota(jnp.int32, (tm, tm), 0)
    c = lax.broadcasted_iota(jnp.int32, (tm, tm), 1)
    tri = jnp.where(r <= c, 1.0, 0.0).astype(BF16)
    hi, mid, lo = _split3(logf)
    cum = _dot(hi, tri) + _dot(mid, tri) + _dot(lo, tri)

    @pl.when(i % tiles_per_batch == 0)
    def _():
        carry_scr[...] = jnp.zeros_like(carry_scr)

    cum = cum + carry_scr[:, 0:1]
    carry_scr[...] = jnp.broadcast_to(cum[:, tm - 1:tm], carry_scr.shape)
    negc_ref[...] = cum * (-LOG2E)

    gw = GROUP_WIDTH
    for jt in range(PROJ_TILES):
        proj_ref[:, jt * gw:(jt + 1) * gw] = _dot(h, w_ref[:, jt * gw:(jt + 1) * gw]).astype(BF16)
    for gi in range(V_GROUPS):
        vt = _dot_nt(wvt_ref[gi], h).astype(BF16)
        if gi == 1:
            for part in range(tm // STICK_TILE):
                vtb_ref[part] = vt[:, part * STICK_TILE:(part + 1) * STICK_TILE]
        else:
            vt_ref[gi // 2] = vt


def _layer_spec(shape, layer, **kw):
    zeros = (0,) * len(shape)
    return pl.BlockSpec((None,) + tuple(shape), lambda *_: (layer,) + zeros, **kw)


def _inproj(x2, g, w, wvt, wf, bf, layer, batch, seq):
    n = x2.shape[0]
    tm = TILE
    tpb = seq // tm
    kern = functools.partial(_inproj_kernel, tiles_per_batch=tpb, tm=tm)
    resident = pl.Buffered(1)
    return pl.pallas_call(
        kern,
        grid=(n // tm,),
        in_specs=[
            pl.BlockSpec((tm, D_MODEL), lambda i: (i, 0)),
            _layer_spec((1, D_MODEL), layer),
            _layer_spec((D_MODEL, PROJ_TILES * GROUP_WIDTH), layer, pipeline_mode=resident),
            _layer_spec((V_GROUPS, GROUP_WIDTH, D_MODEL), layer, pipeline_mode=resident),
            _layer_spec((8, D_MODEL), layer),
            _layer_spec((8, 128), layer),
        ],
        out_specs=[
            pl.BlockSpec((tm, PROJ_TILES * GROUP_WIDTH), lambda i: (i, 0)),
            pl.BlockSpec((2, None, None, GROUP_WIDTH, tm), lambda i: (0, i // tpb, i % tpb, 0, 0)),
            pl.BlockSpec((None, tm // STICK_TILE, GROUP_WIDTH, STICK_TILE), lambda i: (i // tpb, i % tpb, 0, 0)),
            pl.BlockSpec((None, 8, tm), lambda i: (i // tpb, 0, i % tpb)),
        ],
        out_shape=[
            jax.ShapeDtypeStruct((n, PROJ_TILES * GROUP_WIDTH), BF16),
            jax.ShapeDtypeStruct((2, batch, tpb, GROUP_WIDTH, tm), BF16),
            jax.ShapeDtypeStruct((batch, seq // STICK_TILE, GROUP_WIDTH, STICK_TILE), BF16),
            jax.ShapeDtypeStruct((batch, 8, seq), F32),
        ],
        scratch_shapes=[pltpu.VMEM((8, 128), F32)],
        compiler_params=pltpu.CompilerParams(
            dimension_semantics=("arbitrary",), vmem_limit_bytes=VMEM_LIMIT),
        name="inproj",
    )(x2, g, w, wvt, wf, bf)


def _prep_kernel(a_ref, c_ref, d_ref, nc_ref, bnd_ref, gaq, gak, gcq, gck, gdcq, gdckv, gdq, gdk, wuq, wuk, wuvt,
                 cosc, sinc, cosd, sind,
                 qa_ref, ka_ref, qc_ref, kc_ref, qd_ref, kd_ref, vtd_ref, kmt_scr,
                 *, tiles_per_batch, tm):
    i = pl.program_id(0)
    t = i % tiles_per_batch
    hd = HEAD_DIM
    wh = WIDE_HEAD
    blocks_per_tile = tm // MOBA_BLOCK
    lane = lax.broadcasted_iota(jnp.int32, (tm, hd), 1)
    bound_a = bnd_ref[0:1, :]
    bound_c = bnd_ref[1:2, :]
    big_c = bnd_ref[2:3, :]
    bound_d = bnd_ref[3:4, :]

    nc = nc_ref[...]
    for h in range(GROUP_HEADS):
        q = a_ref[:, h * hd:(h + 1) * hd].astype(F32)
        qa_ref[:, h * wh:h * wh + hd] = (_rms_lanes(q, gaq[...]) * (SCALE * LOG2E)).astype(BF16)
        k = a_ref[:, GROUP_WIDTH + h * hd:GROUP_WIDTH + (h + 1) * hd].astype(F32)
        ka_ref[:, h * wh:h * wh + hd] = _rms_lanes(k, gak[...]).astype(BF16)
        hi, mid, lo = (p.astype(F32) for p in _split3(nc[:, h:h + 1]))
        k_aug = jnp.where(lane == 0, hi, jnp.where(lane == 1, mid, jnp.where(lane == 2, lo, jnp.where(
            lane < 6, 1.0, jnp.where(lane == 6, -bound_a, 0.0)))))
        q_aug = jnp.where(lane < 3, 1.0, jnp.where(lane == 3, -hi, jnp.where(lane == 4, -mid, jnp.where(
            lane == 5, -lo, jnp.where(lane == 6, 1.0, 0.0)))))
        ka_ref[:, h * wh + hd:(h + 1) * wh] = k_aug.astype(BF16)
        qa_ref[:, h * wh + hd:(h + 1) * wh] = q_aug.astype(BF16)

    @pl.when(i == 0)
    def _():
        kmt_scr[...] = jnp.zeros_like(kmt_scr)

    cc = cosc[...]
    sc = sinc[...]
    sub8 = lax.broadcasted_iota(jnp.int32, (8, hd), 0)
    key_blk = t * blocks_per_tile + lax.broadcasted_iota(jnp.int32, (tm, hd), 0) // MOBA_BLOCK

    def rope_c(v):
        return v * cc + pltpu.roll(v, hd // 2, axis=1) * sc

    kc_aug = jnp.where(lane == key_blk, big_c, jnp.where(lane == MOBA_MAX_BLOCKS, -(bound_c + big_c), 0.0))
    kc_aug = kc_aug.astype(BF16)
    for h in range(GROUP_HEADS):
        k = rope_c(_rms_lanes(c_ref[:, GROUP_WIDTH + h * hd:GROUP_WIDTH + (h + 1) * hd].astype(F32), gck[...]))
        kc_ref[:, h * wh:h * wh + hd] = k.astype(BF16)
        kc_ref[:, h * wh + hd:(h + 1) * wh] = kc_aug
        for blk in range(blocks_per_tile):
            km = jnp.mean(k[blk * MOBA_BLOCK:(blk + 1) * MOBA_BLOCK], axis=0, keepdims=True)
            nblk = t * blocks_per_tile + blk
            base = pl.multiple_of(h * hd + (nblk // 8) * 8, 8)
            old = kmt_scr[pl.ds(base, 8), h * hd:(h + 1) * hd]
            kmt_scr[pl.ds(base, 8), h * hd:(h + 1) * hd] = jnp.where(sub8 == nblk % 8, km, old)

    qs = []
    for h in range(GROUP_HEADS):
        q = rope_c(_rms_lanes(c_ref[:, h * hd:(h + 1) * hd].astype(F32), gcq[...])) * (SCALE * LOG2E)
        qc_ref[:, h * wh:h * wh + hd] = q.astype(BF16)
        qs.append(q)
    q_all = jnp.concatenate(qs, axis=1)
    qh, qm, _ = _split3(q_all)
    kh, km_, _ = _split3(kmt_scr[...])
    gate_t = _dot_nt(kh, qh) + _dot_nt(km_, qh) + _dot_nt(kh, qm)

    nb = MOBA_MAX_BLOCKS
    blk_row = lax.broadcasted_iota(jnp.int32, (nb, tm), 0)
    own = t * blocks_per_tile + lax.broadcasted_iota(jnp.int32, (nb, tm), 1) // MOBA_BLOCK
    blk_f = blk_row.astype(F32)
    ninf = jnp.float32(-jnp.inf)
    no_rows = jnp.zeros((hd - nb, tm), F32)
    for h in range(GROUP_HEADS):
        g = jnp.where(blk_row < own, gate_t[h * hd:h * hd + nb, :], ninf)
        sel = jnp.where(blk_row == own, 1.0, 0.0)
        for _ in range(MOBA_TOP_K):
            mx = jnp.max(g, axis=0, keepdims=True)
            cand = jnp.where(g == mx, blk_f, 1e9)
            cand = jnp.where(mx > ninf, cand, 1e9)
            idx = jnp.min(cand, axis=0, keepdims=True)
            pick = blk_f == idx
            sel = jnp.where(pick, 1.0, sel)
            g = jnp.where(pick, ninf, g)
        sel_rows = jnp.concatenate([sel, no_rows], axis=0).T
        q_aug = jnp.where(lane == nb, 1.0, sel_rows)
        qc_ref[:, h * wh + hd:(h + 1) * wh] = q_aug.astype(BF16)

    cd = cosd[...]
    sd = sind[...]
    free_lane = lane == MLA_ROPE_DIM // 2

    def rope_d(v):
        return v * cd + pltpu.roll(v, hd // 2, axis=1) * sd

    cq = _rms_lanes(d_ref[:, 0:MLA_Q_RANK].astype(F32), gdcq[...]).astype(BF16)
    qf = _dot(cq, wuq[...])
    for h in range(GROUP_HEADS):
        qn = _rms_lanes(qf[:, h * wh:(h + 1) * wh], gdq[...], denom=MLA_QK_DIM)
        qd_ref[:, h * wh:h * wh + hd] = (qn[:, :hd] * (SCALE_MLA * LOG2E)).astype(BF16)
        q_rot = rope_d(qn[:, hd:]) * (SCALE_MLA * LOG2E)
        qd_ref[:, h * wh + hd:(h + 1) * wh] = jnp.where(free_lane, 1.0, q_rot).astype(BF16)

    ckv = _rms_lanes(d_ref[:, MLA_Q_RANK:MLA_Q_RANK + MLA_KV_RANK].astype(F32), gdckv[...]).astype(BF16)
    knf = _dot(ckv, wuk[...])
    vtd_ref[...] = _dot_nt(wuvt[...], ckv).astype(BF16)
    kr = d_ref[:, MLA_Q_RANK + MLA_KV_RANK:MLA_Q_RANK + MLA_KV_RANK + hd].astype(F32)
    ss_r = _sumsq_lanes(kr)
    gk = gdk[...]
    for h in range(GROUP_HEADS):
        kn = knf[:, h * hd:(h + 1) * hd]
        r = lax.rsqrt((_sumsq_lanes(kn) + ss_r) * (1.0 / MLA_QK_DIM) + EPS)
        kd_ref[:, h * wh:h * wh + hd] = (kn * r * gk[:, :hd]).astype(BF16)
        k_rot = rope_d(kr * r * gk[:, hd:])
        kd_ref[:, h * wh + hd:(h + 1) * wh] = jnp.where(free_lane, -bound_d, k_rot).astype(BF16)


def _prep(proj, negc_col, bounds, gaq, gak, gcq, gck, gdcq, gdckv, gdq, gdk, wuq, wuk, wuvt,
          cosc, sinc, cosd, sind, layer, batch, seq):
    n = proj.shape[0]
    tm = TILE
    tpb = seq // tm
    assert seq // MOBA_BLOCK <= MOBA_MAX_BLOCKS
    kern = functools.partial(_prep_kernel, tiles_per_batch=tpb, tm=tm)
    row = lambda i: (i, 0)
    tab = lambda i: (i % tpb, 0)
    w2 = 2 * GROUP_WIDTH
    return pl.pallas_call(
        kern,
        grid=(n // tm,),
        in_specs=[
            pl.BlockSpec((tm, w2), lambda i: (i, 0)),
            pl.BlockSpec((tm, w2), lambda i: (i, 1)),
            pl.BlockSpec((tm, w2), lambda i: (i, 2)),
            pl.BlockSpec((tm, 8), row),
            _layer_spec((8, HEAD_DIM), layer),
            _layer_spec((1, HEAD_DIM), layer), _layer_spec((1, HEAD_DIM), layer),
            _layer_spec((1, HEAD_DIM), layer), _layer_spec((1, HEAD_DIM), layer),
            _layer_spec((1, MLA_Q_RANK), layer), _layer_spec((1, MLA_KV_RANK), layer),
            _layer_spec((1, WIDE_HEAD), layer), _layer_spec((1, WIDE_HEAD), layer),
            _layer_spec((MLA_Q_RANK, GROUP_HEADS * WIDE_HEAD), layer),
            _layer_spec((MLA_KV_RANK, GROUP_WIDTH), layer),
            _layer_spec((GROUP_WIDTH, MLA_KV_RANK), layer),
            pl.BlockSpec((tm, HEAD_DIM), tab), pl.BlockSpec((tm, HEAD_DIM), tab),
            pl.BlockSpec((tm, HEAD_DIM), tab), pl.BlockSpec((tm, HEAD_DIM), tab),
        ],
        out_specs=[pl.BlockSpec((tm, w2), row)] * 6 + [
            pl.BlockSpec((None, None, GROUP_WIDTH, tm), lambda i: (i // tpb, i % tpb, 0, 0)),
        ],
        out_shape=[jax.ShapeDtypeStruct((n, w2), BF16)] * 6 + [
            jax.ShapeDtypeStruct((batch, tpb, GROUP_WIDTH, tm), BF16),
        ],
        scratch_shapes=[pltpu.VMEM((GROUP_WIDTH, GROUP_WIDTH), F32)],
        compiler_params=pltpu.CompilerParams(
            dimension_semantics=("arbitrary",), vmem_limit_bytes=VMEM_LIMIT),
        name="prep",
    )(proj, proj, proj, negc_col, bounds, gaq, gak, gcq, gck, gdcq, gdckv, gdq, gdk, wuq, wuk, wuvt,
      cosc, sinc, cosd, sind)


def _key_query_iota(t):
    key = lax.broadcasted_iota(jnp.int32, (t, t), 0)
    qry = lax.broadcasted_iota(jnp.int32, (t, t), 1)
    return key, qry


def _softmax_step(s, vt, carry):
    m, acc = carry
    m_new = jnp.maximum(m, jnp.max(s, axis=0, keepdims=True))
    alpha = jnp.exp2(m - m_new)
    p = jnp.exp2(s - m_new)
    return m_new, alpha * acc + _dot(vt, p.astype(BF16))


def _softmax_init(t):
    return (jnp.full((1, t), NEG, F32), jnp.zeros((HEAD_DIM + SUM_ROWS, t), F32))


def _softmax_attn_kernel(q_ref, k_ref, vt_ref, o_ref, acc_scr, *, t, bounded):
    qi = pl.program_id(1)
    key, qry = _key_query_iota(t)
    causal = key <= qry
    dk = WIDE_HEAD
    dv = HEAD_DIM
    ones_rows = jnp.ones((SUM_ROWS, t), BF16)

    def logits(j, h, diag):
        off = pl.multiple_of(j * t, t)
        s = _dot_nt(k_ref[pl.ds(off, t), h * dk:(h + 1) * dk], q_ref[:, h * dk:(h + 1) * dk])
        return jnp.where(causal, s, NEG) if diag else s

    def values(j, h):
        return jnp.concatenate([vt_ref[j, h * dv:(h + 1) * dv, :], ones_rows], axis=0)

    if bounded:
        acc_scr[...] = jnp.zeros_like(acc_scr)

        def step(tiles, diag):
            s = [[logits(j, h, diag) for h in range(GROUP_HEADS)] for j in tiles]
            for n, j in enumerate(tiles):
                for h in range(GROUP_HEADS):
                    acc_scr[h] += _dot(values(j, h), jnp.exp2(s[n][h]).astype(BF16))

        def pair(jj, c):
            step((2 * jj, 2 * jj + 1), False)
            return c

        lax.fori_loop(0, qi // 2, pair, 0)

        @pl.when(qi % 2 == 1)
        def _():
            step((qi - 1,), False)

        step((qi,), True)
        accs = [acc_scr[h] for h in range(GROUP_HEADS)]
    else:
        def step(j, carry, diag):
            out = []
            for h in range(GROUP_HEADS):
                out.append(_softmax_step(logits(j, h, diag), values(j, h), carry[h]))
            return tuple(out)

        carry = step(qi, tuple(_softmax_init(t) for _ in range(GROUP_HEADS)), True)
        carry = lax.fori_loop(0, qi, lambda j, c: step(j, c, False), carry)
        accs = [c[1] for c in carry]

    for h in range(GROUP_HEADS):
        a = accs[h]
        o_ref[:, h * dv:(h + 1) * dv] = (a[:dv] / a[dv:dv + 1]).T.astype(BF16)


def _softmax_attn(name, q_arr, k_arr, vt_arr, vt_group, batch, seq, bounded):
    t = TILE
    nq = seq // t
    n = batch * seq
    kern = functools.partial(_softmax_attn_kernel, t=t, bounded=bounded)
    if vt_group is None:
        vt_spec = pl.BlockSpec((None, nq, GROUP_WIDTH, t), lambda b, i: (b, 0, 0, 0))
    else:
        vt_spec = pl.BlockSpec((None, None, nq, GROUP_WIDTH, t), lambda b, i: (vt_group, b, 0, 0, 0))
    return pl.pallas_call(
        kern,
        grid=(batch, nq),
        in_specs=[
            pl.BlockSpec((t, GROUP_HEADS * WIDE_HEAD), lambda b, i: (b * nq + i, 0)),
            pl.BlockSpec((seq, GROUP_HEADS * WIDE_HEAD), lambda b, i: (b, 0)),
            vt_spec,
        ],
        out_specs=pl.BlockSpec((t, GROUP_WIDTH), lambda b, i: (b * nq + i, 0)),
        out_shape=jax.ShapeDtypeStruct((n, GROUP_WIDTH), BF16),
        scratch_shapes=[pltpu.VMEM((GROUP_HEADS, HEAD_DIM + SUM_ROWS, t), F32)],
        compiler_params=pltpu.CompilerParams(
            dimension_semantics=("parallel", "parallel"), vmem_limit_bytes=VMEM_LIMIT),
        name=name + ("" if bounded else "_online"),
    )(q_arr, k_arr, vt_arr)


def _bounded_or_online(bound, name, *args):
    return lax.cond(bound < MAX_SAFE_BOUND,
                    lambda: _softmax_attn(name, *args, bounded=True),
                    lambda: _softmax_attn(name, *args, bounded=False))


def _stickbreak_kernel(q_ref, k_ref, vt_ref, o_ref, acc_scr, rest_scr, *, t):
    qi = pl.program_id(1)
    key, qry = _key_query_iota(t)
    strict = key < qry
    later = jnp.where(qry > key, 1.0, 0.0).astype(BF16)
    hd = HEAD_DIM
    acc_scr[...] = jnp.zeros_like(acc_scr)
    rest_scr[...] = jnp.zeros_like(rest_scr)

    def step(tiles):
        qs = [(q_ref[:, h * hd:(h + 1) * hd].astype(F32) * (SCALE * LOG2E)).astype(BF16) for h in range(GROUP_HEADS)]
        zs, drops, afters = [], [], []
        for j, diag in tiles:
            off = pl.multiple_of(j * t, t)
            zs.append([_dot_nt(k_ref[pl.ds(off, t), h * hd:(h + 1) * hd], qs[h]) for h in range(GROUP_HEADS)])
        for n, (j, diag) in enumerate(tiles):
            row = []
            for h in range(GROUP_HEADS):
                z = zs[n][h]
                drop = jnp.maximum(z, 0.0) + jnp.log(1.0 + jnp.exp2(-jnp.abs(z))) * LOG2E
                row.append(jnp.where(strict, drop, 0.0) if diag else drop)
            drops.append(row)
            afters.append([_dot(later, d.astype(BF16)) for d in row])
        low = None
        for h in range(GROUP_HEADS):
            used = rest_scr[h]
            for n, (j, diag) in enumerate(tiles):
                z, drop, after = zs[n][h], drops[n][h], afters[n][h]
                w = jnp.exp2((z - drop) - (after + used))
                if diag:
                    w = jnp.where(strict, w, 0.0)
                acc_scr[h] += _dot(vt_ref[j, h * hd:(h + 1) * hd, :], w.astype(BF16))
                used = used + after[0:1] + drop[0:1]
            rest_scr[h] = used
            low = used if low is None else jnp.minimum(low, used)
        return jnp.min(low)

    def cond(c):
        j, low = c
        return jnp.logical_and(j >= 0, low < STICK_DONE)

    def body(c):
        j, _ = c
        return j - 1, step([(j, False)])

    first = lax.cond(qi == 0, lambda: step([(qi, True)]), lambda: step([(qi, True), (qi - 1, False)]))
    lax.while_loop(cond, body, (qi - 2, first))
    for h in range(GROUP_HEADS):
        o_ref[:, h * hd:(h + 1) * hd] = acc_scr[h].T.astype(BF16)


def _stickbreak_attn(proj, vtb, batch, seq):
    t = STICK_TILE
    nq = seq // t
    n = batch * seq
    kern = functools.partial(_stickbreak_kernel, t=t)
    return pl.pallas_call(
        kern,
        grid=(batch, nq),
        in_specs=[
            pl.BlockSpec((t, GROUP_WIDTH), lambda b, i: (b * nq + i, T_BQ)),
            pl.BlockSpec((seq, GROUP_WIDTH), lambda b, i: (b, T_BK)),
            pl.BlockSpec((None, nq, GROUP_WIDTH, t), lambda b, i: (b, 0, 0, 0)),
        ],
        out_specs=pl.BlockSpec((t, GROUP_WIDTH), lambda b, i: (b * nq + i, 0)),
        out_shape=jax.ShapeDtypeStruct((n, GROUP_WIDTH), BF16),
        scratch_shapes=[pltpu.VMEM((GROUP_HEADS, HEAD_DIM, t), F32), pltpu.VMEM((GROUP_HEADS, 1, t), F32)],
        compiler_params=pltpu.CompilerParams(
            dimension_semantics=("parallel", "parallel"), vmem_limit_bytes=VMEM_LIMIT),
        name="stickbreak_attn",
    )(proj, proj, vtb)


def _outproj_kernel(oa_ref, ob_ref, oc_ref, od_ref, g_ref, w_ref, x_ref, gf_ref, out_ref, hf_ref):
    acc = x_ref[...]
    for gi, o_ref in enumerate((oa_ref, ob_ref, oc_ref, od_ref)):
        y = _rms(o_ref[...].astype(F32), g_ref[gi:gi + 1, :]).astype(BF16)
        acc = acc + _dot(y, w_ref[gi * GROUP_WIDTH:(gi + 1) * GROUP_WIDTH, :])
    out_ref[...] = acc
    hf_ref[...] = _rms(acc, gf_ref[...]).astype(BF16)


def _outproj(oa, ob, oc, od, g, w, x2, g_ffn, layer):
    n = x2.shape[0]
    tm = 512
    row = lambda i: (i, 0)
    return pl.pallas_call(
        _outproj_kernel,
        grid=(n // tm,),
        in_specs=[pl.BlockSpec((tm, GROUP_WIDTH), row)] * 4 + [
            _layer_spec((N_GROUPS, GROUP_WIDTH), layer),
            _layer_spec((N_GROUPS * GROUP_WIDTH, D_MODEL), layer, pipeline_mode=pl.Buffered(1)),
            pl.BlockSpec((tm, D_MODEL), row),
            _layer_spec((1, D_MODEL), layer),
        ],
        out_specs=[pl.BlockSpec((tm, D_MODEL), row), pl.BlockSpec((tm, D_MODEL), row)],
        out_shape=[jax.ShapeDtypeStruct((n, D_MODEL), F32), jax.ShapeDtypeStruct((n, D_MODEL), BF16)],
        compiler_params=pltpu.CompilerParams(
            dimension_semantics=("parallel",), vmem_limit_bytes=VMEM_LIMIT),
        name="outproj",
    )(oa, ob, oc, od, g, w, x2, g_ffn)


def _ffn_up_kernel(h_ref, wg_ref, wv_ref, cwg_ref, cwv_ref, cbg_ref, cbv_ref,
                   out_ref, wg_scr, wv_scr, tail_g, tail_v, *, tiles_per_batch, tm):
    i = pl.program_id(1)

    @pl.when(i == 0)
    def _():
        wg_scr[...] = wg_ref[...].astype(BF16)
        wv_scr[...] = wv_ref[...].astype(BF16)

    @pl.when(i % tiles_per_batch == 0)
    def _():
        tail_g[...] = jnp.zeros_like(tail_g)
        tail_v[...] = jnp.zeros_like(tail_v)

    def conv(h, w_scr, cw_ref, cb_ref, prev):
        u = _dot(h, w_scr[...])
        ue = jnp.concatenate([prev, u], axis=0)
        y = cb_ref[...] + cw_ref[2:3, :] * u
        for back in range(1, CONV_WIDTH):
            y = y + cw_ref[CONV_WIDTH - 1 - back:CONV_WIDTH - back, :] * pltpu.roll(ue, back, axis=0)[8:]
        return y, u[u.shape[0] - 8:]

    h = h_ref[...]
    gate, tail_g[...] = conv(h, wg_scr, cwg_ref, cbg_ref, tail_g[...])
    val, tail_v[...] = conv(h, wv_scr, cwv_ref, cbv_ref, tail_v[...])
    out_ref[...] = (gate / (1.0 + jnp.exp(-gate)) * val).astype(BF16)


def _ffn_up(h, w_up, conv_w, conv_b, layer, seq):
    n = h.shape[0]
    tm = FFN_TOKENS
    tn = 512
    nj = D_FF // tn
    tpb = seq // tm
    kern = functools.partial(_ffn_up_kernel, tiles_per_batch=tpb, tm=tm)
    return pl.pallas_call(
        kern,
        grid=(nj, n // tm),
        in_specs=[
            pl.BlockSpec((tm, D_MODEL), lambda j, i: (i, 0)),
            pl.BlockSpec((None, D_MODEL, tn), lambda j, i: (layer, 0, j)),
            pl.BlockSpec((None, D_MODEL, tn), lambda j, i: (layer, 0, j + nj)),
            pl.BlockSpec((None, CONV_WIDTH, tn), lambda j, i: (layer, 0, j)),
            pl.BlockSpec((None, CONV_WIDTH, tn), lambda j, i: (layer, 0, j + nj)),
            pl.BlockSpec((None, 1, tn), lambda j, i: (layer, 0, j)),
            pl.BlockSpec((None, 1, tn), lambda j, i: (layer, 0, j + nj)),
        ],
        out_specs=pl.BlockSpec((tm, tn), lambda j, i: (i, j)),
        out_shape=jax.ShapeDtypeStruct((n, D_FF), BF16),
        scratch_shapes=[pltpu.VMEM((D_MODEL, tn), BF16), pltpu.VMEM((D_MODEL, tn), BF16),
                        pltpu.VMEM((8, tn), F32), pltpu.VMEM((8, tn), F32)],
        compiler_params=pltpu.CompilerParams(
            dimension_semantics=("arbitrary", "arbitrary"), vmem_limit_bytes=VMEM_LIMIT),
        name="ffn_up",
    )(h, w_up, w_up, conv_w, conv_w, conv_b, conv_b)


def _ffn_down_kernel(a_ref, w_ref, x_ref, out_ref):
    a = a_ref[...]
    tn = 2 * MXU_WIDTH
    for c in range(D_MODEL // tn):
        cols = slice(c * tn, (c + 1) * tn)
        out_ref[:, cols] = x_ref[:, cols] + _dot(a, w_ref[:, cols])


def _ffn_down(act, w_down, x2, layer):
    n = x2.shape[0]
    tm = 512
    row = lambda i: (i, 0)
    return pl.pallas_call(
        _ffn_down_kernel,
        grid=(n // tm,),
        in_specs=[
            pl.BlockSpec((tm, D_FF), row),
            _layer_spec((D_FF, D_MODEL), layer, pipeline_mode=pl.Buffered(1)),
            pl.BlockSpec((tm, D_MODEL), row),
        ],
        out_specs=pl.BlockSpec((tm, D_MODEL), row),
        out_shape=jax.ShapeDtypeStruct((n, D_MODEL), F32),
        compiler_params=pltpu.CompilerParams(
            dimension_semantics=("parallel",), vmem_limit_bytes=VMEM_LIMIT),
        name="ffn_down",
    )(act, w_down, x2)


def _rope_tables(seq):
    def angles(dim):
        inv_freq = ROPE_THETA ** (-np.arange(0, dim, 2, dtype=np.float64) / dim)
        ang = np.arange(seq, dtype=np.float64)[:, None] * inv_freq[None, :]
        return np.cos(ang), np.sin(ang)

    c, s = angles(HEAD_DIM)
    cosc = np.concatenate([c, c], axis=1)
    sinc = np.concatenate([-s, s], axis=1)
    c, s = angles(MLA_ROPE_DIM)
    z = np.zeros_like(c)
    cosd = np.concatenate([c, z, c, z], axis=1)
    sind = np.concatenate([-s, z, s, z], axis=1)
    return tuple(jnp.asarray(t, F32) for t in (cosc, sinc, cosd, sind))


def _spread_rope(t):
    half = MLA_ROPE_DIM // 2
    z = jnp.zeros(t.shape[:-1] + (half,), t.dtype)
    return jnp.concatenate([t[..., :half], z, t[..., half:], z], axis=-1)


def _layout_w_in(w):
    gw = GROUP_WIDTH
    o = 0
    pieces = {}
    for name, width in (("aq", gw), ("ak", gw), ("av", gw), ("af", GROUP_HEADS), ("bq", gw), ("bk", gw),
                        ("bv", gw), ("cq", gw), ("ck", gw), ("cv", gw), ("dcq", MLA_Q_RANK),
                        ("dckv", MLA_KV_RANK), ("dkr", MLA_ROPE_DIM)):
        pieces[name] = w[..., o:o + width]
        o += width
    pad = jnp.zeros(w.shape[:-1] + (gw - MLA_KV_RANK - HEAD_DIM,), w.dtype)
    tiles = [pieces["aq"], pieces["ak"], pieces["cq"], pieces["ck"], pieces["dcq"],
             pieces["dckv"], _spread_rope(pieces["dkr"]), pad, pieces["bq"], pieces["bk"]]
    w_main = jnp.concatenate(tiles, axis=-1).astype(BF16)
    wvt = jnp.stack([jnp.swapaxes(pieces[v], -1, -2) for v in ("av", "bv", "cv")], axis=1).astype(BF16)
    wf = jnp.pad(jnp.swapaxes(pieces["af"], -1, -2), ((0, 0), (0, 8 - GROUP_HEADS), (0, 0))).astype(BF16)
    return w_main, wvt, wf


def _layout_w_uq(w):
    w = w.reshape(-1, MLA_Q_RANK, GROUP_HEADS, MLA_QK_DIM)
    w = jnp.concatenate([w[..., :MLA_NOPE_DIM], _spread_rope(w[..., MLA_NOPE_DIM:])], axis=-1)
    return w.reshape(-1, MLA_Q_RANK, GROUP_HEADS * WIDE_HEAD).astype(BF16)


def _layout_w_ukv(w):
    w = w.reshape(-1, MLA_KV_RANK, GROUP_HEADS, 2, HEAD_DIM)
    wk = w[:, :, :, 0, :].reshape(-1, MLA_KV_RANK, GROUP_WIDTH).astype(BF16)
    wvt = jnp.swapaxes(w[:, :, :, 1, :].reshape(-1, MLA_KV_RANK, GROUP_WIDTH), -1, -2).astype(BF16)
    return wk, wvt


def _logit_bound(gq, gk, dim, scale):
    return dim * scale * LOG2E * jnp.max(jnp.abs(gq), axis=-1) * jnp.max(jnp.abs(gk), axis=-1) * 1.01 + 0.5


def _layout_qk_gain(g):
    return jnp.concatenate([g[:, :MLA_NOPE_DIM], _spread_rope(g[:, MLA_NOPE_DIM:])], axis=-1)[:, None, :]


def kernel(x, attn_norm, w_in, b_forget, fox_q_norm, fox_k_norm, moba_q_norm, moba_k_norm, mla_cq_norm, mla_ckv_norm, w_uq, w_ukv, mla_q_norm, mla_k_norm, mix_out_norm, w_out, ffn_norm, w_up, conv_w, conv_b, w_down):
    batch, seq, d_model = x.shape
    depth = w_in.shape[0]
    assert d_model == D_MODEL and seq % TILE == 0 and seq % FFN_TOKENS == 0
    n = batch * seq
    x2 = x.reshape(n, d_model)
    cosc, sinc, cosd, sind = _rope_tables(seq)

    rows = lambda v: v[:, None, :]
    w_main, wvt, wf = _layout_w_in(w_in)
    wuq = _layout_w_uq(w_uq)
    wuk, wuvt = _layout_w_ukv(w_ukv)
    w_out_bf = w_out.astype(BF16)
    w_down_bf = w_down.astype(BF16)
    bf = jnp.pad(jnp.broadcast_to(b_forget[:, :, None], (depth, GROUP_HEADS, 128)),
                 ((0, 0), (0, 8 - GROUP_HEADS), (0, 0)))
    bound_a = _logit_bound(fox_q_norm, fox_k_norm, HEAD_DIM, SCALE)
    bound_c = _logit_bound(moba_q_norm, moba_k_norm, HEAD_DIM, SCALE)
    bound_d = _logit_bound(mla_q_norm, mla_k_norm, MLA_QK_DIM, SCALE_MLA)
    big_c = jnp.maximum(MOBA_MIN_MASK, jnp.exp2(jnp.ceil(jnp.log2(4.0 * bound_c + 64.0))))
    bounds = jnp.stack([bound_a, bound_c, big_c, bound_d] + [jnp.zeros_like(bound_a)] * 4, axis=1)
    bounds = jnp.broadcast_to(bounds[:, :, None], (depth, 8, HEAD_DIM))
    gdq, gdk = _layout_qk_gain(mla_q_norm), _layout_qk_gain(mla_k_norm)
    g_mix = mix_out_norm.reshape(depth, N_GROUPS, GROUP_WIDTH)
    conv_b3 = conv_b[:, None, :]

    for l in range(depth):
        proj, vt_ac, vt_b, negc = _inproj(x2, rows(attn_norm), w_main, wvt, wf, bf, l, batch, seq)
        negc_col = negc.transpose(0, 2, 1).reshape(n, 8)

        qa, ka, qc, kc, qd, kd, vtd = _prep(
            proj, negc_col, bounds, rows(fox_q_norm), rows(fox_k_norm), rows(moba_q_norm), rows(moba_k_norm),
            rows(mla_cq_norm), rows(mla_ckv_norm), gdq, gdk, wuq, wuk, wuvt,
            cosc, sinc, cosd, sind, l, batch, seq)

        o_a = _bounded_or_online(bound_a[l], "fox_attn", qa, ka, vt_ac, 0, batch, seq)
        o_b = _stickbreak_attn(proj, vt_b, batch, seq)
        o_c = _bounded_or_online(bound_c[l], "moba_attn", qc, kc, vt_ac, 1, batch, seq)
        o_d = _bounded_or_online(bound_d[l], "mla_attn", qd, kd, vtd, None, batch, seq)

        x2, h_ffn = _outproj(o_a, o_b, o_c, o_d, g_mix, w_out_bf, x2, rows(ffn_norm), l)

        act = _ffn_up(h_ffn, w_up, conv_w, conv_b3, l, seq)
        x2 = _ffn_down(act, w_down_bf, x2, l)

    return x2.reshape(batch, seq, d_model)
```

```python
import functools
import math

import numpy as np
import jax
import jax.numpy as jnp
from jax import lax
from jax.experimental import pallas as pl
from jax.experimental.pallas import tpu as pltpu

F32 = jnp.float32
BF16 = jnp.bfloat16

D_MODEL = 2048
HEAD_DIM = 128
GROUP_HEADS = 4
GROUP_WIDTH = GROUP_HEADS * HEAD_DIM
N_GROUPS = 4
MOBA_BLOCK = 256
MOBA_TOP_K = 3
MOBA_MAX_BLOCKS = 16
MLA_Q_RANK = 512
MLA_KV_RANK = 256
MLA_NOPE_DIM = 128
MLA_ROPE_DIM = 64
MLA_QK_DIM = MLA_NOPE_DIM + MLA_ROPE_DIM
WIDE_HEAD = 256
D_FF = 5632
CONV_WIDTH = 3
ROPE_THETA = 10000.0
EPS = 1e-6

T_AQ, T_AK, T_CQ, T_CK, T_DCQ, T_DKV, T_BQ, T_BK = range(8)
PROJ_TILES = 8
V_GROUPS = 3

TILE = 512
STICK_TILE = 256
NEG = -1e30
VMEM_LIMIT = 56 * 1024 * 1024
LOG2E = math.log2(math.e)
SCALE = HEAD_DIM ** -0.5
SCALE_MLA = MLA_QK_DIM ** -0.5
FFN_TOKENS = 1024
MXU_WIDTH = 256
SUM_ROWS = 16
MAX_SAFE_BOUND = 56.0
MOBA_MIN_MASK = 256.0
STICK_DONE = 160.0


def _dot(a, b):
    return jnp.dot(a, b, preferred_element_type=F32)


def _dot_nt(a, b):
    return lax.dot_general(a, b, (((1,), (1,)), ((), ())), preferred_element_type=F32)


def _split3(a):
    hi = a.astype(BF16)
    r1 = a - hi.astype(F32)
    mid = r1.astype(BF16)
    lo = (r1 - mid.astype(F32)).astype(BF16)
    return hi, mid, lo


def _log_sigmoid_pair(z):
    sp = jnp.log(1.0 + jnp.exp(-jnp.abs(z)))
    return jnp.minimum(z, 0.0) - sp, -jnp.maximum(z, 0.0) - sp


def _rms(v, g, denom=None):
    if denom is None:
        ms = jnp.mean(v * v, axis=-1, keepdims=True)
    else:
        ms = jnp.sum(v * v, axis=-1, keepdims=True) * (1.0 / denom)
    return v * lax.rsqrt(ms + EPS) * g


def _sumsq_lanes(v):
    return _dot((v * v).astype(BF16), jnp.ones((v.shape[-1], HEAD_DIM), BF16))


def _rms_lanes(v, g, denom=None):
    width = v.shape[-1]
    r = lax.rsqrt(_sumsq_lanes(v) * (1.0 / (denom or width)) + EPS)
    if width > HEAD_DIM:
        r = jnp.concatenate([r] * (width // HEAD_DIM), axis=1)
    return v * r * g


def _inproj_kernel(x_ref, g_ref, w_ref, wvt_ref, wf_ref, bf_ref, proj_ref, vt_ref, vtb_ref, negc_ref,
                   carry_scr, *, tiles_per_batch, tm):
    i = pl.program_id(0)
    h = _rms(x_ref[...], g_ref[...]).astype(BF16)
    z = _dot_nt(wf_ref[...], h) + bf_ref[:, 0:1]
    logf, _ = _log_sigmoid_pair(z)
    r = lax.broadcasted_iota(jnp.int32, (tm, tm), 0)
    c = lax.broadcasted_iota(jnp.int32, (tm, tm), 1)
    tri = jnp.where(r <= c, 1.0, 0.0).astype(BF16)
    hi, mid, lo = _split3(logf)
    cum = _dot(hi, tri) + _dot(mid, tri) + _dot(lo, tri)

    @pl.when(i % tiles_per_batch == 0)
    def _():
        carry_scr[...] = jnp.zeros_like(carry_scr)

    cum = cum + carry_scr[:, 0:1]
    carry_scr[...] = jnp.broadcast_to(cum[:, tm - 1:tm], carry_scr.shape)
    negc_ref[...] = cum * (-LOG2E)

    gw = GROUP_WIDTH
    for jt in range(PROJ_TILES):
        proj_ref[:, jt * gw:(jt + 1) * gw] = _dot(h, w_ref[:, jt * gw:(jt + 1) * gw]).astype(BF16)
    for gi in range(V_GROUPS):
        vt = _dot_nt(wvt_ref[gi], h).astype(BF16)
        if gi == 1:
            for part in range(tm // STICK_TILE):
                vtb_ref[part] = vt[:, part * STICK_TILE:(part + 1) * STICK_TILE]
        else:
            vt_ref[gi // 2] = vt


def _layer_spec(shape, layer, **kw):
    zeros = (0,) * len(shape)
    return pl.BlockSpec((None,) + tuple(shape), lambda *_: (layer,) + zeros, **kw)


def _inproj(x2, g, w, wvt, wf, bf, layer, batch, seq):
    n = x2.shape[0]
    tm = TILE
    tpb = seq // tm
    kern = functools.partial(_inproj_kernel, tiles_per_batch=tpb, tm=tm)
    resident = pl.Buffered(1)
    return pl.pallas_call(
        kern,
        grid=(n // tm,),
        in_specs=[
            pl.BlockSpec((tm, D_MODEL), lambda i: (i, 0)),
            _layer_spec((1, D_MODEL), layer),
            _layer_spec((D_MODEL, PROJ_TILES * GROUP_WIDTH), layer, pipeline_mode=resident),
            _layer_spec((V_GROUPS, GROUP_WIDTH, D_MODEL), layer, pipeline_mode=resident),
            _layer_spec((8, D_MODEL), layer),
            _layer_spec((8, 128), layer),
        ],
        out_specs=[
            pl.BlockSpec((tm, PROJ_TILES * GROUP_WIDTH), lambda i: (i, 0)),
            pl.BlockSpec((2, None, None, GROUP_WIDTH, tm), lambda i: (0, i // tpb, i % tpb, 0, 0)),
            pl.BlockSpec((None, tm // STICK_TILE, GROUP_WIDTH, STICK_TILE), lambda i: (i // tpb, i % tpb, 0, 0)),
            pl.BlockSpec((None, 8, tm), lambda i: (i // tpb, 0, i % tpb)),
        ],
        out_shape=[
            jax.ShapeDtypeStruct((n, PROJ_TILES * GROUP_WIDTH), BF16),
            jax.ShapeDtypeStruct((2, batch, tpb, GROUP_WIDTH, tm), BF16),
            jax.ShapeDtypeStruct((batch, seq // STICK_TILE, GROUP_WIDTH, STICK_TILE), BF16),
            jax.ShapeDtypeStruct((batch, 8, seq), F32),
        ],
        scratch_shapes=[pltpu.VMEM((8, 128), F32)],
        compiler_params=pltpu.CompilerParams(
            dimension_semantics=("arbitrary",), vmem_limit_bytes=VMEM_LIMIT),
        name="inproj",
    )(x2, g, w, wvt, wf, bf)


def _prep_kernel(a_ref, c_ref, d_ref, nc_ref, bnd_ref, gaq, gak, gcq, gck, gdcq, gdckv, gdq, gdk, wuq, wuk, wuvt,
                 cosc, sinc, cosd, sind,
                 qa_ref, ka_ref, qc_ref, kc_ref, qd_ref, kd_ref, vtd_ref, kmt_scr,
                 *, tiles_per_batch, tm):
    i = pl.program_id(0)
    t = i % tiles_per_batch
    hd = HEAD_DIM
    wh = WIDE_HEAD
    blocks_per_tile = tm // MOBA_BLOCK
    lane = lax.broadcasted_iota(jnp.int32, (tm, hd), 1)
    bound_a = bnd_ref[0:1, :]
    bound_c = bnd_ref[1:2, :]
    big_c = bnd_ref[2:3, :]
    bound_d = bnd_ref[3:4, :]

    nc = nc_ref[...]
    for h in range(GROUP_HEADS):
        q = a_ref[:, h * hd:(h + 1) * hd].astype(F32)
        qa_ref[:, h * wh:h * wh + hd] = (_rms_lanes(q, gaq[...]) * (SCALE * LOG2E)).astype(BF16)
        k = a_ref[:, GROUP_WIDTH + h * hd:GROUP_WIDTH + (h + 1) * hd].astype(F32)
        ka_ref[:, h * wh:h * wh + hd] = _rms_lanes(k, gak[...]).astype(BF16)
        hi, mid, lo = (p.astype(F32) for p in _split3(nc[:, h:h + 1]))
        k_aug = jnp.where(lane == 0, hi, jnp.where(lane == 1, mid, jnp.where(lane == 2, lo, jnp.where(
            lane < 6, 1.0, jnp.where(lane == 6, -bound_a, 0.0)))))
        q_aug = jnp.where(lane < 3, 1.0, jnp.where(lane == 3, -hi, jnp.where(lane == 4, -mid, jnp.where(
            lane == 5, -lo, jnp.where(lane == 6, 1.0, 0.0)))))
        ka_ref[:, h * wh + hd:(h + 1) * wh] = k_aug.astype(BF16)
        qa_ref[:, h * wh + hd:(h + 1) * wh] = q_aug.astype(BF16)

    @pl.when(i == 0)
    def _():
        kmt_scr[...] = jnp.zeros_like(kmt_scr)

    cc = cosc[...]
    sc = sinc[...]
    sub8 = lax.broadcasted_iota(jnp.int32, (8, hd), 0)
    key_blk = t * blocks_per_tile + lax.broadcasted_iota(jnp.int32, (tm, hd), 0) // MOBA_BLOCK

    def rope_c(v):
        return v * cc + pltpu.roll(v, hd // 2, axis=1) * sc

    kc_aug = jnp.where(lane == key_blk, big_c, jnp.where(lane == MOBA_MAX_BLOCKS, -(bound_c + big_c), 0.0))
    kc_aug = kc_aug.astype(BF16)
    for h in range(GROUP_HEADS):
        k = rope_c(_rms_lanes(c_ref[:, GROUP_WIDTH + h * hd:GROUP_WIDTH + (h + 1) * hd].astype(F32), gck[...]))
        kc_ref[:, h * wh:h * wh + hd] = k.astype(BF16)
        kc_ref[:, h * wh + hd:(h + 1) * wh] = kc_aug
        for blk in range(blocks_per_tile):
            km = jnp.mean(k[blk * MOBA_BLOCK:(blk + 1) * MOBA_BLOCK], axis=0, keepdims=True)
            nblk = t * blocks_per_tile + blk
            base = pl.multiple_of(h * hd + (nblk // 8) * 8, 8)
            old = kmt_scr[pl.ds(base, 8), h * hd:(h + 1) * hd]
            kmt_scr[pl.ds(base, 8), h * hd:(h + 1) * hd] = jnp.where(sub8 == nblk % 8, km, old)

    qs = []
    for h in range(GROUP_HEADS):
        q = rope_c(_rms_lanes(c_ref[:, h * hd:(h + 1) * hd].astype(F32), gcq[...])) * (SCALE * LOG2E)
        qc_ref[:, h * wh:h * wh + hd] = q.astype(BF16)
        qs.append(q)
    q_all = jnp.concatenate(qs, axis=1)
    qh, qm, _ = _split3(q_all)
    kh, km_, _ = _split3(kmt_scr[...])
    gate_t = _dot_nt(kh, qh) + _dot_nt(km_, qh) + _dot_nt(kh, qm)

    nb = MOBA_MAX_BLOCKS
    blk_row = lax.broadcasted_iota(jnp.int32, (nb, tm), 0)
    own = t * blocks_per_tile + lax.broadcasted_iota(jnp.int32, (nb, tm), 1) // MOBA_BLOCK
    blk_f = blk_row.astype(F32)
    ninf = jnp.float32(-jnp.inf)
    no_rows = jnp.zeros((hd - nb, tm), F32)
    for h in range(GROUP_HEADS):
        g = jnp.where(blk_row < own, gate_t[h * hd:h * hd + nb, :], ninf)
        sel = jnp.where(blk_row == own, 1.0, 0.0)
        for _ in range(MOBA_TOP_K):
            mx = jnp.max(g, axis=0, keepdims=True)
            cand = jnp.where(g == mx, blk_f, 1e9)
            cand = jnp.where(mx > ninf, cand, 1e9)
            idx = jnp.min(cand, axis=0, keepdims=True)
            pick = blk_f == idx
            sel = jnp.where(pick, 1.0, sel)
            g = jnp.where(pick, ninf, g)
        sel_rows = jnp.concatenate([sel, no_rows], axis=0).T
        q_aug = jnp.where(lane == nb, 1.0, sel_rows)
        qc_ref[:, h * wh + hd:(h + 1) * wh] = q_aug.astype(BF16)

    cd = cosd[...]
    sd = sind[...]
    free_lane = lane == MLA_ROPE_DIM // 2

    def rope_d(v):
        return v * cd + pltpu.roll(v, hd // 2, axis=1) * sd

    cq = _rms_lanes(d_ref[:, 0:MLA_Q_RANK].astype(F32), gdcq[...]).astype(BF16)
    qf = _dot(cq, wuq[...])
    for h in range(GROUP_HEADS):
        qn = _rms_lanes(qf[:, h * wh:(h + 1) * wh], gdq[...], denom=MLA_QK_DIM)
        qd_ref[:, h * wh:h * wh + hd] = (qn[:, :hd] * (SCALE_MLA * LOG2E)).astype(BF16)
        q_rot = rope_d(qn[:, hd:]) * (SCALE_MLA * LOG2E)
        qd_ref[:, h * wh + hd:(h + 1) * wh] = jnp.where(free_lane, 1.0, q_rot).astype(BF16)

    ckv = _rms_lanes(d_ref[:, MLA_Q_RANK:MLA_Q_RANK + MLA_KV_RANK].astype(F32), gdckv[...]).astype(BF16)
    knf = _dot(ckv, wuk[...])
    vtd_ref[...] = _dot_nt(wuvt[...], ckv).astype(BF16)
    kr = d_ref[:, MLA_Q_RANK + MLA_KV_RANK:MLA_Q_RANK + MLA_KV_RANK + hd].astype(F32)
    ss_r = _sumsq_lanes(kr)
    gk = gdk[...]
    for h in range(GROUP_HEADS):
        kn = knf[:, h * hd:(h + 1) * hd]
        r = lax.rsqrt((_sumsq_lanes(kn) + ss_r) * (1.0 / MLA_QK_DIM) + EPS)
        kd_ref[:, h * wh:h * wh + hd] = (kn * r * gk[:, :hd]).astype(BF16)
        k_rot = rope_d(kr * r * gk[:, hd:])
        kd_ref[:, h * wh + hd:(h + 1) * wh] = jnp.where(free_lane, -bound_d, k_rot).astype(BF16)


def _prep(proj, negc_col, bounds, gaq, gak, gcq, gck, gdcq, gdckv, gdq, gdk, wuq, wuk, wuvt,
          cosc, sinc, cosd, sind, layer, batch, seq):
    n = proj.shape[0]
    tm = TILE
    tpb = seq // tm
    assert seq // MOBA_BLOCK <= MOBA_MAX_BLOCKS
    kern = functools.partial(_prep_kernel, tiles_per_batch=tpb, tm=tm)
    row = lambda i: (i, 0)
    tab = lambda i: (i % tpb, 0)
    w2 = 2 * GROUP_WIDTH
    return pl.pallas_call(
        kern,
        grid=(n // tm,),
        in_specs=[
            pl.BlockSpec((tm, w2), lambda i: (i, 0)),
            pl.BlockSpec((tm, w2), lambda i: (i, 1)),
            pl.BlockSpec((tm, w2), lambda i: (i, 2)),
            pl.BlockSpec((tm, 8), row),
            _layer_spec((8, HEAD_DIM), layer),
            _layer_spec((1, HEAD_DIM), layer), _layer_spec((1, HEAD_DIM), layer),
            _layer_spec((1, HEAD_DIM), layer), _layer_spec((1, HEAD_DIM), layer),
            _layer_spec((1, MLA_Q_RANK), layer), _layer_spec((1, MLA_KV_RANK), layer),
            _layer_spec((1, WIDE_HEAD), layer), _layer_spec((1, WIDE_HEAD), layer),
            _layer_spec((MLA_Q_RANK, GROUP_HEADS * WIDE_HEAD), layer),
            _layer_spec((MLA_KV_RANK, GROUP_WIDTH), layer),
            _layer_spec((GROUP_WIDTH, MLA_KV_RANK), layer),
            pl.BlockSpec((tm, HEAD_DIM), tab), pl.BlockSpec((tm, HEAD_DIM), tab),
            pl.BlockSpec((tm, HEAD_DIM), tab), pl.BlockSpec((tm, HEAD_DIM), tab),
        ],
        out_specs=[pl.BlockSpec((tm, w2), row)] * 6 + [
            pl.BlockSpec((None, None, GROUP_WIDTH, tm), lambda i: (i // tpb, i % tpb, 0, 0)),
        ],
        out_shape=[jax.ShapeDtypeStruct((n, w2), BF16)] * 6 + [
            jax.ShapeDtypeStruct((batch, tpb, GROUP_WIDTH, tm), BF16),
        ],
        scratch_shapes=[pltpu.VMEM((GROUP_WIDTH, GROUP_WIDTH), F32)],
        compiler_params=pltpu.CompilerParams(
            dimension_semantics=("arbitrary",), vmem_limit_bytes=VMEM_LIMIT),
        name="prep",
    )(proj, proj, proj, negc_col, bounds, gaq, gak, gcq, gck, gdcq, gdckv, gdq, gdk, wuq, wuk, wuvt,
      cosc, sinc, cosd, sind)


def _key_query_iota(t):
    key = lax.broadcasted_iota(jnp.int32, (t, t), 0)
    qry = lax.broadcasted_iota(jnp.int32, (t, t), 1)
    return key, qry


def _softmax_step(s, vt, carry):
    m, acc = carry
    m_new = jnp.maximum(m, jnp.max(s, axis=0, keepdims=True))
    alpha = jnp.exp2(m - m_new)
    p = jnp.exp2(s - m_new)
    return m_new, alpha * acc + _dot(vt, p.astype(BF16))


def _softmax_init(t):
    return (jnp.full((1, t), NEG, F32), jnp.zeros((HEAD_DIM + SUM_ROWS, t), F32))


def _softmax_attn_kernel(q_ref, k_ref, vt_ref, o_ref, acc_scr, *, t, bounded):
    qi = pl.program_id(1)
    key, qry = _key_query_iota(t)
    causal = key <= qry
    dk = WIDE_HEAD
    dv = HEAD_DIM
    ones_rows = jnp.ones((SUM_ROWS, t), BF16)

    def logits(j, h, diag):
        off = pl.multiple_of(j * t, t)
        s = _dot_nt(k_ref[pl.ds(off, t), h * dk:(h + 1) * dk], q_ref[:, h * dk:(h + 1) * dk])
        return jnp.where(causal, s, NEG) if diag else s

    def values(j, h):
        return jnp.concatenate([vt_ref[j, h * dv:(h + 1) * dv, :], ones_rows], axis=0)

    if bounded:
        acc_scr[...] = jnp.zeros_like(acc_scr)

        def step(tiles, diag):
            s = [[logits(j, h, diag) for h in range(GROUP_HEADS)] for j in tiles]
            for n, j in enumerate(tiles):
                for h in range(GROUP_HEADS):
                    acc_scr[h] += _dot(values(j, h), jnp.exp2(s[n][h]).astype(BF16))

        def pair(jj, c):
            step((2 * jj, 2 * jj + 1), False)
            return c

        lax.fori_loop(0, qi // 2, pair, 0)

        @pl.when(qi % 2 == 1)
        def _():
            step((qi - 1,), False)

        step((qi,), True)
        accs = [acc_scr[h] for h in range(GROUP_HEADS)]
    else:
        def step(j, carry, diag):
            out = []
            for h in range(GROUP_HEADS):
                out.append(_softmax_step(logits(j, h, diag), values(j, h), carry[h]))
            return tuple(out)

        carry = step(qi, tuple(_softmax_init(t) for _ in range(GROUP_HEADS)), True)
        carry = lax.fori_loop(0, qi, lambda j, c: step(j, c, False), carry)
        accs = [c[1] for c in carry]

    for h in range(GROUP_HEADS):
        a = accs[h]
        o_ref[:, h * dv:(h + 1) * dv] = (a[:dv] / a[dv:dv + 1]).T.astype(BF16)


def _softmax_attn(name, q_arr, k_arr, vt_arr, vt_group, batch, seq, bounded):
    t = TILE
    nq = seq // t
    n = batch * seq
    kern = functools.partial(_softmax_attn_kernel, t=t, bounded=bounded)
    if vt_group is None:
        vt_spec = pl.BlockSpec((None, nq, GROUP_WIDTH, t), lambda b, i: (b, 0, 0, 0))
    else:
        vt_spec = pl.BlockSpec((None, None, nq, GROUP_WIDTH, t), lambda b, i: (vt_group, b, 0, 0, 0))
    return pl.pallas_call(
        kern,
        grid=(batch, nq),
        in_specs=[
            pl.BlockSpec((t, GROUP_HEADS * WIDE_HEAD), lambda b, i: (b * nq + i, 0)),
            pl.BlockSpec((seq, GROUP_HEADS * WIDE_HEAD), lambda b, i: (b, 0)),
            vt_spec,
        ],
        out_specs=pl.BlockSpec((t, GROUP_WIDTH), lambda b, i: (b * nq + i, 0)),
        out_shape=jax.ShapeDtypeStruct((n, GROUP_WIDTH), BF16),
        scratch_shapes=[pltpu.VMEM((GROUP_HEADS, HEAD_DIM + SUM_ROWS, t), F32)],
        compiler_params=pltpu.CompilerParams(
            dimension_semantics=("parallel", "parallel"), vmem_limit_bytes=VMEM_LIMIT),
        name=name + ("" if bounded else "_online"),
    )(q_arr, k_arr, vt_arr)


def _bounded_or_online(bound, name, *args):
    return lax.cond(bound < MAX_SAFE_BOUND,
                    lambda: _softmax_attn(name, *args, bounded=True),
                    lambda: _softmax_attn(name, *args, bounded=False))


def _stickbreak_kernel(q_ref, k_ref, vt_ref, o_ref, acc_scr, rest_scr, *, t):
    qi = pl.program_id(1)
    key, qry = _key_query_iota(t)
    strict = key < qry
    later = jnp.where(qry > key, 1.0, 0.0).astype(BF16)
    hd = HEAD_DIM
    acc_scr[...] = jnp.zeros_like(acc_scr)
    rest_scr[...] = jnp.zeros_like(rest_scr)

    def step(tiles):
        qs = [(q_ref[:, h * hd:(h + 1) * hd].astype(F32) * (SCALE * LOG2E)).astype(BF16) for h in range(GROUP_HEADS)]
        zs, drops, afters = [], [], []
        for j, diag in tiles:
            off = pl.multiple_of(j * t, t)
            zs.append([_dot_nt(k_ref[pl.ds(off, t), h * hd:(h + 1) * hd], qs[h]) for h in range(GROUP_HEADS)])
        for n, (j, diag) in enumerate(tiles):
            row = []
            for h in range(GROUP_HEADS):
                z = zs[n][h]
                drop = jnp.maximum(z, 0.0) + jnp.log(1.0 + jnp.exp2(-jnp.abs(z))) * LOG2E
                row.append(jnp.where(strict, drop, 0.0) if diag else drop)
            drops.append(row)
            afters.append([_dot(later, d.astype(BF16)) for d in row])
        low = None
        for h in range(GROUP_HEADS):
            used = rest_scr[h]
            for n, (j, diag) in enumerate(tiles):
                z, drop, after = zs[n][h], drops[n][h], afters[n][h]
                w = jnp.exp2((z - drop) - (after + used))
                if diag:
                    w = jnp.where(strict, w, 0.0)
                acc_scr[h] += _dot(vt_ref[j, h * hd:(h + 1) * hd, :], w.astype(BF16))
                used = used + after[0:1] + drop[0:1]
            rest_scr[h] = used
            low = used if low is None else jnp.minimum(low, used)
        return jnp.min(low)

    def cond(c):
        j, low = c
        return jnp.logical_and(j >= 0, low < STICK_DONE)

    def body(c):
        j, _ = c
        return j - 1, step([(j, False)])

    first = lax.cond(qi == 0, lambda: step([(qi, True)]), lambda: step([(qi, True), (qi - 1, False)]))
    lax.while_loop(cond, body, (qi - 2, first))
    for h in range(GROUP_HEADS):
        o_ref[:, h * hd:(h + 1) * hd] = acc_scr[h].T.astype(BF16)


def _stickbreak_attn(proj, vtb, batch, seq):
    t = STICK_TILE
    nq = seq // t
    n = batch * seq
    kern = functools.partial(_stickbreak_kernel, t=t)
    return pl.pallas_call(
        kern,
        grid=(batch, nq),
        in_specs=[
            pl.BlockSpec((t, GROUP_WIDTH), lambda b, i: (b * nq + i, T_BQ)),
            pl.BlockSpec((seq, GROUP_WIDTH), lambda b, i: (b, T_BK)),
            pl.BlockSpec((None, nq, GROUP_WIDTH, t), lambda b, i: (b, 0, 0, 0)),
        ],
        out_specs=pl.BlockSpec((t, GROUP_WIDTH), lambda b, i: (b * nq + i, 0)),
        out_shape=jax.ShapeDtypeStruct((n, GROUP_WIDTH), BF16),
        scratch_shapes=[pltpu.VMEM((GROUP_HEADS, HEAD_DIM, t), F32), pltpu.VMEM((GROUP_HEADS, 1, t), F32)],
        compiler_params=pltpu.CompilerParams(
            dimension_semantics=("parallel", "parallel"), vmem_limit_bytes=VMEM_LIMIT),
        name="stickbreak_attn",
    )(proj, proj, vtb)


def _outproj_kernel(oa_ref, ob_ref, oc_ref, od_ref, g_ref, w_ref, x_ref, gf_ref, out_ref, hf_ref):
    acc = x_ref[...]
    for gi, o_ref in enumerate((oa_ref, ob_ref, oc_ref, od_ref)):
        y = _rms(o_ref[...].astype(F32), g_ref[gi:gi + 1, :]).astype(BF16)
        acc = acc + _dot(y, w_ref[gi * GROUP_WIDTH:(gi + 1) * GROUP_WIDTH, :])
    out_ref[...] = acc
    hf_ref[...] = _rms(acc, gf_ref[...]).astype(BF16)


def _outproj(oa, ob, oc, od, g, w, x2, g_ffn, layer):
    n = x2.shape[0]
    tm = 512
    row = lambda i: (i, 0)
    return pl.pallas_call(
        _outproj_kernel,
        grid=(n // tm,),
        in_specs=[pl.BlockSpec((tm, GROUP_WIDTH), row)] * 4 + [
            _layer_spec((N_GROUPS, GROUP_WIDTH), layer),
            _layer_spec((N_GROUPS * GROUP_WIDTH, D_MODEL), layer, pipeline_mode=pl.Buffered(1)),
            pl.BlockSpec((tm, D_MODEL), row),
            _layer_spec((1, D_MODEL), layer),
        ],
        out_specs=[pl.BlockSpec((tm, D_MODEL), row), pl.BlockSpec((tm, D_MODEL), row)],
        out_shape=[jax.ShapeDtypeStruct((n, D_MODEL), F32), jax.ShapeDtypeStruct((n, D_MODEL), BF16)],
        compiler_params=pltpu.CompilerParams(
            dimension_semantics=("parallel",), vmem_limit_bytes=VMEM_LIMIT),
        name="outproj",
    )(oa, ob, oc, od, g, w, x2, g_ffn)


def _ffn_up_kernel(h_ref, wg_ref, wv_ref, cwg_ref, cwv_ref, cbg_ref, cbv_ref,
                   out_ref, wg_scr, wv_scr, tail_g, tail_v, *, tiles_per_batch, tm):
    i = pl.program_id(1)

    @pl.when(i == 0)
    def _():
        wg_scr[...] = wg_ref[...].astype(BF16)
        wv_scr[...] = wv_ref[...].astype(BF16)

    @pl.when(i % tiles_per_batch == 0)
    def _():
        tail_g[...] = jnp.zeros_like(tail_g)
        tail_v[...] = jnp.zeros_like(tail_v)

    def conv(u, cw_ref, cb_ref, prev):
        ue = jnp.concatenate([prev, u], axis=0)
        y = cb_ref[...] + cw_ref[2:3, :] * u
        for back in range(1, CONV_WIDTH):
            y = y + cw_ref[CONV_WIDTH - 1 - back:CONV_WIDTH - back, :] * pltpu.roll(ue, back, axis=0)[8:]
        return y, u[u.shape[0] - 8:]

    h = h_ref[...]
    u_g = _dot(h, wg_scr[...])
    u_v = _dot(h, wv_scr[...])
    gate, tail_g[...] = conv(u_g, cwg_ref, cbg_ref, tail_g[...])
    val, tail_v[...] = conv(u_v, cwv_ref, cbv_ref, tail_v[...])
    out_ref[...] = (gate / (1.0 + jnp.exp(-gate)) * val).astype(BF16)


def _ffn_up(h, w_up, conv_w, conv_b, layer, seq):
    n = h.shape[0]
    tm = FFN_TOKENS
    tn = 512
    nj = D_FF // tn
    tpb = seq // tm
    kern = functools.partial(_ffn_up_kernel, tiles_per_batch=tpb, tm=tm)
    return pl.pallas_call(
        kern,
        grid=(nj, n // tm),
        in_specs=[
            pl.BlockSpec((tm, D_MODEL), lambda j, i: (i, 0)),
            pl.BlockSpec((None, D_MODEL, tn), lambda j, i: (layer, 0, j)),
            pl.BlockSpec((None, D_MODEL, tn), lambda j, i: (layer, 0, j + nj)),
            pl.BlockSpec((None, CONV_WIDTH, tn), lambda j, i: (layer, 0, j)),
            pl.BlockSpec((None, CONV_WIDTH, tn), lambda j, i: (layer, 0, j + nj)),
            pl.BlockSpec((None, 1, tn), lambda j, i: (layer, 0, j)),
            pl.BlockSpec((None, 1, tn), lambda j, i: (layer, 0, j + nj)),
        ],
        out_specs=pl.BlockSpec((tm, tn), lambda j, i: (i, j)),
        out_shape=jax.ShapeDtypeStruct((n, D_FF), BF16),
        scratch_shapes=[pltpu.VMEM((D_MODEL, tn), BF16), pltpu.VMEM((D_MODEL, tn), BF16),
                        pltpu.VMEM((8, tn), F32), pltpu.VMEM((8, tn), F32)],
        compiler_params=pltpu.CompilerParams(
            dimension_semantics=("arbitrary", "arbitrary"), vmem_limit_bytes=VMEM_LIMIT),
        name="ffn_up",
    )(h, w_up, w_up, conv_w, conv_w, conv_b, conv_b)


def _ffn_down_kernel(a_ref, w_ref, x_ref, out_ref):
    a = a_ref[...]
    tn = 2 * MXU_WIDTH
    for c in range(D_MODEL // tn):
        cols = slice(c * tn, (c + 1) * tn)
        out_ref[:, cols] = x_ref[:, cols] + _dot(a, w_ref[:, cols])


def _ffn_down(act, w_down, x2, layer):
    n = x2.shape[0]
    tm = 512
    row = lambda i: (i, 0)
    return pl.pallas_call(
        _ffn_down_kernel,
        grid=(n // tm,),
        in_specs=[
            pl.BlockSpec((tm, D_FF), row),
            _layer_spec((D_FF, D_MODEL), layer, pipeline_mode=pl.Buffered(1)),
            pl.BlockSpec((tm, D_MODEL), row),
        ],
        out_specs=pl.BlockSpec((tm, D_MODEL), row),
        out_shape=jax.ShapeDtypeStruct((n, D_MODEL), F32),
        compiler_params=pltpu.CompilerParams(
            dimension_semantics=("parallel",), vmem_limit_bytes=VMEM_LIMIT),
        name="ffn_down",
    )(act, w_down, x2)


def _rope_tables(seq):
    def angles(dim):
        inv_freq = ROPE_THETA ** (-np.arange(0, dim, 2, dtype=np.float64) / dim)
        ang = np.arange(seq, dtype=np.float64)[:, None] * inv_freq[None, :]
        return np.cos(ang), np.sin(ang)

    c, s = angles(HEAD_DIM)
    cosc = np.concatenate([c, c], axis=1)
    sinc = np.concatenate([-s, s], axis=1)
    c, s = angles(MLA_ROPE_DIM)
    z = np.zeros_like(c)
    cosd = np.concatenate([c, z, c, z], axis=1)
    sind = np.concatenate([-s, z, s, z], axis=1)
    return tuple(jnp.asarray(t, F32) for t in (cosc, sinc, cosd, sind))


def _spread_rope(t):
    half = MLA_ROPE_DIM // 2
    z = jnp.zeros(t.shape[:-1] + (half,), t.dtype)
    return jnp.concatenate([t[..., :half], z, t[..., half:], z], axis=-1)


IN_OFFSETS = {}
_o = 0
for _name, _width in (("aq", GROUP_WIDTH), ("ak", GROUP_WIDTH), ("av", GROUP_WIDTH), ("af", GROUP_HEADS),
                      ("bq", GROUP_WIDTH), ("bk", GROUP_WIDTH), ("bv", GROUP_WIDTH),
                      ("cq", GROUP_WIDTH), ("ck", GROUP_WIDTH), ("cv", GROUP_WIDTH),
                      ("dcq", MLA_Q_RANK), ("dckv", MLA_KV_RANK), ("dkr", MLA_ROPE_DIM)):
    IN_OFFSETS[_name] = (_o, _width)
    _o += _width
IN_WIDTH = _o
RELAYOUT_ROWS = 256


def _relayout_kernel(w_ref, kr_ref, main_ref, wvt_ref):
    def piece(name):
        o, width = IN_OFFSETS[name]
        return w_ref[:, o:o + width]

    gw = GROUP_WIDTH
    for tile, name in ((T_AQ, "aq"), (T_AK, "ak"), (T_CQ, "cq"), (T_CK, "ck"), (T_DCQ, "dcq"),
                       (T_BQ, "bq"), (T_BK, "bk")):
        main_ref[:, tile * gw:(tile + 1) * gw] = piece(name).astype(BF16)
    base = T_DKV * gw
    main_ref[:, base:base + MLA_KV_RANK] = piece("dckv").astype(BF16)
    main_ref[:, base + MLA_KV_RANK:base + MLA_KV_RANK + HEAD_DIM] = kr_ref[...].astype(BF16)
    main_ref[:, base + MLA_KV_RANK + HEAD_DIM:base + gw] = jnp.zeros(
        (main_ref.shape[0], gw - MLA_KV_RANK - HEAD_DIM), BF16)
    for gi, name in enumerate(("av", "bv", "cv")):
        wvt_ref[gi] = piece(name).T.astype(BF16)


def _layout_w_in(w):
    depth, d_model, _ = w.shape
    rb = RELAYOUT_ROWS
    o, width = IN_OFFSETS["dkr"]
    kr = _spread_rope(w[..., o:o + width])
    o, width = IN_OFFSETS["af"]
    wf = jnp.pad(jnp.swapaxes(w[..., o:o + width], -1, -2), ((0, 0), (0, 8 - GROUP_HEADS), (0, 0))).astype(BF16)
    w_main, wvt = pl.pallas_call(
        _relayout_kernel,
        grid=(depth, d_model // rb),
        in_specs=[
            pl.BlockSpec((None, rb, IN_WIDTH), lambda l, r: (l, r, 0)),
            pl.BlockSpec((None, rb, HEAD_DIM), lambda l, r: (l, r, 0)),
        ],
        out_specs=[
            pl.BlockSpec((None, rb, PROJ_TILES * GROUP_WIDTH), lambda l, r: (l, r, 0)),
            pl.BlockSpec((None, V_GROUPS, GROUP_WIDTH, rb), lambda l, r: (l, 0, 0, r)),
        ],
        out_shape=[
            jax.ShapeDtypeStruct((depth, d_model, PROJ_TILES * GROUP_WIDTH), BF16),
            jax.ShapeDtypeStruct((depth, V_GROUPS, GROUP_WIDTH, d_model), BF16),
        ],
        compiler_params=pltpu.CompilerParams(
            dimension_semantics=("parallel", "parallel"), vmem_limit_bytes=VMEM_LIMIT),
        name="relayout_w_in",
    )(w, kr)
    return w_main, wvt, wf


def _layout_w_uq(w):
    w = w.reshape(-1, MLA_Q_RANK, GROUP_HEADS, MLA_QK_DIM)
    w = jnp.concatenate([w[..., :MLA_NOPE_DIM], _spread_rope(w[..., MLA_NOPE_DIM:])], axis=-1)
    return w.reshape(-1, MLA_Q_RANK, GROUP_HEADS * WIDE_HEAD).astype(BF16)


def _layout_w_ukv(w):
    w = w.reshape(-1, MLA_KV_RANK, GROUP_HEADS, 2, HEAD_DIM)
    wk = w[:, :, :, 0, :].reshape(-1, MLA_KV_RANK, GROUP_WIDTH).astype(BF16)
    wvt = jnp.swapaxes(w[:, :, :, 1, :].reshape(-1, MLA_KV_RANK, GROUP_WIDTH), -1, -2).astype(BF16)
    return wk, wvt


def _logit_bound(gq, gk, dim, scale):
    return dim * scale * LOG2E * jnp.max(jnp.abs(gq), axis=-1) * jnp.max(jnp.abs(gk), axis=-1) * 1.01 + 0.5


def _layout_qk_gain(g):
    return jnp.concatenate([g[:, :MLA_NOPE_DIM], _spread_rope(g[:, MLA_NOPE_DIM:])], axis=-1)[:, None, :]


def kernel(x, attn_norm, w_in, b_forget, fox_q_norm, fox_k_norm, moba_q_norm, moba_k_norm, mla_cq_norm, mla_ckv_norm, w_uq, w_ukv, mla_q_norm, mla_k_norm, mix_out_norm, w_out, ffn_norm, w_up, conv_w, conv_b, w_down):
    batch, seq, d_model = x.shape
    depth = w_in.shape[0]
    assert d_model == D_MODEL and seq % TILE == 0 and seq % FFN_TOKENS == 0
    n = batch * seq
    x2 = x.reshape(n, d_model)
    cosc, sinc, cosd, sind = _rope_tables(seq)

    rows = lambda v: v[:, None, :]
    w_main, wvt, wf = _layout_w_in(w_in)
    wuq = _layout_w_uq(w_uq)
    wuk, wuvt = _layout_w_ukv(w_ukv)
    w_out_bf = w_out.astype(BF16)
    w_down_bf = w_down.astype(BF16)
    bf = jnp.pad(jnp.broadcast_to(b_forget[:, :, None], (depth, GROUP_HEADS, 128)),
                 ((0, 0), (0, 8 - GROUP_HEADS), (0, 0)))
    bound_a = _logit_bound(fox_q_norm, fox_k_norm, HEAD_DIM, SCALE)
    bound_c = _logit_bound(moba_q_norm, moba_k_norm, HEAD_DIM, SCALE)
    bound_d = _logit_bound(mla_q_norm, mla_k_norm, MLA_QK_DIM, SCALE_MLA)
    big_c = jnp.maximum(MOBA_MIN_MASK, jnp.exp2(jnp.ceil(jnp.log2(4.0 * bound_c + 64.0))))
    bounds = jnp.stack([bound_a, bound_c, big_c, bound_d] + [jnp.zeros_like(bound_a)] * 4, axis=1)
    bounds = jnp.broadcast_to(bounds[:, :, None], (depth, 8, HEAD_DIM))
    gdq, gdk = _layout_qk_gain(mla_q_norm), _layout_qk_gain(mla_k_norm)
    g_mix = mix_out_norm.reshape(depth, N_GROUPS, GROUP_WIDTH)
    conv_b3 = conv_b[:, None, :]

    for l in range(depth):
        proj, vt_ac, vt_b, negc = _inproj(x2, rows(attn_norm), w_main, wvt, wf, bf, l, batch, seq)
        negc_col = negc.transpose(0, 2, 1).reshape(n, 8)

        qa, ka, qc, kc, qd, kd, vtd = _prep(
            proj, negc_col, bounds, rows(fox_q_norm), rows(fox_k_norm), rows(moba_q_norm), rows(moba_k_norm),
            rows(mla_cq_norm), rows(mla_ckv_norm), gdq, gdk, wuq, wuk, wuvt,
            cosc, sinc, cosd, sind, l, batch, seq)

        o_a = _bounded_or_online(bound_a[l], "fox_attn", qa, ka, vt_ac, 0, batch, seq)
        o_b = _stickbreak_attn(proj, vt_b, batch, seq)
        o_c = _bounded_or_online(bound_c[l], "moba_attn", qc, kc, vt_ac, 1, batch, seq)
        o_d = _bounded_or_online(bound_d[l], "mla_attn", qd, kd, vtd, None, batch, seq)

        x2, h_ffn = _outproj(o_a, o_b, o_c, o_d, g_mix, w_out_bf, x2, rows(ffn_norm), l)

        act = _ffn_up(h_ffn, w_up, conv_w, conv_b3, l, seq)
        x2 = _ffn_down(act, w_down_bf, x2, l)

    return x2.reshape(batch, seq, d_model)
```

```python
import functools
import math

import numpy as np
import jax
import jax.numpy as jnp
from jax import lax
from jax.experimental import pallas as pl
from jax.experimental.pallas import tpu as pltpu

F32 = jnp.float32
BF16 = jnp.bfloat16

D_MODEL = 2048
HEAD_DIM = 128
GROUP_HEADS = 4
GROUP_WIDTH = GROUP_HEADS * HEAD_DIM
N_GROUPS = 4
MOBA_BLOCK = 256
MOBA_TOP_K = 3
MOBA_MAX_BLOCKS = 16
MLA_Q_RANK = 512
MLA_KV_RANK = 256
MLA_NOPE_DIM = 128
MLA_ROPE_DIM = 64
MLA_QK_DIM = MLA_NOPE_DIM + MLA_ROPE_DIM
WIDE_HEAD = 256
D_FF = 5632
CONV_WIDTH = 3
ROPE_THETA = 10000.0
EPS = 1e-6

T_AQ, T_AK, T_CQ, T_CK, T_DCQ, T_DKV, T_BQ, T_BK = range(8)
PROJ_TILES = 8
V_GROUPS = 3

TILE = 512
STICK_TILE = 256
NEG = -1e30
VMEM_LIMIT = 56 * 1024 * 1024
LOG2E = math.log2(math.e)
SCALE = HEAD_DIM ** -0.5
SCALE_MLA = MLA_QK_DIM ** -0.5
FFN_TOKENS = 1024
MXU_WIDTH = 256
SUM_ROWS = 16
MAX_SAFE_BOUND = 56.0
MOBA_MIN_MASK = 256.0
STICK_DONE = 160.0


def _dot(a, b):
    return jnp.dot(a, b, preferred_element_type=F32)


def _dot_nt(a, b):
    return lax.dot_general(a, b, (((1,), (1,)), ((), ())), preferred_element_type=F32)


def _split3(a):
    hi = a.astype(BF16)
    r1 = a - hi.astype(F32)
    mid = r1.astype(BF16)
    lo = (r1 - mid.astype(F32)).astype(BF16)
    return hi, mid, lo


def _log_sigmoid_pair(z):
    sp = jnp.log(1.0 + jnp.exp(-jnp.abs(z)))
    return jnp.minimum(z, 0.0) - sp, -jnp.maximum(z, 0.0) - sp


def _rms(v, g, denom=None):
    if denom is None:
        ms = jnp.mean(v * v, axis=-1, keepdims=True)
    else:
        ms = jnp.sum(v * v, axis=-1, keepdims=True) * (1.0 / denom)
    return v * lax.rsqrt(ms + EPS) * g


def _sumsq_lanes(v):
    return _dot((v * v).astype(BF16), jnp.ones((v.shape[-1], HEAD_DIM), BF16))


def _rms_lanes(v, g, denom=None):
    width = v.shape[-1]
    r = lax.rsqrt(_sumsq_lanes(v) * (1.0 / (denom or width)) + EPS)
    if width > HEAD_DIM:
        r = jnp.concatenate([r] * (width // HEAD_DIM), axis=1)
    return v * r * g


def _inproj_kernel(x_ref, g_ref, w_ref, wvt_ref, wf_ref, bf_ref, proj_ref, vt_ref, vtb_ref, negc_ref,
                   carry_scr, *, tiles_per_batch, tm):
    i = pl.program_id(0)
    h = _rms(x_ref[...], g_ref[...]).astype(BF16)
    z = _dot_nt(wf_ref[...], h) + bf_ref[:, 0:1]
    logf, _ = _log_sigmoid_pair(z)
    r = lax.broadcasted_iota(jnp.int32, (tm, tm), 0)
    c = lax.broadcasted_iota(jnp.int32, (tm, tm), 1)
    tri = jnp.where(r <= c, 1.0, 0.0).astype(BF16)
    hi, mid, lo = _split3(logf)
    cum = _dot(hi, tri) + _dot(mid, tri) + _dot(lo, tri)

    @pl.when(i % tiles_per_batch == 0)
    def _():
        carry_scr[...] = jnp.zeros_like(carry_scr)

    cum = cum + carry_scr[:, 0:1]
    carry_scr[...] = jnp.broadcast_to(cum[:, tm - 1:tm], carry_scr.shape)
    negc_ref[...] = cum * (-LOG2E)

    gw = GROUP_WIDTH
    for jt in range(PROJ_TILES):
        proj_ref[:, jt * gw:(jt + 1) * gw] = _dot(h, w_ref[:, jt * gw:(jt + 1) * gw]).astype(BF16)
    for gi in range(V_GROUPS):
        vt = _dot_nt(wvt_ref[gi], h).astype(BF16)
        if gi == 1:
            for part in range(tm // STICK_TILE):
                vtb_ref[part] = vt[:, part * STICK_TILE:(part + 1) * STICK_TILE]
        else:
            vt_ref[gi // 2] = vt


def _layer_spec(shape, layer, **kw):
    zeros = (0,) * len(shape)
    return pl.BlockSpec((None,) + tuple(shape), lambda *_: (layer,) + zeros, **kw)


def _inproj(x2, g, w, wvt, wf, bf, layer, batch, seq):
    n = x2.shape[0]
    tm = TILE
    tpb = seq // tm
    kern = functools.partial(_inproj_kernel, tiles_per_batch=tpb, tm=tm)
    resident = pl.Buffered(1)
    return pl.pallas_call(
        kern,
        grid=(n // tm,),
        in_specs=[
            pl.BlockSpec((tm, D_MODEL), lambda i: (i, 0)),
            _layer_spec((1, D_MODEL), layer),
            _layer_spec((D_MODEL, PROJ_TILES * GROUP_WIDTH), layer, pipeline_mode=resident),
            _layer_spec((V_GROUPS, GROUP_WIDTH, D_MODEL), layer, pipeline_mode=resident),
            _layer_spec((8, D_MODEL), layer),
            _layer_spec((8, 128), layer),
        ],
        out_specs=[
            pl.BlockSpec((tm, PROJ_TILES * GROUP_WIDTH), lambda i: (i, 0)),
            pl.BlockSpec((2, None, None, GROUP_WIDTH, tm), lambda i: (0, i // tpb, i % tpb, 0, 0)),
            pl.BlockSpec((None, tm // STICK_TILE, GROUP_WIDTH, STICK_TILE), lambda i: (i // tpb, i % tpb, 0, 0)),
            pl.BlockSpec((None, 8, tm), lambda i: (i // tpb, 0, i % tpb)),
        ],
        out_shape=[
            jax.ShapeDtypeStruct((n, PROJ_TILES * GROUP_WIDTH), BF16),
            jax.ShapeDtypeStruct((2, batch, tpb, GROUP_WIDTH, tm), BF16),
            jax.ShapeDtypeStruct((batch, seq // STICK_TILE, GROUP_WIDTH, STICK_TILE), BF16),
            jax.ShapeDtypeStruct((batch, 8, seq), F32),
        ],
        scratch_shapes=[pltpu.VMEM((8, 128), F32)],
        compiler_params=pltpu.CompilerParams(
            dimension_semantics=("arbitrary",), vmem_limit_bytes=VMEM_LIMIT),
        name="inproj",
    )(x2, g, w, wvt, wf, bf)


def _prep_kernel(a_ref, c_ref, d_ref, nc_ref, bnd_ref, gaq, gak, gcq, gck, gdcq, gdckv, gdq, gdk, wuq, wuk, wuvt,
                 cosc, sinc, cosd, sind,
                 qa_ref, ka_ref, qc_ref, kc_ref, qd_ref, kd_ref, vtd_ref, kmt_scr,
                 *, tiles_per_batch, tm):
    i = pl.program_id(0)
    t = i % tiles_per_batch
    hd = HEAD_DIM
    wh = WIDE_HEAD
    blocks_per_tile = tm // MOBA_BLOCK
    lane = lax.broadcasted_iota(jnp.int32, (tm, hd), 1)
    bound_a = bnd_ref[0:1, :]
    bound_c = bnd_ref[1:2, :]
    big_c = bnd_ref[2:3, :]
    bound_d = bnd_ref[3:4, :]

    nc = nc_ref[...]
    for h in range(GROUP_HEADS):
        q = a_ref[:, h * hd:(h + 1) * hd].astype(F32)
        qa_ref[:, h * wh:h * wh + hd] = (_rms_lanes(q, gaq[...]) * (SCALE * LOG2E)).astype(BF16)
        k = a_ref[:, GROUP_WIDTH + h * hd:GROUP_WIDTH + (h + 1) * hd].astype(F32)
        ka_ref[:, h * wh:h * wh + hd] = _rms_lanes(k, gak[...]).astype(BF16)
        hi, mid, lo = (p.astype(F32) for p in _split3(nc[:, h:h + 1]))
        k_aug = jnp.where(lane == 0, hi, jnp.where(lane == 1, mid, jnp.where(lane == 2, lo, jnp.where(
            lane < 6, 1.0, jnp.where(lane == 6, -bound_a, 0.0)))))
        q_aug = jnp.where(lane < 3, 1.0, jnp.where(lane == 3, -hi, jnp.where(lane == 4, -mid, jnp.where(
            lane == 5, -lo, jnp.where(lane == 6, 1.0, 0.0)))))
        ka_ref[:, h * wh + hd:(h + 1) * wh] = k_aug.astype(BF16)
        qa_ref[:, h * wh + hd:(h + 1) * wh] = q_aug.astype(BF16)

    @pl.when(i == 0)
    def _():
        kmt_scr[...] = jnp.zeros_like(kmt_scr)

    cc = cosc[...]
    sc = sinc[...]
    sub8 = lax.broadcasted_iota(jnp.int32, (8, hd), 0)
    key_blk = t * blocks_per_tile + lax.broadcasted_iota(jnp.int32, (tm, hd), 0) // MOBA_BLOCK

    def rope_c(v):
        return v * cc + pltpu.roll(v, hd // 2, axis=1) * sc

    kc_aug = jnp.where(lane == key_blk, big_c, jnp.where(lane == MOBA_MAX_BLOCKS, -(bound_c + big_c), 0.0))
    kc_aug = kc_aug.astype(BF16)
    for h in range(GROUP_HEADS):
        k = rope_c(_rms_lanes(c_ref[:, GROUP_WIDTH + h * hd:GROUP_WIDTH + (h + 1) * hd].astype(F32), gck[...]))
        kc_ref[:, h * wh:h * wh + hd] = k.astype(BF16)
        kc_ref[:, h * wh + hd:(h + 1) * wh] = kc_aug
        for blk in range(blocks_per_tile):
            km = jnp.mean(k[blk * MOBA_BLOCK:(blk + 1) * MOBA_BLOCK], axis=0, keepdims=True)
            nblk = t * blocks_per_tile + blk
            base = pl.multiple_of(h * hd + (nblk // 8) * 8, 8)
            old = kmt_scr[pl.ds(base, 8), h * hd:(h + 1) * hd]
            kmt_scr[pl.ds(base, 8), h * hd:(h + 1) * hd] = jnp.where(sub8 == nblk % 8, km, old)

    qs = []
    for h in range(GROUP_HEADS):
        q = rope_c(_rms_lanes(c_ref[:, h * hd:(h + 1) * hd].astype(F32), gcq[...])) * (SCALE * LOG2E)
        qc_ref[:, h * wh:h * wh + hd] = q.astype(BF16)
        qs.append(q)
    q_all = jnp.concatenate(qs, axis=1)
    qh, qm, _ = _split3(q_all)
    kh, km_, _ = _split3(kmt_scr[...])
    gate_t = _dot_nt(kh, qh) + _dot_nt(km_, qh) + _dot_nt(kh, qm)

    nb = MOBA_MAX_BLOCKS
    blk_row = lax.broadcasted_iota(jnp.int32, (nb, tm), 0)
    own = t * blocks_per_tile + lax.broadcasted_iota(jnp.int32, (nb, tm), 1) // MOBA_BLOCK
    blk_f = blk_row.astype(F32)
    ninf = jnp.float32(-jnp.inf)
    no_rows = jnp.zeros((hd - nb, tm), F32)
    for h in range(GROUP_HEADS):
        g = jnp.where(blk_row < own, gate_t[h * hd:h * hd + nb, :], ninf)
        sel = jnp.where(blk_row == own, 1.0, 0.0)
        for _ in range(MOBA_TOP_K):
            mx = jnp.max(g, axis=0, keepdims=True)
            cand = jnp.where(g == mx, blk_f, 1e9)
            cand = jnp.where(mx > ninf, cand, 1e9)
            idx = jnp.min(cand, axis=0, keepdims=True)
            pick = blk_f == idx
            sel = jnp.where(pick, 1.0, sel)
            g = jnp.where(pick, ninf, g)
        sel_rows = jnp.concatenate([sel, no_rows], axis=0).T
        q_aug = jnp.where(lane == nb, 1.0, sel_rows)
        qc_ref[:, h * wh + hd:(h + 1) * wh] = q_aug.astype(BF16)

    cd = cosd[...]
    sd = sind[...]
    free_lane = lane == MLA_ROPE_DIM // 2

    def rope_d(v):
        return v * cd + pltpu.roll(v, hd // 2, axis=1) * sd

    cq = _rms_lanes(d_ref[:, 0:MLA_Q_RANK].astype(F32), gdcq[...]).astype(BF16)
    qf = _dot(cq, wuq[...])
    for h in range(GROUP_HEADS):
        qn = _rms_lanes(qf[:, h * wh:(h + 1) * wh], gdq[...], denom=MLA_QK_DIM)
        qd_ref[:, h * wh:h * wh + hd] = (qn[:, :hd] * (SCALE_MLA * LOG2E)).astype(BF16)
        q_rot = rope_d(qn[:, hd:]) * (SCALE_MLA * LOG2E)
        qd_ref[:, h * wh + hd:(h + 1) * wh] = jnp.where(free_lane, 1.0, q_rot).astype(BF16)

    ckv = _rms_lanes(d_ref[:, MLA_Q_RANK:MLA_Q_RANK + MLA_KV_RANK].astype(F32), gdckv[...]).astype(BF16)
    knf = _dot(ckv, wuk[...])
    vtd_ref[...] = _dot_nt(wuvt[...], ckv).astype(BF16)
    kr = d_ref[:, MLA_Q_RANK + MLA_KV_RANK:MLA_Q_RANK + MLA_KV_RANK + hd].astype(F32)
    ss_r = _sumsq_lanes(kr)
    gk = gdk[...]
    for h in range(GROUP_HEADS):
        kn = knf[:, h * hd:(h + 1) * hd]
        r = lax.rsqrt((_sumsq_lanes(kn) + ss_r) * (1.0 / MLA_QK_DIM) + EPS)
        kd_ref[:, h * wh:h * wh + hd] = (kn * r * gk[:, :hd]).astype(BF16)
        k_rot = rope_d(kr * r * gk[:, hd:])
        kd_ref[:, h * wh + hd:(h + 1) * wh] = jnp.where(free_lane, -bound_d, k_rot).astype(BF16)


def _prep(proj, negc_col, bounds, gaq, gak, gcq, gck, gdcq, gdckv, gdq, gdk, wuq, wuk, wuvt,
          cosc, sinc, cosd, sind, layer, batch, seq):
    n = proj.shape[0]
    tm = TILE
    tpb = seq // tm
    assert seq // MOBA_BLOCK <= MOBA_MAX_BLOCKS
    kern = functools.partial(_prep_kernel, tiles_per_batch=tpb, tm=tm)
    row = lambda i: (i, 0)
    tab = lambda i: (i % tpb, 0)
    w2 = 2 * GROUP_WIDTH
    return pl.pallas_call(
        kern,
        grid=(n // tm,),
        in_specs=[
            pl.BlockSpec((tm, w2), lambda i: (i, 0)),
            pl.BlockSpec((tm, w2), lambda i: (i, 1)),
            pl.BlockSpec((tm, w2), lambda i: (i, 2)),
            pl.BlockSpec((tm, 8), row),
            _layer_spec((8, HEAD_DIM), layer),
            _layer_spec((1, HEAD_DIM), layer), _layer_spec((1, HEAD_DIM), layer),
            _layer_spec((1, HEAD_DIM), layer), _layer_spec((1, HEAD_DIM), layer),
            _layer_spec((1, MLA_Q_RANK), layer), _layer_spec((1, MLA_KV_RANK), layer),
            _layer_spec((1, WIDE_HEAD), layer), _layer_spec((1, WIDE_HEAD), layer),
            _layer_spec((MLA_Q_RANK, GROUP_HEADS * WIDE_HEAD), layer),
            _layer_spec((MLA_KV_RANK, GROUP_WIDTH), layer),
            _layer_spec((GROUP_WIDTH, MLA_KV_RANK), layer),
            pl.BlockSpec((tm, HEAD_DIM), tab), pl.BlockSpec((tm, HEAD_DIM), tab),
            pl.BlockSpec((tm, HEAD_DIM), tab), pl.BlockSpec((tm, HEAD_DIM), tab),
        ],
        out_specs=[pl.BlockSpec((tm, w2), row)] * 6 + [
            pl.BlockSpec((None, None, GROUP_WIDTH, tm), lambda i: (i // tpb, i % tpb, 0, 0)),
        ],
        out_shape=[jax.ShapeDtypeStruct((n, w2), BF16)] * 6 + [
            jax.ShapeDtypeStruct((batch, tpb, GROUP_WIDTH, tm), BF16),
        ],
        scratch_shapes=[pltpu.VMEM((GROUP_WIDTH, GROUP_WIDTH), F32)],
        compiler_params=pltpu.CompilerParams(
            dimension_semantics=("arbitrary",), vmem_limit_bytes=VMEM_LIMIT),
        name="prep",
    )(proj, proj, proj, negc_col, bounds, gaq, gak, gcq, gck, gdcq, gdckv, gdq, gdk, wuq, wuk, wuvt,
      cosc, sinc, cosd, sind)


def _key_query_iota(t):
    key = lax.broadcasted_iota(jnp.int32, (t, t), 0)
    qry = lax.broadcasted_iota(jnp.int32, (t, t), 1)
    return key, qry


def _softmax_step(s, vt, carry):
    m, acc = carry
    m_new = jnp.maximum(m, jnp.max(s, axis=0, keepdims=True))
    alpha = jnp.exp2(m - m_new)
    p = jnp.exp2(s - m_new)
    return m_new, alpha * acc + _dot(vt, p.astype(BF16))


def _softmax_init(t):
    return (jnp.full((1, t), NEG, F32), jnp.zeros((HEAD_DIM + SUM_ROWS, t), F32))


def _softmax_attn_kernel(q_ref, k_ref, vt_ref, o_ref, acc_scr, *, t, bounded):
    qi = pl.program_id(1)
    key, qry = _key_query_iota(t)
    causal = key <= qry
    dk = WIDE_HEAD
    dv = HEAD_DIM
    ones_rows = jnp.ones((SUM_ROWS, t), BF16)

    def logits(j, h, diag):
        off = pl.multiple_of(j * t, t)
        s = _dot_nt(k_ref[pl.ds(off, t), h * dk:(h + 1) * dk], q_ref[:, h * dk:(h + 1) * dk])
        return jnp.where(causal, s, NEG) if diag else s

    def values(j, h):
        return jnp.concatenate([vt_ref[j, h * dv:(h + 1) * dv, :], ones_rows], axis=0)

    if bounded:
        acc_scr[...] = jnp.zeros_like(acc_scr)

        def step(*tiles):
            s = [[logits(j, h, diag) for h in range(GROUP_HEADS)] for j, diag in tiles]
            for n, (j, _) in enumerate(tiles):
                for h in range(GROUP_HEADS):
                    acc_scr[h] += _dot(values(j, h), jnp.exp2(s[n][h]).astype(BF16))

        left = jnp.maximum(qi - 1, 0)

        def pair(jj, c):
            step((2 * jj, False), (2 * jj + 1, False))
            return c

        lax.fori_loop(0, left // 2, pair, 0)

        @pl.when(left % 2 == 1)
        def _():
            step((left - 1, False))

        @pl.when(qi == 0)
        def _():
            step((qi, True))

        @pl.when(qi > 0)
        def _():
            step((qi, True), (qi - 1, False))

        accs = [acc_scr[h] for h in range(GROUP_HEADS)]
    else:
        def step(j, carry, diag):
            out = []
            for h in range(GROUP_HEADS):
                out.append(_softmax_step(logits(j, h, diag), values(j, h), carry[h]))
            return tuple(out)

        carry = step(qi, tuple(_softmax_init(t) for _ in range(GROUP_HEADS)), True)
        carry = lax.fori_loop(0, qi, lambda j, c: step(j, c, False), carry)
        accs = [c[1] for c in carry]

    for h in range(GROUP_HEADS):
        a = accs[h]
        o_ref[:, h * dv:(h + 1) * dv] = (a[:dv] / a[dv:dv + 1]).T.astype(BF16)


def _softmax_attn(name, q_arr, k_arr, vt_arr, vt_group, batch, seq, bounded):
    t = TILE
    nq = seq // t
    n = batch * seq
    kern = functools.partial(_softmax_attn_kernel, t=t, bounded=bounded)
    if vt_group is None:
        vt_spec = pl.BlockSpec((None, nq, GROUP_WIDTH, t), lambda b, i: (b, 0, 0, 0))
    else:
        vt_spec = pl.BlockSpec((None, None, nq, GROUP_WIDTH, t), lambda b, i: (vt_group, b, 0, 0, 0))
    return pl.pallas_call(
        kern,
        grid=(batch, nq),
        in_specs=[
            pl.BlockSpec((t, GROUP_HEADS * WIDE_HEAD), lambda b, i: (b * nq + i, 0)),
            pl.BlockSpec((seq, GROUP_HEADS * WIDE_HEAD), lambda b, i: (b, 0)),
            vt_spec,
        ],
        out_specs=pl.BlockSpec((t, GROUP_WIDTH), lambda b, i: (b * nq + i, 0)),
        out_shape=jax.ShapeDtypeStruct((n, GROUP_WIDTH), BF16),
        scratch_shapes=[pltpu.VMEM((GROUP_HEADS, HEAD_DIM + SUM_ROWS, t), F32)],
        compiler_params=pltpu.CompilerParams(
            dimension_semantics=("parallel", "parallel"), vmem_limit_bytes=VMEM_LIMIT),
        name=name + ("" if bounded else "_online"),
    )(q_arr, k_arr, vt_arr)


def _bounded_or_online(bound, name, *args):
    return lax.cond(bound < MAX_SAFE_BOUND,
                    lambda: _softmax_attn(name, *args, bounded=True),
                    lambda: _softmax_attn(name, *args, bounded=False))


def _stickbreak_kernel(q_ref, k_ref, vt_ref, o_ref, acc_scr, rest_scr, *, t):
    qi = pl.program_id(1)
    key, qry = _key_query_iota(t)
    strict = key < qry
    later = jnp.where(qry > key, 1.0, 0.0).astype(BF16)
    hd = HEAD_DIM
    acc_scr[...] = jnp.zeros_like(acc_scr)
    rest_scr[...] = jnp.zeros_like(rest_scr)

    def step(tiles):
        qs = [(q_ref[:, h * hd:(h + 1) * hd].astype(F32) * (SCALE * LOG2E)).astype(BF16) for h in range(GROUP_HEADS)]
        zs, drops, afters = [], [], []
        for j, diag in tiles:
            off = pl.multiple_of(j * t, t)
            zs.append([_dot_nt(k_ref[pl.ds(off, t), h * hd:(h + 1) * hd], qs[h]) for h in range(GROUP_HEADS)])
        for n, (j, diag) in enumerate(tiles):
            row = []
            for h in range(GROUP_HEADS):
                z = zs[n][h]
                drop = jnp.maximum(z, 0.0) + jnp.log(1.0 + jnp.exp2(-jnp.abs(z))) * LOG2E
                row.append(jnp.where(strict, drop, 0.0) if diag else drop)
            drops.append(row)
            afters.append([_dot(later, d.astype(BF16)) for d in row])
        low = None
        for h in range(GROUP_HEADS):
            used = rest_scr[h]
            for n, (j, diag) in enumerate(tiles):
                z, drop, after = zs[n][h], drops[n][h], afters[n][h]
                w = jnp.exp2((z - drop) - (after + used))
                if diag:
                    w = jnp.where(strict, w, 0.0)
                acc_scr[h] += _dot(vt_ref[j, h * hd:(h + 1) * hd, :], w.astype(BF16))
                used = used + after[0:1] + drop[0:1]
            rest_scr[h] = used
            low = used if low is None else jnp.minimum(low, used)
        return jnp.min(low)

    def cond(c):
        j, low = c
        return jnp.logical_and(j >= 0, low < STICK_DONE)

    def body(c):
        j, _ = c
        return j - 1, step([(j, False)])

    first = lax.cond(qi == 0, lambda: step([(qi, True)]), lambda: step([(qi, True), (qi - 1, False)]))
    lax.while_loop(cond, body, (qi - 2, first))
    for h in range(GROUP_HEADS):
        o_ref[:, h * hd:(h + 1) * hd] = acc_scr[h].T.astype(BF16)


def _stickbreak_attn(proj, vtb, batch, seq):
    t = STICK_TILE
    nq = seq // t
    n = batch * seq
    kern = functools.partial(_stickbreak_kernel, t=t)
    return pl.pallas_call(
        kern,
        grid=(batch, nq),
        in_specs=[
            pl.BlockSpec((t, GROUP_WIDTH), lambda b, i: (b * nq + i, T_BQ)),
            pl.BlockSpec((seq, GROUP_WIDTH), lambda b, i: (b, T_BK)),
            pl.BlockSpec((None, nq, GROUP_WIDTH, t), lambda b, i: (b, 0, 0, 0)),
        ],
        out_specs=pl.BlockSpec((t, GROUP_WIDTH), lambda b, i: (b * nq + i, 0)),
        out_shape=jax.ShapeDtypeStruct((n, GROUP_WIDTH), BF16),
        scratch_shapes=[pltpu.VMEM((GROUP_HEADS, HEAD_DIM, t), F32), pltpu.VMEM((GROUP_HEADS, 1, t), F32)],
        compiler_params=pltpu.CompilerParams(
            dimension_semantics=("parallel", "parallel"), vmem_limit_bytes=VMEM_LIMIT),
        name="stickbreak_attn",
    )(proj, proj, vtb)


def _outproj_kernel(oa_ref, ob_ref, oc_ref, od_ref, g_ref, w_ref, x_ref, gf_ref, out_ref, hf_ref):
    acc = x_ref[...]
    for gi, o_ref in enumerate((oa_ref, ob_ref, oc_ref, od_ref)):
        y = _rms(o_ref[...].astype(F32), g_ref[gi:gi + 1, :]).astype(BF16)
        acc = acc + _dot(y, w_ref[gi * GROUP_WIDTH:(gi + 1) * GROUP_WIDTH, :])
    out_ref[...] = acc
    hf_ref[...] = _rms(acc, gf_ref[...]).astype(BF16)


def _outproj(oa, ob, oc, od, g, w, x2, g_ffn, layer):
    n = x2.shape[0]
    tm = 512
    row = lambda i: (i, 0)
    return pl.pallas_call(
        _outproj_kernel,
        grid=(n // tm,),
        in_specs=[pl.BlockSpec((tm, GROUP_WIDTH), row)] * 4 + [
            _layer_spec((N_GROUPS, GROUP_WIDTH), layer),
            _layer_spec((N_GROUPS * GROUP_WIDTH, D_MODEL), layer, pipeline_mode=pl.Buffered(1)),
            pl.BlockSpec((tm, D_MODEL), row),
            _layer_spec((1, D_MODEL), layer),
        ],
        out_specs=[pl.BlockSpec((tm, D_MODEL), row), pl.BlockSpec((tm, D_MODEL), row)],
        out_shape=[jax.ShapeDtypeStruct((n, D_MODEL), F32), jax.ShapeDtypeStruct((n, D_MODEL), BF16)],
        compiler_params=pltpu.CompilerParams(
            dimension_semantics=("parallel",), vmem_limit_bytes=VMEM_LIMIT),
        name="outproj",
    )(oa, ob, oc, od, g, w, x2, g_ffn)


def _ffn_up_kernel(h_ref, wg_ref, wv_ref, cwg_ref, cwv_ref, cbg_ref, cbv_ref,
                   out_ref, wg_scr, wv_scr, tail_g, tail_v, *, tiles_per_batch, tm):
    i = pl.program_id(1)

    @pl.when(i == 0)
    def _():
        wg_scr[...] = wg_ref[...].astype(BF16)
        wv_scr[...] = wv_ref[...].astype(BF16)

    @pl.when(i % tiles_per_batch == 0)
    def _():
        tail_g[...] = jnp.zeros_like(tail_g)
        tail_v[...] = jnp.zeros_like(tail_v)

    def conv(u, cw_ref, cb_ref, prev):
        ue = jnp.concatenate([prev, u], axis=0)
        y = cb_ref[...] + cw_ref[2:3, :] * u
        for back in range(1, CONV_WIDTH):
            y = y + cw_ref[CONV_WIDTH - 1 - back:CONV_WIDTH - back, :] * pltpu.roll(ue, back, axis=0)[8:]
        return y, u[u.shape[0] - 8:]

    h = h_ref[...]
    u_g = _dot(h, wg_scr[...])
    u_v = _dot(h, wv_scr[...])
    gate, tail_g[...] = conv(u_g, cwg_ref, cbg_ref, tail_g[...])
    val, tail_v[...] = conv(u_v, cwv_ref, cbv_ref, tail_v[...])
    out_ref[...] = (gate / (1.0 + jnp.exp(-gate)) * val).astype(BF16)


def _ffn_up(h, w_up, conv_w, conv_b, layer, seq):
    n = h.shape[0]
    tm = FFN_TOKENS
    tn = 512
    nj = D_FF // tn
    tpb = seq // tm
    kern = functools.partial(_ffn_up_kernel, tiles_per_batch=tpb, tm=tm)
    return pl.pallas_call(
        kern,
        grid=(nj, n // tm),
        in_specs=[
            pl.BlockSpec((tm, D_MODEL), lambda j, i: (i, 0)),
            pl.BlockSpec((None, D_MODEL, tn), lambda j, i: (layer, 0, j)),
            pl.BlockSpec((None, D_MODEL, tn), lambda j, i: (layer, 0, j + nj)),
            pl.BlockSpec((None, CONV_WIDTH, tn), lambda j, i: (layer, 0, j)),
            pl.BlockSpec((None, CONV_WIDTH, tn), lambda j, i: (layer, 0, j + nj)),
            pl.BlockSpec((None, 1, tn), lambda j, i: (layer, 0, j)),
            pl.BlockSpec((None, 1, tn), lambda j, i: (layer, 0, j + nj)),
        ],
        out_specs=pl.BlockSpec((tm, tn), lambda j, i: (i, j)),
        out_shape=jax.ShapeDtypeStruct((n, D_FF), BF16),
        scratch_shapes=[pltpu.VMEM((D_MODEL, tn), BF16), pltpu.VMEM((D_MODEL, tn), BF16),
                        pltpu.VMEM((8, tn), F32), pltpu.VMEM((8, tn), F32)],
        compiler_params=pltpu.CompilerParams(
            dimension_semantics=("arbitrary", "arbitrary"), vmem_limit_bytes=VMEM_LIMIT),
        name="ffn_up",
    )(h, w_up, w_up, conv_w, conv_w, conv_b, conv_b)


def _ffn_down_kernel(a_ref, w_ref, x_ref, out_ref):
    a = a_ref[...]
    tn = 2 * MXU_WIDTH
    for c in range(D_MODEL // tn):
        cols = slice(c * tn, (c + 1) * tn)
        out_ref[:, cols] = x_ref[:, cols] + _dot(a, w_ref[:, cols])


def _ffn_down(act, w_down, x2, layer):
    n = x2.shape[0]
    tm = 512
    row = lambda i: (i, 0)
    return pl.pallas_call(
        _ffn_down_kernel,
        grid=(n // tm,),
        in_specs=[
            pl.BlockSpec((tm, D_FF), row),
            _layer_spec((D_FF, D_MODEL), layer, pipeline_mode=pl.Buffered(1)),
            pl.BlockSpec((tm, D_MODEL), row),
        ],
        out_specs=pl.BlockSpec((tm, D_MODEL), row),
        out_shape=jax.ShapeDtypeStruct((n, D_MODEL), F32),
        compiler_params=pltpu.CompilerParams(
            dimension_semantics=("parallel",), vmem_limit_bytes=VMEM_LIMIT),
        name="ffn_down",
    )(act, w_down, x2)


def _rope_tables(seq):
    def angles(dim):
        inv_freq = ROPE_THETA ** (-np.arange(0, dim, 2, dtype=np.float64) / dim)
        ang = np.arange(seq, dtype=np.float64)[:, None] * inv_freq[None, :]
        return np.cos(ang), np.sin(ang)

    c, s = angles(HEAD_DIM)
    cosc = np.concatenate([c, c], axis=1)
    sinc = np.concatenate([-s, s], axis=1)
    c, s = angles(MLA_ROPE_DIM)
    z = np.zeros_like(c)
    cosd = np.concatenate([c, z, c, z], axis=1)
    sind = np.concatenate([-s, z, s, z], axis=1)
    return tuple(jnp.asarray(t, F32) for t in (cosc, sinc, cosd, sind))


def _spread_rope(t):
    half = MLA_ROPE_DIM // 2
    z = jnp.zeros(t.shape[:-1] + (half,), t.dtype)
    return jnp.concatenate([t[..., :half], z, t[..., half:], z], axis=-1)


IN_OFFSETS = {}
_o = 0
for _name, _width in (("aq", GROUP_WIDTH), ("ak", GROUP_WIDTH), ("av", GROUP_WIDTH), ("af", GROUP_HEADS),
                      ("bq", GROUP_WIDTH), ("bk", GROUP_WIDTH), ("bv", GROUP_WIDTH),
                      ("cq", GROUP_WIDTH), ("ck", GROUP_WIDTH), ("cv", GROUP_WIDTH),
                      ("dcq", MLA_Q_RANK), ("dckv", MLA_KV_RANK), ("dkr", MLA_ROPE_DIM)):
    IN_OFFSETS[_name] = (_o, _width)
    _o += _width
IN_WIDTH = _o
RELAYOUT_ROWS = 256


def _relayout_kernel(w_ref, kr_ref, main_ref, wvt_ref):
    def piece(name):
        o, width = IN_OFFSETS[name]
        return w_ref[:, o:o + width]

    gw = GROUP_WIDTH
    for tile, name in ((T_AQ, "aq"), (T_AK, "ak"), (T_CQ, "cq"), (T_CK, "ck"), (T_DCQ, "dcq"),
                       (T_BQ, "bq"), (T_BK, "bk")):
        main_ref[:, tile * gw:(tile + 1) * gw] = piece(name).astype(BF16)
    base = T_DKV * gw
    main_ref[:, base:base + MLA_KV_RANK] = piece("dckv").astype(BF16)
    main_ref[:, base + MLA_KV_RANK:base + MLA_KV_RANK + HEAD_DIM] = kr_ref[...].astype(BF16)
    main_ref[:, base + MLA_KV_RANK + HEAD_DIM:base + gw] = jnp.zeros(
        (main_ref.shape[0], gw - MLA_KV_RANK - HEAD_DIM), BF16)
    for gi, name in enumerate(("av", "bv", "cv")):
        wvt_ref[gi] = piece(name).T.astype(BF16)


def _layout_w_in(w):
    depth, d_model, _ = w.shape
    rb = RELAYOUT_ROWS
    o, width = IN_OFFSETS["dkr"]
    kr = _spread_rope(w[..., o:o + width])
    o, width = IN_OFFSETS["af"]
    wf = jnp.pad(jnp.swapaxes(w[..., o:o + width], -1, -2), ((0, 0), (0, 8 - GROUP_HEADS), (0, 0))).astype(BF16)
    w_main, wvt = pl.pallas_call(
        _relayout_kernel,
        grid=(depth, d_model // rb),
        in_specs=[
            pl.BlockSpec((None, rb, IN_WIDTH), lambda l, r: (l, r, 0)),
            pl.BlockSpec((None, rb, HEAD_DIM), lambda l, r: (l, r, 0)),
        ],
        out_specs=[
            pl.BlockSpec((None, rb, PROJ_TILES * GROUP_WIDTH), lambda l, r: (l, r, 0)),
            pl.BlockSpec((None, V_GROUPS, GROUP_WIDTH, rb), lambda l, r: (l, 0, 0, r)),
        ],
        out_shape=[
            jax.ShapeDtypeStruct((depth, d_model, PROJ_TILES * GROUP_WIDTH), BF16),
            jax.ShapeDtypeStruct((depth, V_GROUPS, GROUP_WIDTH, d_model), BF16),
        ],
        compiler_params=pltpu.CompilerParams(
            dimension_semantics=("parallel", "parallel"), vmem_limit_bytes=VMEM_LIMIT),
        name="relayout_w_in",
    )(w, kr)
    return w_main, wvt, wf


def _layout_w_uq(w):
    w = w.reshape(-1, MLA_Q_RANK, GROUP_HEADS, MLA_QK_DIM)
    w = jnp.concatenate([w[..., :MLA_NOPE_DIM], _spread_rope(w[..., MLA_NOPE_DIM:])], axis=-1)
    return w.reshape(-1, MLA_Q_RANK, GROUP_HEADS * WIDE_HEAD).astype(BF16)


def _layout_w_ukv(w):
    w = w.reshape(-1, MLA_KV_RANK, GROUP_HEADS, 2, HEAD_DIM)
    wk = w[:, :, :, 0, :].reshape(-1, MLA_KV_RANK, GROUP_WIDTH).astype(BF16)
    wvt = jnp.swapaxes(w[:, :, :, 1, :].reshape(-1, MLA_KV_RANK, GROUP_WIDTH), -1, -2).astype(BF16)
    return wk, wvt


def _logit_bound(gq, gk, dim, scale):
    return dim * scale * LOG2E * jnp.max(jnp.abs(gq), axis=-1) * jnp.max(jnp.abs(gk), axis=-1) * 1.01 + 0.5


def _layout_qk_gain(g):
    return jnp.concatenate([g[:, :MLA_NOPE_DIM], _spread_rope(g[:, MLA_NOPE_DIM:])], axis=-1)[:, None, :]


def kernel(x, attn_norm, w_in, b_forget, fox_q_norm, fox_k_norm, moba_q_norm, moba_k_norm, mla_cq_norm, mla_ckv_norm, w_uq, w_ukv, mla_q_norm, mla_k_norm, mix_out_norm, w_out, ffn_norm, w_up, conv_w, conv_b, w_down):
    batch, seq, d_model = x.shape
    depth = w_in.shape[0]
    assert d_model == D_MODEL and seq % TILE == 0 and seq % FFN_TOKENS == 0
    n = batch * seq
    x2 = x.reshape(n, d_model)
    cosc, sinc, cosd, sind = _rope_tables(seq)

    rows = lambda v: v[:, None, :]
    w_main, wvt, wf = _layout_w_in(w_in)
    wuq = _layout_w_uq(w_uq)
    wuk, wuvt = _layout_w_ukv(w_ukv)
    w_out_bf = w_out.astype(BF16)
    w_down_bf = w_down.astype(BF16)
    bf = jnp.pad(jnp.broadcast_to(b_forget[:, :, None], (depth, GROUP_HEADS, 128)),
                 ((0, 0), (0, 8 - GROUP_HEADS), (0, 0)))
    bound_a = _logit_bound(fox_q_norm, fox_k_norm, HEAD_DIM, SCALE)
    bound_c = _logit_bound(moba_q_norm, moba_k_norm, HEAD_DIM, SCALE)
    bound_d = _logit_bound(mla_q_norm, mla_k_norm, MLA_QK_DIM, SCALE_MLA)
    big_c = jnp.maximum(MOBA_MIN_MASK, jnp.exp2(jnp.ceil(jnp.log2(4.0 * bound_c + 64.0))))
    bounds = jnp.stack([bound_a, bound_c, big_c, bound_d] + [jnp.zeros_like(bound_a)] * 4, axis=1)
    bounds = jnp.broadcast_to(bounds[:, :, None], (depth, 8, HEAD_DIM))
    gdq, gdk = _layout_qk_gain(mla_q_norm), _layout_qk_gain(mla_k_norm)
    g_mix = mix_out_norm.reshape(depth, N_GROUPS, GROUP_WIDTH)
    conv_b3 = conv_b[:, None, :]

    for l in range(depth):
        proj, vt_ac, vt_b, negc = _inproj(x2, rows(attn_norm), w_main, wvt, wf, bf, l, batch, seq)
        negc_col = negc.transpose(0, 2, 1).reshape(n, 8)

        qa, ka, qc, kc, qd, kd, vtd = _prep(
            proj, negc_col, bounds, rows(fox_q_norm), rows(fox_k_norm), rows(moba_q_norm), rows(moba_k_norm),
            rows(mla_cq_norm), rows(mla_ckv_norm), gdq, gdk, wuq, wuk, wuvt,
            cosc, sinc, cosd, sind, l, batch, seq)

        o_a = _bounded_or_online(bound_a[l], "fox_attn", qa, ka, vt_ac, 0, batch, seq)
        o_b = _stickbreak_attn(proj, vt_b, batch, seq)
        o_c = _bounded_or_online(bound_c[l], "moba_attn", qc, kc, vt_ac, 1, batch, seq)
        o_d = _bounded_or_online(bound_d[l], "mla_attn", qd, kd, vtd, None, batch, seq)

        x2, h_ffn = _outproj(o_a, o_b, o_c, o_d, g_mix, w_out_bf, x2, rows(ffn_norm), l)

        act = _ffn_up(h_ffn, w_up, conv_w, conv_b3, l, seq)
        x2 = _ffn_down(act, w_down_bf, x2, l)

    return x2.reshape(batch, seq, d_model)
```

```python
import functools
import math

import numpy as np
import jax
import jax.numpy as jnp
from jax import lax
from jax.experimental import pallas as pl
from jax.experimental.pallas import tpu as pltpu

F32 = jnp.float32
BF16 = jnp.bfloat16

D_MODEL = 2048
HEAD_DIM = 128
GROUP_HEADS = 4
GROUP_WIDTH = GROUP_HEADS * HEAD_DIM
N_GROUPS = 4
MOBA_BLOCK = 256
MOBA_TOP_K = 3
MOBA_MAX_BLOCKS = 16
MLA_Q_RANK = 512
MLA_KV_RANK = 256
MLA_NOPE_DIM = 128
MLA_ROPE_DIM = 64
MLA_QK_DIM = MLA_NOPE_DIM + MLA_ROPE_DIM
WIDE_HEAD = 256
D_FF = 5632
CONV_WIDTH = 3
ROPE_THETA = 10000.0
EPS = 1e-6

T_AQ, T_AK, T_CQ, T_CK, T_DCQ, T_DKV, T_BQ, T_BK = range(8)
PROJ_TILES = 8
V_GROUPS = 3

TILE = 512
STICK_TILE = 256
NEG = -1e30
VMEM_LIMIT = 56 * 1024 * 1024
LOG2E = math.log2(math.e)
SCALE = HEAD_DIM ** -0.5
SCALE_MLA = MLA_QK_DIM ** -0.5
FFN_TOKENS = 1024
MXU_WIDTH = 256
SUM_ROWS = 16
MAX_SAFE_BOUND = 56.0
MOBA_MIN_MASK = 256.0
STICK_DONE = 160.0


def _dot(a, b):
    return jnp.dot(a, b, preferred_element_type=F32)


def _dot_nt(a, b):
    return lax.dot_general(a, b, (((1,), (1,)), ((), ())), preferred_element_type=F32)


def _split3(a):
    hi = a.astype(BF16)
    r1 = a - hi.astype(F32)
    mid = r1.astype(BF16)
    lo = (r1 - mid.astype(F32)).astype(BF16)
    return hi, mid, lo


def _log_sigmoid_pair(z):
    sp = jnp.log(1.0 + jnp.exp(-jnp.abs(z)))
    return jnp.minimum(z, 0.0) - sp, -jnp.maximum(z, 0.0) - sp


def _rms(v, g, denom=None):
    if denom is None:
        ms = jnp.mean(v * v, axis=-1, keepdims=True)
    else:
        ms = jnp.sum(v * v, axis=-1, keepdims=True) * (1.0 / denom)
    return v * lax.rsqrt(ms + EPS) * g


def _sumsq_lanes(v):
    return _dot((v * v).astype(BF16), jnp.ones((v.shape[-1], HEAD_DIM), BF16))


def _rms_lanes(v, g, denom=None):
    width = v.shape[-1]
    r = lax.rsqrt(_sumsq_lanes(v) * (1.0 / (denom or width)) + EPS)
    if width > HEAD_DIM:
        r = jnp.concatenate([r] * (width // HEAD_DIM), axis=1)
    return v * r * g


def _inproj_kernel(x_ref, g_ref, w_ref, wvt_ref, wf_ref, bf_ref, proj_ref, vt_ref, vtb_ref, negc_ref,
                   carry_scr, *, tiles_per_batch, tm):
    i = pl.program_id(0)
    h = _rms(x_ref[...], g_ref[...]).astype(BF16)
    z = _dot_nt(wf_ref[...], h) + bf_ref[:, 0:1]
    logf, _ = _log_sigmoid_pair(z)
    r = lax.broadcasted_iota(jnp.int32, (tm, tm), 0)
    c = lax.broadcasted_iota(jnp.int32, (tm, tm), 1)
    tri = jnp.where(r <= c, 1.0, 0.0).astype(BF16)
    hi, mid, lo = _split3(logf)
    cum = _dot(hi, tri) + _dot(mid, tri) + _dot(lo, tri)

    @pl.when(i % tiles_per_batch == 0)
    def _():
        carry_scr[...] = jnp.zeros_like(carry_scr)

    cum = cum + carry_scr[:, 0:1]
    carry_scr[...] = jnp.broadcast_to(cum[:, tm - 1:tm], carry_scr.shape)
    negc_ref[...] = cum * (-LOG2E)

    gw = GROUP_WIDTH
    for jt in range(PROJ_TILES):
        proj_ref[:, jt * gw:(jt + 1) * gw] = _dot(h, w_ref[:, jt * gw:(jt + 1) * gw]).astype(BF16)
    for gi in range(V_GROUPS):
        vt = _dot_nt(wvt_ref[gi], h).astype(BF16)
        if gi == 1:
            for part in range(tm // STICK_TILE):
                vtb_ref[part] = vt[:, part * STICK_TILE:(part + 1) * STICK_TILE]
        else:
            vt_ref[gi // 2] = vt


def _layer_spec(shape, layer, **kw):
    zeros = (0,) * len(shape)
    return pl.BlockSpec((None,) + tuple(shape), lambda *_: (layer,) + zeros, **kw)


def _inproj(x2, g, w, wvt, wf, bf, layer, batch, seq):
    n = x2.shape[0]
    tm = TILE
    tpb = seq // tm
    kern = functools.partial(_inproj_kernel, tiles_per_batch=tpb, tm=tm)
    resident = pl.Buffered(1)
    return pl.pallas_call(
        kern,
        grid=(n // tm,),
        in_specs=[
            pl.BlockSpec((tm, D_MODEL), lambda i: (i, 0)),
            _layer_spec((1, D_MODEL), layer),
            _layer_spec((D_MODEL, PROJ_TILES * GROUP_WIDTH), layer, pipeline_mode=resident),
            _layer_spec((V_GROUPS, GROUP_WIDTH, D_MODEL), layer, pipeline_mode=resident),
            _layer_spec((8, D_MODEL), layer),
            _layer_spec((8, 128), layer),
        ],
        out_specs=[
            pl.BlockSpec((tm, PROJ_TILES * GROUP_WIDTH), lambda i: (i, 0)),
            pl.BlockSpec((2, None, None, GROUP_WIDTH, tm), lambda i: (0, i // tpb, i % tpb, 0, 0)),
            pl.BlockSpec((None, tm // STICK_TILE, GROUP_WIDTH, STICK_TILE), lambda i: (i // tpb, i % tpb, 0, 0)),
            pl.BlockSpec((None, 8, tm), lambda i: (i // tpb, 0, i % tpb)),
        ],
        out_shape=[
            jax.ShapeDtypeStruct((n, PROJ_TILES * GROUP_WIDTH), BF16),
            jax.ShapeDtypeStruct((2, batch, tpb, GROUP_WIDTH, tm), BF16),
            jax.ShapeDtypeStruct((batch, seq // STICK_TILE, GROUP_WIDTH, STICK_TILE), BF16),
            jax.ShapeDtypeStruct((batch, 8, seq), F32),
        ],
        scratch_shapes=[pltpu.VMEM((8, 128), F32)],
        compiler_params=pltpu.CompilerParams(
            dimension_semantics=("arbitrary",), vmem_limit_bytes=VMEM_LIMIT),
        name="inproj",
    )(x2, g, w, wvt, wf, bf)


def _prep_kernel(a_ref, c_ref, d_ref, nc_ref, bnd_ref, gaq, gak, gcq, gck, gdcq, gdckv, gdq, gdk, wuq, wuk, wuvt,
                 cosc, sinc, cosd, sind,
                 qa_ref, ka_ref, qc_ref, kc_ref, qd_ref, kd_ref, vtd_ref, kmt_scr,
                 *, tiles_per_batch, tm):
    i = pl.program_id(0)
    t = i % tiles_per_batch
    hd = HEAD_DIM
    wh = WIDE_HEAD
    blocks_per_tile = tm // MOBA_BLOCK
    lane = lax.broadcasted_iota(jnp.int32, (tm, hd), 1)
    bound_a = bnd_ref[0:1, :]
    bound_c = bnd_ref[1:2, :]
    big_c = bnd_ref[2:3, :]
    bound_d = bnd_ref[3:4, :]

    his, mids, los = (p.astype(F32) for p in _split3(nc_ref[...]))
    lane1 = lane[0:1]
    k_const = jnp.where(lane1 < 6, 1.0, jnp.where(lane1 == 6, -bound_a, 0.0))
    q_const = jnp.where(lane1 < 3, 1.0, jnp.where(lane1 == 6, 1.0, 0.0))
    gaq_scaled = gaq[...] * (SCALE * LOG2E)
    for h in range(GROUP_HEADS):
        q = a_ref[:, h * hd:(h + 1) * hd].astype(F32)
        qa_ref[:, h * wh:h * wh + hd] = _rms_lanes(q, gaq_scaled).astype(BF16)
        k = a_ref[:, GROUP_WIDTH + h * hd:GROUP_WIDTH + (h + 1) * hd].astype(F32)
        ka_ref[:, h * wh:h * wh + hd] = _rms_lanes(k, gak[...]).astype(BF16)
        hi, mid, lo = his[:, h:h + 1], mids[:, h:h + 1], los[:, h:h + 1]
        k_parts = jnp.where(lane == 0, hi, jnp.where(lane == 1, mid, lo))
        q_parts = jnp.where(lane == 3, -hi, jnp.where(lane == 4, -mid, -lo))
        k_aug = jnp.where(lane < 3, k_parts, k_const)
        q_aug = jnp.where(lane < 3, q_const, jnp.where(lane < 6, q_parts, q_const))
        ka_ref[:, h * wh + hd:(h + 1) * wh] = k_aug.astype(BF16)
        qa_ref[:, h * wh + hd:(h + 1) * wh] = q_aug.astype(BF16)

    @pl.when(i == 0)
    def _():
        kmt_scr[...] = jnp.zeros_like(kmt_scr)

    cc = cosc[...]
    sc = sinc[...]
    sub8 = lax.broadcasted_iota(jnp.int32, (8, hd), 0)
    key_blk = t * blocks_per_tile + lax.broadcasted_iota(jnp.int32, (tm, hd), 0) // MOBA_BLOCK

    def rope_c(v):
        return v * cc + pltpu.roll(v, hd // 2, axis=1) * sc

    kc_aug = jnp.where(lane == key_blk, big_c, jnp.where(lane == MOBA_MAX_BLOCKS, -(bound_c + big_c), 0.0))
    kc_aug = kc_aug.astype(BF16)
    for h in range(GROUP_HEADS):
        k = rope_c(_rms_lanes(c_ref[:, GROUP_WIDTH + h * hd:GROUP_WIDTH + (h + 1) * hd].astype(F32), gck[...]))
        kc_ref[:, h * wh:h * wh + hd] = k.astype(BF16)
        kc_ref[:, h * wh + hd:(h + 1) * wh] = kc_aug
        for blk in range(blocks_per_tile):
            km = jnp.mean(k[blk * MOBA_BLOCK:(blk + 1) * MOBA_BLOCK], axis=0, keepdims=True)
            nblk = t * blocks_per_tile + blk
            base = pl.multiple_of(h * hd + (nblk // 8) * 8, 8)
            old = kmt_scr[pl.ds(base, 8), h * hd:(h + 1) * hd]
            kmt_scr[pl.ds(base, 8), h * hd:(h + 1) * hd] = jnp.where(sub8 == nblk % 8, km, old)

    qs = []
    gcq_scaled = gcq[...] * (SCALE * LOG2E)
    for h in range(GROUP_HEADS):
        q = rope_c(_rms_lanes(c_ref[:, h * hd:(h + 1) * hd].astype(F32), gcq_scaled))
        qc_ref[:, h * wh:h * wh + hd] = q.astype(BF16)
        qs.append(q)
    q_all = jnp.concatenate(qs, axis=1)
    qh, qm, _ = _split3(q_all)
    kh, km_, _ = _split3(kmt_scr[...])
    gate_t = _dot_nt(kh, qh) + _dot_nt(km_, qh) + _dot_nt(kh, qm)

    nb = MOBA_MAX_BLOCKS
    blk_row = lax.broadcasted_iota(jnp.int32, (nb, tm), 0)
    own = t * blocks_per_tile + lax.broadcasted_iota(jnp.int32, (nb, tm), 1) // MOBA_BLOCK
    blk_f = blk_row.astype(F32)
    ninf = jnp.float32(-jnp.inf)
    no_rows = jnp.zeros((hd - nb, tm), F32)
    for h in range(GROUP_HEADS):
        g = jnp.where(blk_row < own, gate_t[h * hd:h * hd + nb, :], ninf)
        sel = jnp.where(blk_row == own, 1.0, 0.0)
        for _ in range(MOBA_TOP_K):
            mx = jnp.max(g, axis=0, keepdims=True)
            cand = jnp.where(g == mx, blk_f, 1e9)
            cand = jnp.where(mx > ninf, cand, 1e9)
            idx = jnp.min(cand, axis=0, keepdims=True)
            pick = blk_f == idx
            sel = jnp.where(pick, 1.0, sel)
            g = jnp.where(pick, ninf, g)
        sel_rows = jnp.concatenate([sel, no_rows], axis=0).T
        q_aug = jnp.where(lane == nb, 1.0, sel_rows)
        qc_ref[:, h * wh + hd:(h + 1) * wh] = q_aug.astype(BF16)

    cd = cosd[...]
    sd = sind[...]
    free_lane = lane == MLA_ROPE_DIM // 2

    def rope_d(v):
        return v * cd + pltpu.roll(v, hd // 2, axis=1) * sd

    cq = _rms_lanes(d_ref[:, 0:MLA_Q_RANK].astype(F32), gdcq[...]).astype(BF16)
    qf = _dot(cq, wuq[...])
    gdq_scaled = gdq[...] * (SCALE_MLA * LOG2E)
    for h in range(GROUP_HEADS):
        qn = _rms_lanes(qf[:, h * wh:(h + 1) * wh], gdq_scaled, denom=MLA_QK_DIM)
        qd_ref[:, h * wh:h * wh + hd] = qn[:, :hd].astype(BF16)
        qd_ref[:, h * wh + hd:(h + 1) * wh] = jnp.where(free_lane, 1.0, rope_d(qn[:, hd:])).astype(BF16)

    ckv = _rms_lanes(d_ref[:, MLA_Q_RANK:MLA_Q_RANK + MLA_KV_RANK].astype(F32), gdckv[...]).astype(BF16)
    knf = _dot(ckv, wuk[...])
    vtd_ref[...] = _dot_nt(wuvt[...], ckv).astype(BF16)
    kr = d_ref[:, MLA_Q_RANK + MLA_KV_RANK:MLA_Q_RANK + MLA_KV_RANK + hd].astype(F32)
    ss_r = _sumsq_lanes(kr)
    gk = gdk[...]
    for h in range(GROUP_HEADS):
        kn = knf[:, h * hd:(h + 1) * hd]
        r = lax.rsqrt((_sumsq_lanes(kn) + ss_r) * (1.0 / MLA_QK_DIM) + EPS)
        kd_ref[:, h * wh:h * wh + hd] = (kn * r * gk[:, :hd]).astype(BF16)
        k_rot = rope_d(kr * r * gk[:, hd:])
        kd_ref[:, h * wh + hd:(h + 1) * wh] = jnp.where(free_lane, -bound_d, k_rot).astype(BF16)


def _prep(proj, negc_col, bounds, gaq, gak, gcq, gck, gdcq, gdckv, gdq, gdk, wuq, wuk, wuvt,
          cosc, sinc, cosd, sind, layer, batch, seq):
    n = proj.shape[0]
    tm = TILE
    tpb = seq // tm
    assert seq // MOBA_BLOCK <= MOBA_MAX_BLOCKS
    kern = functools.partial(_prep_kernel, tiles_per_batch=tpb, tm=tm)
    row = lambda i: (i, 0)
    tab = lambda i: (i % tpb, 0)
    w2 = 2 * GROUP_WIDTH
    return pl.pallas_call(
        kern,
        grid=(n // tm,),
        in_specs=[
            pl.BlockSpec((tm, w2), lambda i: (i, 0)),
            pl.BlockSpec((tm, w2), lambda i: (i, 1)),
            pl.BlockSpec((tm, w2), lambda i: (i, 2)),
            pl.BlockSpec((tm, 8), row),
            _layer_spec((8, HEAD_DIM), layer),
            _layer_spec((1, HEAD_DIM), layer), _layer_spec((1, HEAD_DIM), layer),
            _layer_spec((1, HEAD_DIM), layer), _layer_spec((1, HEAD_DIM), layer),
            _layer_spec((1, MLA_Q_RANK), layer), _layer_spec((1, MLA_KV_RANK), layer),
            _layer_spec((1, WIDE_HEAD), layer), _layer_spec((1, WIDE_HEAD), layer),
            _layer_spec((MLA_Q_RANK, GROUP_HEADS * WIDE_HEAD), layer),
            _layer_spec((MLA_KV_RANK, GROUP_WIDTH), layer),
            _layer_spec((GROUP_WIDTH, MLA_KV_RANK), layer),
            pl.BlockSpec((tm, HEAD_DIM), tab), pl.BlockSpec((tm, HEAD_DIM), tab),
            pl.BlockSpec((tm, HEAD_DIM), tab), pl.BlockSpec((tm, HEAD_DIM), tab),
        ],
        out_specs=[pl.BlockSpec((tm, w2), row)] * 6 + [
            pl.BlockSpec((None, None, GROUP_WIDTH, tm), lambda i: (i // tpb, i % tpb, 0, 0)),
        ],
        out_shape=[jax.ShapeDtypeStruct((n, w2), BF16)] * 6 + [
            jax.ShapeDtypeStruct((batch, tpb, GROUP_WIDTH, tm), BF16),
        ],
        scratch_shapes=[pltpu.VMEM((GROUP_WIDTH, GROUP_WIDTH), F32)],
        compiler_params=pltpu.CompilerParams(
            dimension_semantics=("arbitrary",), vmem_limit_bytes=VMEM_LIMIT),
        name="prep",
    )(proj, proj, proj, negc_col, bounds, gaq, gak, gcq, gck, gdcq, gdckv, gdq, gdk, wuq, wuk, wuvt,
      cosc, sinc, cosd, sind)


def _key_query_iota(t):
    key = lax.broadcasted_iota(jnp.int32, (t, t), 0)
    qry = lax.broadcasted_iota(jnp.int32, (t, t), 1)
    return key, qry


def _softmax_step(s, vt, carry):
    m, acc = carry
    m_new = jnp.maximum(m, jnp.max(s, axis=0, keepdims=True))
    alpha = jnp.exp2(m - m_new)
    p = jnp.exp2(s - m_new)
    return m_new, alpha * acc + _dot(vt, p.astype(BF16))


def _softmax_init(t):
    return (jnp.full((1, t), NEG, F32), jnp.zeros((HEAD_DIM + SUM_ROWS, t), F32))


def _softmax_attn_kernel(q_ref, k_ref, vt_ref, o_ref, acc_scr, *, t, bounded):
    qi = pl.program_id(1)
    key, qry = _key_query_iota(t)
    causal = key <= qry
    dk = WIDE_HEAD
    dv = HEAD_DIM
    ones_rows = jnp.ones((SUM_ROWS, t), BF16)

    def logits(j, h, diag):
        off = pl.multiple_of(j * t, t)
        s = _dot_nt(k_ref[pl.ds(off, t), h * dk:(h + 1) * dk], q_ref[:, h * dk:(h + 1) * dk])
        return jnp.where(causal, s, NEG) if diag else s

    def values(j, h):
        return jnp.concatenate([vt_ref[j, h * dv:(h + 1) * dv, :], ones_rows], axis=0)

    if bounded:
        acc_scr[...] = jnp.zeros_like(acc_scr)

        def step(*tiles):
            s = [[logits(j, h, diag) for h in range(GROUP_HEADS)] for j, diag in tiles]
            for n, (j, _) in enumerate(tiles):
                for h in range(GROUP_HEADS):
                    acc_scr[h] += _dot(values(j, h), jnp.exp2(s[n][h]).astype(BF16))

        left = jnp.maximum(qi - 1, 0)

        def pair(jj, c):
            step((2 * jj, False), (2 * jj + 1, False))
            return c

        lax.fori_loop(0, left // 2, pair, 0)

        @pl.when(left % 2 == 1)
        def _():
            step((left - 1, False))

        @pl.when(qi == 0)
        def _():
            step((qi, True))

        @pl.when(qi > 0)
        def _():
            step((qi, True), (qi - 1, False))

        accs = [acc_scr[h] for h in range(GROUP_HEADS)]
    else:
        def step(j, carry, diag):
            out = []
            for h in range(GROUP_HEADS):
                out.append(_softmax_step(logits(j, h, diag), values(j, h), carry[h]))
            return tuple(out)

        carry = step(qi, tuple(_softmax_init(t) for _ in range(GROUP_HEADS)), True)
        carry = lax.fori_loop(0, qi, lambda j, c: step(j, c, False), carry)
        accs = [c[1] for c in carry]

    for h in range(GROUP_HEADS):
        a = accs[h]
        o_ref[:, h * dv:(h + 1) * dv] = (a[:dv] / a[dv:dv + 1]).T.astype(BF16)


def _softmax_attn(name, q_arr, k_arr, vt_arr, vt_group, batch, seq, bounded):
    t = TILE
    nq = seq // t
    n = batch * seq
    kern = functools.partial(_softmax_attn_kernel, t=t, bounded=bounded)
    if vt_group is None:
        vt_spec = pl.BlockSpec((None, nq, GROUP_WIDTH, t), lambda b, i: (b, 0, 0, 0))
    else:
        vt_spec = pl.BlockSpec((None, None, nq, GROUP_WIDTH, t), lambda b, i: (vt_group, b, 0, 0, 0))
    return pl.pallas_call(
        kern,
        grid=(batch, nq),
        in_specs=[
            pl.BlockSpec((t, GROUP_HEADS * WIDE_HEAD), lambda b, i: (b * nq + i, 0)),
            pl.BlockSpec((seq, GROUP_HEADS * WIDE_HEAD), lambda b, i: (b, 0)),
            vt_spec,
        ],
        out_specs=pl.BlockSpec((t, GROUP_WIDTH), lambda b, i: (b * nq + i, 0)),
        out_shape=jax.ShapeDtypeStruct((n, GROUP_WIDTH), BF16),
        scratch_shapes=[pltpu.VMEM((GROUP_HEADS, HEAD_DIM + SUM_ROWS, t), F32)],
        compiler_params=pltpu.CompilerParams(
            dimension_semantics=("parallel", "parallel"), vmem_limit_bytes=VMEM_LIMIT),
        name=name + ("" if bounded else "_online"),
    )(q_arr, k_arr, vt_arr)


def _bounded_or_online(bound, name, *args):
    return lax.cond(bound < MAX_SAFE_BOUND,
                    lambda: _softmax_attn(name, *args, bounded=True),
                    lambda: _softmax_attn(name, *args, bounded=False))


def _stickbreak_kernel(q_ref, k_ref, vt_ref, o_ref, acc_scr, rest_scr, *, t):
    qi = pl.program_id(1)
    key, qry = _key_query_iota(t)
    strict = key < qry
    later = jnp.where(qry > key, 1.0, 0.0).astype(BF16)
    hd = HEAD_DIM
    acc_scr[...] = jnp.zeros_like(acc_scr)
    rest_scr[...] = jnp.zeros_like(rest_scr)

    def step(tiles):
        qs = [(q_ref[:, h * hd:(h + 1) * hd].astype(F32) * (SCALE * LOG2E)).astype(BF16) for h in range(GROUP_HEADS)]
        zs, drops, afters = [], [], []
        for j, diag in tiles:
            off = pl.multiple_of(j * t, t)
            zs.append([_dot_nt(k_ref[pl.ds(off, t), h * hd:(h + 1) * hd], qs[h]) for h in range(GROUP_HEADS)])
        for n, (j, diag) in enumerate(tiles):
            row = []
            for h in range(GROUP_HEADS):
                z = zs[n][h]
                drop = jnp.maximum(z, 0.0) + jnp.log(1.0 + jnp.exp2(-jnp.abs(z))) * LOG2E
                row.append(jnp.where(strict, drop, 0.0) if diag else drop)
            drops.append(row)
            afters.append([_dot(later, d.astype(BF16)) for d in row])
        low = None
        for h in range(GROUP_HEADS):
            used = rest_scr[h]
            for n, (j, diag) in enumerate(tiles):
                z, drop, after = zs[n][h], drops[n][h], afters[n][h]
                w = jnp.exp2((z - drop) - (after + used))
                if diag:
                    w = jnp.where(strict, w, 0.0)
                acc_scr[h] += _dot(vt_ref[j, h * hd:(h + 1) * hd, :], w.astype(BF16))
                used = used + after[0:1] + drop[0:1]
            rest_scr[h] = used
            low = used if low is None else jnp.minimum(low, used)
        return jnp.min(low)

    def cond(c):
        j, low = c
        return jnp.logical_and(j >= 0, low < STICK_DONE)

    def body(c):
        j, _ = c
        return j - 1, step([(j, False)])

    first = lax.cond(qi == 0, lambda: step([(qi, True)]), lambda: step([(qi, True), (qi - 1, False)]))
    lax.while_loop(cond, body, (qi - 2, first))
    for h in range(GROUP_HEADS):
        o_ref[:, h * hd:(h + 1) * hd] = acc_scr[h].T.astype(BF16)


def _stickbreak_attn(proj, vtb, batch, seq):
    t = STICK_TILE
    nq = seq // t
    n = batch * seq
    kern = functools.partial(_stickbreak_kernel, t=t)
    return pl.pallas_call(
        kern,
        grid=(batch, nq),
        in_specs=[
            pl.BlockSpec((t, GROUP_WIDTH), lambda b, i: (b * nq + i, T_BQ)),
            pl.BlockSpec((seq, GROUP_WIDTH), lambda b, i: (b, T_BK)),
            pl.BlockSpec((None, nq, GROUP_WIDTH, t), lambda b, i: (b, 0, 0, 0)),
        ],
        out_specs=pl.BlockSpec((t, GROUP_WIDTH), lambda b, i: (b * nq + i, 0)),
        out_shape=jax.ShapeDtypeStruct((n, GROUP_WIDTH), BF16),
        scratch_shapes=[pltpu.VMEM((GROUP_HEADS, HEAD_DIM, t), F32), pltpu.VMEM((GROUP_HEADS, 1, t), F32)],
        compiler_params=pltpu.CompilerParams(
            dimension_semantics=("parallel", "parallel"), vmem_limit_bytes=VMEM_LIMIT),
        name="stickbreak_attn",
    )(proj, proj, vtb)


def _outproj_kernel(oa_ref, ob_ref, oc_ref, od_ref, g_ref, w_ref, x_ref, gf_ref, out_ref, hf_ref):
    ys = [_rms(o_ref[...].astype(F32), g_ref[gi:gi + 1, :]).astype(BF16)
          for gi, o_ref in enumerate((oa_ref, ob_ref, oc_ref, od_ref))]
    acc = x_ref[...] + _dot(jnp.concatenate(ys, axis=1), w_ref[...])
    out_ref[...] = acc
    hf_ref[...] = _rms(acc, gf_ref[...]).astype(BF16)


def _outproj(oa, ob, oc, od, g, w, x2, g_ffn, layer):
    n = x2.shape[0]
    tm = 512
    row = lambda i: (i, 0)
    return pl.pallas_call(
        _outproj_kernel,
        grid=(n // tm,),
        in_specs=[pl.BlockSpec((tm, GROUP_WIDTH), row)] * 4 + [
            _layer_spec((N_GROUPS, GROUP_WIDTH), layer),
            _layer_spec((N_GROUPS * GROUP_WIDTH, D_MODEL), layer, pipeline_mode=pl.Buffered(1)),
            pl.BlockSpec((tm, D_MODEL), row),
            _layer_spec((1, D_MODEL), layer),
        ],
        out_specs=[pl.BlockSpec((tm, D_MODEL), row), pl.BlockSpec((tm, D_MODEL), row)],
        out_shape=[jax.ShapeDtypeStruct((n, D_MODEL), F32), jax.ShapeDtypeStruct((n, D_MODEL), BF16)],
        compiler_params=pltpu.CompilerParams(
            dimension_semantics=("parallel",), vmem_limit_bytes=VMEM_LIMIT),
        name="outproj",
    )(oa, ob, oc, od, g, w, x2, g_ffn)


def _ffn_up_kernel(h_ref, wg_ref, wv_ref, cwg_ref, cwv_ref, cbg_ref, cbv_ref,
                   out_ref, wg_scr, wv_scr, tail_g, tail_v, *, tiles_per_batch, tm):
    i = pl.program_id(1)

    @pl.when(i == 0)
    def _():
        wg_scr[...] = wg_ref[...].astype(BF16)
        wv_scr[...] = wv_ref[...].astype(BF16)

    @pl.when(i % tiles_per_batch == 0)
    def _():
        tail_g[...] = jnp.zeros_like(tail_g)
        tail_v[...] = jnp.zeros_like(tail_v)

    def conv(u, cw_ref, cb_ref, prev):
        ue = jnp.concatenate([prev, u], axis=0)
        y = cb_ref[...] + cw_ref[2:3, :] * u
        for back in range(1, CONV_WIDTH):
            y = y + cw_ref[CONV_WIDTH - 1 - back:CONV_WIDTH - back, :] * pltpu.roll(ue, back, axis=0)[8:]
        return y, u[u.shape[0] - 8:]

    h = h_ref[...]
    u_g = _dot(h, wg_scr[...])
    u_v = _dot(h, wv_scr[...])
    gate, tail_g[...] = conv(u_g, cwg_ref, cbg_ref, tail_g[...])
    val, tail_v[...] = conv(u_v, cwv_ref, cbv_ref, tail_v[...])
    out_ref[...] = (gate / (1.0 + jnp.exp(-gate)) * val).astype(BF16)


def _ffn_up(h, w_up, conv_w, conv_b, layer, seq):
    n = h.shape[0]
    tm = FFN_TOKENS
    tn = 512
    nj = D_FF // tn
    tpb = seq // tm
    kern = functools.partial(_ffn_up_kernel, tiles_per_batch=tpb, tm=tm)
    return pl.pallas_call(
        kern,
        grid=(nj, n // tm),
        in_specs=[
            pl.BlockSpec((tm, D_MODEL), lambda j, i: (i, 0)),
            pl.BlockSpec((None, D_MODEL, tn), lambda j, i: (layer, 0, j)),
            pl.BlockSpec((None, D_MODEL, tn), lambda j, i: (layer, 0, j + nj)),
            pl.BlockSpec((None, CONV_WIDTH, tn), lambda j, i: (layer, 0, j)),
            pl.BlockSpec((None, CONV_WIDTH, tn), lambda j, i: (layer, 0, j + nj)),
            pl.BlockSpec((None, 1, tn), lambda j, i: (layer, 0, j)),
            pl.BlockSpec((None, 1, tn), lambda j, i: (layer, 0, j + nj)),
        ],
        out_specs=pl.BlockSpec((tm, tn), lambda j, i: (i, j)),
        out_shape=jax.ShapeDtypeStruct((n, D_FF), BF16),
        scratch_shapes=[pltpu.VMEM((D_MODEL, tn), BF16), pltpu.VMEM((D_MODEL, tn), BF16),
                        pltpu.VMEM((8, tn), F32), pltpu.VMEM((8, tn), F32)],
        compiler_params=pltpu.CompilerParams(
            dimension_semantics=("arbitrary", "arbitrary"), vmem_limit_bytes=VMEM_LIMIT),
        name="ffn_up",
    )(h, w_up, w_up, conv_w, conv_w, conv_b, conv_b)


def _ffn_down_kernel(a_ref, w_ref, x_ref, out_ref):
    a = a_ref[...]
    tn = 2 * MXU_WIDTH
    for c in range(D_MODEL // tn):
        cols = slice(c * tn, (c + 1) * tn)
        out_ref[:, cols] = x_ref[:, cols] + _dot(a, w_ref[:, cols])


def _ffn_down(act, w_down, x2, layer):
    n = x2.shape[0]
    tm = 512
    row = lambda i: (i, 0)
    return pl.pallas_call(
        _ffn_down_kernel,
        grid=(n // tm,),
        in_specs=[
            pl.BlockSpec((tm, D_FF), row),
            _layer_spec((D_FF, D_MODEL), layer, pipeline_mode=pl.Buffered(1)),
            pl.BlockSpec((tm, D_MODEL), row),
        ],
        out_specs=pl.BlockSpec((tm, D_MODEL), row),
        out_shape=jax.ShapeDtypeStruct((n, D_MODEL), F32),
        compiler_params=pltpu.CompilerParams(
            dimension_semantics=("parallel",), vmem_limit_bytes=VMEM_LIMIT),
        name="ffn_down",
    )(act, w_down, x2)


def _rope_tables(seq):
    def angles(dim):
        inv_freq = ROPE_THETA ** (-np.arange(0, dim, 2, dtype=np.float64) / dim)
        ang = np.arange(seq, dtype=np.float64)[:, None] * inv_freq[None, :]
        return np.cos(ang), np.sin(ang)

    c, s = angles(HEAD_DIM)
    cosc = np.concatenate([c, c], axis=1)
    sinc = np.concatenate([-s, s], axis=1)
    c, s = angles(MLA_ROPE_DIM)
    z = np.zeros_like(c)
    cosd = np.concatenate([c, z, c, z], axis=1)
    sind = np.concatenate([-s, z, s, z], axis=1)
    return tuple(jnp.asarray(t, F32) for t in (cosc, sinc, cosd, sind))


def _spread_rope(t):
    half = MLA_ROPE_DIM // 2
    z = jnp.zeros(t.shape[:-1] + (half,), t.dtype)
    return jnp.concatenate([t[..., :half], z, t[..., half:], z], axis=-1)


IN_OFFSETS = {}
_o = 0
for _name, _width in (("aq", GROUP_WIDTH), ("ak", GROUP_WIDTH), ("av", GROUP_WIDTH), ("af", GROUP_HEADS),
                      ("bq", GROUP_WIDTH), ("bk", GROUP_WIDTH), ("bv", GROUP_WIDTH),
                      ("cq", GROUP_WIDTH), ("ck", GROUP_WIDTH), ("cv", GROUP_WIDTH),
                      ("dcq", MLA_Q_RANK), ("dckv", MLA_KV_RANK), ("dkr", MLA_ROPE_DIM)):
    IN_OFFSETS[_name] = (_o, _width)
    _o += _width
IN_WIDTH = _o
RELAYOUT_ROWS = 256


def _relayout_kernel(w_ref, kr_ref, main_ref, wvt_ref):
    def piece(name):
        o, width = IN_OFFSETS[name]
        return w_ref[:, o:o + width]

    gw = GROUP_WIDTH
    for tile, name in ((T_AQ, "aq"), (T_AK, "ak"), (T_CQ, "cq"), (T_CK, "ck"), (T_DCQ, "dcq"),
                       (T_BQ, "bq"), (T_BK, "bk")):
        main_ref[:, tile * gw:(tile + 1) * gw] = piece(name).astype(BF16)
    base = T_DKV * gw
    main_ref[:, base:base + MLA_KV_RANK] = piece("dckv").astype(BF16)
    main_ref[:, base + MLA_KV_RANK:base + MLA_KV_RANK + HEAD_DIM] = kr_ref[...].astype(BF16)
    main_ref[:, base + MLA_KV_RANK + HEAD_DIM:base + gw] = jnp.zeros(
        (main_ref.shape[0], gw - MLA_KV_RANK - HEAD_DIM), BF16)
    for gi, name in enumerate(("av", "bv", "cv")):
        wvt_ref[gi] = piece(name).T.astype(BF16)


def _layout_w_in(w):
    depth, d_model, _ = w.shape
    rb = RELAYOUT_ROWS
    o, width = IN_OFFSETS["dkr"]
    kr = _spread_rope(w[..., o:o + width])
    o, width = IN_OFFSETS["af"]
    wf = jnp.pad(jnp.swapaxes(w[..., o:o + width], -1, -2), ((0, 0), (0, 8 - GROUP_HEADS), (0, 0))).astype(BF16)
    w_main, wvt = pl.pallas_call(
        _relayout_kernel,
        grid=(depth, d_model // rb),
        in_specs=[
            pl.BlockSpec((None, rb, IN_WIDTH), lambda l, r: (l, r, 0)),
            pl.BlockSpec((None, rb, HEAD_DIM), lambda l, r: (l, r, 0)),
        ],
        out_specs=[
            pl.BlockSpec((None, rb, PROJ_TILES * GROUP_WIDTH), lambda l, r: (l, r, 0)),
            pl.BlockSpec((None, V_GROUPS, GROUP_WIDTH, rb), lambda l, r: (l, 0, 0, r)),
        ],
        out_shape=[
            jax.ShapeDtypeStruct((depth, d_model, PROJ_TILES * GROUP_WIDTH), BF16),
            jax.ShapeDtypeStruct((depth, V_GROUPS, GROUP_WIDTH, d_model), BF16),
        ],
        compiler_params=pltpu.CompilerParams(
            dimension_semantics=("parallel", "parallel"), vmem_limit_bytes=VMEM_LIMIT),
        name="relayout_w_in",
    )(w, kr)
    return w_main, wvt, wf


def _layout_w_uq(w):
    w = w.reshape(-1, MLA_Q_RANK, GROUP_HEADS, MLA_QK_DIM)
    w = jnp.concatenate([w[..., :MLA_NOPE_DIM], _spread_rope(w[..., MLA_NOPE_DIM:])], axis=-1)
    return w.reshape(-1, MLA_Q_RANK, GROUP_HEADS * WIDE_HEAD).astype(BF16)


def _layout_w_ukv(w):
    w = w.reshape(-1, MLA_KV_RANK, GROUP_HEADS, 2, HEAD_DIM)
    wk = w[:, :, :, 0, :].reshape(-1, MLA_KV_RANK, GROUP_WIDTH).astype(BF16)
    wvt = jnp.swapaxes(w[:, :, :, 1, :].reshape(-1, MLA_KV_RANK, GROUP_WIDTH), -1, -2).astype(BF16)
    return wk, wvt


def _logit_bound(gq, gk, dim, scale):
    return dim * scale * LOG2E * jnp.max(jnp.abs(gq), axis=-1) * jnp.max(jnp.abs(gk), axis=-1) * 1.01 + 0.5


def _layout_qk_gain(g):
    return jnp.concatenate([g[:, :MLA_NOPE_DIM], _spread_rope(g[:, MLA_NOPE_DIM:])], axis=-1)[:, None, :]


def kernel(x, attn_norm, w_in, b_forget, fox_q_norm, fox_k_norm, moba_q_norm, moba_k_norm, mla_cq_norm, mla_ckv_norm, w_uq, w_ukv, mla_q_norm, mla_k_norm, mix_out_norm, w_out, ffn_norm, w_up, conv_w, conv_b, w_down):
    batch, seq, d_model = x.shape
    depth = w_in.shape[0]
    assert d_model == D_MODEL and seq % TILE == 0 and seq % FFN_TOKENS == 0
    n = batch * seq
    x2 = x.reshape(n, d_model)
    cosc, sinc, cosd, sind = _rope_tables(seq)

    rows = lambda v: v[:, None, :]
    w_main, wvt, wf = _layout_w_in(w_in)
    wuq = _layout_w_uq(w_uq)
    wuk, wuvt = _layout_w_ukv(w_ukv)
    w_out_bf = w_out.astype(BF16)
    w_down_bf = w_down.astype(BF16)
    bf = jnp.pad(jnp.broadcast_to(b_forget[:, :, None], (depth, GROUP_HEADS, 128)),
                 ((0, 0), (0, 8 - GROUP_HEADS), (0, 0)))
    bound_a = _logit_bound(fox_q_norm, fox_k_norm, HEAD_DIM, SCALE)
    bound_c = _logit_bound(moba_q_norm, moba_k_norm, HEAD_DIM, SCALE)
    bound_d = _logit_bound(mla_q_norm, mla_k_norm, MLA_QK_DIM, SCALE_MLA)
    big_c = jnp.maximum(MOBA_MIN_MASK, jnp.exp2(jnp.ceil(jnp.log2(4.0 * bound_c + 64.0))))
    bounds = jnp.stack([bound_a, bound_c, big_c, bound_d] + [jnp.zeros_like(bound_a)] * 4, axis=1)
    bounds = jnp.broadcast_to(bounds[:, :, None], (depth, 8, HEAD_DIM))
    gdq, gdk = _layout_qk_gain(mla_q_norm), _layout_qk_gain(mla_k_norm)
    g_mix = mix_out_norm.reshape(depth, N_GROUPS, GROUP_WIDTH)
    conv_b3 = conv_b[:, None, :]

    for l in range(depth):
        proj, vt_ac, vt_b, negc = _inproj(x2, rows(attn_norm), w_main, wvt, wf, bf, l, batch, seq)
        negc_col = negc.transpose(0, 2, 1).reshape(n, 8)

        qa, ka, qc, kc, qd, kd, vtd = _prep(
            proj, negc_col, bounds, rows(fox_q_norm), rows(fox_k_norm), rows(moba_q_norm), rows(moba_k_norm),
            rows(mla_cq_norm), rows(mla_ckv_norm), gdq, gdk, wuq, wuk, wuvt,
            cosc, sinc, cosd, sind, l, batch, seq)

        o_a = _bounded_or_online(bound_a[l], "fox_attn", qa, ka, vt_ac, 0, batch, seq)
        o_b = _stickbreak_attn(proj, vt_b, batch, seq)
        o_c = _bounded_or_online(bound_c[l], "moba_attn", qc, kc, vt_ac, 1, batch, seq)
        o_d = _bounded_or_online(bound_d[l], "mla_attn", qd, kd, vtd, None, batch, seq)

        x2, h_ffn = _outproj(o_a, o_b, o_c, o_d, g_mix, w_out_bf, x2, rows(ffn_norm), l)

        act = _ffn_up(h_ffn, w_up, conv_w, conv_b3, l, seq)
        x2 = _ffn_down(act, w_down_bf, x2, l)

    return x2.reshape(batch, seq, d_model)
```

```python
import functools
import math

import numpy as np
import jax
import jax.numpy as jnp
from jax import lax
from jax.experimental import pallas as pl
from jax.experimental.pallas import tpu as pltpu

F32 = jnp.float32
BF16 = jnp.bfloat16

D_MODEL = 2048
HEAD_DIM = 128
GROUP_HEADS = 4
GROUP_WIDTH = GROUP_HEADS * HEAD_DIM
N_GROUPS = 4
MOBA_BLOCK = 256
MOBA_TOP_K = 3
MOBA_MAX_BLOCKS = 16
MLA_Q_RANK = 512
MLA_KV_RANK = 256
MLA_NOPE_DIM = 128
MLA_ROPE_DIM = 64
MLA_QK_DIM = MLA_NOPE_DIM + MLA_ROPE_DIM
WIDE_HEAD = 256
D_FF = 5632
CONV_WIDTH = 3
ROPE_THETA = 10000.0
EPS = 1e-6

T_AQ, T_AK, T_CQ, T_CK, T_DCQ, T_DKV, T_BQ, T_BK = range(8)
PROJ_TILES = 8
V_GROUPS = 3

TILE = 512
STICK_TILE = 256
NEG = -1e30
VMEM_LIMIT = 56 * 1024 * 1024
LOG2E = math.log2(math.e)
SCALE = HEAD_DIM ** -0.5
SCALE_MLA = MLA_QK_DIM ** -0.5
FFN_TOKENS = 1024
MXU_WIDTH = 256
SUM_ROWS = 16
MAX_SAFE_BOUND = 56.0
MOBA_MIN_MASK = 256.0
STICK_DONE = 160.0


def _dot(a, b):
    return jnp.dot(a, b, preferred_element_type=F32)


def _dot_nt(a, b):
    return lax.dot_general(a, b, (((1,), (1,)), ((), ())), preferred_element_type=F32)


def _split3(a):
    hi = a.astype(BF16)
    r1 = a - hi.astype(F32)
    mid = r1.astype(BF16)
    lo = (r1 - mid.astype(F32)).astype(BF16)
    return hi, mid, lo


def _log_sigmoid_pair(z):
    sp = jnp.log(1.0 + jnp.exp(-jnp.abs(z)))
    return jnp.minimum(z, 0.0) - sp, -jnp.maximum(z, 0.0) - sp


def _rms(v, g, denom=None):
    if denom is None:
        ms = jnp.mean(v * v, axis=-1, keepdims=True)
    else:
        ms = jnp.sum(v * v, axis=-1, keepdims=True) * (1.0 / denom)
    return v * lax.rsqrt(ms + EPS) * g


def _sumsq_lanes(v):
    return _dot((v * v).astype(BF16), jnp.ones((v.shape[-1], HEAD_DIM), BF16))


def _rms_lanes(v, g, denom=None):
    width = v.shape[-1]
    r = lax.rsqrt(_sumsq_lanes(v) * (1.0 / (denom or width)) + EPS)
    if width > HEAD_DIM:
        r = jnp.concatenate([r] * (width // HEAD_DIM), axis=1)
    return v * r * g


def _inproj_kernel(x_ref, g_ref, w_ref, wvt_ref, bf_ref, proj_ref, vt_ref, vtb_ref, negc_ref,
                   carry_scr, *, tiles_per_batch, tm):
    i = pl.program_id(0)
    gw = GROUP_WIDTH
    h = _rms(x_ref[...], g_ref[...]).astype(BF16)
    dkv = _dot(h, w_ref[:, T_DKV * gw:(T_DKV + 1) * gw])
    proj_ref[:, T_DKV * gw:(T_DKV + 1) * gw] = dkv.astype(BF16)
    z = dkv[:, FORGET_LANE:FORGET_LANE + HEAD_DIM].T[0:8] + bf_ref[:, 0:1]
    logf, _ = _log_sigmoid_pair(z)
    r = lax.broadcasted_iota(jnp.int32, (tm, tm), 0)
    c = lax.broadcasted_iota(jnp.int32, (tm, tm), 1)
    tri = jnp.where(r <= c, 1.0, 0.0).astype(BF16)
    hi, mid, lo = _split3(logf)
    cum = _dot(hi, tri) + _dot(mid, tri) + _dot(lo, tri)

    @pl.when(i % tiles_per_batch == 0)
    def _():
        carry_scr[...] = jnp.zeros_like(carry_scr)

    cum = cum + carry_scr[:, 0:1]
    carry_scr[...] = jnp.broadcast_to(cum[:, tm - 1:tm], carry_scr.shape)
    negc_ref[...] = cum * (-LOG2E)

    for jt in range(PROJ_TILES):
        if jt != T_DKV:
            proj_ref[:, jt * gw:(jt + 1) * gw] = _dot(h, w_ref[:, jt * gw:(jt + 1) * gw]).astype(BF16)
    for gi in range(V_GROUPS):
        vt = _dot_nt(wvt_ref[gi], h).astype(BF16)
        if gi == 1:
            for part in range(tm // STICK_TILE):
                vtb_ref[part] = vt[:, part * STICK_TILE:(part + 1) * STICK_TILE]
        else:
            vt_ref[gi // 2] = vt


def _layer_spec(shape, layer, **kw):
    zeros = (0,) * len(shape)
    return pl.BlockSpec((None,) + tuple(shape), lambda *_: (layer,) + zeros, **kw)


def _inproj(x2, g, w, wvt, bf, layer, batch, seq):
    n = x2.shape[0]
    tm = TILE
    tpb = seq // tm
    kern = functools.partial(_inproj_kernel, tiles_per_batch=tpb, tm=tm)
    resident = pl.Buffered(1)
    return pl.pallas_call(
        kern,
        grid=(n // tm,),
        in_specs=[
            pl.BlockSpec((tm, D_MODEL), lambda i: (i, 0)),
            _layer_spec((1, D_MODEL), layer),
            _layer_spec((D_MODEL, PROJ_TILES * GROUP_WIDTH), layer, pipeline_mode=resident),
            _layer_spec((V_GROUPS, GROUP_WIDTH, D_MODEL), layer, pipeline_mode=resident),
            _layer_spec((8, 128), layer),
        ],
        out_specs=[
            pl.BlockSpec((tm, PROJ_TILES * GROUP_WIDTH), lambda i: (i, 0)),
            pl.BlockSpec((2, None, None, GROUP_WIDTH, tm), lambda i: (0, i // tpb, i % tpb, 0, 0)),
            pl.BlockSpec((None, tm // STICK_TILE, GROUP_WIDTH, STICK_TILE), lambda i: (i // tpb, i % tpb, 0, 0)),
            pl.BlockSpec((None, 8, tm), lambda i: (i // tpb, 0, i % tpb)),
        ],
        out_shape=[
            jax.ShapeDtypeStruct((n, PROJ_TILES * GROUP_WIDTH), BF16),
            jax.ShapeDtypeStruct((2, batch, tpb, GROUP_WIDTH, tm), BF16),
            jax.ShapeDtypeStruct((batch, seq // STICK_TILE, GROUP_WIDTH, STICK_TILE), BF16),
            jax.ShapeDtypeStruct((batch, 8, seq), F32),
        ],
        scratch_shapes=[pltpu.VMEM((8, 128), F32)],
        compiler_params=pltpu.CompilerParams(
            dimension_semantics=("arbitrary",), vmem_limit_bytes=VMEM_LIMIT),
        name="inproj",
    )(x2, g, w, wvt, bf)


def _prep_kernel(a_ref, c_ref, d_ref, nc_ref, bnd_ref, gaq, gak, gcq, gck, gdcq, gdckv, gdq, gdk, wuq, wuk, wuvt,
                 cosc, sinc, cosd, sind,
                 qa_ref, ka_ref, qc_ref, kc_ref, qd_ref, kd_ref, vtd_ref, kmt_scr,
                 *, tiles_per_batch, tm):
    i = pl.program_id(0)
    t = i % tiles_per_batch
    hd = HEAD_DIM
    wh = WIDE_HEAD
    blocks_per_tile = tm // MOBA_BLOCK
    lane = lax.broadcasted_iota(jnp.int32, (tm, hd), 1)
    bound_a = bnd_ref[0:1, :]
    bound_c = bnd_ref[1:2, :]
    big_c = bnd_ref[2:3, :]
    bound_d = bnd_ref[3:4, :]

    his, mids, los = (p.astype(F32) for p in _split3(nc_ref[...]))
    lane1 = lane[0:1]
    k_const = jnp.where(lane1 < 6, 1.0, jnp.where(lane1 == 6, -bound_a, 0.0))
    q_const = jnp.where(lane1 < 3, 1.0, jnp.where(lane1 == 6, 1.0, 0.0))
    gaq_scaled = gaq[...] * (SCALE * LOG2E)
    for h in range(GROUP_HEADS):
        q = a_ref[:, h * hd:(h + 1) * hd].astype(F32)
        qa_ref[:, h * wh:h * wh + hd] = _rms_lanes(q, gaq_scaled).astype(BF16)
        k = a_ref[:, GROUP_WIDTH + h * hd:GROUP_WIDTH + (h + 1) * hd].astype(F32)
        ka_ref[:, h * wh:h * wh + hd] = _rms_lanes(k, gak[...]).astype(BF16)
        hi, mid, lo = his[:, h:h + 1], mids[:, h:h + 1], los[:, h:h + 1]
        k_parts = jnp.where(lane == 0, hi, jnp.where(lane == 1, mid, lo))
        q_parts = jnp.where(lane == 3, -hi, jnp.where(lane == 4, -mid, -lo))
        k_aug = jnp.where(lane < 3, k_parts, k_const)
        q_aug = jnp.where(lane < 3, q_const, jnp.where(lane < 6, q_parts, q_const))
        ka_ref[:, h * wh + hd:(h + 1) * wh] = k_aug.astype(BF16)
        qa_ref[:, h * wh + hd:(h + 1) * wh] = q_aug.astype(BF16)

    @pl.when(i == 0)
    def _():
        kmt_scr[...] = jnp.zeros_like(kmt_scr)

    cc = cosc[...]
    sc = sinc[...]
    sub8 = lax.broadcasted_iota(jnp.int32, (8, hd), 0)
    key_blk = t * blocks_per_tile + lax.broadcasted_iota(jnp.int32, (tm, hd), 0) // MOBA_BLOCK

    def rope_c(v):
        return v * cc + pltpu.roll(v, hd // 2, axis=1) * sc

    kc_aug = jnp.where(lane == key_blk, big_c, jnp.where(lane == MOBA_MAX_BLOCKS, -(bound_c + big_c), 0.0))
    kc_aug = kc_aug.astype(BF16)
    for h in range(GROUP_HEADS):
        k = rope_c(_rms_lanes(c_ref[:, GROUP_WIDTH + h * hd:GROUP_WIDTH + (h + 1) * hd].astype(F32), gck[...]))
        kc_ref[:, h * wh:h * wh + hd] = k.astype(BF16)
        kc_ref[:, h * wh + hd:(h + 1) * wh] = kc_aug
        for blk in range(blocks_per_tile):
            km = jnp.mean(k[blk * MOBA_BLOCK:(blk + 1) * MOBA_BLOCK], axis=0, keepdims=True)
            nblk = t * blocks_per_tile + blk
            base = pl.multiple_of(h * hd + (nblk // 8) * 8, 8)
            old = kmt_scr[pl.ds(base, 8), h * hd:(h + 1) * hd]
            kmt_scr[pl.ds(base, 8), h * hd:(h + 1) * hd] = jnp.where(sub8 == nblk % 8, km, old)

    qs = []
    gcq_scaled = gcq[...] * (SCALE * LOG2E)
    for h in range(GROUP_HEADS):
        q = rope_c(_rms_lanes(c_ref[:, h * hd:(h + 1) * hd].astype(F32), gcq_scaled))
        qc_ref[:, h * wh:h * wh + hd] = q.astype(BF16)
        qs.append(q)
    q_all = jnp.concatenate(qs, axis=1)
    qh, qm, _ = _split3(q_all)
    kh, km_, _ = _split3(kmt_scr[...])
    gate_t = _dot_nt(kh, qh) + _dot_nt(km_, qh) + _dot_nt(kh, qm)

    nb = MOBA_MAX_BLOCKS
    blk_row = lax.broadcasted_iota(jnp.int32, (nb, tm), 0)
    own = t * blocks_per_tile + lax.broadcasted_iota(jnp.int32, (nb, tm), 1) // MOBA_BLOCK
    blk_f = blk_row.astype(F32)
    ninf = jnp.float32(-jnp.inf)
    no_rows = jnp.zeros((hd - nb, tm), F32)
    for h in range(GROUP_HEADS):
        g = jnp.where(blk_row < own, gate_t[h * hd:h * hd + nb, :], ninf)
        sel = jnp.where(blk_row == own, 1.0, 0.0)
        for _ in range(MOBA_TOP_K):
            mx = jnp.max(g, axis=0, keepdims=True)
            cand = jnp.where(g == mx, blk_f, 1e9)
            cand = jnp.where(mx > ninf, cand, 1e9)
            idx = jnp.min(cand, axis=0, keepdims=True)
            pick = blk_f == idx
            sel = jnp.where(pick, 1.0, sel)
            g = jnp.where(pick, ninf, g)
        sel_rows = jnp.concatenate([sel, no_rows], axis=0).T
        q_aug = jnp.where(lane == nb, 1.0, sel_rows)
        qc_ref[:, h * wh + hd:(h + 1) * wh] = q_aug.astype(BF16)

    cd = cosd[...]
    sd = sind[...]
    free_lane = lane == MLA_ROPE_DIM // 2

    def rope_d(v):
        return v * cd + pltpu.roll(v, hd // 2, axis=1) * sd

    cq = _rms_lanes(d_ref[:, 0:MLA_Q_RANK].astype(F32), gdcq[...]).astype(BF16)
    qf = _dot(cq, wuq[...])
    gdq_scaled = gdq[...] * (SCALE_MLA * LOG2E)
    for h in range(GROUP_HEADS):
        qn = _rms_lanes(qf[:, h * wh:(h + 1) * wh], gdq_scaled, denom=MLA_QK_DIM)
        qd_ref[:, h * wh:h * wh + hd] = qn[:, :hd].astype(BF16)
        qd_ref[:, h * wh + hd:(h + 1) * wh] = jnp.where(free_lane, 1.0, rope_d(qn[:, hd:])).astype(BF16)

    ckv = _rms_lanes(d_ref[:, MLA_Q_RANK:MLA_Q_RANK + MLA_KV_RANK].astype(F32), gdckv[...]).astype(BF16)
    knf = _dot(ckv, wuk[...])
    vtd_ref[...] = _dot_nt(wuvt[...], ckv).astype(BF16)
    kr = d_ref[:, MLA_Q_RANK + MLA_KV_RANK:MLA_Q_RANK + MLA_KV_RANK + hd].astype(F32)
    ss_r = _sumsq_lanes(kr)
    gk = gdk[...]
    for h in range(GROUP_HEADS):
        kn = knf[:, h * hd:(h + 1) * hd]
        r = lax.rsqrt((_sumsq_lanes(kn) + ss_r) * (1.0 / MLA_QK_DIM) + EPS)
        kd_ref[:, h * wh:h * wh + hd] = (kn * r * gk[:, :hd]).astype(BF16)
        k_rot = rope_d(kr * r * gk[:, hd:])
        kd_ref[:, h * wh + hd:(h + 1) * wh] = jnp.where(free_lane, -bound_d, k_rot).astype(BF16)


def _prep(proj, negc_col, bounds, gaq, gak, gcq, gck, gdcq, gdckv, gdq, gdk, wuq, wuk, wuvt,
          cosc, sinc, cosd, sind, layer, batch, seq):
    n = proj.shape[0]
    tm = TILE
    tpb = seq // tm
    assert seq // MOBA_BLOCK <= MOBA_MAX_BLOCKS
    kern = functools.partial(_prep_kernel, tiles_per_batch=tpb, tm=tm)
    row = lambda i: (i, 0)
    tab = lambda i: (i % tpb, 0)
    w2 = 2 * GROUP_WIDTH
    return pl.pallas_call(
        kern,
        grid=(n // tm,),
        in_specs=[
            pl.BlockSpec((tm, w2), lambda i: (i, 0)),
            pl.BlockSpec((tm, w2), lambda i: (i, 1)),
            pl.BlockSpec((tm, w2), lambda i: (i, 2)),
            pl.BlockSpec((tm, 8), row),
            _layer_spec((8, HEAD_DIM), layer),
            _layer_spec((1, HEAD_DIM), layer), _layer_spec((1, HEAD_DIM), layer),
            _layer_spec((1, HEAD_DIM), layer), _layer_spec((1, HEAD_DIM), layer),
            _layer_spec((1, MLA_Q_RANK), layer), _layer_spec((1, MLA_KV_RANK), layer),
            _layer_spec((1, WIDE_HEAD), layer), _layer_spec((1, WIDE_HEAD), layer),
            _layer_spec((MLA_Q_RANK, GROUP_HEADS * WIDE_HEAD), layer),
            _layer_spec((MLA_KV_RANK, GROUP_WIDTH), layer),
            _layer_spec((GROUP_WIDTH, MLA_KV_RANK), layer),
            pl.BlockSpec((tm, HEAD_DIM), tab), pl.BlockSpec((tm, HEAD_DIM), tab),
            pl.BlockSpec((tm, HEAD_DIM), tab), pl.BlockSpec((tm, HEAD_DIM), tab),
        ],
        out_specs=[pl.BlockSpec((tm, w2), row)] * 6 + [
            pl.BlockSpec((None, None, GROUP_WIDTH, tm), lambda i: (i // tpb, i % tpb, 0, 0)),
        ],
        out_shape=[jax.ShapeDtypeStruct((n, w2), BF16)] * 6 + [
            jax.ShapeDtypeStruct((batch, tpb, GROUP_WIDTH, tm), BF16),
        ],
        scratch_shapes=[pltpu.VMEM((GROUP_WIDTH, GROUP_WIDTH), F32)],
        compiler_params=pltpu.CompilerParams(
            dimension_semantics=("arbitrary",), vmem_limit_bytes=VMEM_LIMIT),
        name="prep",
    )(proj, proj, proj, negc_col, bounds, gaq, gak, gcq, gck, gdcq, gdckv, gdq, gdk, wuq, wuk, wuvt,
      cosc, sinc, cosd, sind)


def _key_query_iota(t):
    key = lax.broadcasted_iota(jnp.int32, (t, t), 0)
    qry = lax.broadcasted_iota(jnp.int32, (t, t), 1)
    return key, qry


def _softmax_step(s, vt, carry):
    m, acc = carry
    m_new = jnp.maximum(m, jnp.max(s, axis=0, keepdims=True))
    alpha = jnp.exp2(m - m_new)
    p = jnp.exp2(s - m_new)
    return m_new, alpha * acc + _dot(vt, p.astype(BF16))


def _softmax_init(t):
    return (jnp.full((1, t), NEG, F32), jnp.zeros((HEAD_DIM + SUM_ROWS, t), F32))


def _softmax_attn_kernel(q_ref, k_ref, vt_ref, o_ref, acc_scr, *, t, bounded):
    qi = pl.program_id(1)
    key, qry = _key_query_iota(t)
    causal = key <= qry
    dk = WIDE_HEAD
    dv = HEAD_DIM
    ones_rows = jnp.ones((SUM_ROWS, t), BF16)

    def logits(j, h, diag):
        off = pl.multiple_of(j * t, t)
        s = _dot_nt(k_ref[pl.ds(off, t), h * dk:(h + 1) * dk], q_ref[:, h * dk:(h + 1) * dk])
        return jnp.where(causal, s, NEG) if diag else s

    def values(j, h):
        return jnp.concatenate([vt_ref[j, h * dv:(h + 1) * dv, :], ones_rows], axis=0)

    if bounded:
        acc_scr[...] = jnp.zeros_like(acc_scr)

        def step(*tiles):
            s = [[logits(j, h, diag) for h in range(GROUP_HEADS)] for j, diag in tiles]
            for n, (j, _) in enumerate(tiles):
                for h in range(GROUP_HEADS):
                    acc_scr[h] += _dot(values(j, h), jnp.exp2(s[n][h]).astype(BF16))

        left = jnp.maximum(qi - 1, 0)

        def pair(jj, c):
            step((2 * jj, False), (2 * jj + 1, False))
            return c

        lax.fori_loop(0, left // 2, pair, 0)

        @pl.when(left % 2 == 1)
        def _():
            step((left - 1, False))

        @pl.when(qi == 0)
        def _():
            step((qi, True))

        @pl.when(qi > 0)
        def _():
            step((qi, True), (qi - 1, False))

        accs = [acc_scr[h] for h in range(GROUP_HEADS)]
    else:
        def step(j, carry, diag):
            out = []
            for h in range(GROUP_HEADS):
                out.append(_softmax_step(logits(j, h, diag), values(j, h), carry[h]))
            return tuple(out)

        carry = step(qi, tuple(_softmax_init(t) for _ in range(GROUP_HEADS)), True)
        carry = lax.fori_loop(0, qi, lambda j, c: step(j, c, False), carry)
        accs = [c[1] for c in carry]

    for h in range(GROUP_HEADS):
        a = accs[h]
        o_ref[:, h * dv:(h + 1) * dv] = (a[:dv] / a[dv:dv + 1]).T.astype(BF16)


def _softmax_attn(name, q_arr, k_arr, vt_arr, vt_group, batch, seq, bounded):
    t = TILE
    nq = seq // t
    n = batch * seq
    kern = functools.partial(_softmax_attn_kernel, t=t, bounded=bounded)
    if vt_group is None:
        vt_spec = pl.BlockSpec((None, nq, GROUP_WIDTH, t), lambda b, i: (b, 0, 0, 0))
    else:
        vt_spec = pl.BlockSpec((None, None, nq, GROUP_WIDTH, t), lambda b, i: (vt_group, b, 0, 0, 0))
    return pl.pallas_call(
        kern,
        grid=(batch, nq),
        in_specs=[
            pl.BlockSpec((t, GROUP_HEADS * WIDE_HEAD), lambda b, i: (b * nq + i, 0)),
            pl.BlockSpec((seq, GROUP_HEADS * WIDE_HEAD), lambda b, i: (b, 0)),
            vt_spec,
        ],
        out_specs=pl.BlockSpec((t, GROUP_WIDTH), lambda b, i: (b * nq + i, 0)),
        out_shape=jax.ShapeDtypeStruct((n, GROUP_WIDTH), BF16),
        scratch_shapes=[pltpu.VMEM((GROUP_HEADS, HEAD_DIM + SUM_ROWS, t), F32)],
        compiler_params=pltpu.CompilerParams(
            dimension_semantics=("parallel", "parallel"), vmem_limit_bytes=VMEM_LIMIT),
        name=name + ("" if bounded else "_online"),
    )(q_arr, k_arr, vt_arr)


def _bounded_or_online(bound, name, *args):
    return lax.cond(bound < MAX_SAFE_BOUND,
                    lambda: _softmax_attn(name, *args, bounded=True),
                    lambda: _softmax_attn(name, *args, bounded=False))


def _stickbreak_kernel(q_ref, k_ref, vt_ref, o_ref, acc_scr, rest_scr, *, t):
    qi = pl.program_id(1)
    key, qry = _key_query_iota(t)
    strict = key < qry
    later = jnp.where(qry > key, 1.0, 0.0).astype(BF16)
    hd = HEAD_DIM
    acc_scr[...] = jnp.zeros_like(acc_scr)
    rest_scr[...] = jnp.zeros_like(rest_scr)

    def step(tiles):
        qs = [(q_ref[:, h * hd:(h + 1) * hd].astype(F32) * (SCALE * LOG2E)).astype(BF16) for h in range(GROUP_HEADS)]
        zs, drops, afters = [], [], []
        for j, diag in tiles:
            off = pl.multiple_of(j * t, t)
            zs.append([_dot_nt(k_ref[pl.ds(off, t), h * hd:(h + 1) * hd], qs[h]) for h in range(GROUP_HEADS)])
        for n, (j, diag) in enumerate(tiles):
            row = []
            for h in range(GROUP_HEADS):
                z = zs[n][h]
                drop = jnp.maximum(z, 0.0) + jnp.log(1.0 + jnp.exp2(-jnp.abs(z))) * LOG2E
                row.append(jnp.where(strict, drop, 0.0) if diag else drop)
            drops.append(row)
            afters.append([_dot(later, d.astype(BF16)) for d in row])
        low = None
        for h in range(GROUP_HEADS):
            used = rest_scr[h]
            for n, (j, diag) in enumerate(tiles):
                z, drop, after = zs[n][h], drops[n][h], afters[n][h]
                w = jnp.exp2((z - drop) - (after + used))
                if diag:
                    w = jnp.where(strict, w, 0.0)
                acc_scr[h] += _dot(vt_ref[j, h * hd:(h + 1) * hd, :], w.astype(BF16))
                used = used + after[0:1] + drop[0:1]
            rest_scr[h] = used
            low = used if low is None else jnp.minimum(low, used)
        return jnp.min(low)

    def cond(c):
        j, low = c
        return jnp.logical_and(j >= 0, low < STICK_DONE)

    def body(c):
        j, _ = c
        return j - 1, step([(j, False)])

    first = lax.cond(qi == 0, lambda: step([(qi, True)]), lambda: step([(qi, True), (qi - 1, False)]))
    lax.while_loop(cond, body, (qi - 2, first))
    for h in range(GROUP_HEADS):
        o_ref[:, h * hd:(h + 1) * hd] = acc_scr[h].T.astype(BF16)


def _stickbreak_attn(proj, vtb, batch, seq):
    t = STICK_TILE
    nq = seq // t
    n = batch * seq
    kern = functools.partial(_stickbreak_kernel, t=t)
    return pl.pallas_call(
        kern,
        grid=(batch, nq),
        in_specs=[
            pl.BlockSpec((t, GROUP_WIDTH), lambda b, i: (b * nq + i, T_BQ)),
            pl.BlockSpec((seq, GROUP_WIDTH), lambda b, i: (b, T_BK)),
            pl.BlockSpec((None, nq, GROUP_WIDTH, t), lambda b, i: (b, 0, 0, 0)),
        ],
        out_specs=pl.BlockSpec((t, GROUP_WIDTH), lambda b, i: (b * nq + i, 0)),
        out_shape=jax.ShapeDtypeStruct((n, GROUP_WIDTH), BF16),
        scratch_shapes=[pltpu.VMEM((GROUP_HEADS, HEAD_DIM, t), F32), pltpu.VMEM((GROUP_HEADS, 1, t), F32)],
        compiler_params=pltpu.CompilerParams(
            dimension_semantics=("parallel", "parallel"), vmem_limit_bytes=VMEM_LIMIT),
        name="stickbreak_attn",
    )(proj, proj, vtb)


def _outproj_kernel(oa_ref, ob_ref, oc_ref, od_ref, g_ref, w_ref, x_ref, gf_ref, out_ref, hf_ref):
    ys = [_rms(o_ref[...].astype(F32), g_ref[gi:gi + 1, :]).astype(BF16)
          for gi, o_ref in enumerate((oa_ref, ob_ref, oc_ref, od_ref))]
    acc = x_ref[...] + _dot(jnp.concatenate(ys, axis=1), w_ref[...])
    out_ref[...] = acc
    hf_ref[...] = _rms(acc, gf_ref[...]).astype(BF16)


def _outproj(oa, ob, oc, od, g, w, x2, g_ffn, layer):
    n = x2.shape[0]
    tm = 512
    row = lambda i: (i, 0)
    return pl.pallas_call(
        _outproj_kernel,
        grid=(n // tm,),
        in_specs=[pl.BlockSpec((tm, GROUP_WIDTH), row)] * 4 + [
            _layer_spec((N_GROUPS, GROUP_WIDTH), layer),
            _layer_spec((N_GROUPS * GROUP_WIDTH, D_MODEL), layer, pipeline_mode=pl.Buffered(1)),
            pl.BlockSpec((tm, D_MODEL), row),
            _layer_spec((1, D_MODEL), layer),
        ],
        out_specs=[pl.BlockSpec((tm, D_MODEL), row), pl.BlockSpec((tm, D_MODEL), row)],
        out_shape=[jax.ShapeDtypeStruct((n, D_MODEL), F32), jax.ShapeDtypeStruct((n, D_MODEL), BF16)],
        compiler_params=pltpu.CompilerParams(
            dimension_semantics=("parallel",), vmem_limit_bytes=VMEM_LIMIT),
        name="outproj",
    )(oa, ob, oc, od, g, w, x2, g_ffn)


def _ffn_up_kernel(h_ref, wg_ref, wv_ref, cwg_ref, cwv_ref, cbg_ref, cbv_ref,
                   out_ref, wg_scr, wv_scr, tail_g, tail_v, *, tiles_per_batch, tm):
    i = pl.program_id(1)

    @pl.when(i == 0)
    def _():
        wg_scr[...] = wg_ref[...].astype(BF16)
        wv_scr[...] = wv_ref[...].astype(BF16)

    @pl.when(i % tiles_per_batch == 0)
    def _():
        tail_g[...] = jnp.zeros_like(tail_g)
        tail_v[...] = jnp.zeros_like(tail_v)

    def conv(u, cw_ref, cb_ref, prev):
        ue = jnp.concatenate([prev, u], axis=0)
        y = cb_ref[...] + cw_ref[2:3, :] * u
        for back in range(1, CONV_WIDTH):
            y = y + cw_ref[CONV_WIDTH - 1 - back:CONV_WIDTH - back, :] * pltpu.roll(ue, back, axis=0)[8:]
        return y, u[u.shape[0] - 8:]

    h = h_ref[...]
    u_g = _dot(h, wg_scr[...])
    u_v = _dot(h, wv_scr[...])
    gate, tail_g[...] = conv(u_g, cwg_ref, cbg_ref, tail_g[...])
    val, tail_v[...] = conv(u_v, cwv_ref, cbv_ref, tail_v[...])
    out_ref[...] = (gate / (1.0 + jnp.exp(-gate)) * val).astype(BF16)


def _ffn_up(h, w_up, conv_w, conv_b, layer, seq):
    n = h.shape[0]
    tm = FFN_TOKENS
    tn = 512
    nj = D_FF // tn
    tpb = seq // tm
    kern = functools.partial(_ffn_up_kernel, tiles_per_batch=tpb, tm=tm)
    return pl.pallas_call(
        kern,
        grid=(nj, n // tm),
        in_specs=[
            pl.BlockSpec((tm, D_MODEL), lambda j, i: (i, 0)),
            pl.BlockSpec((None, D_MODEL, tn), lambda j, i: (layer, 0, j)),
            pl.BlockSpec((None, D_MODEL, tn), lambda j, i: (layer, 0, j + nj)),
            pl.BlockSpec((None, CONV_WIDTH, tn), lambda j, i: (layer, 0, j)),
            pl.BlockSpec((None, CONV_WIDTH, tn), lambda j, i: (layer, 0, j + nj)),
            pl.BlockSpec((None, 1, tn), lambda j, i: (layer, 0, j)),
            pl.BlockSpec((None, 1, tn), lambda j, i: (layer, 0, j + nj)),
        ],
        out_specs=pl.BlockSpec((tm, tn), lambda j, i: (i, j)),
        out_shape=jax.ShapeDtypeStruct((n, D_FF), BF16),
        scratch_shapes=[pltpu.VMEM((D_MODEL, tn), BF16), pltpu.VMEM((D_MODEL, tn), BF16),
                        pltpu.VMEM((8, tn), F32), pltpu.VMEM((8, tn), F32)],
        compiler_params=pltpu.CompilerParams(
            dimension_semantics=("arbitrary", "arbitrary"), vmem_limit_bytes=VMEM_LIMIT),
        name="ffn_up",
    )(h, w_up, w_up, conv_w, conv_w, conv_b, conv_b)


def _ffn_down_kernel(a_ref, w_ref, x_ref, out_ref):
    a = a_ref[...]
    tn = 2 * MXU_WIDTH
    for c in range(D_MODEL // tn):
        cols = slice(c * tn, (c + 1) * tn)
        out_ref[:, cols] = x_ref[:, cols] + _dot(a, w_ref[:, cols])


def _ffn_down(act, w_down, x2, layer):
    n = x2.shape[0]
    tm = 512
    row = lambda i: (i, 0)
    return pl.pallas_call(
        _ffn_down_kernel,
        grid=(n // tm,),
        in_specs=[
            pl.BlockSpec((tm, D_FF), row),
            _layer_spec((D_FF, D_MODEL), layer, pipeline_mode=pl.Buffered(1)),
            pl.BlockSpec((tm, D_MODEL), row),
        ],
        out_specs=pl.BlockSpec((tm, D_MODEL), row),
        out_shape=jax.ShapeDtypeStruct((n, D_MODEL), F32),
        compiler_params=pltpu.CompilerParams(
            dimension_semantics=("parallel",), vmem_limit_bytes=VMEM_LIMIT),
        name="ffn_down",
    )(act, w_down, x2)


def _rope_tables(seq):
    def angles(dim):
        inv_freq = ROPE_THETA ** (-np.arange(0, dim, 2, dtype=np.float64) / dim)
        ang = np.arange(seq, dtype=np.float64)[:, None] * inv_freq[None, :]
        return np.cos(ang), np.sin(ang)

    c, s = angles(HEAD_DIM)
    cosc = np.concatenate([c, c], axis=1)
    sinc = np.concatenate([-s, s], axis=1)
    c, s = angles(MLA_ROPE_DIM)
    z = np.zeros_like(c)
    cosd = np.concatenate([c, z, c, z], axis=1)
    sind = np.concatenate([-s, z, s, z], axis=1)
    return tuple(jnp.asarray(t, F32) for t in (cosc, sinc, cosd, sind))


def _spread_rope(t):
    half = MLA_ROPE_DIM // 2
    z = jnp.zeros(t.shape[:-1] + (half,), t.dtype)
    return jnp.concatenate([t[..., :half], z, t[..., half:], z], axis=-1)


IN_OFFSETS = {}
_o = 0
for _name, _width in (("aq", GROUP_WIDTH), ("ak", GROUP_WIDTH), ("av", GROUP_WIDTH), ("af", GROUP_HEADS),
                      ("bq", GROUP_WIDTH), ("bk", GROUP_WIDTH), ("bv", GROUP_WIDTH),
                      ("cq", GROUP_WIDTH), ("ck", GROUP_WIDTH), ("cv", GROUP_WIDTH),
                      ("dcq", MLA_Q_RANK), ("dckv", MLA_KV_RANK), ("dkr", MLA_ROPE_DIM)):
    IN_OFFSETS[_name] = (_o, _width)
    _o += _width
IN_WIDTH = _o
RELAYOUT_ROWS = 256
FORGET_LANE = MLA_KV_RANK + HEAD_DIM


def _relayout_kernel(w_ref, kr_ref, main_ref, wvt_ref):
    def piece(name):
        o, width = IN_OFFSETS[name]
        return w_ref[:, o:o + width]

    gw = GROUP_WIDTH
    for tile, name in ((T_AQ, "aq"), (T_AK, "ak"), (T_CQ, "cq"), (T_CK, "ck"), (T_DCQ, "dcq"),
                       (T_BQ, "bq"), (T_BK, "bk")):
        main_ref[:, tile * gw:(tile + 1) * gw] = piece(name).astype(BF16)
    base = T_DKV * gw
    main_ref[:, base:base + MLA_KV_RANK] = piece("dckv").astype(BF16)
    main_ref[:, base + MLA_KV_RANK:base + MLA_KV_RANK + HEAD_DIM] = kr_ref[...].astype(BF16)
    main_ref[:, base + FORGET_LANE:base + gw] = jnp.zeros((main_ref.shape[0], gw - FORGET_LANE), BF16)
    main_ref[:, base + FORGET_LANE:base + FORGET_LANE + GROUP_HEADS] = piece("af").astype(BF16)
    for gi, name in enumerate(("av", "bv", "cv")):
        wvt_ref[gi] = piece(name).T.astype(BF16)


def _layout_w_in(w):
    depth, d_model, _ = w.shape
    rb = RELAYOUT_ROWS
    o, width = IN_OFFSETS["dkr"]
    kr = _spread_rope(w[..., o:o + width])
    w_main, wvt = pl.pallas_call(
        _relayout_kernel,
        grid=(depth, d_model // rb),
        in_specs=[
            pl.BlockSpec((None, rb, IN_WIDTH), lambda l, r: (l, r, 0)),
            pl.BlockSpec((None, rb, HEAD_DIM), lambda l, r: (l, r, 0)),
        ],
        out_specs=[
            pl.BlockSpec((None, rb, PROJ_TILES * GROUP_WIDTH), lambda l, r: (l, r, 0)),
            pl.BlockSpec((None, V_GROUPS, GROUP_WIDTH, rb), lambda l, r: (l, 0, 0, r)),
        ],
        out_shape=[
            jax.ShapeDtypeStruct((depth, d_model, PROJ_TILES * GROUP_WIDTH), BF16),
            jax.ShapeDtypeStruct((depth, V_GROUPS, GROUP_WIDTH, d_model), BF16),
        ],
        compiler_params=pltpu.CompilerParams(
            dimension_semantics=("parallel", "parallel"), vmem_limit_bytes=VMEM_LIMIT),
        name="relayout_w_in",
    )(w, kr)
    return w_main, wvt


def _layout_w_uq(w):
    w = w.reshape(-1, MLA_Q_RANK, GROUP_HEADS, MLA_QK_DIM)
    w = jnp.concatenate([w[..., :MLA_NOPE_DIM], _spread_rope(w[..., MLA_NOPE_DIM:])], axis=-1)
    return w.reshape(-1, MLA_Q_RANK, GROUP_HEADS * WIDE_HEAD).astype(BF16)


def _layout_w_ukv(w):
    w = w.reshape(-1, MLA_KV_RANK, GROUP_HEADS, 2, HEAD_DIM)
    wk = w[:, :, :, 0, :].reshape(-1, MLA_KV_RANK, GROUP_WIDTH).astype(BF16)
    wvt = jnp.swapaxes(w[:, :, :, 1, :].reshape(-1, MLA_KV_RANK, GROUP_WIDTH), -1, -2).astype(BF16)
    return wk, wvt


def _logit_bound(gq, gk, dim, scale):
    return dim * scale * LOG2E * jnp.max(jnp.abs(gq), axis=-1) * jnp.max(jnp.abs(gk), axis=-1) * 1.01 + 0.5


def _layout_qk_gain(g):
    return jnp.concatenate([g[:, :MLA_NOPE_DIM], _spread_rope(g[:, MLA_NOPE_DIM:])], axis=-1)[:, None, :]


def kernel(x, attn_norm, w_in, b_forget, fox_q_norm, fox_k_norm, moba_q_norm, moba_k_norm, mla_cq_norm, mla_ckv_norm, w_uq, w_ukv, mla_q_norm, mla_k_norm, mix_out_norm, w_out, ffn_norm, w_up, conv_w, conv_b, w_down):
    batch, seq, d_model = x.shape
    depth = w_in.shape[0]
    assert d_model == D_MODEL and seq % TILE == 0 and seq % FFN_TOKENS == 0
    n = batch * seq
    x2 = x.reshape(n, d_model)
    cosc, sinc, cosd, sind = _rope_tables(seq)

    rows = lambda v: v[:, None, :]
    w_main, wvt = _layout_w_in(w_in)
    wuq = _layout_w_uq(w_uq)
    wuk, wuvt = _layout_w_ukv(w_ukv)
    w_out_bf = w_out.astype(BF16)
    w_down_bf = w_down.astype(BF16)
    bf = jnp.pad(jnp.broadcast_to(b_forget[:, :, None], (depth, GROUP_HEADS, 128)),
                 ((0, 0), (0, 8 - GROUP_HEADS), (0, 0)))
    bound_a = _logit_bound(fox_q_norm, fox_k_norm, HEAD_DIM, SCALE)
    bound_c = _logit_bound(moba_q_norm, moba_k_norm, HEAD_DIM, SCALE)
    bound_d = _logit_bound(mla_q_norm, mla_k_norm, MLA_QK_DIM, SCALE_MLA)
    big_c = jnp.maximum(MOBA_MIN_MASK, jnp.exp2(jnp.ceil(jnp.log2(4.0 * bound_c + 64.0))))
    bounds = jnp.stack([bound_a, bound_c, big_c, bound_d] + [jnp.zeros_like(bound_a)] * 4, axis=1)
    bounds = jnp.broadcast_to(bounds[:, :, None], (depth, 8, HEAD_DIM))
    gdq, gdk = _layout_qk_gain(mla_q_norm), _layout_qk_gain(mla_k_norm)
    g_mix = mix_out_norm.reshape(depth, N_GROUPS, GROUP_WIDTH)
    conv_b3 = conv_b[:, None, :]

    for l in range(depth):
        proj, vt_ac, vt_b, negc = _inproj(x2, rows(attn_norm), w_main, wvt, bf, l, batch, seq)
        negc_col = negc.transpose(0, 2, 1).reshape(n, 8)

        qa, ka, qc, kc, qd, kd, vtd = _prep(
            proj, negc_col, bounds, rows(fox_q_norm), rows(fox_k_norm), rows(moba_q_norm), rows(moba_k_norm),
            rows(mla_cq_norm), rows(mla_ckv_norm), gdq, gdk, wuq, wuk, wuvt,
            cosc, sinc, cosd, sind, l, batch, seq)

        o_a = _bounded_or_online(bound_a[l], "fox_attn", qa, ka, vt_ac, 0, batch, seq)
        o_b = _stickbreak_attn(proj, vt_b, batch, seq)
        o_c = _bounded_or_online(bound_c[l], "moba_attn", qc, kc, vt_ac, 1, batch, seq)
        o_d = _bounded_or_online(bound_d[l], "mla_attn", qd, kd, vtd, None, batch, seq)

        x2, h_ffn = _outproj(o_a, o_b, o_c, o_d, g_mix, w_out_bf, x2, rows(ffn_norm), l)

        act = _ffn_up(h_ffn, w_up, conv_w, conv_b3, l, seq)
        x2 = _ffn_down(act, w_down_bf, x2, l)

    return x2.reshape(batch, seq, d_model)
```

```python
import functools
import math

import numpy as np
import jax
import jax.numpy as jnp
from jax import lax
from jax.experimental import pallas as pl
from jax.experimental.pallas import tpu as pltpu

F32 = jnp.float32
BF16 = jnp.bfloat16

D_MODEL = 2048
HEAD_DIM = 128
GROUP_HEADS = 4
GROUP_WIDTH = GROUP_HEADS * HEAD_DIM
N_GROUPS = 4
MOBA_BLOCK = 256
MOBA_TOP_K = 3
MOBA_MAX_BLOCKS = 16
MLA_Q_RANK = 512
MLA_KV_RANK = 256
MLA_NOPE_DIM = 128
MLA_ROPE_DIM = 64
MLA_QK_DIM = MLA_NOPE_DIM + MLA_ROPE_DIM
WIDE_HEAD = 256
D_FF = 5632
CONV_WIDTH = 3
ROPE_THETA = 10000.0
EPS = 1e-6

T_AQ, T_AK, T_CQ, T_CK, T_DCQ, T_DKV, T_BQ, T_BK = range(8)
PROJ_TILES = 8
V_GROUPS = 3

TILE = 512
STICK_TILE = 256
NEG = -1e30
VMEM_LIMIT = 56 * 1024 * 1024
LOG2E = math.log2(math.e)
SCALE = HEAD_DIM ** -0.5
SCALE_MLA = MLA_QK_DIM ** -0.5
FFN_TOKENS = 1024
MXU_WIDTH = 256
SUM_ROWS = 16
MAX_SAFE_BOUND = 56.0
MOBA_MIN_MASK = 256.0
STICK_DONE = 160.0


def _dot(a, b):
    return jnp.dot(a, b, preferred_element_type=F32)


def _dot_nt(a, b):
    return lax.dot_general(a, b, (((1,), (1,)), ((), ())), preferred_element_type=F32)


def _split3(a):
    hi = a.astype(BF16)
    r1 = a - hi.astype(F32)
    mid = r1.astype(BF16)
    lo = (r1 - mid.astype(F32)).astype(BF16)
    return hi, mid, lo


def _log_sigmoid_pair(z):
    sp = jnp.log(1.0 + jnp.exp(-jnp.abs(z)))
    return jnp.minimum(z, 0.0) - sp, -jnp.maximum(z, 0.0) - sp


def _rms(v, g, denom=None):
    if denom is None:
        ms = jnp.mean(v * v, axis=-1, keepdims=True)
    else:
        ms = jnp.sum(v * v, axis=-1, keepdims=True) * (1.0 / denom)
    return v * lax.rsqrt(ms + EPS) * g


def _sumsq_lanes(v):
    return _dot((v * v).astype(BF16), jnp.ones((v.shape[-1], HEAD_DIM), BF16))


def _rms_lanes(v, g, denom=None):
    width = v.shape[-1]
    r = lax.rsqrt(_sumsq_lanes(v) * (1.0 / (denom or width)) + EPS)
    if width > HEAD_DIM:
        r = jnp.concatenate([r] * (width // HEAD_DIM), axis=1)
    return v * r * g


def _inproj_kernel(x_ref, g_ref, w_ref, wvt_ref, bf_ref, proj_ref, vt_ref, vtb_ref, negc_ref,
                   carry_scr, *, tiles_per_batch, tm):
    i = pl.program_id(0)
    gw = GROUP_WIDTH
    h = _rms(x_ref[...], g_ref[...]).astype(BF16)
    dkv = _dot(h, w_ref[:, T_DKV * gw:(T_DKV + 1) * gw])
    proj_ref[:, T_DKV * gw:(T_DKV + 1) * gw] = dkv.astype(BF16)
    z = dkv[:, FORGET_LANE:FORGET_LANE + HEAD_DIM].T[0:8] + bf_ref[:, 0:1]
    logf, _ = _log_sigmoid_pair(z)
    r = lax.broadcasted_iota(jnp.int32, (tm, tm), 0)
    c = lax.broadcasted_iota(jnp.int32, (tm, tm), 1)
    tri = jnp.where(r <= c, 1.0, 0.0).astype(BF16)
    parts = jnp.concatenate([p.astype(F32) for p in _split3(logf)], axis=0).astype(BF16)
    sums = _dot(parts, tri)
    cum = sums[0:8] + sums[8:16] + sums[16:24]

    @pl.when(i % tiles_per_batch == 0)
    def _():
        carry_scr[...] = jnp.zeros_like(carry_scr)

    cum = cum + carry_scr[:, 0:1]
    carry_scr[...] = jnp.broadcast_to(cum[:, tm - 1:tm], carry_scr.shape)
    negc_ref[...] = cum * (-LOG2E)

    for jt in range(PROJ_TILES):
        if jt != T_DKV:
            proj_ref[:, jt * gw:(jt + 1) * gw] = _dot(h, w_ref[:, jt * gw:(jt + 1) * gw]).astype(BF16)
    for gi in range(V_GROUPS):
        vt = _dot_nt(wvt_ref[gi], h).astype(BF16)
        if gi == 1:
            for part in range(tm // STICK_TILE):
                vtb_ref[part] = vt[:, part * STICK_TILE:(part + 1) * STICK_TILE]
        else:
            vt_ref[gi // 2] = vt


def _layer_spec(shape, layer, **kw):
    zeros = (0,) * len(shape)
    return pl.BlockSpec((None,) + tuple(shape), lambda *_: (layer,) + zeros, **kw)


def _inproj(x2, g, w, wvt, bf, layer, batch, seq):
    n = x2.shape[0]
    tm = TILE
    tpb = seq // tm
    kern = functools.partial(_inproj_kernel, tiles_per_batch=tpb, tm=tm)
    resident = pl.Buffered(1)
    return pl.pallas_call(
        kern,
        grid=(n // tm,),
        in_specs=[
            pl.BlockSpec((tm, D_MODEL), lambda i: (i, 0)),
            _layer_spec((1, D_MODEL), layer),
            _layer_spec((D_MODEL, PROJ_TILES * GROUP_WIDTH), layer, pipeline_mode=resident),
            _layer_spec((V_GROUPS, GROUP_WIDTH, D_MODEL), layer, pipeline_mode=resident),
            _layer_spec((8, 128), layer),
        ],
        out_specs=[
            pl.BlockSpec((tm, PROJ_TILES * GROUP_WIDTH), lambda i: (i, 0)),
            pl.BlockSpec((2, None, None, GROUP_WIDTH, tm), lambda i: (0, i // tpb, i % tpb, 0, 0)),
            pl.BlockSpec((None, tm // STICK_TILE, GROUP_WIDTH, STICK_TILE), lambda i: (i // tpb, i % tpb, 0, 0)),
            pl.BlockSpec((None, 8, tm), lambda i: (i // tpb, 0, i % tpb)),
        ],
        out_shape=[
            jax.ShapeDtypeStruct((n, PROJ_TILES * GROUP_WIDTH), BF16),
            jax.ShapeDtypeStruct((2, batch, tpb, GROUP_WIDTH, tm), BF16),
            jax.ShapeDtypeStruct((batch, seq // STICK_TILE, GROUP_WIDTH, STICK_TILE), BF16),
            jax.ShapeDtypeStruct((batch, 8, seq), F32),
        ],
        scratch_shapes=[pltpu.VMEM((8, 128), F32)],
        compiler_params=pltpu.CompilerParams(
            dimension_semantics=("arbitrary",), vmem_limit_bytes=VMEM_LIMIT),
        name="inproj",
    )(x2, g, w, wvt, bf)


def _prep_kernel(a_ref, c_ref, d_ref, nc_ref, bnd_ref, gaq, gak, gcq, gck, gdcq, gdckv, gdq, gdk, wuq, wuk, wuvt,
                 cosc, sinc, cosd, sind,
                 qa_ref, ka_ref, qc_ref, kc_ref, qd_ref, kd_ref, vtd_ref, kmt_scr,
                 *, tiles_per_batch, tm):
    i = pl.program_id(0)
    t = i % tiles_per_batch
    hd = HEAD_DIM
    wh = WIDE_HEAD
    blocks_per_tile = tm // MOBA_BLOCK
    lane = lax.broadcasted_iota(jnp.int32, (tm, hd), 1)
    bound_a = bnd_ref[0:1, :]
    bound_c = bnd_ref[1:2, :]
    big_c = bnd_ref[2:3, :]
    bound_d = bnd_ref[3:4, :]

    his, mids, los = (p.astype(F32) for p in _split3(nc_ref[...]))
    lane1 = lane[0:1]
    k_const = jnp.where(lane1 < 6, 1.0, jnp.where(lane1 == 6, -bound_a, 0.0))
    q_const = jnp.where(lane1 < 3, 1.0, jnp.where(lane1 == 6, 1.0, 0.0))
    gaq_scaled = gaq[...] * (SCALE * LOG2E)
    for h in range(GROUP_HEADS):
        q = a_ref[:, h * hd:(h + 1) * hd].astype(F32)
        qa_ref[:, h * wh:h * wh + hd] = _rms_lanes(q, gaq_scaled).astype(BF16)
        k = a_ref[:, GROUP_WIDTH + h * hd:GROUP_WIDTH + (h + 1) * hd].astype(F32)
        ka_ref[:, h * wh:h * wh + hd] = _rms_lanes(k, gak[...]).astype(BF16)
        hi, mid, lo = his[:, h:h + 1], mids[:, h:h + 1], los[:, h:h + 1]
        k_parts = jnp.where(lane == 0, hi, jnp.where(lane == 1, mid, lo))
        q_parts = jnp.where(lane == 3, -hi, jnp.where(lane == 4, -mid, -lo))
        k_aug = jnp.where(lane < 3, k_parts, k_const)
        q_aug = jnp.where(lane < 3, q_const, jnp.where(lane < 6, q_parts, q_const))
        ka_ref[:, h * wh + hd:(h + 1) * wh] = k_aug.astype(BF16)
        qa_ref[:, h * wh + hd:(h + 1) * wh] = q_aug.astype(BF16)

    @pl.when(i == 0)
    def _():
        kmt_scr[...] = jnp.zeros_like(kmt_scr)

    cc = cosc[...]
    sc = sinc[...]
    sub8 = lax.broadcasted_iota(jnp.int32, (8, hd), 0)
    key_blk = t * blocks_per_tile + lax.broadcasted_iota(jnp.int32, (tm, hd), 0) // MOBA_BLOCK

    def rope_c(v):
        return v * cc + pltpu.roll(v, hd // 2, axis=1) * sc

    kc_aug = jnp.where(lane == key_blk, big_c, jnp.where(lane == MOBA_MAX_BLOCKS, -(bound_c + big_c), 0.0))
    kc_aug = kc_aug.astype(BF16)
    for h in range(GROUP_HEADS):
        k = rope_c(_rms_lanes(c_ref[:, GROUP_WIDTH + h * hd:GROUP_WIDTH + (h + 1) * hd].astype(F32), gck[...]))
        kc_ref[:, h * wh:h * wh + hd] = k.astype(BF16)
        kc_ref[:, h * wh + hd:(h + 1) * wh] = kc_aug
        for blk in range(blocks_per_tile):
            km = jnp.mean(k[blk * MOBA_BLOCK:(blk + 1) * MOBA_BLOCK], axis=0, keepdims=True)
            nblk = t * blocks_per_tile + blk
            base = pl.multiple_of(h * hd + (nblk // 8) * 8, 8)
            old = kmt_scr[pl.ds(base, 8), h * hd:(h + 1) * hd]
            kmt_scr[pl.ds(base, 8), h * hd:(h + 1) * hd] = jnp.where(sub8 == nblk % 8, km, old)

    qs = []
    gcq_scaled = gcq[...] * (SCALE * LOG2E)
    for h in range(GROUP_HEADS):
        q = rope_c(_rms_lanes(c_ref[:, h * hd:(h + 1) * hd].astype(F32), gcq_scaled))
        qc_ref[:, h * wh:h * wh + hd] = q.astype(BF16)
        qs.append(q)
    q_all = jnp.concatenate(qs, axis=1)
    qh, qm, _ = _split3(q_all)
    kh, km_, _ = _split3(kmt_scr[...])
    gate_t = _dot_nt(kh, qh) + _dot_nt(km_, qh) + _dot_nt(kh, qm)

    nb = MOBA_MAX_BLOCKS
    blk_row = lax.broadcasted_iota(jnp.int32, (nb, tm), 0)
    own = t * blocks_per_tile + lax.broadcasted_iota(jnp.int32, (nb, tm), 1) // MOBA_BLOCK
    blk_f = blk_row.astype(F32)
    ninf = jnp.float32(-jnp.inf)
    no_rows = jnp.zeros((hd - nb, tm), F32)
    for h in range(GROUP_HEADS):
        g = jnp.where(blk_row < own, gate_t[h * hd:h * hd + nb, :], ninf)
        sel = jnp.where(blk_row == own, 1.0, 0.0)
        for _ in range(MOBA_TOP_K):
            mx = jnp.max(g, axis=0, keepdims=True)
            cand = jnp.where(g == mx, blk_f, 1e9)
            cand = jnp.where(mx > ninf, cand, 1e9)
            idx = jnp.min(cand, axis=0, keepdims=True)
            pick = blk_f == idx
            sel = jnp.where(pick, 1.0, sel)
            g = jnp.where(pick, ninf, g)
        sel_rows = jnp.concatenate([sel, no_rows], axis=0).T
        q_aug = jnp.where(lane == nb, 1.0, sel_rows)
        qc_ref[:, h * wh + hd:(h + 1) * wh] = q_aug.astype(BF16)

    cd = cosd[...]
    sd = sind[...]
    free_lane = lane == MLA_ROPE_DIM // 2

    def rope_d(v):
        return v * cd + pltpu.roll(v, hd // 2, axis=1) * sd

    cq = _rms_lanes(d_ref[:, 0:MLA_Q_RANK].astype(F32), gdcq[...]).astype(BF16)
    qf = _dot(cq, wuq[...])
    gdq_scaled = gdq[...] * (SCALE_MLA * LOG2E)
    for h in range(GROUP_HEADS):
        qn = _rms_lanes(qf[:, h * wh:(h + 1) * wh], gdq_scaled, denom=MLA_QK_DIM)
        qd_ref[:, h * wh:h * wh + hd] = qn[:, :hd].astype(BF16)
        qd_ref[:, h * wh + hd:(h + 1) * wh] = jnp.where(free_lane, 1.0, rope_d(qn[:, hd:])).astype(BF16)

    ckv = _rms_lanes(d_ref[:, MLA_Q_RANK:MLA_Q_RANK + MLA_KV_RANK].astype(F32), gdckv[...]).astype(BF16)
    knf = _dot(ckv, wuk[...])
    vtd_ref[...] = _dot_nt(wuvt[...], ckv).astype(BF16)
    kr = d_ref[:, MLA_Q_RANK + MLA_KV_RANK:MLA_Q_RANK + MLA_KV_RANK + hd].astype(F32)
    ss_r = _sumsq_lanes(kr)
    gk = gdk[...]
    for h in range(GROUP_HEADS):
        kn = knf[:, h * hd:(h + 1) * hd]
        r = lax.rsqrt((_sumsq_lanes(kn) + ss_r) * (1.0 / MLA_QK_DIM) + EPS)
        kd_ref[:, h * wh:h * wh + hd] = (kn * r * gk[:, :hd]).astype(BF16)
        k_rot = rope_d(kr * r * gk[:, hd:])
        kd_ref[:, h * wh + hd:(h + 1) * wh] = jnp.where(free_lane, -bound_d, k_rot).astype(BF16)


def _prep(proj, negc_col, bounds, gaq, gak, gcq, gck, gdcq, gdckv, gdq, gdk, wuq, wuk, wuvt,
          cosc, sinc, cosd, sind, layer, batch, seq):
    n = proj.shape[0]
    tm = TILE
    tpb = seq // tm
    assert seq // MOBA_BLOCK <= MOBA_MAX_BLOCKS
    kern = functools.partial(_prep_kernel, tiles_per_batch=tpb, tm=tm)
    row = lambda i: (i, 0)
    tab = lambda i: (i % tpb, 0)
    w2 = 2 * GROUP_WIDTH
    return pl.pallas_call(
        kern,
        grid=(n // tm,),
        in_specs=[
            pl.BlockSpec((tm, w2), lambda i: (i, 0)),
            pl.BlockSpec((tm, w2), lambda i: (i, 1)),
            pl.BlockSpec((tm, w2), lambda i: (i, 2)),
            pl.BlockSpec((tm, 8), row),
            _layer_spec((8, HEAD_DIM), layer),
            _layer_spec((1, HEAD_DIM), layer), _layer_spec((1, HEAD_DIM), layer),
            _layer_spec((1, HEAD_DIM), layer), _layer_spec((1, HEAD_DIM), layer),
            _layer_spec((1, MLA_Q_RANK), layer), _layer_spec((1, MLA_KV_RANK), layer),
            _layer_spec((1, WIDE_HEAD), layer), _layer_spec((1, WIDE_HEAD), layer),
            _layer_spec((MLA_Q_RANK, GROUP_HEADS * WIDE_HEAD), layer),
            _layer_spec((MLA_KV_RANK, GROUP_WIDTH), layer),
            _layer_spec((GROUP_WIDTH, MLA_KV_RANK), layer),
            pl.BlockSpec((tm, HEAD_DIM), tab), pl.BlockSpec((tm, HEAD_DIM), tab),
            pl.BlockSpec((tm, HEAD_DIM), tab), pl.BlockSpec((tm, HEAD_DIM), tab),
        ],
        out_specs=[pl.BlockSpec((tm, w2), row)] * 6 + [
            pl.BlockSpec((None, None, GROUP_WIDTH, tm), lambda i: (i // tpb, i % tpb, 0, 0)),
        ],
        out_shape=[jax.ShapeDtypeStruct((n, w2), BF16)] * 6 + [
            jax.ShapeDtypeStruct((batch, tpb, GROUP_WIDTH, tm), BF16),
        ],
        scratch_shapes=[pltpu.VMEM((GROUP_WIDTH, GROUP_WIDTH), F32)],
        compiler_params=pltpu.CompilerParams(
            dimension_semantics=("arbitrary",), vmem_limit_bytes=VMEM_LIMIT),
        name="prep",
    )(proj, proj, proj, negc_col, bounds, gaq, gak, gcq, gck, gdcq, gdckv, gdq, gdk, wuq, wuk, wuvt,
      cosc, sinc, cosd, sind)


def _key_query_iota(t):
    key = lax.broadcasted_iota(jnp.int32, (t, t), 0)
    qry = lax.broadcasted_iota(jnp.int32, (t, t), 1)
    return key, qry


def _softmax_step(s, vt, carry):
    m, acc = carry
    m_new = jnp.maximum(m, jnp.max(s, axis=0, keepdims=True))
    alpha = jnp.exp2(m - m_new)
    p = jnp.exp2(s - m_new)
    return m_new, alpha * acc + _dot(vt, p.astype(BF16))


def _softmax_init(t):
    return (jnp.full((1, t), NEG, F32), jnp.zeros((HEAD_DIM + SUM_ROWS, t), F32))


def _softmax_attn_kernel(q_ref, k_ref, vt_ref, o_ref, acc_scr, *, t, bounded):
    qi = pl.program_id(1)
    key, qry = _key_query_iota(t)
    causal = key <= qry
    dk = WIDE_HEAD
    dv = HEAD_DIM
    ones_rows = jnp.ones((SUM_ROWS, t), BF16)

    def logits(j, h, diag):
        off = pl.multiple_of(j * t, t)
        s = _dot_nt(k_ref[pl.ds(off, t), h * dk:(h + 1) * dk], q_ref[:, h * dk:(h + 1) * dk])
        return jnp.where(causal, s, NEG) if diag else s

    def values(j, h):
        return jnp.concatenate([vt_ref[j, h * dv:(h + 1) * dv, :], ones_rows], axis=0)

    if bounded:
        acc_scr[...] = jnp.zeros_like(acc_scr)

        def step(*tiles):
            s = [[logits(j, h, diag) for h in range(GROUP_HEADS)] for j, diag in tiles]
            for n, (j, _) in enumerate(tiles):
                for h in range(GROUP_HEADS):
                    acc_scr[h] += _dot(values(j, h), jnp.exp2(s[n][h]).astype(BF16))

        left = jnp.maximum(qi - 1, 0)

        def pair(jj, c):
            step((2 * jj, False), (2 * jj + 1, False))
            return c

        lax.fori_loop(0, left // 2, pair, 0)

        @pl.when(left % 2 == 1)
        def _():
            step((left - 1, False))

        @pl.when(qi == 0)
        def _():
            step((qi, True))

        @pl.when(qi > 0)
        def _():
            step((qi, True), (qi - 1, False))

        accs = [acc_scr[h] for h in range(GROUP_HEADS)]
    else:
        def step(j, carry, diag):
            out = []
            for h in range(GROUP_HEADS):
                out.append(_softmax_step(logits(j, h, diag), values(j, h), carry[h]))
            return tuple(out)

        carry = step(qi, tuple(_softmax_init(t) for _ in range(GROUP_HEADS)), True)
        carry = lax.fori_loop(0, qi, lambda j, c: step(j, c, False), carry)
        accs = [c[1] for c in carry]

    for h in range(GROUP_HEADS):
        a = accs[h]
        o_ref[:, h * dv:(h + 1) * dv] = (a[:dv] / a[dv:dv + 1]).T.astype(BF16)


def _softmax_attn(name, q_arr, k_arr, vt_arr, vt_group, batch, seq, bounded):
    t = TILE
    nq = seq // t
    n = batch * seq
    kern = functools.partial(_softmax_attn_kernel, t=t, bounded=bounded)
    if vt_group is None:
        vt_spec = pl.BlockSpec((None, nq, GROUP_WIDTH, t), lambda b, i: (b, 0, 0, 0))
    else:
        vt_spec = pl.BlockSpec((None, None, nq, GROUP_WIDTH, t), lambda b, i: (vt_group, b, 0, 0, 0))
    return pl.pallas_call(
        kern,
        grid=(batch, nq),
        in_specs=[
            pl.BlockSpec((t, GROUP_HEADS * WIDE_HEAD), lambda b, i: (b * nq + i, 0)),
            pl.BlockSpec((seq, GROUP_HEADS * WIDE_HEAD), lambda b, i: (b, 0)),
            vt_spec,
        ],
        out_specs=pl.BlockSpec((t, GROUP_WIDTH), lambda b, i: (b * nq + i, 0)),
        out_shape=jax.ShapeDtypeStruct((n, GROUP_WIDTH), BF16),
        scratch_shapes=[pltpu.VMEM((GROUP_HEADS, HEAD_DIM + SUM_ROWS, t), F32)],
        compiler_params=pltpu.CompilerParams(
            dimension_semantics=("parallel", "parallel"), vmem_limit_bytes=VMEM_LIMIT),
        name=name + ("" if bounded else "_online"),
    )(q_arr, k_arr, vt_arr)


def _bounded_or_online(bound, name, *args):
    return lax.cond(bound < MAX_SAFE_BOUND,
                    lambda: _softmax_attn(name, *args, bounded=True),
                    lambda: _softmax_attn(name, *args, bounded=False))


def _stickbreak_kernel(q_ref, k_ref, vt_ref, o_ref, acc_scr, rest_scr, *, t):
    qi = pl.program_id(1)
    key, qry = _key_query_iota(t)
    strict = key < qry
    later = jnp.where(qry > key, 1.0, 0.0).astype(BF16)
    hd = HEAD_DIM
    acc_scr[...] = jnp.zeros_like(acc_scr)
    rest_scr[...] = jnp.zeros_like(rest_scr)

    def step(tiles):
        qs = [(q_ref[:, h * hd:(h + 1) * hd].astype(F32) * (SCALE * LOG2E)).astype(BF16) for h in range(GROUP_HEADS)]
        zs, drops, afters = [], [], []
        for j, diag in tiles:
            off = pl.multiple_of(j * t, t)
            zs.append([_dot_nt(k_ref[pl.ds(off, t), h * hd:(h + 1) * hd], qs[h]) for h in range(GROUP_HEADS)])
        for n, (j, diag) in enumerate(tiles):
            row = []
            for h in range(GROUP_HEADS):
                z = zs[n][h]
                drop = jnp.maximum(z, 0.0) + jnp.log(1.0 + jnp.exp2(-jnp.abs(z))) * LOG2E
                row.append(jnp.where(strict, drop, 0.0) if diag else drop)
            drops.append(row)
            afters.append([_dot(later, d.astype(BF16)) for d in row])
        low = None
        for h in range(GROUP_HEADS):
            used = rest_scr[h]
            for n, (j, diag) in enumerate(tiles):
                z, drop, after = zs[n][h], drops[n][h], afters[n][h]
                w = jnp.exp2((z - drop) - (after + used))
                if diag:
                    w = jnp.where(strict, w, 0.0)
                acc_scr[h] += _dot(vt_ref[j, h * hd:(h + 1) * hd, :], w.astype(BF16))
                used = used + after[0:1] + drop[0:1]
            rest_scr[h] = used
            low = used if low is None else jnp.minimum(low, used)
        return jnp.min(low)

    def cond(c):
        j, low = c
        return jnp.logical_and(j >= 0, low < STICK_DONE)

    def body(c):
        j, _ = c
        return j - 1, step([(j, False)])

    first = lax.cond(qi == 0, lambda: step([(qi, True)]), lambda: step([(qi, True), (qi - 1, False)]))
    lax.while_loop(cond, body, (qi - 2, first))
    for h in range(GROUP_HEADS):
        o_ref[:, h * hd:(h + 1) * hd] = acc_scr[h].T.astype(BF16)


def _stickbreak_attn(proj, vtb, batch, seq):
    t = STICK_TILE
    nq = seq // t
    n = batch * seq
    kern = functools.partial(_stickbreak_kernel, t=t)
    return pl.pallas_call(
        kern,
        grid=(batch, nq),
        in_specs=[
            pl.BlockSpec((t, GROUP_WIDTH), lambda b, i: (b * nq + i, T_BQ)),
            pl.BlockSpec((seq, GROUP_WIDTH), lambda b, i: (b, T_BK)),
            pl.BlockSpec((None, nq, GROUP_WIDTH, t), lambda b, i: (b, 0, 0, 0)),
        ],
        out_specs=pl.BlockSpec((t, GROUP_WIDTH), lambda b, i: (b * nq + i, 0)),
        out_shape=jax.ShapeDtypeStruct((n, GROUP_WIDTH), BF16),
        scratch_shapes=[pltpu.VMEM((GROUP_HEADS, HEAD_DIM, t), F32), pltpu.VMEM((GROUP_HEADS, 1, t), F32)],
        compiler_params=pltpu.CompilerParams(
            dimension_semantics=("parallel", "parallel"), vmem_limit_bytes=VMEM_LIMIT),
        name="stickbreak_attn",
    )(proj, proj, vtb)


def _outproj_kernel(oa_ref, ob_ref, oc_ref, od_ref, g_ref, w_ref, x_ref, gf_ref, out_ref, hf_ref):
    ys = [_rms(o_ref[...].astype(F32), g_ref[gi:gi + 1, :]).astype(BF16)
          for gi, o_ref in enumerate((oa_ref, ob_ref, oc_ref, od_ref))]
    acc = x_ref[...] + _dot(jnp.concatenate(ys, axis=1), w_ref[...])
    out_ref[...] = acc
    hf_ref[...] = _rms(acc, gf_ref[...]).astype(BF16)


def _outproj(oa, ob, oc, od, g, w, x2, g_ffn, layer):
    n = x2.shape[0]
    tm = 512
    row = lambda i: (i, 0)
    return pl.pallas_call(
        _outproj_kernel,
        grid=(n // tm,),
        in_specs=[pl.BlockSpec((tm, GROUP_WIDTH), row)] * 4 + [
            _layer_spec((N_GROUPS, GROUP_WIDTH), layer),
            _layer_spec((N_GROUPS * GROUP_WIDTH, D_MODEL), layer, pipeline_mode=pl.Buffered(1)),
            pl.BlockSpec((tm, D_MODEL), row),
            _layer_spec((1, D_MODEL), layer),
        ],
        out_specs=[pl.BlockSpec((tm, D_MODEL), row), pl.BlockSpec((tm, D_MODEL), row)],
        out_shape=[jax.ShapeDtypeStruct((n, D_MODEL), F32), jax.ShapeDtypeStruct((n, D_MODEL), BF16)],
        compiler_params=pltpu.CompilerParams(
            dimension_semantics=("parallel",), vmem_limit_bytes=VMEM_LIMIT),
        name="outproj",
    )(oa, ob, oc, od, g, w, x2, g_ffn)


def _ffn_up_kernel(h_ref, wg_ref, wv_ref, cwg_ref, cwv_ref, cbg_ref, cbv_ref,
                   out_ref, wg_scr, wv_scr, tail_g, tail_v, *, tiles_per_batch, tm):
    i = pl.program_id(1)

    @pl.when(i == 0)
    def _():
        wg_scr[...] = wg_ref[...].astype(BF16)
        wv_scr[...] = wv_ref[...].astype(BF16)

    @pl.when(i % tiles_per_batch == 0)
    def _():
        tail_g[...] = jnp.zeros_like(tail_g)
        tail_v[...] = jnp.zeros_like(tail_v)

    def conv(u, cw_ref, cb_ref, prev):
        ue = jnp.concatenate([prev, u], axis=0)
        y = cb_ref[...] + cw_ref[2:3, :] * u
        for back in range(1, CONV_WIDTH):
            y = y + cw_ref[CONV_WIDTH - 1 - back:CONV_WIDTH - back, :] * pltpu.roll(ue, back, axis=0)[8:]
        return y, u[u.shape[0] - 8:]

    h = h_ref[...]
    u_g = _dot(h, wg_scr[...])
    u_v = _dot(h, wv_scr[...])
    gate, tail_g[...] = conv(u_g, cwg_ref, cbg_ref, tail_g[...])
    val, tail_v[...] = conv(u_v, cwv_ref, cbv_ref, tail_v[...])
    out_ref[...] = (gate / (1.0 + jnp.exp(-gate)) * val).astype(BF16)


def _ffn_up(h, w_up, conv_w, conv_b, layer, seq):
    n = h.shape[0]
    tm = FFN_TOKENS
    tn = 512
    nj = D_FF // tn
    tpb = seq // tm
    kern = functools.partial(_ffn_up_kernel, tiles_per_batch=tpb, tm=tm)
    return pl.pallas_call(
        kern,
        grid=(nj, n // tm),
        in_specs=[
            pl.BlockSpec((tm, D_MODEL), lambda j, i: (i, 0)),
            pl.BlockSpec((None, D_MODEL, tn), lambda j, i: (layer, 0, j)),
            pl.BlockSpec((None, D_MODEL, tn), lambda j, i: (layer, 0, j + nj)),
            pl.BlockSpec((None, CONV_WIDTH, tn), lambda j, i: (layer, 0, j)),
            pl.BlockSpec((None, CONV_WIDTH, tn), lambda j, i: (layer, 0, j + nj)),
            pl.BlockSpec((None, 1, tn), lambda j, i: (layer, 0, j)),
            pl.BlockSpec((None, 1, tn), lambda j, i: (layer, 0, j + nj)),
        ],
        out_specs=pl.BlockSpec((tm, tn), lambda j, i: (i, j)),
        out_shape=jax.ShapeDtypeStruct((n, D_FF), BF16),
        scratch_shapes=[pltpu.VMEM((D_MODEL, tn), BF16), pltpu.VMEM((D_MODEL, tn), BF16),
                        pltpu.VMEM((8, tn), F32), pltpu.VMEM((8, tn), F32)],
        compiler_params=pltpu.CompilerParams(
            dimension_semantics=("arbitrary", "arbitrary"), vmem_limit_bytes=VMEM_LIMIT),
        name="ffn_up",
    )(h, w_up, w_up, conv_w, conv_w, conv_b, conv_b)


def _ffn_down_kernel(a_ref, w_ref, x_ref, out_ref):
    a = a_ref[...]
    tn = 2 * MXU_WIDTH
    for c in range(D_MODEL // tn):
        cols = slice(c * tn, (c + 1) * tn)
        out_ref[:, cols] = x_ref[:, cols] + _dot(a, w_ref[:, cols])


def _ffn_down(act, w_down, x2, layer):
    n = x2.shape[0]
    tm = 512
    row = lambda i: (i, 0)
    return pl.pallas_call(
        _ffn_down_kernel,
        grid=(n // tm,),
        in_specs=[
            pl.BlockSpec((tm, D_FF), row),
            _layer_spec((D_FF, D_MODEL), layer, pipeline_mode=pl.Buffered(1)),
            pl.BlockSpec((tm, D_MODEL), row),
        ],
        out_specs=pl.BlockSpec((tm, D_MODEL), row),
        out_shape=jax.ShapeDtypeStruct((n, D_MODEL), F32),
        compiler_params=pltpu.CompilerParams(
            dimension_semantics=("parallel",), vmem_limit_bytes=VMEM_LIMIT),
        name="ffn_down",
    )(act, w_down, x2)


def _rope_tables(seq):
    def angles(dim):
        inv_freq = ROPE_THETA ** (-np.arange(0, dim, 2, dtype=np.float64) / dim)
        ang = np.arange(seq, dtype=np.float64)[:, None] * inv_freq[None, :]
        return np.cos(ang), np.sin(ang)

    c, s = angles(HEAD_DIM)
    cosc = np.concatenate([c, c], axis=1)
    sinc = np.concatenate([-s, s], axis=1)
    c, s = angles(MLA_ROPE_DIM)
    z = np.zeros_like(c)
    cosd = np.concatenate([c, z, c, z], axis=1)
    sind = np.concatenate([-s, z, s, z], axis=1)
    return tuple(jnp.asarray(t, F32) for t in (cosc, sinc, cosd, sind))


def _spread_rope(t):
    half = MLA_ROPE_DIM // 2
    z = jnp.zeros(t.shape[:-1] + (half,), t.dtype)
    return jnp.concatenate([t[..., :half], z, t[..., half:], z], axis=-1)


IN_OFFSETS = {}
_o = 0
for _name, _width in (("aq", GROUP_WIDTH), ("ak", GROUP_WIDTH), ("av", GROUP_WIDTH), ("af", GROUP_HEADS),
                      ("bq", GROUP_WIDTH), ("bk", GROUP_WIDTH), ("bv", GROUP_WIDTH),
                      ("cq", GROUP_WIDTH), ("ck", GROUP_WIDTH), ("cv", GROUP_WIDTH),
                      ("dcq", MLA_Q_RANK), ("dckv", MLA_KV_RANK), ("dkr", MLA_ROPE_DIM)):
    IN_OFFSETS[_name] = (_o, _width)
    _o += _width
IN_WIDTH = _o
RELAYOUT_ROWS = 256
FORGET_LANE = MLA_KV_RANK + HEAD_DIM


def _relayout_kernel(w_ref, kr_ref, main_ref, wvt_ref):
    def piece(name):
        o, width = IN_OFFSETS[name]
        return w_ref[:, o:o + width]

    gw = GROUP_WIDTH
    for tile, name in ((T_AQ, "aq"), (T_AK, "ak"), (T_CQ, "cq"), (T_CK, "ck"), (T_DCQ, "dcq"),
                       (T_BQ, "bq"), (T_BK, "bk")):
        main_ref[:, tile * gw:(tile + 1) * gw] = piece(name).astype(BF16)
    base = T_DKV * gw
    main_ref[:, base:base + MLA_KV_RANK] = piece("dckv").astype(BF16)
    main_ref[:, base + MLA_KV_RANK:base + MLA_KV_RANK + HEAD_DIM] = kr_ref[...].astype(BF16)
    main_ref[:, base + FORGET_LANE:base + gw] = jnp.zeros((main_ref.shape[0], gw - FORGET_LANE), BF16)
    main_ref[:, base + FORGET_LANE:base + FORGET_LANE + GROUP_HEADS] = piece("af").astype(BF16)
    for gi, name in enumerate(("av", "bv", "cv")):
        wvt_ref[gi] = piece(name).T.astype(BF16)


def _layout_w_in(w):
    depth, d_model, _ = w.shape
    rb = RELAYOUT_ROWS
    o, width = IN_OFFSETS["dkr"]
    kr = _spread_rope(w[..., o:o + width])
    w_main, wvt = pl.pallas_call(
        _relayout_kernel,
        grid=(depth, d_model // rb),
        in_specs=[
            pl.BlockSpec((None, rb, IN_WIDTH), lambda l, r: (l, r, 0)),
            pl.BlockSpec((None, rb, HEAD_DIM), lambda l, r: (l, r, 0)),
        ],
        out_specs=[
            pl.BlockSpec((None, rb, PROJ_TILES * GROUP_WIDTH), lambda l, r: (l, r, 0)),
            pl.BlockSpec((None, V_GROUPS, GROUP_WIDTH, rb), lambda l, r: (l, 0, 0, r)),
        ],
        out_shape=[
            jax.ShapeDtypeStruct((depth, d_model, PROJ_TILES * GROUP_WIDTH), BF16),
            jax.ShapeDtypeStruct((depth, V_GROUPS, GROUP_WIDTH, d_model), BF16),
        ],
        compiler_params=pltpu.CompilerParams(
            dimension_semantics=("parallel", "parallel"), vmem_limit_bytes=VMEM_LIMIT),
        name="relayout_w_in",
    )(w, kr)
    return w_main, wvt


def _layout_w_uq(w):
    w = w.reshape(-1, MLA_Q_RANK, GROUP_HEADS, MLA_QK_DIM)
    w = jnp.concatenate([w[..., :MLA_NOPE_DIM], _spread_rope(w[..., MLA_NOPE_DIM:])], axis=-1)
    return w.reshape(-1, MLA_Q_RANK, GROUP_HEADS * WIDE_HEAD).astype(BF16)


def _layout_w_ukv(w):
    w = w.reshape(-1, MLA_KV_RANK, GROUP_HEADS, 2, HEAD_DIM)
    wk = w[:, :, :, 0, :].reshape(-1, MLA_KV_RANK, GROUP_WIDTH).astype(BF16)
    wvt = jnp.swapaxes(w[:, :, :, 1, :].reshape(-1, MLA_KV_RANK, GROUP_WIDTH), -1, -2).astype(BF16)
    return wk, wvt


def _logit_bound(gq, gk, dim, scale):
    return dim * scale * LOG2E * jnp.max(jnp.abs(gq), axis=-1) * jnp.max(jnp.abs(gk), axis=-1) * 1.01 + 0.5


def _layout_qk_gain(g):
    return jnp.concatenate([g[:, :MLA_NOPE_DIM], _spread_rope(g[:, MLA_NOPE_DIM:])], axis=-1)[:, None, :]


def kernel(x, attn_norm, w_in, b_forget, fox_q_norm, fox_k_norm, moba_q_norm, moba_k_norm, mla_cq_norm, mla_ckv_norm, w_uq, w_ukv, mla_q_norm, mla_k_norm, mix_out_norm, w_out, ffn_norm, w_up, conv_w, conv_b, w_down):
    batch, seq, d_model = x.shape
    depth = w_in.shape[0]
    assert d_model == D_MODEL and seq % TILE == 0 and seq % FFN_TOKENS == 0
    n = batch * seq
    x2 = x.reshape(n, d_model)
    cosc, sinc, cosd, sind = _rope_tables(seq)

    rows = lambda v: v[:, None, :]
    w_main, wvt = _layout_w_in(w_in)
    wuq = _layout_w_uq(w_uq)
    wuk, wuvt = _layout_w_ukv(w_ukv)
    w_out_bf = w_out.astype(BF16)
    w_down_bf = w_down.astype(BF16)
    bf = jnp.pad(jnp.broadcast_to(b_forget[:, :, None], (depth, GROUP_HEADS, 128)),
                 ((0, 0), (0, 8 - GROUP_HEADS), (0, 0)))
    bound_a = _logit_bound(fox_q_norm, fox_k_norm, HEAD_DIM, SCALE)
    bound_c = _logit_bound(moba_q_norm, moba_k_norm, HEAD_DIM, SCALE)
    bound_d = _logit_bound(mla_q_norm, mla_k_norm, MLA_QK_DIM, SCALE_MLA)
    big_c = jnp.maximum(MOBA_MIN_MASK, jnp.exp2(jnp.ceil(jnp.log2(4.0 * bound_c + 64.0))))
    bounds = jnp.stack([bound_a, bound_c, big_c, bound_d] + [jnp.zeros_like(bound_a)] * 4, axis=1)
    bounds = jnp.broadcast_to(bounds[:, :, None], (depth, 8, HEAD_DIM))
    gdq, gdk = _layout_qk_gain(mla_q_norm), _layout_qk_gain(mla_k_norm)
    g_mix = mix_out_norm.reshape(depth, N_GROUPS, GROUP_WIDTH)
    conv_b3 = conv_b[:, None, :]

    for l in range(depth):
        proj, vt_ac, vt_b, negc = _inproj(x2, rows(attn_norm), w_main, wvt, bf, l, batch, seq)
        negc_col = negc.transpose(0, 2, 1).reshape(n, 8)

        qa, ka, qc, kc, qd, kd, vtd = _prep(
            proj, negc_col, bounds, rows(fox_q_norm), rows(fox_k_norm), rows(moba_q_norm), rows(moba_k_norm),
            rows(mla_cq_norm), rows(mla_ckv_norm), gdq, gdk, wuq, wuk, wuvt,
            cosc, sinc, cosd, sind, l, batch, seq)

        o_a = _bounded_or_online(bound_a[l], "fox_attn", qa, ka, vt_ac, 0, batch, seq)
        o_b = _stickbreak_attn(proj, vt_b, batch, seq)
        o_c = _bounded_or_online(bound_c[l], "moba_attn", qc, kc, vt_ac, 1, batch, seq)
        o_d = _bounded_or_online(bound_d[l], "mla_attn", qd, kd, vtd, None, batch, seq)

        x2, h_ffn = _outproj(o_a, o_b, o_c, o_d, g_mix, w_out_bf, x2, rows(ffn_norm), l)

        act = _ffn_up(h_ffn, w_up, conv_w, conv_b3, l, seq)
        x2 = _ffn_down(act, w_down_bf, x2, l)

    return x2.reshape(batch, seq, d_model)
```

```python
import functools
import math

import numpy as np
import jax
import jax.numpy as jnp
from jax import lax
from jax.experimental import pallas as pl
from jax.experimental.pallas import tpu as pltpu

F32 = jnp.float32
BF16 = jnp.bfloat16

D_MODEL = 2048
HEAD_DIM = 128
GROUP_HEADS = 4
GROUP_WIDTH = GROUP_HEADS * HEAD_DIM
N_GROUPS = 4
MOBA_BLOCK = 256
MOBA_TOP_K = 3
MOBA_MAX_BLOCKS = 16
MLA_Q_RANK = 512
MLA_KV_RANK = 256
MLA_NOPE_DIM = 128
MLA_ROPE_DIM = 64
MLA_QK_DIM = MLA_NOPE_DIM + MLA_ROPE_DIM
WIDE_HEAD = 256
D_FF = 5632
CONV_WIDTH = 3
ROPE_THETA = 10000.0
EPS = 1e-6

T_AQ, T_AK, T_CQ, T_CK, T_DCQ, T_DKV, T_BQ, T_BK = range(8)
PROJ_TILES = 8
V_GROUPS = 3

TILE = 512
STICK_TILE = 256
NEG = -1e30
VMEM_LIMIT = 56 * 1024 * 1024
LOG2E = math.log2(math.e)
SCALE = HEAD_DIM ** -0.5
SCALE_MLA = MLA_QK_DIM ** -0.5
FFN_TOKENS = 1024
MXU_WIDTH = 256
SUM_ROWS = 16
MAX_SAFE_BOUND = 56.0
MOBA_MIN_MASK = 256.0
STICK_DONE = 160.0


def _dot(a, b):
    return jnp.dot(a, b, preferred_element_type=F32)


def _dot_nt(a, b):
    return lax.dot_general(a, b, (((1,), (1,)), ((), ())), preferred_element_type=F32)


def _split3(a):
    hi = a.astype(BF16)
    r1 = a - hi.astype(F32)
    mid = r1.astype(BF16)
    lo = (r1 - mid.astype(F32)).astype(BF16)
    return hi, mid, lo


def _log_sigmoid_pair(z):
    sp = jnp.log(1.0 + jnp.exp(-jnp.abs(z)))
    return jnp.minimum(z, 0.0) - sp, -jnp.maximum(z, 0.0) - sp


def _rms(v, g, denom=None):
    if denom is None:
        ms = jnp.mean(v * v, axis=-1, keepdims=True)
    else:
        ms = jnp.sum(v * v, axis=-1, keepdims=True) * (1.0 / denom)
    return v * lax.rsqrt(ms + EPS) * g


def _sumsq_lanes(v):
    return _dot((v * v).astype(BF16), jnp.ones((v.shape[-1], HEAD_DIM), BF16))


def _rms_lanes(v, g, denom=None):
    width = v.shape[-1]
    r = lax.rsqrt(_sumsq_lanes(v) * (1.0 / (denom or width)) + EPS)
    if width > HEAD_DIM:
        r = jnp.concatenate([r] * (width // HEAD_DIM), axis=1)
    return v * r * g


def _inproj_kernel(x_ref, g_ref, w_ref, wvt_ref, bf_ref, proj_ref, vt_ref, vtb_ref, negc_ref,
                   carry_scr, *, tiles_per_batch, tm):
    i = pl.program_id(0)
    gw = GROUP_WIDTH
    h = _rms(x_ref[...], g_ref[...]).astype(BF16)
    dkv = _dot(h, w_ref[:, T_DKV * gw:(T_DKV + 1) * gw])
    proj_ref[:, T_DKV * gw:(T_DKV + 1) * gw] = dkv.astype(BF16)
    z = dkv[:, FORGET_LANE:FORGET_LANE + HEAD_DIM].T[0:8] + bf_ref[:, 0:1]
    logf, _ = _log_sigmoid_pair(z)
    r = lax.broadcasted_iota(jnp.int32, (tm, tm), 0)
    c = lax.broadcasted_iota(jnp.int32, (tm, tm), 1)
    tri = jnp.where(r <= c, 1.0, 0.0).astype(BF16)
    parts = jnp.concatenate([p.astype(F32) for p in _split3(logf)], axis=0).astype(BF16)
    sums = _dot(parts, tri)
    cum = sums[0:8] + sums[8:16] + sums[16:24]

    @pl.when(i % tiles_per_batch == 0)
    def _():
        carry_scr[...] = jnp.zeros_like(carry_scr)

    cum = cum + carry_scr[:, 0:1]
    carry_scr[...] = jnp.broadcast_to(cum[:, tm - 1:tm], carry_scr.shape)
    negc_ref[...] = cum * (-LOG2E)

    for jt in range(PROJ_TILES):
        if jt != T_DKV:
            proj_ref[:, jt * gw:(jt + 1) * gw] = _dot(h, w_ref[:, jt * gw:(jt + 1) * gw]).astype(BF16)
    for gi in range(V_GROUPS):
        vt = _dot_nt(wvt_ref[gi], h).astype(BF16)
        if gi == 1:
            for part in range(tm // STICK_TILE):
                vtb_ref[part] = vt[:, part * STICK_TILE:(part + 1) * STICK_TILE]
        else:
            vt_ref[gi // 2] = vt


def _layer_spec(shape, layer, **kw):
    zeros = (0,) * len(shape)
    return pl.BlockSpec((None,) + tuple(shape), lambda *_: (layer,) + zeros, **kw)


def _inproj(x2, g, w, wvt, bf, layer, batch, seq):
    n = x2.shape[0]
    tm = TILE
    tpb = seq // tm
    kern = functools.partial(_inproj_kernel, tiles_per_batch=tpb, tm=tm)
    resident = pl.Buffered(1)
    return pl.pallas_call(
        kern,
        grid=(n // tm,),
        in_specs=[
            pl.BlockSpec((tm, D_MODEL), lambda i: (i, 0)),
            _layer_spec((1, D_MODEL), layer),
            _layer_spec((D_MODEL, PROJ_TILES * GROUP_WIDTH), layer, pipeline_mode=resident),
            _layer_spec((V_GROUPS, GROUP_WIDTH, D_MODEL), layer, pipeline_mode=resident),
            _layer_spec((8, 128), layer),
        ],
        out_specs=[
            pl.BlockSpec((tm, PROJ_TILES * GROUP_WIDTH), lambda i: (i, 0)),
            pl.BlockSpec((2, None, None, GROUP_WIDTH, tm), lambda i: (0, i // tpb, i % tpb, 0, 0)),
            pl.BlockSpec((None, tm // STICK_TILE, GROUP_WIDTH, STICK_TILE), lambda i: (i // tpb, i % tpb, 0, 0)),
            pl.BlockSpec((None, 8, tm), lambda i: (i // tpb, 0, i % tpb)),
        ],
        out_shape=[
            jax.ShapeDtypeStruct((n, PROJ_TILES * GROUP_WIDTH), BF16),
            jax.ShapeDtypeStruct((2, batch, tpb, GROUP_WIDTH, tm), BF16),
            jax.ShapeDtypeStruct((batch, seq // STICK_TILE, GROUP_WIDTH, STICK_TILE), BF16),
            jax.ShapeDtypeStruct((batch, 8, seq), F32),
        ],
        scratch_shapes=[pltpu.VMEM((8, 128), F32)],
        compiler_params=pltpu.CompilerParams(
            dimension_semantics=("arbitrary",), vmem_limit_bytes=VMEM_LIMIT),
        name="inproj",
    )(x2, g, w, wvt, bf)


def _prep_kernel(a_ref, c_ref, d_ref, nc_ref, bnd_ref, gaq, gak, gcq, gck, gdcq, gdckv, gdq, gdk, wuq, wuk, wuvt,
                 cosc, sinc, cosd, sind, wo_ref,
                 qa_ref, ka_ref, qc_ref, kc_ref, qd_ref, kd_ref, vtd_ref, wo_out_ref, kmt_scr,
                 *, tiles_per_batch, tm):
    i = pl.program_id(0)
    t = i % tiles_per_batch
    hd = HEAD_DIM
    wh = WIDE_HEAD
    blocks_per_tile = tm // MOBA_BLOCK
    lane = lax.broadcasted_iota(jnp.int32, (tm, hd), 1)
    bound_a = bnd_ref[0:1, :]
    bound_c = bnd_ref[1:2, :]
    big_c = bnd_ref[2:3, :]
    bound_d = bnd_ref[3:4, :]

    his, mids, los = (p.astype(F32) for p in _split3(nc_ref[...]))
    lane1 = lane[0:1]
    k_const = jnp.where(lane1 < 6, 1.0, jnp.where(lane1 == 6, -bound_a, 0.0))
    q_const = jnp.where(lane1 < 3, 1.0, jnp.where(lane1 == 6, 1.0, 0.0))
    gaq_scaled = gaq[...] * (SCALE * LOG2E)
    for h in range(GROUP_HEADS):
        q = a_ref[:, h * hd:(h + 1) * hd].astype(F32)
        qa_ref[:, h * wh:h * wh + hd] = _rms_lanes(q, gaq_scaled).astype(BF16)
        k = a_ref[:, GROUP_WIDTH + h * hd:GROUP_WIDTH + (h + 1) * hd].astype(F32)
        ka_ref[:, h * wh:h * wh + hd] = _rms_lanes(k, gak[...]).astype(BF16)
        hi, mid, lo = his[:, h:h + 1], mids[:, h:h + 1], los[:, h:h + 1]
        k_parts = jnp.where(lane == 0, hi, jnp.where(lane == 1, mid, lo))
        q_parts = jnp.where(lane == 3, -hi, jnp.where(lane == 4, -mid, -lo))
        k_aug = jnp.where(lane < 3, k_parts, k_const)
        q_aug = jnp.where(lane < 3, q_const, jnp.where(lane < 6, q_parts, q_const))
        ka_ref[:, h * wh + hd:(h + 1) * wh] = k_aug.astype(BF16)
        qa_ref[:, h * wh + hd:(h + 1) * wh] = q_aug.astype(BF16)

    @pl.when(i == 0)
    def _():
        kmt_scr[...] = jnp.zeros_like(kmt_scr)

    cc = cosc[...]
    sc = sinc[...]
    sub8 = lax.broadcasted_iota(jnp.int32, (8, hd), 0)
    key_blk = t * blocks_per_tile + lax.broadcasted_iota(jnp.int32, (tm, hd), 0) // MOBA_BLOCK

    def rope_c(v):
        return v * cc + pltpu.roll(v, hd // 2, axis=1) * sc

    kc_aug = jnp.where(lane == key_blk, big_c, jnp.where(lane == MOBA_MAX_BLOCKS, -(bound_c + big_c), 0.0))
    kc_aug = kc_aug.astype(BF16)
    for h in range(GROUP_HEADS):
        k = rope_c(_rms_lanes(c_ref[:, GROUP_WIDTH + h * hd:GROUP_WIDTH + (h + 1) * hd].astype(F32), gck[...]))
        kc_ref[:, h * wh:h * wh + hd] = k.astype(BF16)
        kc_ref[:, h * wh + hd:(h + 1) * wh] = kc_aug
        for blk in range(blocks_per_tile):
            km = jnp.mean(k[blk * MOBA_BLOCK:(blk + 1) * MOBA_BLOCK], axis=0, keepdims=True)
            nblk = t * blocks_per_tile + blk
            base = pl.multiple_of(h * hd + (nblk // 8) * 8, 8)
            old = kmt_scr[pl.ds(base, 8), h * hd:(h + 1) * hd]
            kmt_scr[pl.ds(base, 8), h * hd:(h + 1) * hd] = jnp.where(sub8 == nblk % 8, km, old)

    qs = []
    gcq_scaled = gcq[...] * (SCALE * LOG2E)
    for h in range(GROUP_HEADS):
        q = rope_c(_rms_lanes(c_ref[:, h * hd:(h + 1) * hd].astype(F32), gcq_scaled))
        qc_ref[:, h * wh:h * wh + hd] = q.astype(BF16)
        qs.append(q)
    q_all = jnp.concatenate(qs, axis=1)
    qh, qm, _ = _split3(q_all)
    kh, km_, _ = _split3(kmt_scr[...])
    gate_t = _dot_nt(kh, qh) + _dot_nt(km_, qh) + _dot_nt(kh, qm)

    nb = MOBA_MAX_BLOCKS
    blk_row = lax.broadcasted_iota(jnp.int32, (nb, tm), 0)
    own = t * blocks_per_tile + lax.broadcasted_iota(jnp.int32, (nb, tm), 1) // MOBA_BLOCK
    blk_f = blk_row.astype(F32)
    ninf = jnp.float32(-jnp.inf)
    no_rows = jnp.zeros((hd - nb, tm), F32)
    for h in range(GROUP_HEADS):
        g = jnp.where(blk_row < own, gate_t[h * hd:h * hd + nb, :], ninf)
        sel = jnp.where(blk_row == own, 1.0, 0.0)
        for _ in range(MOBA_TOP_K):
            mx = jnp.max(g, axis=0, keepdims=True)
            cand = jnp.where(g == mx, blk_f, 1e9)
            cand = jnp.where(mx > ninf, cand, 1e9)
            idx = jnp.min(cand, axis=0, keepdims=True)
            pick = blk_f == idx
            sel = jnp.where(pick, 1.0, sel)
            g = jnp.where(pick, ninf, g)
        sel_rows = jnp.concatenate([sel, no_rows], axis=0).T
        q_aug = jnp.where(lane == nb, 1.0, sel_rows)
        qc_ref[:, h * wh + hd:(h + 1) * wh] = q_aug.astype(BF16)

    cd = cosd[...]
    sd = sind[...]
    free_lane = lane == MLA_ROPE_DIM // 2

    def rope_d(v):
        return v * cd + pltpu.roll(v, hd // 2, axis=1) * sd

    cq = _rms_lanes(d_ref[:, 0:MLA_Q_RANK].astype(F32), gdcq[...]).astype(BF16)
    qf = _dot(cq, wuq[...])
    gdq_scaled = gdq[...] * (SCALE_MLA * LOG2E)
    for h in range(GROUP_HEADS):
        qn = _rms_lanes(qf[:, h * wh:(h + 1) * wh], gdq_scaled, denom=MLA_QK_DIM)
        qd_ref[:, h * wh:h * wh + hd] = qn[:, :hd].astype(BF16)
        qd_ref[:, h * wh + hd:(h + 1) * wh] = jnp.where(free_lane, 1.0, rope_d(qn[:, hd:])).astype(BF16)

    ckv = _rms_lanes(d_ref[:, MLA_Q_RANK:MLA_Q_RANK + MLA_KV_RANK].astype(F32), gdckv[...]).astype(BF16)
    knf = _dot(ckv, wuk[...])
    vtd_ref[...] = _dot_nt(wuvt[...], ckv).astype(BF16)
    wo_out_ref[...] = wo_ref[...].astype(BF16)
    kr = d_ref[:, MLA_Q_RANK + MLA_KV_RANK:MLA_Q_RANK + MLA_KV_RANK + hd].astype(F32)
    ss_r = _sumsq_lanes(kr)
    gk = gdk[...]
    for h in range(GROUP_HEADS):
        kn = knf[:, h * hd:(h + 1) * hd]
        r = lax.rsqrt((_sumsq_lanes(kn) + ss_r) * (1.0 / MLA_QK_DIM) + EPS)
        kd_ref[:, h * wh:h * wh + hd] = (kn * r * gk[:, :hd]).astype(BF16)
        k_rot = rope_d(kr * r * gk[:, hd:])
        kd_ref[:, h * wh + hd:(h + 1) * wh] = jnp.where(free_lane, -bound_d, k_rot).astype(BF16)


def _prep(proj, negc_col, bounds, gaq, gak, gcq, gck, gdcq, gdckv, gdq, gdk, wuq, wuk, wuvt,
          cosc, sinc, cosd, sind, w_out, layer, batch, seq):
    n = proj.shape[0]
    tm = TILE
    tpb = seq // tm
    assert seq // MOBA_BLOCK <= MOBA_MAX_BLOCKS
    d_mix = N_GROUPS * GROUP_WIDTH
    slab = d_mix // (n // tm)
    assert slab * (n // tm) == d_mix and slab % 16 == 0
    kern = functools.partial(_prep_kernel, tiles_per_batch=tpb, tm=tm)
    row = lambda i: (i, 0)
    tab = lambda i: (i % tpb, 0)
    w2 = 2 * GROUP_WIDTH
    return pl.pallas_call(
        kern,
        grid=(n // tm,),
        in_specs=[
            pl.BlockSpec((tm, w2), lambda i: (i, 0)),
            pl.BlockSpec((tm, w2), lambda i: (i, 1)),
            pl.BlockSpec((tm, w2), lambda i: (i, 2)),
            pl.BlockSpec((tm, 8), row),
            _layer_spec((8, HEAD_DIM), layer),
            _layer_spec((1, HEAD_DIM), layer), _layer_spec((1, HEAD_DIM), layer),
            _layer_spec((1, HEAD_DIM), layer), _layer_spec((1, HEAD_DIM), layer),
            _layer_spec((1, MLA_Q_RANK), layer), _layer_spec((1, MLA_KV_RANK), layer),
            _layer_spec((1, WIDE_HEAD), layer), _layer_spec((1, WIDE_HEAD), layer),
            _layer_spec((MLA_Q_RANK, GROUP_HEADS * WIDE_HEAD), layer),
            _layer_spec((MLA_KV_RANK, GROUP_WIDTH), layer),
            _layer_spec((GROUP_WIDTH, MLA_KV_RANK), layer),
            pl.BlockSpec((tm, HEAD_DIM), tab), pl.BlockSpec((tm, HEAD_DIM), tab),
            pl.BlockSpec((tm, HEAD_DIM), tab), pl.BlockSpec((tm, HEAD_DIM), tab),
            pl.BlockSpec((None, slab, D_MODEL), lambda i: (layer, i, 0)),
        ],
        out_specs=[pl.BlockSpec((tm, w2), row)] * 6 + [
            pl.BlockSpec((None, None, GROUP_WIDTH, tm), lambda i: (i // tpb, i % tpb, 0, 0)),
            pl.BlockSpec((slab, D_MODEL), row),
        ],
        out_shape=[jax.ShapeDtypeStruct((n, w2), BF16)] * 6 + [
            jax.ShapeDtypeStruct((batch, tpb, GROUP_WIDTH, tm), BF16),
            jax.ShapeDtypeStruct((d_mix, D_MODEL), BF16),
        ],
        scratch_shapes=[pltpu.VMEM((GROUP_WIDTH, GROUP_WIDTH), F32)],
        compiler_params=pltpu.CompilerParams(
            dimension_semantics=("arbitrary",), vmem_limit_bytes=VMEM_LIMIT),
        name="prep",
    )(proj, proj, proj, negc_col, bounds, gaq, gak, gcq, gck, gdcq, gdckv, gdq, gdk, wuq, wuk, wuvt,
      cosc, sinc, cosd, sind, w_out)


def _key_query_iota(t):
    key = lax.broadcasted_iota(jnp.int32, (t, t), 0)
    qry = lax.broadcasted_iota(jnp.int32, (t, t), 1)
    return key, qry


def _softmax_step(s, vt, carry):
    m, acc = carry
    m_new = jnp.maximum(m, jnp.max(s, axis=0, keepdims=True))
    alpha = jnp.exp2(m - m_new)
    p = jnp.exp2(s - m_new)
    return m_new, alpha * acc + _dot(vt, p.astype(BF16))


def _softmax_init(t):
    return (jnp.full((1, t), NEG, F32), jnp.zeros((HEAD_DIM + SUM_ROWS, t), F32))


def _softmax_attn_kernel(q_ref, k_ref, vt_ref, o_ref, acc_scr, *, t, bounded):
    qi = pl.program_id(1)
    key, qry = _key_query_iota(t)
    causal = key <= qry
    dk = WIDE_HEAD
    dv = HEAD_DIM
    ones_rows = jnp.ones((SUM_ROWS, t), BF16)

    def logits(j, h, diag):
        off = pl.multiple_of(j * t, t)
        s = _dot_nt(k_ref[pl.ds(off, t), h * dk:(h + 1) * dk], q_ref[:, h * dk:(h + 1) * dk])
        return jnp.where(causal, s, NEG) if diag else s

    def values(j, h):
        return jnp.concatenate([vt_ref[j, h * dv:(h + 1) * dv, :], ones_rows], axis=0)

    if bounded:
        acc_scr[...] = jnp.zeros_like(acc_scr)

        def step(*tiles):
            s = [[logits(j, h, diag) for h in range(GROUP_HEADS)] for j, diag in tiles]
            for n, (j, _) in enumerate(tiles):
                for h in range(GROUP_HEADS):
                    acc_scr[h] += _dot(values(j, h), jnp.exp2(s[n][h]).astype(BF16))

        left = jnp.maximum(qi - 1, 0)

        def pair(jj, c):
            step((2 * jj, False), (2 * jj + 1, False))
            return c

        lax.fori_loop(0, left // 2, pair, 0)

        @pl.when(left % 2 == 1)
        def _():
            step((left - 1, False))

        @pl.when(qi == 0)
        def _():
            step((qi, True))

        @pl.when(qi > 0)
        def _():
            step((qi, True), (qi - 1, False))

        accs = [acc_scr[h] for h in range(GROUP_HEADS)]
    else:
        def step(j, carry, diag):
            out = []
            for h in range(GROUP_HEADS):
                out.append(_softmax_step(logits(j, h, diag), values(j, h), carry[h]))
            return tuple(out)

        carry = step(qi, tuple(_softmax_init(t) for _ in range(GROUP_HEADS)), True)
        carry = lax.fori_loop(0, qi, lambda j, c: step(j, c, False), carry)
        accs = [c[1] for c in carry]

    for h in range(GROUP_HEADS):
        a = accs[h]
        o_ref[:, h * dv:(h + 1) * dv] = (a[:dv] / a[dv:dv + 1]).T.astype(BF16)


def _softmax_attn(name, q_arr, k_arr, vt_arr, vt_group, batch, seq, bounded):
    t = TILE
    nq = seq // t
    n = batch * seq
    kern = functools.partial(_softmax_attn_kernel, t=t, bounded=bounded)
    if vt_group is None:
        vt_spec = pl.BlockSpec((None, nq, GROUP_WIDTH, t), lambda b, i: (b, 0, 0, 0))
    else:
        vt_spec = pl.BlockSpec((None, None, nq, GROUP_WIDTH, t), lambda b, i: (vt_group, b, 0, 0, 0))
    return pl.pallas_call(
        kern,
        grid=(batch, nq),
        in_specs=[
            pl.BlockSpec((t, GROUP_HEADS * WIDE_HEAD), lambda b, i: (b * nq + i, 0)),
            pl.BlockSpec((seq, GROUP_HEADS * WIDE_HEAD), lambda b, i: (b, 0)),
            vt_spec,
        ],
        out_specs=pl.BlockSpec((t, GROUP_WIDTH), lambda b, i: (b * nq + i, 0)),
        out_shape=jax.ShapeDtypeStruct((n, GROUP_WIDTH), BF16),
        scratch_shapes=[pltpu.VMEM((GROUP_HEADS, HEAD_DIM + SUM_ROWS, t), F32)],
        compiler_params=pltpu.CompilerParams(
            dimension_semantics=("parallel", "parallel"), vmem_limit_bytes=VMEM_LIMIT),
        name=name + ("" if bounded else "_online"),
    )(q_arr, k_arr, vt_arr)


def _bounded_or_online(bound, name, *args):
    return lax.cond(bound < MAX_SAFE_BOUND,
                    lambda: _softmax_attn(name, *args, bounded=True),
                    lambda: _softmax_attn(name, *args, bounded=False))


def _stickbreak_kernel(q_ref, k_ref, vt_ref, o_ref, acc_scr, rest_scr, *, t):
    qi = pl.program_id(1)
    key, qry = _key_query_iota(t)
    strict = key < qry
    later = jnp.where(qry > key, 1.0, 0.0).astype(BF16)
    hd = HEAD_DIM
    acc_scr[...] = jnp.zeros_like(acc_scr)
    rest_scr[...] = jnp.zeros_like(rest_scr)

    def step(tiles):
        qs = [(q_ref[:, h * hd:(h + 1) * hd].astype(F32) * (SCALE * LOG2E)).astype(BF16) for h in range(GROUP_HEADS)]
        zs, drops, afters = [], [], []
        for j, diag in tiles:
            off = pl.multiple_of(j * t, t)
            zs.append([_dot_nt(k_ref[pl.ds(off, t), h * hd:(h + 1) * hd], qs[h]) for h in range(GROUP_HEADS)])
        for n, (j, diag) in enumerate(tiles):
            row = []
            for h in range(GROUP_HEADS):
                z = zs[n][h]
                drop = jnp.maximum(z, 0.0) + jnp.log(1.0 + jnp.exp2(-jnp.abs(z))) * LOG2E
                row.append(jnp.where(strict, drop, 0.0) if diag else drop)
            drops.append(row)
            afters.append([_dot(later, d.astype(BF16)) for d in row])
        low = None
        for h in range(GROUP_HEADS):
            used = rest_scr[h]
            for n, (j, diag) in enumerate(tiles):
                z, drop, after = zs[n][h], drops[n][h], afters[n][h]
                w = jnp.exp2((z - drop) - (after + used))
                if diag:
                    w = jnp.where(strict, w, 0.0)
                acc_scr[h] += _dot(vt_ref[j, h * hd:(h + 1) * hd, :], w.astype(BF16))
                used = used + after[0:1] + drop[0:1]
            rest_scr[h] = used
            low = used if low is None else jnp.minimum(low, used)
        return jnp.min(low)

    def cond(c):
        j, low = c
        return jnp.logical_and(j >= 0, low < STICK_DONE)

    def body(c):
        j, _ = c
        return j - 1, step([(j, False)])

    first = lax.cond(qi == 0, lambda: step([(qi, True)]), lambda: step([(qi, True), (qi - 1, False)]))
    lax.while_loop(cond, body, (qi - 2, first))
    for h in range(GROUP_HEADS):
        o_ref[:, h * hd:(h + 1) * hd] = acc_scr[h].T.astype(BF16)


def _stickbreak_attn(proj, vtb, batch, seq):
    t = STICK_TILE
    nq = seq // t
    n = batch * seq
    kern = functools.partial(_stickbreak_kernel, t=t)
    return pl.pallas_call(
        kern,
        grid=(batch, nq),
        in_specs=[
            pl.BlockSpec((t, GROUP_WIDTH), lambda b, i: (b * nq + i, T_BQ)),
            pl.BlockSpec((seq, GROUP_WIDTH), lambda b, i: (b, T_BK)),
            pl.BlockSpec((None, nq, GROUP_WIDTH, t), lambda b, i: (b, 0, 0, 0)),
        ],
        out_specs=pl.BlockSpec((t, GROUP_WIDTH), lambda b, i: (b * nq + i, 0)),
        out_shape=jax.ShapeDtypeStruct((n, GROUP_WIDTH), BF16),
        scratch_shapes=[pltpu.VMEM((GROUP_HEADS, HEAD_DIM, t), F32), pltpu.VMEM((GROUP_HEADS, 1, t), F32)],
        compiler_params=pltpu.CompilerParams(
            dimension_semantics=("parallel", "parallel"), vmem_limit_bytes=VMEM_LIMIT),
        name="stickbreak_attn",
    )(proj, proj, vtb)


def _outproj_kernel(oa_ref, ob_ref, oc_ref, od_ref, g_ref, w_ref, x_ref, gf_ref, out_ref, hf_ref):
    ys = [_rms(o_ref[...].astype(F32), g_ref[gi:gi + 1, :]).astype(BF16)
          for gi, o_ref in enumerate((oa_ref, ob_ref, oc_ref, od_ref))]
    acc = x_ref[...] + _dot(jnp.concatenate(ys, axis=1), w_ref[...])
    out_ref[...] = acc
    hf_ref[...] = _rms(acc, gf_ref[...]).astype(BF16)


def _outproj(oa, ob, oc, od, g, w, x2, g_ffn, layer):
    n = x2.shape[0]
    tm = 512
    row = lambda i: (i, 0)
    return pl.pallas_call(
        _outproj_kernel,
        grid=(n // tm,),
        in_specs=[pl.BlockSpec((tm, GROUP_WIDTH), row)] * 4 + [
            _layer_spec((N_GROUPS, GROUP_WIDTH), layer),
            pl.BlockSpec((N_GROUPS * GROUP_WIDTH, D_MODEL), lambda i: (0, 0), pipeline_mode=pl.Buffered(1)),
            pl.BlockSpec((tm, D_MODEL), row),
            _layer_spec((1, D_MODEL), layer),
        ],
        out_specs=[pl.BlockSpec((tm, D_MODEL), row), pl.BlockSpec((tm, D_MODEL), row)],
        out_shape=[jax.ShapeDtypeStruct((n, D_MODEL), F32), jax.ShapeDtypeStruct((n, D_MODEL), BF16)],
        compiler_params=pltpu.CompilerParams(
            dimension_semantics=("parallel",), vmem_limit_bytes=VMEM_LIMIT),
        name="outproj",
    )(oa, ob, oc, od, g, w, x2, g_ffn)


def _ffn_up_kernel(h_ref, wg_ref, wv_ref, cwg_ref, cwv_ref, cbg_ref, cbv_ref, wd_ref,
                   out_ref, wd_out_ref, wg_scr, wv_scr, tail_g, tail_v, *, tiles_per_batch, tm):
    i = pl.program_id(1)
    wd_out_ref[...] = wd_ref[...].astype(BF16)

    @pl.when(i == 0)
    def _():
        wg_scr[...] = wg_ref[...].astype(BF16)
        wv_scr[...] = wv_ref[...].astype(BF16)

    @pl.when(i % tiles_per_batch == 0)
    def _():
        tail_g[...] = jnp.zeros_like(tail_g)
        tail_v[...] = jnp.zeros_like(tail_v)

    def conv(u, cw_ref, cb_ref, prev):
        ue = jnp.concatenate([prev, u], axis=0)
        y = cb_ref[...] + cw_ref[2:3, :] * u
        for back in range(1, CONV_WIDTH):
            y = y + cw_ref[CONV_WIDTH - 1 - back:CONV_WIDTH - back, :] * pltpu.roll(ue, back, axis=0)[8:]
        return y, u[u.shape[0] - 8:]

    h = h_ref[...]
    u_g = _dot(h, wg_scr[...])
    u_v = _dot(h, wv_scr[...])
    gate, tail_g[...] = conv(u_g, cwg_ref, cbg_ref, tail_g[...])
    val, tail_v[...] = conv(u_v, cwv_ref, cbv_ref, tail_v[...])
    out_ref[...] = (gate / (1.0 + jnp.exp(-gate)) * val).astype(BF16)


def _ffn_up(h, w_up, conv_w, conv_b, w_down, layer, seq):
    n = h.shape[0]
    tm = FFN_TOKENS
    tn = 512
    nj = D_FF // tn
    nt = n // tm
    tpb = seq // tm
    slab = D_FF // (nj * nt)
    assert slab * nj * nt == D_FF and slab % 16 == 0
    kern = functools.partial(_ffn_up_kernel, tiles_per_batch=tpb, tm=tm)
    return pl.pallas_call(
        kern,
        grid=(nj, nt),
        in_specs=[
            pl.BlockSpec((tm, D_MODEL), lambda j, i: (i, 0)),
            pl.BlockSpec((None, D_MODEL, tn), lambda j, i: (layer, 0, j)),
            pl.BlockSpec((None, D_MODEL, tn), lambda j, i: (layer, 0, j + nj)),
            pl.BlockSpec((None, CONV_WIDTH, tn), lambda j, i: (layer, 0, j)),
            pl.BlockSpec((None, CONV_WIDTH, tn), lambda j, i: (layer, 0, j + nj)),
            pl.BlockSpec((None, 1, tn), lambda j, i: (layer, 0, j)),
            pl.BlockSpec((None, 1, tn), lambda j, i: (layer, 0, j + nj)),
            pl.BlockSpec((None, slab, D_MODEL), lambda j, i: (layer, j * nt + i, 0)),
        ],
        out_specs=[pl.BlockSpec((tm, tn), lambda j, i: (i, j)),
                   pl.BlockSpec((slab, D_MODEL), lambda j, i: (j * nt + i, 0))],
        out_shape=[jax.ShapeDtypeStruct((n, D_FF), BF16), jax.ShapeDtypeStruct((D_FF, D_MODEL), BF16)],
        scratch_shapes=[pltpu.VMEM((D_MODEL, tn), BF16), pltpu.VMEM((D_MODEL, tn), BF16),
                        pltpu.VMEM((8, tn), F32), pltpu.VMEM((8, tn), F32)],
        compiler_params=pltpu.CompilerParams(
            dimension_semantics=("arbitrary", "arbitrary"), vmem_limit_bytes=VMEM_LIMIT),
        name="ffn_up",
    )(h, w_up, w_up, conv_w, conv_w, conv_b, conv_b, w_down)


def _ffn_down_kernel(a_ref, w_ref, x_ref, out_ref):
    a = a_ref[...]
    tn = 2 * MXU_WIDTH
    for c in range(D_MODEL // tn):
        cols = slice(c * tn, (c + 1) * tn)
        out_ref[:, cols] = x_ref[:, cols] + _dot(a, w_ref[:, cols])


def _ffn_down(act, w_down, x2):
    n = x2.shape[0]
    tm = 512
    row = lambda i: (i, 0)
    return pl.pallas_call(
        _ffn_down_kernel,
        grid=(n // tm,),
        in_specs=[
            pl.BlockSpec((tm, D_FF), row),
            pl.BlockSpec((D_FF, D_MODEL), lambda i: (0, 0), pipeline_mode=pl.Buffered(1)),
            pl.BlockSpec((tm, D_MODEL), row),
        ],
        out_specs=pl.BlockSpec((tm, D_MODEL), row),
        out_shape=jax.ShapeDtypeStruct((n, D_MODEL), F32),
        compiler_params=pltpu.CompilerParams(
            dimension_semantics=("parallel",), vmem_limit_bytes=VMEM_LIMIT),
        name="ffn_down",
    )(act, w_down, x2)


def _rope_tables(seq):
    def angles(dim):
        inv_freq = ROPE_THETA ** (-np.arange(0, dim, 2, dtype=np.float64) / dim)
        ang = np.arange(seq, dtype=np.float64)[:, None] * inv_freq[None, :]
        return np.cos(ang), np.sin(ang)

    c, s = angles(HEAD_DIM)
    cosc = np.concatenate([c, c], axis=1)
    sinc = np.concatenate([-s, s], axis=1)
    c, s = angles(MLA_ROPE_DIM)
    z = np.zeros_like(c)
    cosd = np.concatenate([c, z, c, z], axis=1)
    sind = np.concatenate([-s, z, s, z], axis=1)
    return tuple(jnp.asarray(t, F32) for t in (cosc, sinc, cosd, sind))


def _spread_rope(t):
    half = MLA_ROPE_DIM // 2
    z = jnp.zeros(t.shape[:-1] + (half,), t.dtype)
    return jnp.concatenate([t[..., :half], z, t[..., half:], z], axis=-1)


IN_OFFSETS = {}
_o = 0
for _name, _width in (("aq", GROUP_WIDTH), ("ak", GROUP_WIDTH), ("av", GROUP_WIDTH), ("af", GROUP_HEADS),
                      ("bq", GROUP_WIDTH), ("bk", GROUP_WIDTH), ("bv", GROUP_WIDTH),
                      ("cq", GROUP_WIDTH), ("ck", GROUP_WIDTH), ("cv", GROUP_WIDTH),
                      ("dcq", MLA_Q_RANK), ("dckv", MLA_KV_RANK), ("dkr", MLA_ROPE_DIM)):
    IN_OFFSETS[_name] = (_o, _width)
    _o += _width
IN_WIDTH = _o
RELAYOUT_ROWS = 256
FORGET_LANE = MLA_KV_RANK + HEAD_DIM


def _relayout_kernel(w_ref, kr_ref, main_ref, wvt_ref):
    def piece(name):
        o, width = IN_OFFSETS[name]
        return w_ref[:, o:o + width]

    gw = GROUP_WIDTH
    for tile, name in ((T_AQ, "aq"), (T_AK, "ak"), (T_CQ, "cq"), (T_CK, "ck"), (T_DCQ, "dcq"),
                       (T_BQ, "bq"), (T_BK, "bk")):
        main_ref[:, tile * gw:(tile + 1) * gw] = piece(name).astype(BF16)
    base = T_DKV * gw
    main_ref[:, base:base + MLA_KV_RANK] = piece("dckv").astype(BF16)
    main_ref[:, base + MLA_KV_RANK:base + MLA_KV_RANK + HEAD_DIM] = kr_ref[...].astype(BF16)
    main_ref[:, base + FORGET_LANE:base + gw] = jnp.zeros((main_ref.shape[0], gw - FORGET_LANE), BF16)
    main_ref[:, base + FORGET_LANE:base + FORGET_LANE + GROUP_HEADS] = piece("af").astype(BF16)
    for gi, name in enumerate(("av", "bv", "cv")):
        wvt_ref[gi] = piece(name).T.astype(BF16)


def _layout_w_in(w):
    depth, d_model, _ = w.shape
    rb = RELAYOUT_ROWS
    o, width = IN_OFFSETS["dkr"]
    kr = _spread_rope(w[..., o:o + width])
    w_main, wvt = pl.pallas_call(
        _relayout_kernel,
        grid=(depth, d_model // rb),
        in_specs=[
            pl.BlockSpec((None, rb, IN_WIDTH), lambda l, r: (l, r, 0)),
            pl.BlockSpec((None, rb, HEAD_DIM), lambda l, r: (l, r, 0)),
        ],
        out_specs=[
            pl.BlockSpec((None, rb, PROJ_TILES * GROUP_WIDTH), lambda l, r: (l, r, 0)),
            pl.BlockSpec((None, V_GROUPS, GROUP_WIDTH, rb), lambda l, r: (l, 0, 0, r)),
        ],
        out_shape=[
            jax.ShapeDtypeStruct((depth, d_model, PROJ_TILES * GROUP_WIDTH), BF16),
            jax.ShapeDtypeStruct((depth, V_GROUPS, GROUP_WIDTH, d_model), BF16),
        ],
        compiler_params=pltpu.CompilerParams(
            dimension_semantics=("parallel", "parallel"), vmem_limit_bytes=VMEM_LIMIT),
        name="relayout_w_in",
    )(w, kr)
    return w_main, wvt


def _layout_w_uq(w):
    w = w.reshape(-1, MLA_Q_RANK, GROUP_HEADS, MLA_QK_DIM)
    w = jnp.concatenate([w[..., :MLA_NOPE_DIM], _spread_rope(w[..., MLA_NOPE_DIM:])], axis=-1)
    return w.reshape(-1, MLA_Q_RANK, GROUP_HEADS * WIDE_HEAD).astype(BF16)


def _layout_w_ukv(w):
    w = w.reshape(-1, MLA_KV_RANK, GROUP_HEADS, 2, HEAD_DIM)
    wk = w[:, :, :, 0, :].reshape(-1, MLA_KV_RANK, GROUP_WIDTH).astype(BF16)
    wvt = jnp.swapaxes(w[:, :, :, 1, :].reshape(-1, MLA_KV_RANK, GROUP_WIDTH), -1, -2).astype(BF16)
    return wk, wvt


def _logit_bound(gq, gk, dim, scale):
    return dim * scale * LOG2E * jnp.max(jnp.abs(gq), axis=-1) * jnp.max(jnp.abs(gk), axis=-1) * 1.01 + 0.5


def _layout_qk_gain(g):
    return jnp.concatenate([g[:, :MLA_NOPE_DIM], _spread_rope(g[:, MLA_NOPE_DIM:])], axis=-1)[:, None, :]


def kernel(x, attn_norm, w_in, b_forget, fox_q_norm, fox_k_norm, moba_q_norm, moba_k_norm, mla_cq_norm, mla_ckv_norm, w_uq, w_ukv, mla_q_norm, mla_k_norm, mix_out_norm, w_out, ffn_norm, w_up, conv_w, conv_b, w_down):
    batch, seq, d_model = x.shape
    depth = w_in.shape[0]
    assert d_model == D_MODEL and seq % TILE == 0 and seq % FFN_TOKENS == 0
    n = batch * seq
    x2 = x.reshape(n, d_model)
    cosc, sinc, cosd, sind = _rope_tables(seq)

    rows = lambda v: v[:, None, :]
    w_main, wvt = _layout_w_in(w_in)
    wuq = _layout_w_uq(w_uq)
    wuk, wuvt = _layout_w_ukv(w_ukv)
    bf = jnp.pad(jnp.broadcast_to(b_forget[:, :, None], (depth, GROUP_HEADS, 128)),
                 ((0, 0), (0, 8 - GROUP_HEADS), (0, 0)))
    bound_a = _logit_bound(fox_q_norm, fox_k_norm, HEAD_DIM, SCALE)
    bound_c = _logit_bound(moba_q_norm, moba_k_norm, HEAD_DIM, SCALE)
    bound_d = _logit_bound(mla_q_norm, mla_k_norm, MLA_QK_DIM, SCALE_MLA)
    big_c = jnp.maximum(MOBA_MIN_MASK, jnp.exp2(jnp.ceil(jnp.log2(4.0 * bound_c + 64.0))))
    bounds = jnp.stack([bound_a, bound_c, big_c, bound_d] + [jnp.zeros_like(bound_a)] * 4, axis=1)
    bounds = jnp.broadcast_to(bounds[:, :, None], (depth, 8, HEAD_DIM))
    gdq, gdk = _layout_qk_gain(mla_q_norm), _layout_qk_gain(mla_k_norm)
    g_mix = mix_out_norm.reshape(depth, N_GROUPS, GROUP_WIDTH)
    conv_b3 = conv_b[:, None, :]

    for l in range(depth):
        proj, vt_ac, vt_b, negc = _inproj(x2, rows(attn_norm), w_main, wvt, bf, l, batch, seq)
        negc_col = negc.transpose(0, 2, 1).reshape(n, 8)

        qa, ka, qc, kc, qd, kd, vtd, w_out_bf = _prep(
            proj, negc_col, bounds, rows(fox_q_norm), rows(fox_k_norm), rows(moba_q_norm), rows(moba_k_norm),
            rows(mla_cq_norm), rows(mla_ckv_norm), gdq, gdk, wuq, wuk, wuvt,
            cosc, sinc, cosd, sind, w_out, l, batch, seq)

        o_a = _bounded_or_online(bound_a[l], "fox_attn", qa, ka, vt_ac, 0, batch, seq)
        o_b = _stickbreak_attn(proj, vt_b, batch, seq)
        o_c = _bounded_or_online(bound_c[l], "moba_attn", qc, kc, vt_ac, 1, batch, seq)
        o_d = _bounded_or_online(bound_d[l], "mla_attn", qd, kd, vtd, None, batch, seq)

        x2, h_ffn = _outproj(o_a, o_b, o_c, o_d, g_mix, w_out_bf, x2, rows(ffn_norm), l)

        act, w_down_bf = _ffn_up(h_ffn, w_up, conv_w, conv_b3, w_down, l, seq)
        x2 = _ffn_down(act, w_down_bf, x2)

    return x2.reshape(batch, seq, d_model)
```

```python
import functools
import math

import numpy as np
import jax
import jax.numpy as jnp
from jax import lax
from jax.experimental import pallas as pl
from jax.experimental.pallas import tpu as pltpu

F32 = jnp.float32
BF16 = jnp.bfloat16

D_MODEL = 2048
HEAD_DIM = 128
GROUP_HEADS = 4
GROUP_WIDTH = GROUP_HEADS * HEAD_DIM
N_GROUPS = 4
MOBA_BLOCK = 256
MOBA_TOP_K = 3
MOBA_MAX_BLOCKS = 16
MLA_Q_RANK = 512
MLA_KV_RANK = 256
MLA_NOPE_DIM = 128
MLA_ROPE_DIM = 64
MLA_QK_DIM = MLA_NOPE_DIM + MLA_ROPE_DIM
WIDE_HEAD = 256
D_FF = 5632
CONV_WIDTH = 3
ROPE_THETA = 10000.0
EPS = 1e-6

T_AQ, T_AK, T_CQ, T_CK, T_DCQ, T_DKV, T_BQ, T_BK = range(8)
PROJ_TILES = 8
V_GROUPS = 3

TILE = 512
STICK_TILE = 256
NEG = -1e30
VMEM_LIMIT = 56 * 1024 * 1024
LOG2E = math.log2(math.e)
SCALE = HEAD_DIM ** -0.5
SCALE_MLA = MLA_QK_DIM ** -0.5
FFN_TOKENS = 1024
MXU_WIDTH = 256
SUM_ROWS = 16
MAX_SAFE_BOUND = 56.0
MOBA_MIN_MASK = 256.0
STICK_DONE = 160.0


def _dot(a, b):
    return jnp.dot(a, b, preferred_element_type=F32)


def _dot_nt(a, b):
    return lax.dot_general(a, b, (((1,), (1,)), ((), ())), preferred_element_type=F32)


def _split3(a):
    hi = a.astype(BF16)
    r1 = a - hi.astype(F32)
    mid = r1.astype(BF16)
    lo = (r1 - mid.astype(F32)).astype(BF16)
    return hi, mid, lo


def _log_sigmoid_pair(z):
    sp = jnp.log(1.0 + jnp.exp(-jnp.abs(z)))
    return jnp.minimum(z, 0.0) - sp, -jnp.maximum(z, 0.0) - sp


def _rms(v, g, denom=None):
    if denom is None:
        ms = jnp.mean(v * v, axis=-1, keepdims=True)
    else:
        ms = jnp.sum(v * v, axis=-1, keepdims=True) * (1.0 / denom)
    return v * lax.rsqrt(ms + EPS) * g


def _sumsq_lanes(v):
    return _dot((v * v).astype(BF16), jnp.ones((v.shape[-1], HEAD_DIM), BF16))


def _rms_lanes(v, g, denom=None):
    width = v.shape[-1]
    r = lax.rsqrt(_sumsq_lanes(v) * (1.0 / (denom or width)) + EPS)
    if width > HEAD_DIM:
        r = jnp.concatenate([r] * (width // HEAD_DIM), axis=1)
    return v * r * g


def _inproj_kernel(x_ref, g_ref, w_ref, wvt_ref, bf_ref, proj_ref, vt_ref, vtb_ref, negc_ref,
                   carry_scr, *, tiles_per_batch, tm):
    i = pl.program_id(0)
    gw = GROUP_WIDTH
    h = _rms(x_ref[...], g_ref[...]).astype(BF16)
    dkv = _dot(h, w_ref[:, T_DKV * gw:(T_DKV + 1) * gw])
    proj_ref[:, T_DKV * gw:(T_DKV + 1) * gw] = dkv.astype(BF16)
    z = dkv[:, FORGET_LANE:FORGET_LANE + HEAD_DIM].T[0:8] + bf_ref[:, 0:1]
    logf, _ = _log_sigmoid_pair(z)
    r = lax.broadcasted_iota(jnp.int32, (tm, tm), 0)
    c = lax.broadcasted_iota(jnp.int32, (tm, tm), 1)
    tri = jnp.where(r <= c, 1.0, 0.0).astype(BF16)
    parts = jnp.concatenate([p.astype(F32) for p in _split3(logf)], axis=0).astype(BF16)
    sums = _dot(parts, tri)
    cum = sums[0:8] + sums[8:16] + sums[16:24]

    @pl.when(i % tiles_per_batch == 0)
    def _():
        carry_scr[...] = jnp.zeros_like(carry_scr)

    cum = cum + carry_scr[:, 0:1]
    carry_scr[...] = jnp.broadcast_to(cum[:, tm - 1:tm], carry_scr.shape)
    nc = cum * (-LOG2E)
    nc = jnp.concatenate([nc, jnp.zeros((HEAD_DIM - 8, tm), F32)], axis=0).T
    negc_ref[...] = nc[:, 0:8]

    for jt in range(PROJ_TILES):
        if jt != T_DKV:
            proj_ref[:, jt * gw:(jt + 1) * gw] = _dot(h, w_ref[:, jt * gw:(jt + 1) * gw]).astype(BF16)
    for gi in range(V_GROUPS):
        vt = _dot_nt(wvt_ref[gi], h).astype(BF16)
        if gi == 1:
            for part in range(tm // STICK_TILE):
                vtb_ref[part] = vt[:, part * STICK_TILE:(part + 1) * STICK_TILE]
        else:
            vt_ref[gi // 2] = vt


def _layer_spec(shape, layer, **kw):
    zeros = (0,) * len(shape)
    return pl.BlockSpec((None,) + tuple(shape), lambda *_: (layer,) + zeros, **kw)


def _inproj(x2, g, w, wvt, bf, layer, batch, seq):
    n = x2.shape[0]
    tm = TILE
    tpb = seq // tm
    kern = functools.partial(_inproj_kernel, tiles_per_batch=tpb, tm=tm)
    resident = pl.Buffered(1)
    return pl.pallas_call(
        kern,
        grid=(n // tm,),
        in_specs=[
            pl.BlockSpec((tm, D_MODEL), lambda i: (i, 0)),
            _layer_spec((1, D_MODEL), layer),
            _layer_spec((D_MODEL, PROJ_TILES * GROUP_WIDTH), layer, pipeline_mode=resident),
            _layer_spec((V_GROUPS, GROUP_WIDTH, D_MODEL), layer, pipeline_mode=resident),
            _layer_spec((8, 128), layer),
        ],
        out_specs=[
            pl.BlockSpec((tm, PROJ_TILES * GROUP_WIDTH), lambda i: (i, 0)),
            pl.BlockSpec((2, None, None, GROUP_WIDTH, tm), lambda i: (0, i // tpb, i % tpb, 0, 0)),
            pl.BlockSpec((None, tm // STICK_TILE, GROUP_WIDTH, STICK_TILE), lambda i: (i // tpb, i % tpb, 0, 0)),
            pl.BlockSpec((tm, 8), lambda i: (i, 0)),
        ],
        out_shape=[
            jax.ShapeDtypeStruct((n, PROJ_TILES * GROUP_WIDTH), BF16),
            jax.ShapeDtypeStruct((2, batch, tpb, GROUP_WIDTH, tm), BF16),
            jax.ShapeDtypeStruct((batch, seq // STICK_TILE, GROUP_WIDTH, STICK_TILE), BF16),
            jax.ShapeDtypeStruct((n, 8), F32),
        ],
        scratch_shapes=[pltpu.VMEM((8, 128), F32)],
        compiler_params=pltpu.CompilerParams(
            dimension_semantics=("arbitrary",), vmem_limit_bytes=VMEM_LIMIT),
        name="inproj",
    )(x2, g, w, wvt, bf)


def _prep_kernel(a_ref, c_ref, d_ref, nc_ref, bnd_ref, gaq, gak, gcq, gck, gdcq, gdckv, gdq, gdk, wuq, wuk, wuvt,
                 cosc, sinc, cosd, sind, wo_ref,
                 qa_ref, ka_ref, qc_ref, kc_ref, qd_ref, kd_ref, vtd_ref, wo_out_ref, kmt_scr,
                 *, tiles_per_batch, tm):
    i = pl.program_id(0)
    t = i % tiles_per_batch
    hd = HEAD_DIM
    wh = WIDE_HEAD
    blocks_per_tile = tm // MOBA_BLOCK
    lane = lax.broadcasted_iota(jnp.int32, (tm, hd), 1)
    bound_a = bnd_ref[0:1, :]
    bound_c = bnd_ref[1:2, :]
    big_c = bnd_ref[2:3, :]
    bound_d = bnd_ref[3:4, :]

    his, mids, los = (p.astype(F32) for p in _split3(nc_ref[...]))
    lane1 = lane[0:1]
    k_const = jnp.where(lane1 < 6, 1.0, jnp.where(lane1 == 6, -bound_a, 0.0))
    q_const = jnp.where(lane1 < 3, 1.0, jnp.where(lane1 == 6, 1.0, 0.0))
    gaq_scaled = gaq[...] * (SCALE * LOG2E)
    for h in range(GROUP_HEADS):
        q = a_ref[:, h * hd:(h + 1) * hd].astype(F32)
        qa_ref[:, h * wh:h * wh + hd] = _rms_lanes(q, gaq_scaled).astype(BF16)
        k = a_ref[:, GROUP_WIDTH + h * hd:GROUP_WIDTH + (h + 1) * hd].astype(F32)
        ka_ref[:, h * wh:h * wh + hd] = _rms_lanes(k, gak[...]).astype(BF16)
        hi, mid, lo = his[:, h:h + 1], mids[:, h:h + 1], los[:, h:h + 1]
        k_parts = jnp.where(lane == 0, hi, jnp.where(lane == 1, mid, lo))
        q_parts = jnp.where(lane == 3, -hi, jnp.where(lane == 4, -mid, -lo))
        k_aug = jnp.where(lane < 3, k_parts, k_const)
        q_aug = jnp.where(lane < 3, q_const, jnp.where(lane < 6, q_parts, q_const))
        ka_ref[:, h * wh + hd:(h + 1) * wh] = k_aug.astype(BF16)
        qa_ref[:, h * wh + hd:(h + 1) * wh] = q_aug.astype(BF16)

    @pl.when(i == 0)
    def _():
        kmt_scr[...] = jnp.zeros_like(kmt_scr)

    cc = cosc[...]
    sc = sinc[...]
    sub8 = lax.broadcasted_iota(jnp.int32, (8, hd), 0)
    key_blk = t * blocks_per_tile + lax.broadcasted_iota(jnp.int32, (tm, hd), 0) // MOBA_BLOCK

    def rope_c(v):
        return v * cc + pltpu.roll(v, hd // 2, axis=1) * sc

    kc_aug = jnp.where(lane == key_blk, big_c, jnp.where(lane == MOBA_MAX_BLOCKS, -(bound_c + big_c), 0.0))
    kc_aug = kc_aug.astype(BF16)
    for h in range(GROUP_HEADS):
        k = rope_c(_rms_lanes(c_ref[:, GROUP_WIDTH + h * hd:GROUP_WIDTH + (h + 1) * hd].astype(F32), gck[...]))
        kc_ref[:, h * wh:h * wh + hd] = k.astype(BF16)
        kc_ref[:, h * wh + hd:(h + 1) * wh] = kc_aug
        for blk in range(blocks_per_tile):
            km = jnp.mean(k[blk * MOBA_BLOCK:(blk + 1) * MOBA_BLOCK], axis=0, keepdims=True)
            nblk = t * blocks_per_tile + blk
            base = pl.multiple_of(h * hd + (nblk // 8) * 8, 8)
            old = kmt_scr[pl.ds(base, 8), h * hd:(h + 1) * hd]
            kmt_scr[pl.ds(base, 8), h * hd:(h + 1) * hd] = jnp.where(sub8 == nblk % 8, km, old)

    qs = []
    gcq_scaled = gcq[...] * (SCALE * LOG2E)
    for h in range(GROUP_HEADS):
        q = rope_c(_rms_lanes(c_ref[:, h * hd:(h + 1) * hd].astype(F32), gcq_scaled))
        qc_ref[:, h * wh:h * wh + hd] = q.astype(BF16)
        qs.append(q)
    q_all = jnp.concatenate(qs, axis=1)
    qh, qm, _ = _split3(q_all)
    kh, km_, _ = _split3(kmt_scr[...])
    gate_t = _dot_nt(kh, qh) + _dot_nt(km_, qh) + _dot_nt(kh, qm)

    nb = MOBA_MAX_BLOCKS
    blk_row = lax.broadcasted_iota(jnp.int32, (nb, tm), 0)
    own = t * blocks_per_tile + lax.broadcasted_iota(jnp.int32, (nb, tm), 1) // MOBA_BLOCK
    blk_f = blk_row.astype(F32)
    ninf = jnp.float32(-jnp.inf)
    no_rows = jnp.zeros((hd - nb, tm), F32)
    for h in range(GROUP_HEADS):
        g = jnp.where(blk_row < own, gate_t[h * hd:h * hd + nb, :], ninf)
        sel = jnp.where(blk_row == own, 1.0, 0.0)
        for _ in range(MOBA_TOP_K):
            mx = jnp.max(g, axis=0, keepdims=True)
            cand = jnp.where(g == mx, blk_f, 1e9)
            cand = jnp.where(mx > ninf, cand, 1e9)
            idx = jnp.min(cand, axis=0, keepdims=True)
            pick = blk_f == idx
            sel = jnp.where(pick, 1.0, sel)
            g = jnp.where(pick, ninf, g)
        sel_rows = jnp.concatenate([sel, no_rows], axis=0).T
        q_aug = jnp.where(lane == nb, 1.0, sel_rows)
        qc_ref[:, h * wh + hd:(h + 1) * wh] = q_aug.astype(BF16)

    cd = cosd[...]
    sd = sind[...]
    free_lane = lane == MLA_ROPE_DIM // 2

    def rope_d(v):
        return v * cd + pltpu.roll(v, hd // 2, axis=1) * sd

    cq = _rms_lanes(d_ref[:, 0:MLA_Q_RANK].astype(F32), gdcq[...]).astype(BF16)
    qf = _dot(cq, wuq[...])
    gdq_scaled = gdq[...] * (SCALE_MLA * LOG2E)
    for h in range(GROUP_HEADS):
        qn = _rms_lanes(qf[:, h * wh:(h + 1) * wh], gdq_scaled, denom=MLA_QK_DIM)
        qd_ref[:, h * wh:h * wh + hd] = qn[:, :hd].astype(BF16)
        qd_ref[:, h * wh + hd:(h + 1) * wh] = jnp.where(free_lane, 1.0, rope_d(qn[:, hd:])).astype(BF16)

    ckv = _rms_lanes(d_ref[:, MLA_Q_RANK:MLA_Q_RANK + MLA_KV_RANK].astype(F32), gdckv[...]).astype(BF16)
    knf = _dot(ckv, wuk[...])
    vtd_ref[...] = _dot_nt(wuvt[...], ckv).astype(BF16)
    wo_out_ref[...] = wo_ref[...].astype(BF16)
    kr = d_ref[:, MLA_Q_RANK + MLA_KV_RANK:MLA_Q_RANK + MLA_KV_RANK + hd].astype(F32)
    ss_r = _sumsq_lanes(kr)
    gk = gdk[...]
    for h in range(GROUP_HEADS):
        kn = knf[:, h * hd:(h + 1) * hd]
        r = lax.rsqrt((_sumsq_lanes(kn) + ss_r) * (1.0 / MLA_QK_DIM) + EPS)
        kd_ref[:, h * wh:h * wh + hd] = (kn * r * gk[:, :hd]).astype(BF16)
        k_rot = rope_d(kr * r * gk[:, hd:])
        kd_ref[:, h * wh + hd:(h + 1) * wh] = jnp.where(free_lane, -bound_d, k_rot).astype(BF16)


def _prep(proj, negc_col, bounds, gaq, gak, gcq, gck, gdcq, gdckv, gdq, gdk, wuq, wuk, wuvt,
          cosc, sinc, cosd, sind, w_out, layer, batch, seq):
    n = proj.shape[0]
    tm = TILE
    tpb = seq // tm
    assert seq // MOBA_BLOCK <= MOBA_MAX_BLOCKS
    d_mix = N_GROUPS * GROUP_WIDTH
    slab = d_mix // (n // tm)
    assert slab * (n // tm) == d_mix and slab % 16 == 0
    kern = functools.partial(_prep_kernel, tiles_per_batch=tpb, tm=tm)
    row = lambda i: (i, 0)
    tab = lambda i: (i % tpb, 0)
    w2 = 2 * GROUP_WIDTH
    return pl.pallas_call(
        kern,
        grid=(n // tm,),
        in_specs=[
            pl.BlockSpec((tm, w2), lambda i: (i, 0)),
            pl.BlockSpec((tm, w2), lambda i: (i, 1)),
            pl.BlockSpec((tm, w2), lambda i: (i, 2)),
            pl.BlockSpec((tm, 8), row),
            _layer_spec((8, HEAD_DIM), layer),
            _layer_spec((1, HEAD_DIM), layer), _layer_spec((1, HEAD_DIM), layer),
            _layer_spec((1, HEAD_DIM), layer), _layer_spec((1, HEAD_DIM), layer),
            _layer_spec((1, MLA_Q_RANK), layer), _layer_spec((1, MLA_KV_RANK), layer),
            _layer_spec((1, WIDE_HEAD), layer), _layer_spec((1, WIDE_HEAD), layer),
            _layer_spec((MLA_Q_RANK, GROUP_HEADS * WIDE_HEAD), layer),
            _layer_spec((MLA_KV_RANK, GROUP_WIDTH), layer),
            _layer_spec((GROUP_WIDTH, MLA_KV_RANK), layer),
            pl.BlockSpec((tm, HEAD_DIM), tab), pl.BlockSpec((tm, HEAD_DIM), tab),
            pl.BlockSpec((tm, HEAD_DIM), tab), pl.BlockSpec((tm, HEAD_DIM), tab),
            pl.BlockSpec((None, slab, D_MODEL), lambda i: (layer, i, 0)),
        ],
        out_specs=[pl.BlockSpec((tm, w2), row)] * 6 + [
            pl.BlockSpec((None, None, GROUP_WIDTH, tm), lambda i: (i // tpb, i % tpb, 0, 0)),
            pl.BlockSpec((slab, D_MODEL), row),
        ],
        out_shape=[jax.ShapeDtypeStruct((n, w2), BF16)] * 6 + [
            jax.ShapeDtypeStruct((batch, tpb, GROUP_WIDTH, tm), BF16),
            jax.ShapeDtypeStruct((d_mix, D_MODEL), BF16),
        ],
        scratch_shapes=[pltpu.VMEM((GROUP_WIDTH, GROUP_WIDTH), F32)],
        compiler_params=pltpu.CompilerParams(
            dimension_semantics=("arbitrary",), vmem_limit_bytes=VMEM_LIMIT),
        name="prep",
    )(proj, proj, proj, negc_col, bounds, gaq, gak, gcq, gck, gdcq, gdckv, gdq, gdk, wuq, wuk, wuvt,
      cosc, sinc, cosd, sind, w_out)


def _key_query_iota(t):
    key = lax.broadcasted_iota(jnp.int32, (t, t), 0)
    qry = lax.broadcasted_iota(jnp.int32, (t, t), 1)
    return key, qry


def _softmax_step(s, vt, carry):
    m, acc = carry
    m_new = jnp.maximum(m, jnp.max(s, axis=0, keepdims=True))
    alpha = jnp.exp2(m - m_new)
    p = jnp.exp2(s - m_new)
    return m_new, alpha * acc + _dot(vt, p.astype(BF16))


def _softmax_init(t):
    return (jnp.full((1, t), NEG, F32), jnp.zeros((HEAD_DIM + SUM_ROWS, t), F32))


def _softmax_attn_kernel(q_ref, k_ref, vt_ref, o_ref, acc_scr, *, t, bounded):
    qi = pl.program_id(1)
    key, qry = _key_query_iota(t)
    causal = key <= qry
    dk = WIDE_HEAD
    dv = HEAD_DIM
    ones_rows = jnp.ones((SUM_ROWS, t), BF16)

    def logits(j, h, diag):
        off = pl.multiple_of(j * t, t)
        s = _dot_nt(k_ref[pl.ds(off, t), h * dk:(h + 1) * dk], q_ref[:, h * dk:(h + 1) * dk])
        return jnp.where(causal, s, NEG) if diag else s

    def values(j, h):
        return jnp.concatenate([vt_ref[j, h * dv:(h + 1) * dv, :], ones_rows], axis=0)

    if bounded:
        acc_scr[...] = jnp.zeros_like(acc_scr)

        def step(*tiles):
            s = [[logits(j, h, diag) for h in range(GROUP_HEADS)] for j, diag in tiles]
            for n, (j, _) in enumerate(tiles):
                for h in range(GROUP_HEADS):
                    acc_scr[h] += _dot(values(j, h), jnp.exp2(s[n][h]).astype(BF16))

        left = jnp.maximum(qi - 1, 0)

        def pair(jj, c):
            step((2 * jj, False), (2 * jj + 1, False))
            return c

        lax.fori_loop(0, left // 2, pair, 0)

        @pl.when(left % 2 == 1)
        def _():
            step((left - 1, False))

        @pl.when(qi == 0)
        def _():
            step((qi, True))

        @pl.when(qi > 0)
        def _():
            step((qi, True), (qi - 1, False))

        accs = [acc_scr[h] for h in range(GROUP_HEADS)]
    else:
        def step(j, carry, diag):
            out = []
            for h in range(GROUP_HEADS):
                out.append(_softmax_step(logits(j, h, diag), values(j, h), carry[h]))
            return tuple(out)

        carry = step(qi, tuple(_softmax_init(t) for _ in range(GROUP_HEADS)), True)
        carry = lax.fori_loop(0, qi, lambda j, c: step(j, c, False), carry)
        accs = [c[1] for c in carry]

    for h in range(GROUP_HEADS):
        a = accs[h]
        o_ref[:, h * dv:(h + 1) * dv] = (a[:dv] / a[dv:dv + 1]).T.astype(BF16)


def _softmax_attn(name, q_arr, k_arr, vt_arr, vt_group, batch, seq, bounded):
    t = TILE
    nq = seq // t
    n = batch * seq
    kern = functools.partial(_softmax_attn_kernel, t=t, bounded=bounded)
    if vt_group is None:
        vt_spec = pl.BlockSpec((None, nq, GROUP_WIDTH, t), lambda b, i: (b, 0, 0, 0))
    else:
        vt_spec = pl.BlockSpec((None, None, nq, GROUP_WIDTH, t), lambda b, i: (vt_group, b, 0, 0, 0))
    return pl.pallas_call(
        kern,
        grid=(batch, nq),
        in_specs=[
            pl.BlockSpec((t, GROUP_HEADS * WIDE_HEAD), lambda b, i: (b * nq + i, 0)),
            pl.BlockSpec((seq, GROUP_HEADS * WIDE_HEAD), lambda b, i: (b, 0)),
            vt_spec,
        ],
        out_specs=pl.BlockSpec((t, GROUP_WIDTH), lambda b, i: (b * nq + i, 0)),
        out_shape=jax.ShapeDtypeStruct((n, GROUP_WIDTH), BF16),
        scratch_shapes=[pltpu.VMEM((GROUP_HEADS, HEAD_DIM + SUM_ROWS, t), F32)],
        compiler_params=pltpu.CompilerParams(
            dimension_semantics=("parallel", "parallel"), vmem_limit_bytes=VMEM_LIMIT),
        name=name + ("" if bounded else "_online"),
    )(q_arr, k_arr, vt_arr)


def _bounded_or_online(bound, name, *args):
    return lax.cond(bound < MAX_SAFE_BOUND,
                    lambda: _softmax_attn(name, *args, bounded=True),
                    lambda: _softmax_attn(name, *args, bounded=False))


def _stickbreak_kernel(q_ref, k_ref, vt_ref, o_ref, acc_scr, rest_scr, *, t):
    qi = pl.program_id(1)
    key, qry = _key_query_iota(t)
    strict = key < qry
    later = jnp.where(qry > key, 1.0, 0.0).astype(BF16)
    hd = HEAD_DIM
    acc_scr[...] = jnp.zeros_like(acc_scr)
    rest_scr[...] = jnp.zeros_like(rest_scr)

    def step(tiles):
        qs = [(q_ref[:, h * hd:(h + 1) * hd].astype(F32) * (SCALE * LOG2E)).astype(BF16) for h in range(GROUP_HEADS)]
        zs, drops, afters = [], [], []
        for j, diag in tiles:
            off = pl.multiple_of(j * t, t)
            zs.append([_dot_nt(k_ref[pl.ds(off, t), h * hd:(h + 1) * hd], qs[h]) for h in range(GROUP_HEADS)])
        for n, (j, diag) in enumerate(tiles):
            row = []
            for h in range(GROUP_HEADS):
                z = zs[n][h]
                drop = jnp.maximum(z, 0.0) + jnp.log(1.0 + jnp.exp2(-jnp.abs(z))) * LOG2E
                row.append(jnp.where(strict, drop, 0.0) if diag else drop)
            drops.append(row)
            afters.append([_dot(later, d.astype(BF16)) for d in row])
        low = None
        for h in range(GROUP_HEADS):
            used = rest_scr[h]
            for n, (j, diag) in enumerate(tiles):
                z, drop, after = zs[n][h], drops[n][h], afters[n][h]
                w = jnp.exp2((z - drop) - (after + used))
                if diag:
                    w = jnp.where(strict, w, 0.0)
                acc_scr[h] += _dot(vt_ref[j, h * hd:(h + 1) * hd, :], w.astype(BF16))
                used = used + after[0:1] + drop[0:1]
            rest_scr[h] = used
            low = used if low is None else jnp.minimum(low, used)
        return jnp.min(low)

    def cond(c):
        j, low = c
        return jnp.logical_and(j >= 0, low < STICK_DONE)

    def body(c):
        j, _ = c
        return j - 1, step([(j, False)])

    first = lax.cond(qi == 0, lambda: step([(qi, True)]), lambda: step([(qi, True), (qi - 1, False)]))
    lax.while_loop(cond, body, (qi - 2, first))
    for h in range(GROUP_HEADS):
        o_ref[:, h * hd:(h + 1) * hd] = acc_scr[h].T.astype(BF16)


def _stickbreak_attn(proj, vtb, batch, seq):
    t = STICK_TILE
    nq = seq // t
    n = batch * seq
    kern = functools.partial(_stickbreak_kernel, t=t)
    return pl.pallas_call(
        kern,
        grid=(batch, nq),
        in_specs=[
            pl.BlockSpec((t, GROUP_WIDTH), lambda b, i: (b * nq + i, T_BQ)),
            pl.BlockSpec((seq, GROUP_WIDTH), lambda b, i: (b, T_BK)),
            pl.BlockSpec((None, nq, GROUP_WIDTH, t), lambda b, i: (b, 0, 0, 0)),
        ],
        out_specs=pl.BlockSpec((t, GROUP_WIDTH), lambda b, i: (b * nq + i, 0)),
        out_shape=jax.ShapeDtypeStruct((n, GROUP_WIDTH), BF16),
        scratch_shapes=[pltpu.VMEM((GROUP_HEADS, HEAD_DIM, t), F32), pltpu.VMEM((GROUP_HEADS, 1, t), F32)],
        compiler_params=pltpu.CompilerParams(
            dimension_semantics=("parallel", "parallel"), vmem_limit_bytes=VMEM_LIMIT),
        name="stickbreak_attn",
    )(proj, proj, vtb)


def _outproj_kernel(oa_ref, ob_ref, oc_ref, od_ref, g_ref, w_ref, x_ref, gf_ref, out_ref, hf_ref):
    ys = [_rms(o_ref[...].astype(F32), g_ref[gi:gi + 1, :]).astype(BF16)
          for gi, o_ref in enumerate((oa_ref, ob_ref, oc_ref, od_ref))]
    acc = x_ref[...] + _dot(jnp.concatenate(ys, axis=1), w_ref[...])
    out_ref[...] = acc
    hf_ref[...] = _rms(acc, gf_ref[...]).astype(BF16)


def _outproj(oa, ob, oc, od, g, w, x2, g_ffn, layer):
    n = x2.shape[0]
    tm = 512
    row = lambda i: (i, 0)
    return pl.pallas_call(
        _outproj_kernel,
        grid=(n // tm,),
        in_specs=[pl.BlockSpec((tm, GROUP_WIDTH), row)] * 4 + [
            _layer_spec((N_GROUPS, GROUP_WIDTH), layer),
            pl.BlockSpec((N_GROUPS * GROUP_WIDTH, D_MODEL), lambda i: (0, 0), pipeline_mode=pl.Buffered(1)),
            pl.BlockSpec((tm, D_MODEL), row),
            _layer_spec((1, D_MODEL), layer),
        ],
        out_specs=[pl.BlockSpec((tm, D_MODEL), row), pl.BlockSpec((tm, D_MODEL), row)],
        out_shape=[jax.ShapeDtypeStruct((n, D_MODEL), F32), jax.ShapeDtypeStruct((n, D_MODEL), BF16)],
        compiler_params=pltpu.CompilerParams(
            dimension_semantics=("parallel",), vmem_limit_bytes=VMEM_LIMIT),
        name="outproj",
    )(oa, ob, oc, od, g, w, x2, g_ffn)


def _ffn_up_kernel(h_ref, wg_ref, wv_ref, cwg_ref, cwv_ref, cbg_ref, cbv_ref, wd_ref,
                   out_ref, wd_out_ref, wg_scr, wv_scr, tail_g, tail_v, *, tiles_per_batch, tm):
    i = pl.program_id(1)
    wd_out_ref[...] = wd_ref[...].astype(BF16)

    @pl.when(i == 0)
    def _():
        wg_scr[...] = wg_ref[...].astype(BF16)
        wv_scr[...] = wv_ref[...].astype(BF16)

    @pl.when(i % tiles_per_batch == 0)
    def _():
        tail_g[...] = jnp.zeros_like(tail_g)
        tail_v[...] = jnp.zeros_like(tail_v)

    def conv(u, cw_ref, cb_ref, prev):
        ue = jnp.concatenate([prev, u], axis=0)
        y = cb_ref[...] + cw_ref[2:3, :] * u
        for back in range(1, CONV_WIDTH):
            y = y + cw_ref[CONV_WIDTH - 1 - back:CONV_WIDTH - back, :] * pltpu.roll(ue, back, axis=0)[8:]
        return y, u[u.shape[0] - 8:]

    h = h_ref[...]
    u_g = _dot(h, wg_scr[...])
    u_v = _dot(h, wv_scr[...])
    gate, tail_g[...] = conv(u_g, cwg_ref, cbg_ref, tail_g[...])
    val, tail_v[...] = conv(u_v, cwv_ref, cbv_ref, tail_v[...])
    out_ref[...] = (gate / (1.0 + jnp.exp(-gate)) * val).astype(BF16)


def _ffn_up(h, w_up, conv_w, conv_b, w_down, layer, seq):
    n = h.shape[0]
    tm = FFN_TOKENS
    tn = 512
    nj = D_FF // tn
    nt = n // tm
    tpb = seq // tm
    slab = D_FF // (nj * nt)
    assert slab * nj * nt == D_FF and slab % 16 == 0
    kern = functools.partial(_ffn_up_kernel, tiles_per_batch=tpb, tm=tm)
    return pl.pallas_call(
        kern,
        grid=(nj, nt),
        in_specs=[
            pl.BlockSpec((tm, D_MODEL), lambda j, i: (i, 0)),
            pl.BlockSpec((None, D_MODEL, tn), lambda j, i: (layer, 0, j)),
            pl.BlockSpec((None, D_MODEL, tn), lambda j, i: (layer, 0, j + nj)),
            pl.BlockSpec((None, CONV_WIDTH, tn), lambda j, i: (layer, 0, j)),
            pl.BlockSpec((None, CONV_WIDTH, tn), lambda j, i: (layer, 0, j + nj)),
            pl.BlockSpec((None, 1, tn), lambda j, i: (layer, 0, j)),
            pl.BlockSpec((None, 1, tn), lambda j, i: (layer, 0, j + nj)),
            pl.BlockSpec((None, slab, D_MODEL), lambda j, i: (layer, j * nt + i, 0)),
        ],
        out_specs=[pl.BlockSpec((tm, tn), lambda j, i: (i, j)),
                   pl.BlockSpec((slab, D_MODEL), lambda j, i: (j * nt + i, 0))],
        out_shape=[jax.ShapeDtypeStruct((n, D_FF), BF16), jax.ShapeDtypeStruct((D_FF, D_MODEL), BF16)],
        scratch_shapes=[pltpu.VMEM((D_MODEL, tn), BF16), pltpu.VMEM((D_MODEL, tn), BF16),
                        pltpu.VMEM((8, tn), F32), pltpu.VMEM((8, tn), F32)],
        compiler_params=pltpu.CompilerParams(
            dimension_semantics=("arbitrary", "arbitrary"), vmem_limit_bytes=VMEM_LIMIT),
        name="ffn_up",
    )(h, w_up, w_up, conv_w, conv_w, conv_b, conv_b, w_down)


def _ffn_down_kernel(a_ref, w_ref, x_ref, out_ref):
    a = a_ref[...]
    tn = 2 * MXU_WIDTH
    for c in range(D_MODEL // tn):
        cols = slice(c * tn, (c + 1) * tn)
        out_ref[:, cols] = x_ref[:, cols] + _dot(a, w_ref[:, cols])


def _ffn_down(act, w_down, x2):
    n = x2.shape[0]
    tm = 512
    row = lambda i: (i, 0)
    return pl.pallas_call(
        _ffn_down_kernel,
        grid=(n // tm,),
        in_specs=[
            pl.BlockSpec((tm, D_FF), row),
            pl.BlockSpec((D_FF, D_MODEL), lambda i: (0, 0), pipeline_mode=pl.Buffered(1)),
            pl.BlockSpec((tm, D_MODEL), row),
        ],
        out_specs=pl.BlockSpec((tm, D_MODEL), row),
        out_shape=jax.ShapeDtypeStruct((n, D_MODEL), F32),
        compiler_params=pltpu.CompilerParams(
            dimension_semantics=("parallel",), vmem_limit_bytes=VMEM_LIMIT),
        name="ffn_down",
    )(act, w_down, x2)


def _rope_tables(seq):
    def angles(dim):
        inv_freq = ROPE_THETA ** (-np.arange(0, dim, 2, dtype=np.float64) / dim)
        ang = np.arange(seq, dtype=np.float64)[:, None] * inv_freq[None, :]
        return np.cos(ang), np.sin(ang)

    c, s = angles(HEAD_DIM)
    cosc = np.concatenate([c, c], axis=1)
    sinc = np.concatenate([-s, s], axis=1)
    c, s = angles(MLA_ROPE_DIM)
    z = np.zeros_like(c)
    cosd = np.concatenate([c, z, c, z], axis=1)
    sind = np.concatenate([-s, z, s, z], axis=1)
    return tuple(jnp.asarray(t, F32) for t in (cosc, sinc, cosd, sind))


def _spread_rope(t):
    half = MLA_ROPE_DIM // 2
    z = jnp.zeros(t.shape[:-1] + (half,), t.dtype)
    return jnp.concatenate([t[..., :half], z, t[..., half:], z], axis=-1)


IN_OFFSETS = {}
_o = 0
for _name, _width in (("aq", GROUP_WIDTH), ("ak", GROUP_WIDTH), ("av", GROUP_WIDTH), ("af", GROUP_HEADS),
                      ("bq", GROUP_WIDTH), ("bk", GROUP_WIDTH), ("bv", GROUP_WIDTH),
                      ("cq", GROUP_WIDTH), ("ck", GROUP_WIDTH), ("cv", GROUP_WIDTH),
                      ("dcq", MLA_Q_RANK), ("dckv", MLA_KV_RANK), ("dkr", MLA_ROPE_DIM)):
    IN_OFFSETS[_name] = (_o, _width)
    _o += _width
IN_WIDTH = _o
RELAYOUT_ROWS = 256
FORGET_LANE = MLA_KV_RANK + HEAD_DIM


def _relayout_kernel(w_ref, kr_ref, main_ref, wvt_ref):
    def piece(name):
        o, width = IN_OFFSETS[name]
        return w_ref[:, o:o + width]

    gw = GROUP_WIDTH
    for tile, name in ((T_AQ, "aq"), (T_AK, "ak"), (T_CQ, "cq"), (T_CK, "ck"), (T_DCQ, "dcq"),
                       (T_BQ, "bq"), (T_BK, "bk")):
        main_ref[:, tile * gw:(tile + 1) * gw] = piece(name).astype(BF16)
    base = T_DKV * gw
    main_ref[:, base:base + MLA_KV_RANK] = piece("dckv").astype(BF16)
    main_ref[:, base + MLA_KV_RANK:base + MLA_KV_RANK + HEAD_DIM] = kr_ref[...].astype(BF16)
    main_ref[:, base + FORGET_LANE:base + gw] = jnp.zeros((main_ref.shape[0], gw - FORGET_LANE), BF16)
    main_ref[:, base + FORGET_LANE:base + FORGET_LANE + GROUP_HEADS] = piece("af").astype(BF16)
    for gi, name in enumerate(("av", "bv", "cv")):
        wvt_ref[gi] = piece(name).T.astype(BF16)


def _layout_w_in(w):
    depth, d_model, _ = w.shape
    rb = RELAYOUT_ROWS
    o, width = IN_OFFSETS["dkr"]
    kr = _spread_rope(w[..., o:o + width])
    w_main, wvt = pl.pallas_call(
        _relayout_kernel,
        grid=(depth, d_model // rb),
        in_specs=[
            pl.BlockSpec((None, rb, IN_WIDTH), lambda l, r: (l, r, 0)),
            pl.BlockSpec((None, rb, HEAD_DIM), lambda l, r: (l, r, 0)),
        ],
        out_specs=[
            pl.BlockSpec((None, rb, PROJ_TILES * GROUP_WIDTH), lambda l, r: (l, r, 0)),
            pl.BlockSpec((None, V_GROUPS, GROUP_WIDTH, rb), lambda l, r: (l, 0, 0, r)),
        ],
        out_shape=[
            jax.ShapeDtypeStruct((depth, d_model, PROJ_TILES * GROUP_WIDTH), BF16),
            jax.ShapeDtypeStruct((depth, V_GROUPS, GROUP_WIDTH, d_model), BF16),
        ],
        compiler_params=pltpu.CompilerParams(
            dimension_semantics=("parallel", "parallel"), vmem_limit_bytes=VMEM_LIMIT),
        name="relayout_w_in",
    )(w, kr)
    return w_main, wvt


def _layout_w_uq(w):
    w = w.reshape(-1, MLA_Q_RANK, GROUP_HEADS, MLA_QK_DIM)
    w = jnp.concatenate([w[..., :MLA_NOPE_DIM], _spread_rope(w[..., MLA_NOPE_DIM:])], axis=-1)
    return w.reshape(-1, MLA_Q_RANK, GROUP_HEADS * WIDE_HEAD).astype(BF16)


def _layout_w_ukv(w):
    w = w.reshape(-1, MLA_KV_RANK, GROUP_HEADS, 2, HEAD_DIM)
    wk = w[:, :, :, 0, :].reshape(-1, MLA_KV_RANK, GROUP_WIDTH).astype(BF16)
    wvt = jnp.swapaxes(w[:, :, :, 1, :].reshape(-1, MLA_KV_RANK, GROUP_WIDTH), -1, -2).astype(BF16)
    return wk, wvt


def _logit_bound(gq, gk, dim, scale):
    return dim * scale * LOG2E * jnp.max(jnp.abs(gq), axis=-1) * jnp.max(jnp.abs(gk), axis=-1) * 1.01 + 0.5


def _layout_qk_gain(g):
    return jnp.concatenate([g[:, :MLA_NOPE_DIM], _spread_rope(g[:, MLA_NOPE_DIM:])], axis=-1)[:, None, :]


def kernel(x, attn_norm, w_in, b_forget, fox_q_norm, fox_k_norm, moba_q_norm, moba_k_norm, mla_cq_norm, mla_ckv_norm, w_uq, w_ukv, mla_q_norm, mla_k_norm, mix_out_norm, w_out, ffn_norm, w_up, conv_w, conv_b, w_down):
    batch, seq, d_model = x.shape
    depth = w_in.shape[0]
    assert d_model == D_MODEL and seq % TILE == 0 and seq % FFN_TOKENS == 0
    n = batch * seq
    x2 = x.reshape(n, d_model)
    cosc, sinc, cosd, sind = _rope_tables(seq)

    rows = lambda v: v[:, None, :]
    w_main, wvt = _layout_w_in(w_in)
    wuq = _layout_w_uq(w_uq)
    wuk, wuvt = _layout_w_ukv(w_ukv)
    bf = jnp.pad(jnp.broadcast_to(b_forget[:, :, None], (depth, GROUP_HEADS, 128)),
                 ((0, 0), (0, 8 - GROUP_HEADS), (0, 0)))
    bound_a = _logit_bound(fox_q_norm, fox_k_norm, HEAD_DIM, SCALE)
    bound_c = _logit_bound(moba_q_norm, moba_k_norm, HEAD_DIM, SCALE)
    bound_d = _logit_bound(mla_q_norm, mla_k_norm, MLA_QK_DIM, SCALE_MLA)
    big_c = jnp.maximum(MOBA_MIN_MASK, jnp.exp2(jnp.ceil(jnp.log2(4.0 * bound_c + 64.0))))
    bounds = jnp.stack([bound_a, bound_c, big_c, bound_d] + [jnp.zeros_like(bound_a)] * 4, axis=1)
    bounds = jnp.broadcast_to(bounds[:, :, None], (depth, 8, HEAD_DIM))
    gdq, gdk = _layout_qk_gain(mla_q_norm), _layout_qk_gain(mla_k_norm)
    g_mix = mix_out_norm.reshape(depth, N_GROUPS, GROUP_WIDTH)
    conv_b3 = conv_b[:, None, :]

    for l in range(depth):
        proj, vt_ac, vt_b, negc_col = _inproj(x2, rows(attn_norm), w_main, wvt, bf, l, batch, seq)

        qa, ka, qc, kc, qd, kd, vtd, w_out_bf = _prep(
            proj, negc_col, bounds, rows(fox_q_norm), rows(fox_k_norm), rows(moba_q_norm), rows(moba_k_norm),
            rows(mla_cq_norm), rows(mla_ckv_norm), gdq, gdk, wuq, wuk, wuvt,
            cosc, sinc, cosd, sind, w_out, l, batch, seq)

        o_a = _bounded_or_online(bound_a[l], "fox_attn", qa, ka, vt_ac, 0, batch, seq)
        o_b = _stickbreak_attn(proj, vt_b, batch, seq)
        o_c = _bounded_or_online(bound_c[l], "moba_attn", qc, kc, vt_ac, 1, batch, seq)
        o_d = _bounded_or_online(bound_d[l], "mla_attn", qd, kd, vtd, None, batch, seq)

        x2, h_ffn = _outproj(o_a, o_b, o_c, o_d, g_mix, w_out_bf, x2, rows(ffn_norm), l)

        act, w_down_bf = _ffn_up(h_ffn, w_up, conv_w, conv_b3, w_down, l, seq)
        x2 = _ffn_down(act, w_down_bf, x2)

    return x2.reshape(batch, seq, d_model)
```

```python
import functools
import math

import numpy as np
import jax
import jax.numpy as jnp
from jax import lax
from jax.experimental import pallas as pl
from jax.experimental.pallas import tpu as pltpu

F32 = jnp.float32
BF16 = jnp.bfloat16

D_MODEL = 2048
HEAD_DIM = 128
GROUP_HEADS = 4
GROUP_WIDTH = GROUP_HEADS * HEAD_DIM
N_GROUPS = 4
MOBA_BLOCK = 256
MOBA_TOP_K = 3
MOBA_MAX_BLOCKS = 16
MLA_Q_RANK = 512
MLA_KV_RANK = 256
MLA_NOPE_DIM = 128
MLA_ROPE_DIM = 64
MLA_QK_DIM = MLA_NOPE_DIM + MLA_ROPE_DIM
WIDE_HEAD = 256
D_FF = 5632
CONV_WIDTH = 3
ROPE_THETA = 10000.0
EPS = 1e-6

T_AQ, T_AK, T_CQ, T_CK, T_DCQ, T_DKV, T_BQ, T_BK = range(8)
PROJ_TILES = 8
V_GROUPS = 3

TILE = 512
STICK_TILE = 256
NEG = -1e30
VMEM_LIMIT = 56 * 1024 * 1024
LOG2E = math.log2(math.e)
SCALE = HEAD_DIM ** -0.5
SCALE_MLA = MLA_QK_DIM ** -0.5
FFN_TOKENS = 1024
MXU_WIDTH = 256
SUM_ROWS = 16
MAX_SAFE_BOUND = 56.0
MOBA_MIN_MASK = 256.0
STICK_DONE = 160.0


def _dot(a, b):
    return jnp.dot(a, b, preferred_element_type=F32)


def _dot_nt(a, b):
    return lax.dot_general(a, b, (((1,), (1,)), ((), ())), preferred_element_type=F32)


def _split3(a):
    hi = a.astype(BF16)
    r1 = a - hi.astype(F32)
    mid = r1.astype(BF16)
    lo = (r1 - mid.astype(F32)).astype(BF16)
    return hi, mid, lo


def _log_sigmoid_pair(z):
    sp = jnp.log(1.0 + jnp.exp(-jnp.abs(z)))
    return jnp.minimum(z, 0.0) - sp, -jnp.maximum(z, 0.0) - sp


def _rms(v, g, denom=None):
    if denom is None:
        ms = jnp.mean(v * v, axis=-1, keepdims=True)
    else:
        ms = jnp.sum(v * v, axis=-1, keepdims=True) * (1.0 / denom)
    return v * lax.rsqrt(ms + EPS) * g


def _sumsq_lanes(v):
    return _dot((v * v).astype(BF16), jnp.ones((v.shape[-1], HEAD_DIM), BF16))


def _rms_lanes(v, g, denom=None):
    width = v.shape[-1]
    r = lax.rsqrt(_sumsq_lanes(v) * (1.0 / (denom or width)) + EPS)
    if width > HEAD_DIM:
        r = jnp.concatenate([r] * (width // HEAD_DIM), axis=1)
    return v * r * g


def _inproj_kernel(x_ref, g_ref, w_ref, wvt_ref, bf_ref, proj_ref, vt_ref, vtb_ref, negc_ref,
                   carry_scr, *, tiles_per_batch, tm):
    i = pl.program_id(0)
    gw = GROUP_WIDTH
    x = x_ref[...]
    h = (x * g_ref[...]).astype(BF16)
    inv_rms = lax.rsqrt(jnp.mean(x * x, axis=-1, keepdims=True) + EPS)
    inv_cols = jnp.broadcast_to(inv_rms, (tm, HEAD_DIM))
    inv_row = inv_cols.T[0:1]
    inv_tile = jnp.concatenate([inv_cols] * (gw // HEAD_DIM), axis=1)
    dkv = _dot(h, w_ref[:, T_DKV * gw:(T_DKV + 1) * gw]) * inv_tile
    proj_ref[:, T_DKV * gw:(T_DKV + 1) * gw] = dkv.astype(BF16)
    z = dkv[:, FORGET_LANE:FORGET_LANE + HEAD_DIM].T[0:8] + bf_ref[:, 0:1]
    logf, _ = _log_sigmoid_pair(z)
    r = lax.broadcasted_iota(jnp.int32, (tm, tm), 0)
    c = lax.broadcasted_iota(jnp.int32, (tm, tm), 1)
    tri = jnp.where(r <= c, 1.0, 0.0).astype(BF16)
    parts = jnp.concatenate([p.astype(F32) for p in _split3(logf)], axis=0).astype(BF16)
    sums = _dot(parts, tri)
    cum = sums[0:8] + sums[8:16] + sums[16:24]

    @pl.when(i % tiles_per_batch == 0)
    def _():
        carry_scr[...] = jnp.zeros_like(carry_scr)

    cum = cum + carry_scr[:, 0:1]
    carry_scr[...] = jnp.broadcast_to(cum[:, tm - 1:tm], carry_scr.shape)
    nc = cum * (-LOG2E)
    nc = jnp.concatenate([nc, jnp.zeros((HEAD_DIM - 8, tm), F32)], axis=0).T
    negc_ref[...] = nc[:, 0:8]

    for jt in range(PROJ_TILES):
        if jt != T_DKV:
            proj_ref[:, jt * gw:(jt + 1) * gw] = (_dot(h, w_ref[:, jt * gw:(jt + 1) * gw]) * inv_tile).astype(BF16)
    for gi in range(V_GROUPS):
        vt = (_dot_nt(wvt_ref[gi], h) * inv_row).astype(BF16)
        if gi == 1:
            for part in range(tm // STICK_TILE):
                vtb_ref[part] = vt[:, part * STICK_TILE:(part + 1) * STICK_TILE]
        else:
            vt_ref[gi // 2] = vt


def _layer_spec(shape, layer, **kw):
    zeros = (0,) * len(shape)
    return pl.BlockSpec((None,) + tuple(shape), lambda *_: (layer,) + zeros, **kw)


def _inproj(x2, g, w, wvt, bf, layer, batch, seq):
    n = x2.shape[0]
    tm = TILE
    tpb = seq // tm
    kern = functools.partial(_inproj_kernel, tiles_per_batch=tpb, tm=tm)
    resident = pl.Buffered(1)
    return pl.pallas_call(
        kern,
        grid=(n // tm,),
        in_specs=[
            pl.BlockSpec((tm, D_MODEL), lambda i: (i, 0)),
            _layer_spec((1, D_MODEL), layer),
            _layer_spec((D_MODEL, PROJ_TILES * GROUP_WIDTH), layer, pipeline_mode=resident),
            _layer_spec((V_GROUPS, GROUP_WIDTH, D_MODEL), layer, pipeline_mode=resident),
            _layer_spec((8, 128), layer),
        ],
        out_specs=[
            pl.BlockSpec((tm, PROJ_TILES * GROUP_WIDTH), lambda i: (i, 0)),
            pl.BlockSpec((2, None, None, GROUP_WIDTH, tm), lambda i: (0, i // tpb, i % tpb, 0, 0)),
            pl.BlockSpec((None, tm // STICK_TILE, GROUP_WIDTH, STICK_TILE), lambda i: (i // tpb, i % tpb, 0, 0)),
            pl.BlockSpec((tm, 8), lambda i: (i, 0)),
        ],
        out_shape=[
            jax.ShapeDtypeStruct((n, PROJ_TILES * GROUP_WIDTH), BF16),
            jax.ShapeDtypeStruct((2, batch, tpb, GROUP_WIDTH, tm), BF16),
            jax.ShapeDtypeStruct((batch, seq // STICK_TILE, GROUP_WIDTH, STICK_TILE), BF16),
            jax.ShapeDtypeStruct((n, 8), F32),
        ],
        scratch_shapes=[pltpu.VMEM((8, 128), F32)],
        compiler_params=pltpu.CompilerParams(
            dimension_semantics=("arbitrary",), vmem_limit_bytes=VMEM_LIMIT),
        name="inproj",
    )(x2, g, w, wvt, bf)


def _prep_kernel(a_ref, c_ref, d_ref, nc_ref, bnd_ref, gaq, gak, gcq, gck, gdcq, gdckv, gdq, gdk, wuq, wuk, wuvt,
                 cosc, sinc, cosd, sind, wo_ref,
                 qa_ref, ka_ref, qc_ref, kc_ref, qd_ref, kd_ref, vtd_ref, wo_out_ref, kmt_scr,
                 *, tiles_per_batch, tm):
    i = pl.program_id(0)
    t = i % tiles_per_batch
    hd = HEAD_DIM
    wh = WIDE_HEAD
    blocks_per_tile = tm // MOBA_BLOCK
    lane = lax.broadcasted_iota(jnp.int32, (tm, hd), 1)
    bound_a = bnd_ref[0:1, :]
    bound_c = bnd_ref[1:2, :]
    big_c = bnd_ref[2:3, :]
    bound_d = bnd_ref[3:4, :]

    his, mids, los = (p.astype(F32) for p in _split3(nc_ref[...]))
    lane1 = lane[0:1]
    k_const = jnp.where(lane1 < 6, 1.0, jnp.where(lane1 == 6, -bound_a, 0.0))
    q_const = jnp.where(lane1 < 3, 1.0, jnp.where(lane1 == 6, 1.0, 0.0))
    gaq_scaled = gaq[...] * (SCALE * LOG2E)
    for h in range(GROUP_HEADS):
        q = a_ref[:, h * hd:(h + 1) * hd].astype(F32)
        qa_ref[:, h * wh:h * wh + hd] = _rms_lanes(q, gaq_scaled).astype(BF16)
        k = a_ref[:, GROUP_WIDTH + h * hd:GROUP_WIDTH + (h + 1) * hd].astype(F32)
        ka_ref[:, h * wh:h * wh + hd] = _rms_lanes(k, gak[...]).astype(BF16)
        hi, mid, lo = his[:, h:h + 1], mids[:, h:h + 1], los[:, h:h + 1]
        k_parts = jnp.where(lane == 0, hi, jnp.where(lane == 1, mid, lo))
        q_parts = jnp.where(lane == 3, -hi, jnp.where(lane == 4, -mid, -lo))
        k_aug = jnp.where(lane < 3, k_parts, k_const)
        q_aug = jnp.where(lane < 3, q_const, jnp.where(lane < 6, q_parts, q_const))
        ka_ref[:, h * wh + hd:(h + 1) * wh] = k_aug.astype(BF16)
        qa_ref[:, h * wh + hd:(h + 1) * wh] = q_aug.astype(BF16)

    @pl.when(i == 0)
    def _():
        kmt_scr[...] = jnp.zeros_like(kmt_scr)

    cc = cosc[...]
    sc = sinc[...]
    sub8 = lax.broadcasted_iota(jnp.int32, (8, hd), 0)
    key_blk = t * blocks_per_tile + lax.broadcasted_iota(jnp.int32, (tm, hd), 0) // MOBA_BLOCK

    def rope_c(v):
        return v * cc + pltpu.roll(v, hd // 2, axis=1) * sc

    kc_aug = jnp.where(lane == key_blk, big_c, jnp.where(lane == MOBA_MAX_BLOCKS, -(bound_c + big_c), 0.0))
    kc_aug = kc_aug.astype(BF16)
    for h in range(GROUP_HEADS):
        k = rope_c(_rms_lanes(c_ref[:, GROUP_WIDTH + h * hd:GROUP_WIDTH + (h + 1) * hd].astype(F32), gck[...]))
        kc_ref[:, h * wh:h * wh + hd] = k.astype(BF16)
        kc_ref[:, h * wh + hd:(h + 1) * wh] = kc_aug
        for blk in range(blocks_per_tile):
            km = jnp.mean(k[blk * MOBA_BLOCK:(blk + 1) * MOBA_BLOCK], axis=0, keepdims=True)
            nblk = t * blocks_per_tile + blk
            base = pl.multiple_of(h * hd + (nblk // 8) * 8, 8)
            old = kmt_scr[pl.ds(base, 8), h * hd:(h + 1) * hd]
            kmt_scr[pl.ds(base, 8), h * hd:(h + 1) * hd] = jnp.where(sub8 == nblk % 8, km, old)

    qs = []
    gcq_scaled = gcq[...] * (SCALE * LOG2E)
    for h in range(GROUP_HEADS):
        q = rope_c(_rms_lanes(c_ref[:, h * hd:(h + 1) * hd].astype(F32), gcq_scaled))
        qc_ref[:, h * wh:h * wh + hd] = q.astype(BF16)
        qs.append(q)
    q_all = jnp.concatenate(qs, axis=1)
    qh, qm, _ = _split3(q_all)
    kh, km_, _ = _split3(kmt_scr[...])
    gate_t = _dot_nt(kh, qh) + _dot_nt(km_, qh) + _dot_nt(kh, qm)

    nb = MOBA_MAX_BLOCKS
    blk_row = lax.broadcasted_iota(jnp.int32, (nb, tm), 0)
    own = t * blocks_per_tile + lax.broadcasted_iota(jnp.int32, (nb, tm), 1) // MOBA_BLOCK
    blk_f = blk_row.astype(F32)
    ninf = jnp.float32(-jnp.inf)
    no_rows = jnp.zeros((hd - nb, tm), F32)
    for h in range(GROUP_HEADS):
        g = jnp.where(blk_row < own, gate_t[h * hd:h * hd + nb, :], ninf)
        sel = jnp.where(blk_row == own, 1.0, 0.0)
        for _ in range(MOBA_TOP_K):
            mx = jnp.max(g, axis=0, keepdims=True)
            cand = jnp.where(g == mx, blk_f, 1e9)
            cand = jnp.where(mx > ninf, cand, 1e9)
            idx = jnp.min(cand, axis=0, keepdims=True)
            pick = blk_f == idx
            sel = jnp.where(pick, 1.0, sel)
            g = jnp.where(pick, ninf, g)
        sel_rows = jnp.concatenate([sel, no_rows], axis=0).T
        q_aug = jnp.where(lane == nb, 1.0, sel_rows)
        qc_ref[:, h * wh + hd:(h + 1) * wh] = q_aug.astype(BF16)

    cd = cosd[...]
    sd = sind[...]
    free_lane = lane == MLA_ROPE_DIM // 2

    def rope_d(v):
        return v * cd + pltpu.roll(v, hd // 2, axis=1) * sd

    cq = _rms_lanes(d_ref[:, 0:MLA_Q_RANK].astype(F32), gdcq[...]).astype(BF16)
    qf = _dot(cq, wuq[...])
    gdq_scaled = gdq[...] * (SCALE_MLA * LOG2E)
    for h in range(GROUP_HEADS):
        qn = _rms_lanes(qf[:, h * wh:(h + 1) * wh], gdq_scaled, denom=MLA_QK_DIM)
        qd_ref[:, h * wh:h * wh + hd] = qn[:, :hd].astype(BF16)
        qd_ref[:, h * wh + hd:(h + 1) * wh] = jnp.where(free_lane, 1.0, rope_d(qn[:, hd:])).astype(BF16)

    ckv = _rms_lanes(d_ref[:, MLA_Q_RANK:MLA_Q_RANK + MLA_KV_RANK].astype(F32), gdckv[...]).astype(BF16)
    knf = _dot(ckv, wuk[...])
    vtd_ref[...] = _dot_nt(wuvt[...], ckv).astype(BF16)
    wo_out_ref[...] = wo_ref[...].astype(BF16)
    kr = d_ref[:, MLA_Q_RANK + MLA_KV_RANK:MLA_Q_RANK + MLA_KV_RANK + hd].astype(F32)
    ss_r = _sumsq_lanes(kr)
    gk = gdk[...]
    for h in range(GROUP_HEADS):
        kn = knf[:, h * hd:(h + 1) * hd]
        r = lax.rsqrt((_sumsq_lanes(kn) + ss_r) * (1.0 / MLA_QK_DIM) + EPS)
        kd_ref[:, h * wh:h * wh + hd] = (kn * r * gk[:, :hd]).astype(BF16)
        k_rot = rope_d(kr * r * gk[:, hd:])
        kd_ref[:, h * wh + hd:(h + 1) * wh] = jnp.where(free_lane, -bound_d, k_rot).astype(BF16)


def _prep(proj, negc_col, bounds, gaq, gak, gcq, gck, gdcq, gdckv, gdq, gdk, wuq, wuk, wuvt,
          cosc, sinc, cosd, sind, w_out, layer, batch, seq):
    n = proj.shape[0]
    tm = TILE
    tpb = seq // tm
    assert seq // MOBA_BLOCK <= MOBA_MAX_BLOCKS
    d_mix = N_GROUPS * GROUP_WIDTH
    slab = d_mix // (n // tm)
    assert slab * (n // tm) == d_mix and slab % 16 == 0
    kern = functools.partial(_prep_kernel, tiles_per_batch=tpb, tm=tm)
    row = lambda i: (i, 0)
    tab = lambda i: (i % tpb, 0)
    w2 = 2 * GROUP_WIDTH
    return pl.pallas_call(
        kern,
        grid=(n // tm,),
        in_specs=[
            pl.BlockSpec((tm, w2), lambda i: (i, 0)),
            pl.BlockSpec((tm, w2), lambda i: (i, 1)),
            pl.BlockSpec((tm, w2), lambda i: (i, 2)),
            pl.BlockSpec((tm, 8), row),
            _layer_spec((8, HEAD_DIM), layer),
            _layer_spec((1, HEAD_DIM), layer), _layer_spec((1, HEAD_DIM), layer),
            _layer_spec((1, HEAD_DIM), layer), _layer_spec((1, HEAD_DIM), layer),
            _layer_spec((1, MLA_Q_RANK), layer), _layer_spec((1, MLA_KV_RANK), layer),
            _layer_spec((1, WIDE_HEAD), layer), _layer_spec((1, WIDE_HEAD), layer),
            _layer_spec((MLA_Q_RANK, GROUP_HEADS * WIDE_HEAD), layer),
            _layer_spec((MLA_KV_RANK, GROUP_WIDTH), layer),
            _layer_spec((GROUP_WIDTH, MLA_KV_RANK), layer),
            pl.BlockSpec((tm, HEAD_DIM), tab), pl.BlockSpec((tm, HEAD_DIM), tab),
            pl.BlockSpec((tm, HEAD_DIM), tab), pl.BlockSpec((tm, HEAD_DIM), tab),
            pl.BlockSpec((None, slab, D_MODEL), lambda i: (layer, i, 0)),
        ],
        out_specs=[pl.BlockSpec((tm, w2), row)] * 6 + [
            pl.BlockSpec((None, None, GROUP_WIDTH, tm), lambda i: (i // tpb, i % tpb, 0, 0)),
            pl.BlockSpec((slab, D_MODEL), row),
        ],
        out_shape=[jax.ShapeDtypeStruct((n, w2), BF16)] * 6 + [
            jax.ShapeDtypeStruct((batch, tpb, GROUP_WIDTH, tm), BF16),
            jax.ShapeDtypeStruct((d_mix, D_MODEL), BF16),
        ],
        scratch_shapes=[pltpu.VMEM((GROUP_WIDTH, GROUP_WIDTH), F32)],
        compiler_params=pltpu.CompilerParams(
            dimension_semantics=("arbitrary",), vmem_limit_bytes=VMEM_LIMIT),
        name="prep",
    )(proj, proj, proj, negc_col, bounds, gaq, gak, gcq, gck, gdcq, gdckv, gdq, gdk, wuq, wuk, wuvt,
      cosc, sinc, cosd, sind, w_out)


def _key_query_iota(t):
    key = lax.broadcasted_iota(jnp.int32, (t, t), 0)
    qry = lax.broadcasted_iota(jnp.int32, (t, t), 1)
    return key, qry


def _softmax_step(s, vt, carry):
    m, acc = carry
    m_new = jnp.maximum(m, jnp.max(s, axis=0, keepdims=True))
    alpha = jnp.exp2(m - m_new)
    p = jnp.exp2(s - m_new)
    return m_new, alpha * acc + _dot(vt, p.astype(BF16))


def _softmax_init(t):
    return (jnp.full((1, t), NEG, F32), jnp.zeros((HEAD_DIM + SUM_ROWS, t), F32))


def _softmax_attn_kernel(q_ref, k_ref, vt_ref, o_ref, acc_scr, *, t, bounded):
    qi = pl.program_id(1)
    key, qry = _key_query_iota(t)
    causal = key <= qry
    dk = WIDE_HEAD
    dv = HEAD_DIM
    ones_rows = jnp.ones((SUM_ROWS, t), BF16)

    def logits(j, h, diag):
        off = pl.multiple_of(j * t, t)
        s = _dot_nt(k_ref[pl.ds(off, t), h * dk:(h + 1) * dk], q_ref[:, h * dk:(h + 1) * dk])
        return jnp.where(causal, s, NEG) if diag else s

    def values(j, h):
        return jnp.concatenate([vt_ref[j, h * dv:(h + 1) * dv, :], ones_rows], axis=0)

    if bounded:
        acc_scr[...] = jnp.zeros_like(acc_scr)

        def step(*tiles):
            s = [[logits(j, h, diag) for h in range(GROUP_HEADS)] for j, diag in tiles]
            for n, (j, _) in enumerate(tiles):
                for h in range(GROUP_HEADS):
                    acc_scr[h] += _dot(values(j, h), jnp.exp2(s[n][h]).astype(BF16))

        left = jnp.maximum(qi - 1, 0)

        def pair(jj, c):
            step((2 * jj, False), (2 * jj + 1, False))
            return c

        lax.fori_loop(0, left // 2, pair, 0)

        @pl.when(left % 2 == 1)
        def _():
            step((left - 1, False))

        @pl.when(qi == 0)
        def _():
            step((qi, True))

        @pl.when(qi > 0)
        def _():
            step((qi, True), (qi - 1, False))

        accs = [acc_scr[h] for h in range(GROUP_HEADS)]
    else:
        def step(j, carry, diag):
            out = []
            for h in range(GROUP_HEADS):
                out.append(_softmax_step(logits(j, h, diag), values(j, h), carry[h]))
            return tuple(out)

        carry = step(qi, tuple(_softmax_init(t) for _ in range(GROUP_HEADS)), True)
        carry = lax.fori_loop(0, qi, lambda j, c: step(j, c, False), carry)
        accs = [c[1] for c in carry]

    for h in range(GROUP_HEADS):
        a = accs[h]
        o_ref[:, h * dv:(h + 1) * dv] = (a[:dv] / a[dv:dv + 1]).T.astype(BF16)


def _softmax_attn(name, q_arr, k_arr, vt_arr, vt_group, batch, seq, bounded):
    t = TILE
    nq = seq // t
    n = batch * seq
    kern = functools.partial(_softmax_attn_kernel, t=t, bounded=bounded)
    if vt_group is None:
        vt_spec = pl.BlockSpec((None, nq, GROUP_WIDTH, t), lambda b, i: (b, 0, 0, 0))
    else:
        vt_spec = pl.BlockSpec((None, None, nq, GROUP_WIDTH, t), lambda b, i: (vt_group, b, 0, 0, 0))
    return pl.pallas_call(
        kern,
        grid=(batch, nq),
        in_specs=[
            pl.BlockSpec((t, GROUP_HEADS * WIDE_HEAD), lambda b, i: (b * nq + i, 0)),
            pl.BlockSpec((seq, GROUP_HEADS * WIDE_HEAD), lambda b, i: (b, 0)),
            vt_spec,
        ],
        out_specs=pl.BlockSpec((t, GROUP_WIDTH), lambda b, i: (b * nq + i, 0)),
        out_shape=jax.ShapeDtypeStruct((n, GROUP_WIDTH), BF16),
        scratch_shapes=[pltpu.VMEM((GROUP_HEADS, HEAD_DIM + SUM_ROWS, t), F32)],
        compiler_params=pltpu.CompilerParams(
            dimension_semantics=("parallel", "parallel"), vmem_limit_bytes=VMEM_LIMIT),
        name=name + ("" if bounded else "_online"),
    )(q_arr, k_arr, vt_arr)


def _bounded_or_online(bound, name, *args):
    return lax.cond(bound < MAX_SAFE_BOUND,
                    lambda: _softmax_attn(name, *args, bounded=True),
                    lambda: _softmax_attn(name, *args, bounded=False))


def _stickbreak_kernel(q_ref, k_ref, vt_ref, o_ref, acc_scr, rest_scr, *, t):
    qi = pl.program_id(1)
    key, qry = _key_query_iota(t)
    strict = key < qry
    later = jnp.where(qry > key, 1.0, 0.0).astype(BF16)
    hd = HEAD_DIM
    acc_scr[...] = jnp.zeros_like(acc_scr)
    rest_scr[...] = jnp.zeros_like(rest_scr)

    def step(tiles):
        qs = [(q_ref[:, h * hd:(h + 1) * hd].astype(F32) * (SCALE * LOG2E)).astype(BF16) for h in range(GROUP_HEADS)]
        zs, drops, afters = [], [], []
        for j, diag in tiles:
            off = pl.multiple_of(j * t, t)
            zs.append([_dot_nt(k_ref[pl.ds(off, t), h * hd:(h + 1) * hd], qs[h]) for h in range(GROUP_HEADS)])
        for n, (j, diag) in enumerate(tiles):
            row = []
            for h in range(GROUP_HEADS):
                z = zs[n][h]
                drop = jnp.maximum(z, 0.0) + jnp.log(1.0 + jnp.exp2(-jnp.abs(z))) * LOG2E
                row.append(jnp.where(strict, drop, 0.0) if diag else drop)
            drops.append(row)
            afters.append([_dot(later, d.astype(BF16)) for d in row])
        low = None
        for h in range(GROUP_HEADS):
            used = rest_scr[h]
            for n, (j, diag) in enumerate(tiles):
                z, drop, after = zs[n][h], drops[n][h], afters[n][h]
                w = jnp.exp2((z - drop) - (after + used))
                if diag:
                    w = jnp.where(strict, w, 0.0)
                acc_scr[h] += _dot(vt_ref[j, h * hd:(h + 1) * hd, :], w.astype(BF16))
                used = used + after[0:1] + drop[0:1]
            rest_scr[h] = used
            low = used if low is None else jnp.minimum(low, used)
        return jnp.min(low)

    def cond(c):
        j, low = c
        return jnp.logical_and(j >= 0, low < STICK_DONE)

    def body(c):
        j, _ = c
        return j - 1, step([(j, False)])

    first = lax.cond(qi == 0, lambda: step([(qi, True)]), lambda: step([(qi, True), (qi - 1, False)]))
    lax.while_loop(cond, body, (qi - 2, first))
    for h in range(GROUP_HEADS):
        o_ref[:, h * hd:(h + 1) * hd] = acc_scr[h].T.astype(BF16)


def _stickbreak_attn(proj, vtb, batch, seq):
    t = STICK_TILE
    nq = seq // t
    n = batch * seq
    kern = functools.partial(_stickbreak_kernel, t=t)
    return pl.pallas_call(
        kern,
        grid=(batch, nq),
        in_specs=[
            pl.BlockSpec((t, GROUP_WIDTH), lambda b, i: (b * nq + i, T_BQ)),
            pl.BlockSpec((seq, GROUP_WIDTH), lambda b, i: (b, T_BK)),
            pl.BlockSpec((None, nq, GROUP_WIDTH, t), lambda b, i: (b, 0, 0, 0)),
        ],
        out_specs=pl.BlockSpec((t, GROUP_WIDTH), lambda b, i: (b * nq + i, 0)),
        out_shape=jax.ShapeDtypeStruct((n, GROUP_WIDTH), BF16),
        scratch_shapes=[pltpu.VMEM((GROUP_HEADS, HEAD_DIM, t), F32), pltpu.VMEM((GROUP_HEADS, 1, t), F32)],
        compiler_params=pltpu.CompilerParams(
            dimension_semantics=("parallel", "parallel"), vmem_limit_bytes=VMEM_LIMIT),
        name="stickbreak_attn",
    )(proj, proj, vtb)


def _outproj_kernel(oa_ref, ob_ref, oc_ref, od_ref, g_ref, w_ref, x_ref, gf_ref, out_ref, hf_ref):
    ys = [_rms(o_ref[...].astype(F32), g_ref[gi:gi + 1, :]).astype(BF16)
          for gi, o_ref in enumerate((oa_ref, ob_ref, oc_ref, od_ref))]
    acc = x_ref[...] + _dot(jnp.concatenate(ys, axis=1), w_ref[...])
    out_ref[...] = acc
    hf_ref[...] = _rms(acc, gf_ref[...]).astype(BF16)


def _outproj(oa, ob, oc, od, g, w, x2, g_ffn, layer):
    n = x2.shape[0]
    tm = 512
    row = lambda i: (i, 0)
    return pl.pallas_call(
        _outproj_kernel,
        grid=(n // tm,),
        in_specs=[pl.BlockSpec((tm, GROUP_WIDTH), row)] * 4 + [
            _layer_spec((N_GROUPS, GROUP_WIDTH), layer),
            pl.BlockSpec((N_GROUPS * GROUP_WIDTH, D_MODEL), lambda i: (0, 0), pipeline_mode=pl.Buffered(1)),
            pl.BlockSpec((tm, D_MODEL), row),
            _layer_spec((1, D_MODEL), layer),
        ],
        out_specs=[pl.BlockSpec((tm, D_MODEL), row), pl.BlockSpec((tm, D_MODEL), row)],
        out_shape=[jax.ShapeDtypeStruct((n, D_MODEL), F32), jax.ShapeDtypeStruct((n, D_MODEL), BF16)],
        compiler_params=pltpu.CompilerParams(
            dimension_semantics=("parallel",), vmem_limit_bytes=VMEM_LIMIT),
        name="outproj",
    )(oa, ob, oc, od, g, w, x2, g_ffn)


def _ffn_up_kernel(h_ref, wg_ref, wv_ref, cwg_ref, cwv_ref, cbg_ref, cbv_ref, wd_ref,
                   out_ref, wd_out_ref, wg_scr, wv_scr, tail_g, tail_v, *, tiles_per_batch, tm):
    i = pl.program_id(1)
    wd_out_ref[...] = wd_ref[...].astype(BF16)

    @pl.when(i == 0)
    def _():
        wg_scr[...] = wg_ref[...].astype(BF16)
        wv_scr[...] = wv_ref[...].astype(BF16)

    @pl.when(i % tiles_per_batch == 0)
    def _():
        tail_g[...] = jnp.zeros_like(tail_g)
        tail_v[...] = jnp.zeros_like(tail_v)

    def conv(u, cw_ref, cb_ref, prev):
        ue = jnp.concatenate([prev, u], axis=0)
        y = cb_ref[...] + cw_ref[2:3, :] * u
        for back in range(1, CONV_WIDTH):
            y = y + cw_ref[CONV_WIDTH - 1 - back:CONV_WIDTH - back, :] * pltpu.roll(ue, back, axis=0)[8:]
        return y, u[u.shape[0] - 8:]

    h = h_ref[...]
    u_g = _dot(h, wg_scr[...])
    u_v = _dot(h, wv_scr[...])
    gate, tail_g[...] = conv(u_g, cwg_ref, cbg_ref, tail_g[...])
    val, tail_v[...] = conv(u_v, cwv_ref, cbv_ref, tail_v[...])
    out_ref[...] = (gate / (1.0 + jnp.exp(-gate)) * val).astype(BF16)


def _ffn_up(h, w_up, conv_w, conv_b, w_down, layer, seq):
    n = h.shape[0]
    tm = FFN_TOKENS
    tn = 512
    nj = D_FF // tn
    nt = n // tm
    tpb = seq // tm
    slab = D_FF // (nj * nt)
    assert slab * nj * nt == D_FF and slab % 16 == 0
    kern = functools.partial(_ffn_up_kernel, tiles_per_batch=tpb, tm=tm)
    return pl.pallas_call(
        kern,
        grid=(nj, nt),
        in_specs=[
            pl.BlockSpec((tm, D_MODEL), lambda j, i: (i, 0)),
            pl.BlockSpec((None, D_MODEL, tn), lambda j, i: (layer, 0, j)),
            pl.BlockSpec((None, D_MODEL, tn), lambda j, i: (layer, 0, j + nj)),
            pl.BlockSpec((None, CONV_WIDTH, tn), lambda j, i: (layer, 0, j)),
            pl.BlockSpec((None, CONV_WIDTH, tn), lambda j, i: (layer, 0, j + nj)),
            pl.BlockSpec((None, 1, tn), lambda j, i: (layer, 0, j)),
            pl.BlockSpec((None, 1, tn), lambda j, i: (layer, 0, j + nj)),
            pl.BlockSpec((None, slab, D_MODEL), lambda j, i: (layer, j * nt + i, 0)),
        ],
        out_specs=[pl.BlockSpec((tm, tn), lambda j, i: (i, j)),
                   pl.BlockSpec((slab, D_MODEL), lambda j, i: (j * nt + i, 0))],
        out_shape=[jax.ShapeDtypeStruct((n, D_FF), BF16), jax.ShapeDtypeStruct((D_FF, D_MODEL), BF16)],
        scratch_shapes=[pltpu.VMEM((D_MODEL, tn), BF16), pltpu.VMEM((D_MODEL, tn), BF16),
                        pltpu.VMEM((8, tn), F32), pltpu.VMEM((8, tn), F32)],
        compiler_params=pltpu.CompilerParams(
            dimension_semantics=("arbitrary", "arbitrary"), vmem_limit_bytes=VMEM_LIMIT),
        name="ffn_up",
    )(h, w_up, w_up, conv_w, conv_w, conv_b, conv_b, w_down)


def _ffn_down_kernel(a_ref, w_ref, x_ref, out_ref):
    a = a_ref[...]
    tn = 2 * MXU_WIDTH
    for c in range(D_MODEL // tn):
        cols = slice(c * tn, (c + 1) * tn)
        out_ref[:, cols] = x_ref[:, cols] + _dot(a, w_ref[:, cols])


def _ffn_down(act, w_down, x2):
    n = x2.shape[0]
    tm = 512
    row = lambda i: (i, 0)
    return pl.pallas_call(
        _ffn_down_kernel,
        grid=(n // tm,),
        in_specs=[
            pl.BlockSpec((tm, D_FF), row),
            pl.BlockSpec((D_FF, D_MODEL), lambda i: (0, 0), pipeline_mode=pl.Buffered(1)),
            pl.BlockSpec((tm, D_MODEL), row),
        ],
        out_specs=pl.BlockSpec((tm, D_MODEL), row),
        out_shape=jax.ShapeDtypeStruct((n, D_MODEL), F32),
        compiler_params=pltpu.CompilerParams(
            dimension_semantics=("parallel",), vmem_limit_bytes=VMEM_LIMIT),
        name="ffn_down",
    )(act, w_down, x2)


def _rope_tables(seq):
    def angles(dim):
        inv_freq = ROPE_THETA ** (-np.arange(0, dim, 2, dtype=np.float64) / dim)
        ang = np.arange(seq, dtype=np.float64)[:, None] * inv_freq[None, :]
        return np.cos(ang), np.sin(ang)

    c, s = angles(HEAD_DIM)
    cosc = np.concatenate([c, c], axis=1)
    sinc = np.concatenate([-s, s], axis=1)
    c, s = angles(MLA_ROPE_DIM)
    z = np.zeros_like(c)
    cosd = np.concatenate([c, z, c, z], axis=1)
    sind = np.concatenate([-s, z, s, z], axis=1)
    return tuple(jnp.asarray(t, F32) for t in (cosc, sinc, cosd, sind))


def _spread_rope(t):
    half = MLA_ROPE_DIM // 2
    z = jnp.zeros(t.shape[:-1] + (half,), t.dtype)
    return jnp.concatenate([t[..., :half], z, t[..., half:], z], axis=-1)


IN_OFFSETS = {}
_o = 0
for _name, _width in (("aq", GROUP_WIDTH), ("ak", GROUP_WIDTH), ("av", GROUP_WIDTH), ("af", GROUP_HEADS),
                      ("bq", GROUP_WIDTH), ("bk", GROUP_WIDTH), ("bv", GROUP_WIDTH),
                      ("cq", GROUP_WIDTH), ("ck", GROUP_WIDTH), ("cv", GROUP_WIDTH),
                      ("dcq", MLA_Q_RANK), ("dckv", MLA_KV_RANK), ("dkr", MLA_ROPE_DIM)):
    IN_OFFSETS[_name] = (_o, _width)
    _o += _width
IN_WIDTH = _o
RELAYOUT_ROWS = 256
FORGET_LANE = MLA_KV_RANK + HEAD_DIM


def _relayout_kernel(w_ref, kr_ref, main_ref, wvt_ref):
    def piece(name):
        o, width = IN_OFFSETS[name]
        return w_ref[:, o:o + width]

    gw = GROUP_WIDTH
    for tile, name in ((T_AQ, "aq"), (T_AK, "ak"), (T_CQ, "cq"), (T_CK, "ck"), (T_DCQ, "dcq"),
                       (T_BQ, "bq"), (T_BK, "bk")):
        main_ref[:, tile * gw:(tile + 1) * gw] = piece(name).astype(BF16)
    base = T_DKV * gw
    main_ref[:, base:base + MLA_KV_RANK] = piece("dckv").astype(BF16)
    main_ref[:, base + MLA_KV_RANK:base + MLA_KV_RANK + HEAD_DIM] = kr_ref[...].astype(BF16)
    main_ref[:, base + FORGET_LANE:base + gw] = jnp.zeros((main_ref.shape[0], gw - FORGET_LANE), BF16)
    main_ref[:, base + FORGET_LANE:base + FORGET_LANE + GROUP_HEADS] = piece("af").astype(BF16)
    for gi, name in enumerate(("av", "bv", "cv")):
        wvt_ref[gi] = piece(name).T.astype(BF16)


def _layout_w_in(w):
    depth, d_model, _ = w.shape
    rb = RELAYOUT_ROWS
    o, width = IN_OFFSETS["dkr"]
    kr = _spread_rope(w[..., o:o + width])
    w_main, wvt = pl.pallas_call(
        _relayout_kernel,
        grid=(depth, d_model // rb),
        in_specs=[
            pl.BlockSpec((None, rb, IN_WIDTH), lambda l, r: (l, r, 0)),
            pl.BlockSpec((None, rb, HEAD_DIM), lambda l, r: (l, r, 0)),
        ],
        out_specs=[
            pl.BlockSpec((None, rb, PROJ_TILES * GROUP_WIDTH), lambda l, r: (l, r, 0)),
            pl.BlockSpec((None, V_GROUPS, GROUP_WIDTH, rb), lambda l, r: (l, 0, 0, r)),
        ],
        out_shape=[
            jax.ShapeDtypeStruct((depth, d_model, PROJ_TILES * GROUP_WIDTH), BF16),
            jax.ShapeDtypeStruct((depth, V_GROUPS, GROUP_WIDTH, d_model), BF16),
        ],
        compiler_params=pltpu.CompilerParams(
            dimension_semantics=("parallel", "parallel"), vmem_limit_bytes=VMEM_LIMIT),
        name="relayout_w_in",
    )(w, kr)
    return w_main, wvt


def _layout_w_uq(w):
    w = w.reshape(-1, MLA_Q_RANK, GROUP_HEADS, MLA_QK_DIM)
    w = jnp.concatenate([w[..., :MLA_NOPE_DIM], _spread_rope(w[..., MLA_NOPE_DIM:])], axis=-1)
    return w.reshape(-1, MLA_Q_RANK, GROUP_HEADS * WIDE_HEAD).astype(BF16)


def _layout_w_ukv(w):
    w = w.reshape(-1, MLA_KV_RANK, GROUP_HEADS, 2, HEAD_DIM)
    wk = w[:, :, :, 0, :].reshape(-1, MLA_KV_RANK, GROUP_WIDTH).astype(BF16)
    wvt = jnp.swapaxes(w[:, :, :, 1, :].reshape(-1, MLA_KV_RANK, GROUP_WIDTH), -1, -2).astype(BF16)
    return wk, wvt


def _logit_bound(gq, gk, dim, scale):
    return dim * scale * LOG2E * jnp.max(jnp.abs(gq), axis=-1) * jnp.max(jnp.abs(gk), axis=-1) * 1.01 + 0.5


def _layout_qk_gain(g):
    return jnp.concatenate([g[:, :MLA_NOPE_DIM], _spread_rope(g[:, MLA_NOPE_DIM:])], axis=-1)[:, None, :]


def kernel(x, attn_norm, w_in, b_forget, fox_q_norm, fox_k_norm, moba_q_norm, moba_k_norm, mla_cq_norm, mla_ckv_norm, w_uq, w_ukv, mla_q_norm, mla_k_norm, mix_out_norm, w_out, ffn_norm, w_up, conv_w, conv_b, w_down):
    batch, seq, d_model = x.shape
    depth = w_in.shape[0]
    assert d_model == D_MODEL and seq % TILE == 0 and seq % FFN_TOKENS == 0
    n = batch * seq
    x2 = x.reshape(n, d_model)
    cosc, sinc, cosd, sind = _rope_tables(seq)

    rows = lambda v: v[:, None, :]
    w_main, wvt = _layout_w_in(w_in)
    wuq = _layout_w_uq(w_uq)
    wuk, wuvt = _layout_w_ukv(w_ukv)
    bf = jnp.pad(jnp.broadcast_to(b_forget[:, :, None], (depth, GROUP_HEADS, 128)),
                 ((0, 0), (0, 8 - GROUP_HEADS), (0, 0)))
    bound_a = _logit_bound(fox_q_norm, fox_k_norm, HEAD_DIM, SCALE)
    bound_c = _logit_bound(moba_q_norm, moba_k_norm, HEAD_DIM, SCALE)
    bound_d = _logit_bound(mla_q_norm, mla_k_norm, MLA_QK_DIM, SCALE_MLA)
    big_c = jnp.maximum(MOBA_MIN_MASK, jnp.exp2(jnp.ceil(jnp.log2(4.0 * bound_c + 64.0))))
    bounds = jnp.stack([bound_a, bound_c, big_c, bound_d] + [jnp.zeros_like(bound_a)] * 4, axis=1)
    bounds = jnp.broadcast_to(bounds[:, :, None], (depth, 8, HEAD_DIM))
    gdq, gdk = _layout_qk_gain(mla_q_norm), _layout_qk_gain(mla_k_norm)
    g_mix = mix_out_norm.reshape(depth, N_GROUPS, GROUP_WIDTH)
    conv_b3 = conv_b[:, None, :]

    for l in range(depth):
        proj, vt_ac, vt_b, negc_col = _inproj(x2, rows(attn_norm), w_main, wvt, bf, l, batch, seq)

        qa, ka, qc, kc, qd, kd, vtd, w_out_bf = _prep(
            proj, negc_col, bounds, rows(fox_q_norm), rows(fox_k_norm), rows(moba_q_norm), rows(moba_k_norm),
            rows(mla_cq_norm), rows(mla_ckv_norm), gdq, gdk, wuq, wuk, wuvt,
            cosc, sinc, cosd, sind, w_out, l, batch, seq)

        o_a = _bounded_or_online(bound_a[l], "fox_attn", qa, ka, vt_ac, 0, batch, seq)
        o_b = _stickbreak_attn(proj, vt_b, batch, seq)
        o_c = _bounded_or_online(bound_c[l], "moba_attn", qc, kc, vt_ac, 1, batch, seq)
        o_d = _bounded_or_online(bound_d[l], "mla_attn", qd, kd, vtd, None, batch, seq)

        x2, h_ffn = _outproj(o_a, o_b, o_c, o_d, g_mix, w_out_bf, x2, rows(ffn_norm), l)

        act, w_down_bf = _ffn_up(h_ffn, w_up, conv_w, conv_b3, w_down, l, seq)
        x2 = _ffn_down(act, w_down_bf, x2)

    return x2.reshape(batch, seq, d_model)
```
